```python
import jax, jax.numpy as jnp
from jax import lax
import numpy as np

D_MODEL = 1024
BATCH = 8
SEQ = 8192
DEPTH = 2

D_MIX = D_MODEL
SC_WIDTH = D_MIX // 4
SC_GROUPS = 4
SC_KERNEL = 3
GDN_WIDTH = D_MIX // 2
GDN_HEADS = 4
GDN_HEAD_DIM = GDN_WIDTH // GDN_HEADS
GDN_CONV = 4
GDN_CHUNK = 64
SB_WIDTH = D_MIX - SC_WIDTH - GDN_WIDTH
SB_HEADS = 4
SB_HEAD_DIM = SB_WIDTH // SB_HEADS
SB_BLOCK = 128
D_FF = 256 * ((8 * D_MODEL // 3 + 255) // 256)
FFN_CONV = 3
NORM_EPS = 1e-6
SPLIT_SIZES = (SC_WIDTH, SC_WIDTH, SC_WIDTH,
               GDN_WIDTH, GDN_WIDTH, GDN_WIDTH, GDN_WIDTH, GDN_HEADS, GDN_HEADS,
               SB_WIDTH, SB_WIDTH, SB_WIDTH)
D_IN_PROJ = 3 * SC_WIDTH + 4 * GDN_WIDTH + 2 * GDN_HEADS + 3 * SB_WIDTH

kernel_name = 'hymba_style_conv_gdn_stickbreaking_hybrid'


def rmsnorm(x, w):
    xf = x.astype(jnp.float32)
    y = xf * lax.rsqrt(jnp.mean(xf * xf, axis=-1, keepdims=True) + NORM_EPS) * w.astype(jnp.float32)
    return y.astype(x.dtype)


def l2norm(x):
    xf = x.astype(jnp.float32)
    return xf * lax.rsqrt(jnp.sum(xf * xf, axis=-1, keepdims=True) + NORM_EPS)


def causal_dwconv(x, w):
    K, C = w.shape
    return lax.conv_general_dilated(
        x, w[:, None, :].astype(x.dtype), window_strides=(1,), padding=[(K - 1, 0)],
        dimension_numbers=('NWC', 'WIO', 'NWC'), feature_group_count=C)


def split_columns(proj):
    points = [int(p) for p in np.cumsum(np.array(SPLIT_SIZES))[:-1]]
    return jnp.split(proj, points, axis=-1)


def gated_delta_rule_chunked(q, k, v, g, beta):
    f32 = jnp.float32
    Bsz, L, H, Dk = q.shape
    Dv = v.shape[-1]
    N = L // GDN_CHUNK

    def to_chunks(t):
        t = t.reshape((Bsz, N, GDN_CHUNK, H) + t.shape[3:])
        return jnp.moveaxis(t, 3, 1)

    q = to_chunks(q.astype(f32)) * (Dk ** -0.5)
    k = to_chunks(k.astype(f32))
    v = to_chunks(v.astype(f32))
    beta = to_chunks(beta.astype(f32))
    g = jnp.cumsum(to_chunks(g.astype(f32)), axis=-1)
    idx = jnp.arange(GDN_CHUNK)
    causal = idx[:, None] >= idx[None, :]
    strict = idx[:, None] > idx[None, :]
    decay = jnp.exp(jnp.where(causal, g[..., :, None] - g[..., None, :], -jnp.inf))
    k_beta = k * beta[..., None]
    lower = jnp.where(strict, jnp.einsum('bhncd,bhnsd->bhncs', k_beta, k) * decay, 0.0)
    lhs = lower + jnp.eye(GDN_CHUNK, dtype=f32)
    rhs = jnp.concatenate([v * beta[..., None], k_beta * jnp.exp(g)[..., None]], axis=-1)
    sol = lax.linalg.triangular_solve(lhs, rhs, left_side=True, lower=True, unit_diagonal=True)
    u, w = sol[..., :Dv], sol[..., Dv:]
    attn_intra = jnp.where(causal, jnp.einsum('bhncd,bhnsd->bhncs', q, k) * decay, 0.0)

    def step(S, inp):
        q_i, k_i, u_i, w_i, g_i, a_i = inp
        v_new = u_i - jnp.einsum('bhcd,bhde->bhce', w_i, S)
        o = jnp.einsum('bhcd,bhde->bhce', q_i * jnp.exp(g_i)[..., None], S) + \
            jnp.einsum('bhcs,bhse->bhce', a_i, v_new)
        g_last = g_i[..., -1]
        S = S * jnp.exp(g_last)[..., None, None] + \
            jnp.einsum('bhcd,bhce->bhde', k_i * jnp.exp(g_last[..., None] - g_i)[..., None], v_new)
        return S, o

    xs = tuple(jnp.moveaxis(t, 2, 0) for t in (q, k, u, w, g, attn_intra))
    S0 = jnp.zeros((Bsz, H, Dk, Dv), f32)
    _, outs = lax.scan(step, S0, xs)
    return outs.transpose(1, 0, 3, 2, 4).reshape(Bsz, L, H, Dv)


def stick_breaking_attention(q, k, v):
    L, D = q.shape[1], q.shape[-1]
    scale = D ** -0.5
    q_off = jnp.arange(SB_BLOCK)
    outs = []
    for blk in range(L // SB_BLOCK):
        start, end = blk * SB_BLOCK, (blk + 1) * SB_BLOCK
        z = jnp.einsum('bqhd,bkhd->bhqk', q[:, start:end], k[:, :end]).astype(jnp.float32) * scale
        strict = jnp.arange(end)[None, :] < (start + q_off)[:, None]
        log_beta = jax.nn.log_sigmoid(z)
        log_one_minus = jnp.where(strict, jax.nn.log_sigmoid(-z), 0.0)
        tail = lax.cumsum(log_one_minus, axis=3, reverse=True) - log_one_minus
        A = jnp.where(strict, jnp.exp(log_beta + tail), 0.0)
        outs.append(jnp.einsum('bhqk,bkhd->bqhd', A.astype(v.dtype), v[:, :end]))
    return jnp.concatenate(outs, axis=1)


def _fwd_setup_inputs(seed: int = 0) -> dict:
    key = jax.random.key(seed)
    ks = jax.random.split(key, 16)
    f32 = jnp.float32
    nrm = lambda k, shape: jax.random.normal(k, shape, f32)
    dt = jnp.exp(jax.random.uniform(ks[6], (DEPTH, GDN_HEADS), f32, np.log(1e-3), np.log(1e-1)))
    return {
        'x': nrm(ks[0], (BATCH, SEQ, D_MODEL)),
        'w_norm_mix': 1.0 + 0.02 * nrm(ks[1], (DEPTH, D_MODEL)),
        'w_mix_in': nrm(ks[2], (DEPTH, D_MODEL, D_IN_PROJ)) * D_MODEL ** -0.5,
        'w_sconv': nrm(ks[3], (DEPTH, SC_KERNEL, SC_WIDTH)) * SC_KERNEL ** -0.5,
        'w_gdn_conv': nrm(ks[4], (DEPTH, GDN_CONV, 3 * GDN_WIDTH)) * GDN_CONV ** -0.5,
        'gdn_a_log': jnp.log(jax.random.uniform(ks[5], (DEPTH, GDN_HEADS), f32, 1.0, 16.0)),
        'gdn_dt_bias': dt + jnp.log(-jnp.expm1(-dt)),
        'w_gdn_norm': 1.0 + 0.02 * nrm(ks[7], (DEPTH, GDN_HEAD_DIM)),
        'w_mix_out': nrm(ks[8], (DEPTH, D_MIX, D_MODEL)) * D_MIX ** -0.5,
        'w_norm_ffn': 1.0 + 0.02 * nrm(ks[9], (DEPTH, D_MODEL)),
        'w_ffn_up': nrm(ks[10], (DEPTH, D_MODEL, 2 * D_FF)) * D_MODEL ** -0.5,
        'w_ffn_conv': nrm(ks[11], (DEPTH, FFN_CONV, 2 * D_FF)) * FFN_CONV ** -0.5,
        'w_ffn_down': nrm(ks[12], (DEPTH, D_FF, D_MODEL)) * D_FF ** -0.5,
        'w_norm_final': 1.0 + 0.02 * nrm(ks[13], (D_MODEL,)),
    }


def _fwd_reference(x, w_norm_mix, w_mix_in, w_sconv, w_gdn_conv, gdn_a_log, gdn_dt_bias, w_gdn_norm,
              w_mix_out, w_norm_ffn, w_ffn_up, w_ffn_conv, w_ffn_down, w_norm_final):
    Bsz, L, _ = x.shape
    f32 = jnp.float32
    for l in range(DEPTH):
        h = rmsnorm(x, w_norm_mix[l])
        proj = h @ w_mix_in[l]
        (sc_b, sc_c, sc_h, gq, gk, gv, gz, ga, gb, sq, sk, sv) = split_columns(proj)

        y_sc = sc_b * causal_dwconv(sc_c * sc_h, w_sconv[l])

        qkv = jax.nn.silu(causal_dwconv(jnp.concatenate([gq, gk, gv], axis=-1), w_gdn_conv[l]))
        gq, gk, gv = jnp.split(qkv, 3, axis=-1)
        heads = lambda t: t.reshape(Bsz, L, GDN_HEADS, GDN_HEAD_DIM)
        beta = jax.nn.sigmoid(gb.astype(f32))
        g = -jnp.exp(gdn_a_log[l].astype(f32)) * jax.nn.softplus(ga.astype(f32) + gdn_dt_bias[l].astype(f32))
        o = gated_delta_rule_chunked(l2norm(heads(gq)), l2norm(heads(gk)), heads(gv), g, beta)
        o = rmsnorm(o, w_gdn_norm[l]) * jax.nn.silu(heads(gz).astype(f32))
        y_gdn = o.reshape(Bsz, L, GDN_WIDTH).astype(x.dtype)

        sb_heads = lambda t: t.reshape(Bsz, L, SB_HEADS, SB_HEAD_DIM)
        y_sb = stick_breaking_attention(sb_heads(sq), sb_heads(sk), sb_heads(sv)).reshape(Bsz, L, SB_WIDTH)

        x = x + jnp.concatenate([y_sc, y_gdn, y_sb], axis=-1) @ w_mix_out[l]

        h = rmsnorm(x, w_norm_ffn[l])
        u = causal_dwconv(h @ w_ffn_up[l], w_ffn_conv[l])
        gate, val = jnp.split(u, 2, axis=-1)
        x = x + (jax.nn.silu(gate) * val) @ w_ffn_down[l]
    return rmsnorm(x, w_norm_final)


import jax as _jax
import jax.numpy as _jnp

TWIN_FORMAT = 'train_step'
FWD_PARAMS = ['x', 'w_norm_mix', 'w_mix_in', 'w_sconv', 'w_gdn_conv', 'gdn_a_log', 'gdn_dt_bias', 'w_gdn_norm', 'w_mix_out', 'w_norm_ffn', 'w_ffn_up', 'w_ffn_conv', 'w_ffn_down', 'w_norm_final']
TWIN_WEIGHTS = ['w_norm_mix', 'w_mix_in', 'w_sconv', 'w_gdn_conv', 'gdn_a_log', 'gdn_dt_bias', 'w_gdn_norm', 'w_mix_out', 'w_norm_ffn', 'w_ffn_up', 'w_ffn_conv', 'w_ffn_down', 'w_norm_final']
TWIN_DIFF_INPUT = 'x'
TWIN_INPUTS = ['x', 'w_norm_mix', 'w_mix_in', 'w_sconv', 'w_gdn_conv', 'gdn_a_log', 'gdn_dt_bias', 'w_gdn_norm', 'w_mix_out', 'w_norm_ffn', 'w_ffn_up', 'w_ffn_conv', 'w_ffn_down', 'w_norm_final', 'loss_target', 'm_w_norm_mix', 'm_w_mix_in', 'm_w_sconv', 'm_w_gdn_conv', 'm_gdn_a_log', 'm_gdn_dt_bias', 'm_w_gdn_norm', 'm_w_mix_out', 'm_w_norm_ffn', 'm_w_ffn_up', 'm_w_ffn_conv', 'm_w_ffn_down', 'm_w_norm_final', 'v_w_norm_mix', 'v_w_mix_in', 'v_w_sconv', 'v_w_gdn_conv', 'v_gdn_a_log', 'v_gdn_dt_bias', 'v_w_gdn_norm', 'v_w_mix_out', 'v_w_norm_ffn', 'v_w_ffn_up', 'v_w_ffn_conv', 'v_w_ffn_down', 'v_w_norm_final']
TWIN_OUTPUTS = ['loss', 'grad_x', 'grad_w_norm_mix', 'grad_w_mix_in', 'grad_w_sconv', 'grad_w_gdn_conv', 'grad_gdn_a_log', 'grad_gdn_dt_bias', 'grad_w_gdn_norm', 'grad_w_mix_out', 'grad_w_norm_ffn', 'grad_w_ffn_up', 'grad_w_ffn_conv', 'grad_w_ffn_down', 'grad_w_norm_final', 'delta_w_norm_mix', 'delta_w_mix_in', 'delta_w_sconv', 'delta_w_gdn_conv', 'delta_gdn_a_log', 'delta_gdn_dt_bias', 'delta_w_gdn_norm', 'delta_w_mix_out', 'delta_w_norm_ffn', 'delta_w_ffn_up', 'delta_w_ffn_conv', 'delta_w_ffn_down', 'delta_w_norm_final', 'new_m_w_norm_mix', 'new_m_w_mix_in', 'new_m_w_sconv', 'new_m_w_gdn_conv', 'new_m_gdn_a_log', 'new_m_gdn_dt_bias', 'new_m_w_gdn_norm', 'new_m_w_mix_out', 'new_m_w_norm_ffn', 'new_m_w_ffn_up', 'new_m_w_ffn_conv', 'new_m_w_ffn_down', 'new_m_w_norm_final', 'new_v_w_norm_mix', 'new_v_w_mix_in', 'new_v_w_sconv', 'new_v_w_gdn_conv', 'new_v_gdn_a_log', 'new_v_gdn_dt_bias', 'new_v_w_gdn_norm', 'new_v_w_mix_out', 'new_v_w_norm_ffn', 'new_v_w_ffn_up', 'new_v_w_ffn_conv', 'new_v_w_ffn_down', 'new_v_w_norm_final']
TWIN_LEAF_KINDS = {'loss': 'loss', 'grad_x': 'grad_x', 'grad_w_norm_mix': 'grad_w', 'grad_w_mix_in': 'grad_w', 'grad_w_sconv': 'grad_w', 'grad_w_gdn_conv': 'grad_w', 'grad_gdn_a_log': 'grad_w', 'grad_gdn_dt_bias': 'grad_w', 'grad_w_gdn_norm': 'grad_w', 'grad_w_mix_out': 'grad_w', 'grad_w_norm_ffn': 'grad_w', 'grad_w_ffn_up': 'grad_w', 'grad_w_ffn_conv': 'grad_w', 'grad_w_ffn_down': 'grad_w', 'grad_w_norm_final': 'grad_w', 'delta_w_norm_mix': 'delta_w', 'delta_w_mix_in': 'delta_w', 'delta_w_sconv': 'delta_w', 'delta_w_gdn_conv': 'delta_w', 'delta_gdn_a_log': 'delta_w', 'delta_gdn_dt_bias': 'delta_w', 'delta_w_gdn_norm': 'delta_w', 'delta_w_mix_out': 'delta_w', 'delta_w_norm_ffn': 'delta_w', 'delta_w_ffn_up': 'delta_w', 'delta_w_ffn_conv': 'delta_w', 'delta_w_ffn_down': 'delta_w', 'delta_w_norm_final': 'delta_w', 'new_m_w_norm_mix': 'new_m', 'new_m_w_mix_in': 'new_m', 'new_m_w_sconv': 'new_m', 'new_m_w_gdn_conv': 'new_m', 'new_m_gdn_a_log': 'new_m', 'new_m_gdn_dt_bias': 'new_m', 'new_m_w_gdn_norm': 'new_m', 'new_m_w_mix_out': 'new_m', 'new_m_w_norm_ffn': 'new_m', 'new_m_w_ffn_up': 'new_m', 'new_m_w_ffn_conv': 'new_m', 'new_m_w_ffn_down': 'new_m', 'new_m_w_norm_final': 'new_m', 'new_v_w_norm_mix': 'new_v', 'new_v_w_mix_in': 'new_v', 'new_v_w_sconv': 'new_v', 'new_v_w_gdn_conv': 'new_v', 'new_v_gdn_a_log': 'new_v', 'new_v_gdn_dt_bias': 'new_v', 'new_v_w_gdn_norm': 'new_v', 'new_v_w_mix_out': 'new_v', 'new_v_w_norm_ffn': 'new_v', 'new_v_w_ffn_up': 'new_v', 'new_v_w_ffn_conv': 'new_v', 'new_v_w_ffn_down': 'new_v', 'new_v_w_norm_final': 'new_v'}


def _forward(args):
    return _fwd_reference(*[args[k] for k in FWD_PARAMS])


def _output_shape():
    def fwd():
        inp = _fwd_setup_inputs(0)
        return _fwd_reference(*[inp[k] for k in FWD_PARAMS])
    out = _jax.eval_shape(fwd)
    return out.shape, out.dtype

N_MICROBATCH = 1
ADAM_LR = 0.001
ADAM_B1 = 0.9
ADAM_B2 = 0.999
ADAM_EPS = 1e-08
ADAM_WD = 0.01
ADAM_STEP = 10
PER_EXAMPLE_BATCH_AXIS = {'x': 0, 'loss_target': 0}
SHARED_INPUTS = []
_WEIGHT_DTYPES = {'w_norm_mix': _jnp.float32, 'w_mix_in': _jnp.float32, 'w_sconv': _jnp.float32, 'w_gdn_conv': _jnp.float32, 'gdn_a_log': _jnp.float32, 'gdn_dt_bias': _jnp.float32, 'w_gdn_norm': _jnp.float32, 'w_mix_out': _jnp.float32, 'w_norm_ffn': _jnp.float32, 'w_ffn_up': _jnp.float32, 'w_ffn_conv': _jnp.float32, 'w_ffn_down': _jnp.float32, 'w_norm_final': _jnp.float32}
MOMENT_SCALE = {'w_norm_mix': 2.789002e-01, 'w_mix_in': 1.443401e-01, 'w_sconv': 2.392243e-01, 'w_gdn_conv': 1.086066e-01, 'gdn_a_log': 1.135367e+00, 'gdn_dt_bias': 1.076785e+00, 'w_gdn_norm': 2.913665e-01, 'w_mix_out': 1.657930e-01, 'w_norm_ffn': 1.746628e-01, 'w_ffn_up': 7.339241e-02, 'w_ffn_conv': 7.182577e-02, 'w_ffn_down': 1.201965e-01, 'w_norm_final': 6.397979e+01}


def _to_microbatches(a, axis):
    t = _jnp.moveaxis(a, axis, 0)
    t = t.reshape((N_MICROBATCH, t.shape[0] // N_MICROBATCH) + t.shape[1:])
    return _jnp.moveaxis(t, 1, axis + 1)


def setup_inputs(seed: int = 0) -> dict:
    inp = _fwd_setup_inputs(seed)
    key = _jax.random.fold_in(_jax.random.key(seed), 7919)
    shape, _ = _output_shape()
    out = dict(inp)
    out["loss_target"] = _jax.random.normal(_jax.random.fold_in(key, 0), shape, _jnp.float32)
    for i, name in enumerate(TWIN_WEIGHTS):
        w = inp[name].astype(_jnp.float32)
        if MOMENT_SCALE is None:
            s = _jnp.sqrt(_jnp.mean(_jnp.square(w)) + 1e-30)
        else:
            s = MOMENT_SCALE[name]
        km, kv = _jax.random.split(_jax.random.fold_in(key, i + 1))
        out[name] = w
        out["m_" + name] = s * _jax.random.normal(km, w.shape, _jnp.float32)
        out["v_" + name] = (s * s) * _jax.random.uniform(kv, w.shape, _jnp.float32, 0.5, 1.5)
    if N_MICROBATCH > 1:
        for name, axis in PER_EXAMPLE_BATCH_AXIS.items():
            out[name] = _to_microbatches(out[name], axis)
    return {'x': out['x'], 'w_norm_mix': out['w_norm_mix'], 'w_mix_in': out['w_mix_in'], 'w_sconv': out['w_sconv'], 'w_gdn_conv': out['w_gdn_conv'], 'gdn_a_log': out['gdn_a_log'], 'gdn_dt_bias': out['gdn_dt_bias'], 'w_gdn_norm': out['w_gdn_norm'], 'w_mix_out': out['w_mix_out'], 'w_norm_ffn': out['w_norm_ffn'], 'w_ffn_up': out['w_ffn_up'], 'w_ffn_conv': out['w_ffn_conv'], 'w_ffn_down': out['w_ffn_down'], 'w_norm_final': out['w_norm_final'], 'loss_target': out['loss_target'], 'm_w_norm_mix': out['m_w_norm_mix'], 'm_w_mix_in': out['m_w_mix_in'], 'm_w_sconv': out['m_w_sconv'], 'm_w_gdn_conv': out['m_w_gdn_conv'], 'm_gdn_a_log': out['m_gdn_a_log'], 'm_gdn_dt_bias': out['m_gdn_dt_bias'], 'm_w_gdn_norm': out['m_w_gdn_norm'], 'm_w_mix_out': out['m_w_mix_out'], 'm_w_norm_ffn': out['m_w_norm_ffn'], 'm_w_ffn_up': out['m_w_ffn_up'], 'm_w_ffn_conv': out['m_w_ffn_conv'], 'm_w_ffn_down': out['m_w_ffn_down'], 'm_w_norm_final': out['m_w_norm_final'], 'v_w_norm_mix': out['v_w_norm_mix'], 'v_w_mix_in': out['v_w_mix_in'], 'v_w_sconv': out['v_w_sconv'], 'v_w_gdn_conv': out['v_w_gdn_conv'], 'v_gdn_a_log': out['v_gdn_a_log'], 'v_gdn_dt_bias': out['v_gdn_dt_bias'], 'v_w_gdn_norm': out['v_w_gdn_norm'], 'v_w_mix_out': out['v_w_mix_out'], 'v_w_norm_ffn': out['v_w_norm_ffn'], 'v_w_ffn_up': out['v_w_ffn_up'], 'v_w_ffn_conv': out['v_w_ffn_conv'], 'v_w_ffn_down': out['v_w_ffn_down'], 'v_w_norm_final': out['v_w_norm_final']}


def _loss(weights, diff, rest, loss_target):
    with _jax.named_scope("forward"):
        args = {**rest, TWIN_DIFF_INPUT: diff, **{k: w.astype(_WEIGHT_DTYPES[k]) for k, w in weights.items()}}
        y = _forward(args)
    with _jax.named_scope("loss_head"):
        err = _jnp.square(y.astype(_jnp.float32) - loss_target)
        return 0.5 * _jnp.sum(_jnp.mean(err, axis=-1)) if err.ndim else 0.5 * err


def _adamw(w, g, m, v):
    m = ADAM_B1 * m + (1.0 - ADAM_B1) * g
    v = ADAM_B2 * v + (1.0 - ADAM_B2) * _jnp.square(g)
    m_hat = m / (1.0 - ADAM_B1 ** ADAM_STEP)
    v_hat = v / (1.0 - ADAM_B2 ** ADAM_STEP)
    delta = -ADAM_LR * (m_hat / (_jnp.sqrt(v_hat) + ADAM_EPS) + ADAM_WD * w)
    return delta, m, v


def reference(x, w_norm_mix, w_mix_in, w_sconv, w_gdn_conv, gdn_a_log, gdn_dt_bias, w_gdn_norm, w_mix_out, w_norm_ffn, w_ffn_up, w_ffn_conv, w_ffn_down, w_norm_final, loss_target, m_w_norm_mix, m_w_mix_in, m_w_sconv, m_w_gdn_conv, m_gdn_a_log, m_gdn_dt_bias, m_w_gdn_norm, m_w_mix_out, m_w_norm_ffn, m_w_ffn_up, m_w_ffn_conv, m_w_ffn_down, m_w_norm_final, v_w_norm_mix, v_w_mix_in, v_w_sconv, v_w_gdn_conv, v_gdn_a_log, v_gdn_dt_bias, v_w_gdn_norm, v_w_mix_out, v_w_norm_ffn, v_w_ffn_up, v_w_ffn_conv, v_w_ffn_down, v_w_norm_final):
    given = dict(x=x, w_norm_mix=w_norm_mix, w_mix_in=w_mix_in, w_sconv=w_sconv, w_gdn_conv=w_gdn_conv, gdn_a_log=gdn_a_log, gdn_dt_bias=gdn_dt_bias, w_gdn_norm=w_gdn_norm, w_mix_out=w_mix_out, w_norm_ffn=w_norm_ffn, w_ffn_up=w_ffn_up, w_ffn_conv=w_ffn_conv, w_ffn_down=w_ffn_down, w_norm_final=w_norm_final, loss_target=loss_target, m_w_norm_mix=m_w_norm_mix, m_w_mix_in=m_w_mix_in, m_w_sconv=m_w_sconv, m_w_gdn_conv=m_w_gdn_conv, m_gdn_a_log=m_gdn_a_log, m_gdn_dt_bias=m_gdn_dt_bias, m_w_gdn_norm=m_w_gdn_norm, m_w_mix_out=m_w_mix_out, m_w_norm_ffn=m_w_norm_ffn, m_w_ffn_up=m_w_ffn_up, m_w_ffn_conv=m_w_ffn_conv, m_w_ffn_down=m_w_ffn_down, m_w_norm_final=m_w_norm_final, v_w_norm_mix=v_w_norm_mix, v_w_mix_in=v_w_mix_in, v_w_sconv=v_w_sconv, v_w_gdn_conv=v_w_gdn_conv, v_gdn_a_log=v_gdn_a_log, v_gdn_dt_bias=v_gdn_dt_bias, v_w_gdn_norm=v_w_gdn_norm, v_w_mix_out=v_w_mix_out, v_w_norm_ffn=v_w_norm_ffn, v_w_ffn_up=v_w_ffn_up, v_w_ffn_conv=v_w_ffn_conv, v_w_ffn_down=v_w_ffn_down, v_w_norm_final=v_w_norm_final)
    weights = {n: given[n] for n in TWIN_WEIGHTS}
    shared = {n: given[n] for n in SHARED_INPUTS}
    per_example = {n: given[n] for n in ['x']}
    grad_fn = _jax.value_and_grad(_loss, argnums=(0, 1))

    def one_microbatch(ex, loss_target):
        ex = dict(ex)
        diff = ex.pop(TWIN_DIFF_INPUT)
        return grad_fn(weights, diff, {**shared, **ex}, loss_target)

    if N_MICROBATCH == 1:
        loss, (grad_w, grad_x) = one_microbatch(per_example, given["loss_target"])
    else:
        def body(carry, xs):
            loss_sum, grad_sum = carry
            l_k, (gw_k, gx_k) = one_microbatch(xs[0], xs[1])
            with _jax.named_scope("update"):
                return (loss_sum + l_k, _jax.tree.map(_jnp.add, grad_sum, gw_k)), gx_k

        init = (_jnp.zeros((), _jnp.float32), _jax.tree.map(_jnp.zeros_like, weights))
        (loss, grad_w), grad_x = _jax.lax.scan(body, init, (per_example, given["loss_target"]))
    with _jax.named_scope("update"):
        delta_w, new_m, new_v = {}, {}, {}
        for n in TWIN_WEIGHTS:
            delta_w[n], new_m[n], new_v[n] = _adamw(weights[n], grad_w[n], given["m_" + n], given["v_" + n])
    return (loss, grad_x, *[grad_w[n] for n in TWIN_WEIGHTS], *[delta_w[n] for n in TWIN_WEIGHTS],
            *[new_m[n] for n in TWIN_WEIGHTS], *[new_v[n] for n in TWIN_WEIGHTS])
```

```python
import functools

import jax
import jax.numpy as jnp
from jax import lax
from jax.experimental import pallas as pl
from jax.experimental.pallas import tpu as pltpu

F32 = jnp.float32
MXU_DTYPE = jnp.bfloat16
HIGHEST = lax.Precision.HIGHEST

D_MODEL = 1024
DEPTH = 2
SC_WIDTH = 256
SC_KERNEL = 3
GDN_WIDTH = 512
GDN_HEADS = 4
GDN_HEAD_DIM = 128
GDN_CONV = 4
GDN_CHUNK = 64
SB_WIDTH = 256
SB_HEADS = 4
SB_HEAD_DIM = 64
SB_BLOCK = 128
D_FF = 2816
FFN_CONV = 3
NORM_EPS = 1e-6
D_IN_PROJ = 3592
ADAM_LR, ADAM_B1, ADAM_B2, ADAM_EPS, ADAM_WD, ADAM_STEP = 0.001, 0.9, 0.999, 1e-08, 0.01, 10

N_DEV = 8
LANES = 128
SUBLANES = 8
VMEM_LIMIT = 48 * 1024 * 1024

P_QKV, P_GZ, P_SC, P_SB, P_GAB, P_END = 0, 1536, 2048, 2816, 3584, 3840
Y_GDN, Y_SC, Y_SB = 0, 512, 768


def _params(semantics):
    return pltpu.CompilerParams(dimension_semantics=semantics, vmem_limit_bytes=VMEM_LIMIT)


_DIMS = {"nn": (((1,), (0,)), ((), ())), "nt": (((1,), (1,)), ((), ())), "tn": (((0,), (0,)), ((), ()))}


def _matmul(a, b, mode, out_dtype, name, tm, tn, tk, resid=None):
    if mode == "tn":
        (K, M), (K2, N) = a.shape, b.shape
    elif mode == "nt":
        (M, K), (N, K2) = a.shape, b.shape
    else:
        (M, K), (K2, N) = a.shape, b.shape
    assert K == K2 and M % tm == 0 and N % tn == 0 and K % tk == 0, (name, a.shape, b.shape, tm, tn, tk)
    nk = K // tk
    has_resid = resid is not None

    def body(*refs):
        if has_resid:
            a_ref, b_ref, r_ref, o_ref, acc = refs
        else:
            a_ref, b_ref, o_ref, acc = refs
        k = pl.program_id(2)

        @pl.when(k == 0)
        def _():
            acc[...] = jnp.zeros_like(acc)

        acc[...] += lax.dot_general(a_ref[...], b_ref[...], _DIMS[mode], preferred_element_type=F32)

        @pl.when(k == nk - 1)
        def _():
            r = acc[...]
            if has_resid:
                r = r + r_ref[...]
            o_ref[...] = r.astype(out_dtype)

    a_spec = pl.BlockSpec((tk, tm), lambda i, j, k: (k, i)) if mode == "tn" else pl.BlockSpec((tm, tk), lambda i, j, k: (i, k))
    b_spec = pl.BlockSpec((tn, tk), lambda i, j, k: (j, k)) if mode == "nt" else pl.BlockSpec((tk, tn), lambda i, j, k: (k, j))
    o_spec = pl.BlockSpec((tm, tn), lambda i, j, k: (i, j))
    in_specs = [a_spec, b_spec] + ([o_spec] if has_resid else [])
    args = (a, b) + ((resid,) if has_resid else ())
    return pl.pallas_call(
        body, name=name, grid=(M // tm, N // tn, nk), in_specs=in_specs, out_specs=o_spec,
        out_shape=jax.ShapeDtypeStruct((M, N), out_dtype),
        scratch_shapes=[pltpu.VMEM((tm, tn), F32)],
        compiler_params=_params(("parallel", "parallel", "arbitrary")),
    )(*args)


def _rms(x, w):
    return x * lax.rsqrt(jnp.mean(x * x, axis=-1, keepdims=True) + NORM_EPS) * w


ROW_TILE = 512


def _rms_fwd(x, w, name):
    L, Dm = x.shape

    def body(x_ref, w_ref, h_ref):
        h_ref[...] = _rms(x_ref[...], w_ref[...]).astype(h_ref.dtype)

    return pl.pallas_call(
        body, name=name, grid=(L // ROW_TILE,),
        in_specs=[pl.BlockSpec((ROW_TILE, Dm), lambda i: (i, 0)), pl.BlockSpec((1, Dm), lambda i: (0, 0))],
        out_specs=pl.BlockSpec((ROW_TILE, Dm), lambda i: (i, 0)),
        out_shape=jax.ShapeDtypeStruct((L, Dm), MXU_DTYPE),
        compiler_params=_params(("parallel",)),
    )(x, w)


def _rms_bwd(x, w, dh, dres, name):
    L, Dm = x.shape

    def body(x_ref, w_ref, dh_ref, dres_ref, dx_ref, dw_ref):
        _, vjp = jax.vjp(_rms, x_ref[...], w_ref[...])
        dx, dw = vjp(dh_ref[...])
        dx_ref[...] = dres_ref[...] + dx

        @pl.when(pl.program_id(0) == 0)
        def _():
            dw_ref[...] = jnp.zeros_like(dw_ref)

        dw_ref[...] += dw

    row = pl.BlockSpec((ROW_TILE, Dm), lambda i: (i, 0))
    vec = pl.BlockSpec((1, Dm), lambda i: (0, 0))
    return pl.pallas_call(
        body, name=name, grid=(L // ROW_TILE,), in_specs=[row, vec, row, row], out_specs=[row, vec],
        out_shape=[jax.ShapeDtypeStruct((L, Dm), F32), jax.ShapeDtypeStruct((1, Dm), F32)],
        compiler_params=_params(("arbitrary",)),
    )(x, w, dh, dres)


def _loss_head(x, w, target, name):
    L, Dm = x.shape

    def block_loss(xb, wb, tb):
        err = _rms(xb, wb) - tb
        return 0.5 * jnp.sum(jnp.sum(err * err, axis=-1, keepdims=True) * (1.0 / Dm), axis=0, keepdims=True)

    def body(x_ref, w_ref, t_ref, loss_ref, dx_ref, dw_ref):
        val, vjp = jax.vjp(lambda xb, wb: block_loss(xb, wb, t_ref[...]), x_ref[...], w_ref[...])
        dx, dw = vjp(jnp.ones_like(val))
        dx_ref[...] = dx

        @pl.when(pl.program_id(0) == 0)
        def _():
            dw_ref[...] = jnp.zeros_like(dw_ref)
            loss_ref[...] = jnp.zeros_like(loss_ref)

        dw_ref[...] += dw
        loss_ref[...] += val

    row = pl.BlockSpec((ROW_TILE, Dm), lambda i: (i, 0))
    vec = pl.BlockSpec((1, Dm), lambda i: (0, 0))
    one = pl.BlockSpec((1, 1), lambda i: (0, 0))
    return pl.pallas_call(
        body, name=name, grid=(L // ROW_TILE,), in_specs=[row, vec, row], out_specs=[one, row, vec],
        out_shape=[jax.ShapeDtypeStruct((1, 1), F32), jax.ShapeDtypeStruct((L, Dm), F32), jax.ShapeDtypeStruct((1, Dm), F32)],
        compiler_params=_params(("arbitrary",)),
    )(x, w, target)


HALO = SUBLANES


def _conv_specs(L, T, Cb, off):
    main = pl.BlockSpec((T, Cb), lambda j, i: (i, off + j))
    prev = pl.BlockSpec((HALO, Cb), lambda j, i: (jnp.maximum(i * (T // HALO) - 1, 0), off + j))
    nxt = pl.BlockSpec((HALO, Cb), lambda j, i: (jnp.minimum((i + 1) * (T // HALO), L // HALO - 1), off + j))
    return main, prev, nxt


def _conv_fwd(x1, w, K, Cb, ncol, out_dtype, name, x2=None, gate=None):
    (x1a, o1) = x1
    L = x1a.shape[0]
    T = min(ROW_TILE, L)
    has_mul, has_gate = x2 is not None, gate is not None

    def body(*refs):
        it = iter(refs)
        x1m, x1p = next(it), next(it)
        if has_mul:
            x2m, x2p = next(it), next(it)
        if has_gate:
            gm = next(it)
        w_ref, y_ref, scr = next(it), next(it), next(it)
        i = pl.program_id(1)
        p, pp = x1m[...].astype(F32), x1p[...].astype(F32)
        if has_mul:
            p, pp = p * x2m[...], pp * x2p[...]
        scr[0:HALO, :] = jnp.where(i > 0, pp, 0.0)
        scr[HALO:HALO + T, :] = p
        acc = w_ref[K - 1:K, :] * p
        for k in range(K - 1):
            s = K - 1 - k
            acc = acc + w_ref[k:k + 1, :] * scr[HALO - s:HALO - s + T, :]
        if has_gate:
            acc = acc * gm[...]
        y_ref[...] = acc.astype(out_dtype)

    in_specs, args = [], []
    m, p_, _ = _conv_specs(L, T, Cb, o1)
    in_specs += [m, p_]
    args += [x1a, x1a]
    if has_mul:
        m, p_, _ = _conv_specs(L, T, Cb, x2[1])
        in_specs += [m, p_]
        args += [x2[0], x2[0]]
    if has_gate:
        m, _, _ = _conv_specs(L, T, Cb, gate[1])
        in_specs += [m]
        args += [gate[0]]
    in_specs.append(pl.BlockSpec((K, Cb), lambda j, i: (0, j)))
    args.append(w)
    return pl.pallas_call(
        body, name=name, grid=(ncol, L // T), in_specs=in_specs,
        out_specs=pl.BlockSpec((T, Cb), lambda j, i: (i, j)),
        out_shape=jax.ShapeDtypeStruct((L, ncol * Cb), out_dtype),
        scratch_shapes=[pltpu.VMEM((T + HALO, Cb), F32)],
        compiler_params=_params(("parallel", "arbitrary")),
    )(*args)


def _conv_bwd(x1, w, dy, K, Cb, ncol, out_dtype, name, x2=None, gate=None):
    (x1a, o1) = x1
    L = x1a.shape[0]
    T = min(ROW_TILE, L)
    nrow = L // T
    has_mul, has_gate = x2 is not None, gate is not None

    def body(*refs):
        it = iter(refs)
        x1m, x1p = next(it), next(it)
        if has_mul:
            x2m, x2p = next(it), next(it)
        if has_gate:
            gm, gn = next(it), next(it)
        dym, dyn, w_ref = next(it), next(it), next(it)
        dx1_ref = next(it)
        if has_mul:
            dx2_ref = next(it)
        if has_gate:
            dg_ref = next(it)
        dw_ref, scr_p, scr_d = next(it), next(it), next(it)
        i = pl.program_id(1)
        p, pp = x1m[...].astype(F32), x1p[...].astype(F32)
        if has_mul:
            p, pp = p * x2m[...], pp * x2p[...]
        scr_p[0:HALO, :] = jnp.where(i > 0, pp, 0.0)
        scr_p[HALO:HALO + T, :] = p
        dcv, dcn = dym[...].astype(F32), dyn[...].astype(F32)
        if has_gate:
            dcv, dcn = dcv * gm[...], dcn * gn[...]
        scr_d[0:T, :] = dcv
        scr_d[T:T + HALO, :] = jnp.where(i < nrow - 1, dcn, 0.0)

        @pl.when(i == 0)
        def _():
            dw_ref[...] = jnp.zeros_like(dw_ref)

        dp = w_ref[K - 1:K, :] * dcv
        cv = w_ref[K - 1:K, :] * p
        dw_ref[K - 1:K, :] += jnp.sum(dcv * p, axis=0, keepdims=True)
        for k in range(K - 1):
            s = K - 1 - k
            dp = dp + w_ref[k:k + 1, :] * scr_d[s:s + T, :]
            sh = scr_p[HALO - s:HALO - s + T, :]
            dw_ref[k:k + 1, :] += jnp.sum(dcv * sh, axis=0, keepdims=True)
            if has_gate:
                cv = cv + w_ref[k:k + 1, :] * sh
        if has_gate:
            dg_ref[...] = (dym[...].astype(F32) * cv).astype(out_dtype)
        if has_mul:
            dx1_ref[...] = (dp * x2m[...]).astype(out_dtype)
            dx2_ref[...] = (dp * x1m[...]).astype(out_dtype)
        else:
            dx1_ref[...] = dp.astype(out_dtype)

    in_specs, args = [], []
    m, p_, _ = _conv_specs(L, T, Cb, o1)
    in_specs += [m, p_]
    args += [x1a, x1a]
    if has_mul:
        m, p_, _ = _conv_specs(L, T, Cb, x2[1])
        in_specs += [m, p_]
        args += [x2[0], x2[0]]
    if has_gate:
        m, _, n_ = _conv_specs(L, T, Cb, gate[1])
        in_specs += [m, n_]
        args += [gate[0], gate[0]]
    m, _, n_ = _conv_specs(L, T, Cb, dy[1])
    in_specs += [m, n_, pl.BlockSpec((K, Cb), lambda j, i: (0, j))]
    args += [dy[0], dy[0], w]
    out = pl.BlockSpec((T, Cb), lambda j, i: (i, j))
    full = jax.ShapeDtypeStruct((L, ncol * Cb), out_dtype)
    n_out = 1 + int(has_mul) + int(has_gate)
    return pl.pallas_call(
        body, name=name, grid=(ncol, nrow), in_specs=in_specs,
        out_specs=[out] * n_out + [pl.BlockSpec((SUBLANES, Cb), lambda j, i: (0, j))],
        out_shape=[full] * n_out + [jax.ShapeDtypeStruct((SUBLANES, ncol * Cb), F32)],
        scratch_shapes=[pltpu.VMEM((T + HALO, Cb), F32), pltpu.VMEM((T + HALO, Cb), F32)],
        compiler_params=_params(("parallel", "arbitrary")),
    )(*args)


GLU_COLS = 256


def _silu(x):
    return x * (1.0 / (1.0 + jnp.exp(-x)))


def _glu(g, v):
    return _silu(g) * v


def _glu_fwd(u, name):
    L = u.shape[0]
    nb = D_FF // GLU_COLS

    def body(g_ref, v_ref, a_ref):
        a_ref[...] = _glu(g_ref[...], v_ref[...]).astype(a_ref.dtype)

    return pl.pallas_call(
        body, name=name, grid=(L // ROW_TILE, nb),
        in_specs=[pl.BlockSpec((ROW_TILE, GLU_COLS), lambda i, j: (i, j)), pl.BlockSpec((ROW_TILE, GLU_COLS), lambda i, j: (i, nb + j))],
        out_specs=pl.BlockSpec((ROW_TILE, GLU_COLS), lambda i, j: (i, j)),
        out_shape=jax.ShapeDtypeStruct((L, D_FF), MXU_DTYPE),
        compiler_params=_params(("parallel", "parallel")),
    )(u, u)


def _glu_bwd(u, dact, name):
    L = u.shape[0]
    nb = D_FF // GLU_COLS

    def body(g_ref, v_ref, da_ref, du_ref):
        j = pl.program_id(1)
        _, vjp = jax.vjp(_glu, g_ref[...], v_ref[...])
        dg, dv = vjp(da_ref[...])
        du_ref[...] = jnp.where(j < nb, dg, dv)

    return pl.pallas_call(
        body, name=name, grid=(L // ROW_TILE, 2 * nb),
        in_specs=[pl.BlockSpec((ROW_TILE, GLU_COLS), lambda i, j: (i, j % nb)),
                  pl.BlockSpec((ROW_TILE, GLU_COLS), lambda i, j: (i, nb + j % nb)),
                  pl.BlockSpec((ROW_TILE, GLU_COLS), lambda i, j: (i, j % nb))],
        out_specs=pl.BlockSpec((ROW_TILE, GLU_COLS), lambda i, j: (i, j)),
        out_shape=jax.ShapeDtypeStruct((L, 2 * D_FF), F32),
        compiler_params=_params(("parallel", "parallel")),
    )(u, u, dact)


def _bdot_raw(a, b, mode):
    return lax.dot_general(a.astype(MXU_DTYPE), b.astype(MXU_DTYPE), _DIMS[mode], preferred_element_type=F32)


@functools.partial(jax.custom_vjp, nondiff_argnums=(2,))
def _bdot(a, b, mode):
    return _bdot_raw(a, b, mode)


def _bdot_fwd(a, b, mode):
    return _bdot_raw(a, b, mode), (a, b)


def _bdot_bwd(mode, res, ct):
    a, b = res
    if mode == "nn":
        return _bdot_raw(ct, b, "nt"), _bdot_raw(a, ct, "tn")
    if mode == "nt":
        return _bdot_raw(ct, b, "nn"), _bdot_raw(ct, a, "tn")
    return _bdot_raw(b, ct, "nt"), _bdot_raw(a, ct, "nn")


_bdot.defvjp(_bdot_fwd, _bdot_bwd)


def _hdot(a, b, mode="nn"):
    return lax.dot_general(a, b, _DIMS[mode], precision=HIGHEST, preferred_element_type=F32)


@jax.custom_vjp
def _inv_unit_lower(a):
    C = a.shape[0]
    eye = (lax.broadcasted_iota(jnp.int32, (C, C), 0) == lax.broadcasted_iota(jnp.int32, (C, C), 1)).astype(F32)
    t = eye - a
    p = a
    n = 1
    while 2 * n < C:
        p = _hdot(p, p)
        t = t + _hdot(t, p)
        n *= 2
    return t


def _inv_fwd(a):
    t = _inv_unit_lower(a)
    return t, t


def _inv_bwd(t, ct):
    return (-_hdot(_hdot(t, ct, "tn"), t, "nt"),)


_inv_unit_lower.defvjp(_inv_fwd, _inv_bwd)


def _softplus(x):
    return jnp.maximum(x, 0.0) + jnp.log(1.0 + jnp.exp(-jnp.abs(x)))


def _sigmoid(x):
    return 1.0 / (1.0 + jnp.exp(-x))


def _pick_lane(blk, lane):
    ids = lax.broadcasted_iota(jnp.int32, blk.shape, 1)
    return jnp.sum(jnp.where(ids == lane, blk, 0.0), axis=1, keepdims=True)


def _gdn_chunk(cq, ck, cv, gz, gab, alog, dtb, wn, S, h):
    C, Dk = cq.shape
    ga, gb = _pick_lane(gab, h), _pick_lane(gab, GDN_HEADS + h)
    al, db = _pick_lane(alog, h), _pick_lane(dtb, h)
    q, k, v = _silu(cq), _silu(ck), _silu(cv)
    q = q * lax.rsqrt(jnp.sum(q * q, axis=-1, keepdims=True) + NORM_EPS) * (Dk ** -0.5)
    k = k * lax.rsqrt(jnp.sum(k * k, axis=-1, keepdims=True) + NORM_EPS)
    beta = _sigmoid(gb)
    g = -jnp.exp(al) * _softplus(ga + db)
    row = lax.broadcasted_iota(jnp.int32, (C, C), 0)
    col = lax.broadcasted_iota(jnp.int32, (C, C), 1)
    causal, strict = row >= col, row > col
    gcb = _hdot(causal.astype(F32), jnp.broadcast_to(g, (C, Dk)))
    first = (lax.broadcasted_iota(jnp.int32, (C, Dk), 1) == 0).astype(F32)
    gr = _hdot(first, gcb, "nt")
    gc = _pick_lane(gcb, 0)
    decay = jnp.where(causal, jnp.exp(jnp.where(causal, gc - gr, 0.0)), 0.0)
    kb = k * beta
    lower = jnp.where(strict, _bdot(kb, k, "nt") * decay, 0.0)
    t = _inv_unit_lower(lower)
    egc = jnp.exp(gc)
    u = _hdot(t, v * beta)
    w = _hdot(t, kb * egc)
    attn = jnp.where(causal, _bdot(q, k, "nt") * decay, 0.0)
    v_new = u - _bdot(w, S, "nn")
    o = _bdot(q * egc, S, "nn") + _bdot(attn, v_new, "nn")
    g_last = jnp.sum(jnp.where(lax.broadcasted_iota(jnp.int32, (C, 1), 0) == C - 1, gc, 0.0), axis=0, keepdims=True)
    S_new = S * jnp.exp(g_last) + _bdot(k * jnp.exp(g_last - gc), v_new, "tn")
    y = o * lax.rsqrt(jnp.mean(o * o, axis=-1, keepdims=True) + NORM_EPS) * wn * _silu(gz)
    return y, S_new


def _gdn_fwd(cqkv, proj, alog, dtb, wn, name):
    L = cqkv.shape[0]
    C, H, Dh = GDN_CHUNK, GDN_HEADS, GDN_HEAD_DIM
    N = L // C

    def body(c_ref, gz_ref, gab_ref, al_ref, db_ref, wn_ref, y_ref, sall_ref, S):
        n = pl.program_id(0)

        @pl.when(n == 0)
        def _():
            S[...] = jnp.zeros_like(S)

        gab = gab_ref[...]
        for h in range(H):
            s_in = S[h]
            sall_ref[0, h] = s_in
            y, s_new = _gdn_chunk(c_ref[:, h * Dh:(h + 1) * Dh], c_ref[:, (H + h) * Dh:(H + h + 1) * Dh],
                                  c_ref[:, (2 * H + h) * Dh:(2 * H + h + 1) * Dh], gz_ref[:, h * Dh:(h + 1) * Dh],
                                  gab, al_ref[...], db_ref[...], wn_ref[...], s_in, h)
            y_ref[:, h * Dh:(h + 1) * Dh] = y.astype(y_ref.dtype)
            S[h] = s_new

    vec = pl.BlockSpec((1, LANES), lambda n: (0, 0))
    return pl.pallas_call(
        body, name=name, grid=(N,),
        in_specs=[pl.BlockSpec((C, 3 * GDN_WIDTH), lambda n: (n, 0)),
                  pl.BlockSpec((C, GDN_WIDTH), lambda n: (n, P_GZ // GDN_WIDTH)),
                  pl.BlockSpec((C, LANES), lambda n: (n, P_GAB // LANES)), vec, vec, vec],
        out_specs=[pl.BlockSpec((C, GDN_WIDTH), lambda n: (n, 0)), pl.BlockSpec((1, H, Dh, Dh), lambda n: (n, 0, 0, 0))],
        out_shape=[jax.ShapeDtypeStruct((L, GDN_WIDTH), MXU_DTYPE), jax.ShapeDtypeStruct((N, H, Dh, Dh), F32)],
        scratch_shapes=[pltpu.VMEM((H, Dh, Dh), F32)],
        compiler_params=_params(("arbitrary",)),
    )(cqkv, proj, proj, alog, dtb, wn)


def _gdn_bwd(cqkv, proj, alog, dtb, wn, s_all, dy, name):
    L = cqkv.shape[0]
    C, H, Dh = GDN_CHUNK, GDN_HEADS, GDN_HEAD_DIM
    N = L // C

    def body(c_ref, gz_ref, gab_ref, al_ref, db_ref, wn_ref, sall_ref, dy_ref,
             dc_ref, dgz_ref, dgab_ref, dal_ref, ddb_ref, dwn_ref, dS):
        n = pl.program_id(0)

        @pl.when(n == 0)
        def _():
            dS[...] = jnp.zeros_like(dS)
            dal_ref[...] = jnp.zeros_like(dal_ref)
            ddb_ref[...] = jnp.zeros_like(ddb_ref)
            dwn_ref[...] = jnp.zeros_like(dwn_ref)

        gab = gab_ref[...]
        dgab = jnp.zeros_like(gab)
        dal, ddb, dwn = jnp.zeros((1, LANES), F32), jnp.zeros((1, LANES), F32), jnp.zeros((1, LANES), F32)
        for h in range(H):
            f = functools.partial(_gdn_chunk, h=h)
            _, vjp = jax.vjp(f, c_ref[:, h * Dh:(h + 1) * Dh], c_ref[:, (H + h) * Dh:(H + h + 1) * Dh],
                             c_ref[:, (2 * H + h) * Dh:(2 * H + h + 1) * Dh], gz_ref[:, h * Dh:(h + 1) * Dh],
                             gab, al_ref[...], db_ref[...], wn_ref[...], sall_ref[0, h])
            dq, dk, dv, dgz, dgab_h, dal_h, ddb_h, dwn_h, ds = vjp((dy_ref[:, h * Dh:(h + 1) * Dh], dS[h]))
            dc_ref[:, h * Dh:(h + 1) * Dh] = dq
            dc_ref[:, (H + h) * Dh:(H + h + 1) * Dh] = dk
            dc_ref[:, (2 * H + h) * Dh:(2 * H + h + 1) * Dh] = dv
            dgz_ref[:, h * Dh:(h + 1) * Dh] = dgz.astype(dgz_ref.dtype)
            dS[h] = ds
            dgab, dal, ddb, dwn = dgab + dgab_h, dal + dal_h, ddb + ddb_h, dwn + dwn_h
        dgab_ref[...] = dgab
        dal_ref[...] += dal
        ddb_ref[...] += ddb
        dwn_ref[...] += dwn

    vec = pl.BlockSpec((1, LANES), lambda n: (0, 0))
    rev = lambda n: N - 1 - n
    return pl.pallas_call(
        body, name=name, grid=(N,),
        in_specs=[pl.BlockSpec((C, 3 * GDN_WIDTH), lambda n: (rev(n), 0)),
                  pl.BlockSpec((C, GDN_WIDTH), lambda n: (rev(n), P_GZ // GDN_WIDTH)),
                  pl.BlockSpec((C, LANES), lambda n: (rev(n), P_GAB // LANES)), vec, vec, vec,
                  pl.BlockSpec((1, H, Dh, Dh), lambda n: (rev(n), 0, 0, 0)),
                  pl.BlockSpec((C, GDN_WIDTH), lambda n: (rev(n), Y_GDN // GDN_WIDTH))],
        out_specs=[pl.BlockSpec((C, 3 * GDN_WIDTH), lambda n: (rev(n), 0)),
                   pl.BlockSpec((C, GDN_WIDTH), lambda n: (rev(n), 0)),
                   pl.BlockSpec((C, LANES), lambda n: (rev(n), 0)), vec, vec, vec],
        out_shape=[jax.ShapeDtypeStruct((L, 3 * GDN_WIDTH), F32), jax.ShapeDtypeStruct((L, GDN_WIDTH), MXU_DTYPE),
                   jax.ShapeDtypeStruct((L, LANES), F32)] + [jax.ShapeDtypeStruct((1, LANES), F32)] * 3,
        scratch_shapes=[pltpu.VMEM((H, Dh, Dh), F32)],
        compiler_params=_params(("arbitrary",)),
    )(cqkv, proj, proj, alog, dtb, wn, s_all, dy)


def _split_dot(x, m):
    hi = x.astype(MXU_DTYPE)
    lo = (x - hi.astype(F32)).astype(MXU_DTYPE)
    return jnp.dot(hi, m, preferred_element_type=F32) + jnp.dot(lo, m, preferred_element_type=F32)


def _sb_logits(q, kj):
    z = lax.dot_general(q, kj, _DIMS["nt"], preferred_element_type=F32) * (SB_HEAD_DIM ** -0.5)
    t = jnp.exp(-jnp.abs(z))
    lb = jnp.minimum(z, 0.0) - jnp.log(1.0 + t)
    return z, t, lb


def _sb_fwd(q, k, v, name):
    H, L, Dh = q.shape
    B = SB_BLOCK

    def body(q_ref, k_ref, v_ref, o_ref, c_ref):
        i = pl.program_id(1)
        qb = q_ref[0]
        row = lax.broadcasted_iota(jnp.int32, (B, B), 0)
        col = lax.broadcasted_iota(jnp.int32, (B, B), 1)
        after = (row > col).astype(MXU_DTYPE)
        strict = col < row

        def pair(j, c, acc, masked):
            rows = pl.ds(pl.multiple_of(j * B, B), B)
            kj, vj = k_ref[0, rows, :], v_ref[0, rows, :]
            z, _, lb = _sb_logits(qb, kj)
            lom = lb - z
            if masked:
                lom = jnp.where(strict, lom, 0.0)
            tail = _split_dot(lom, after)
            a = jnp.exp(lb + tail + c)
            if masked:
                a = jnp.where(strict, a, 0.0)
            acc = acc + jnp.dot(a.astype(MXU_DTYPE), vj, preferred_element_type=F32)
            return c + jnp.sum(lom, axis=1, keepdims=True), acc

        c, acc = pair(i, jnp.zeros((B, 1), F32), jnp.zeros((B, Dh), F32), True)

        def step(it, carry):
            return pair(i - 1 - it, carry[0], carry[1], False)

        c, acc = lax.fori_loop(0, i, step, (c, acc))
        o_ref[0] = acc
        c_ref[0] = c

    return pl.pallas_call(
        body, name=name, grid=(H, L // B),
        in_specs=[pl.BlockSpec((1, B, Dh), lambda h, i: (h, i, 0)), pl.BlockSpec((1, L, Dh), lambda h, i: (h, 0, 0)),
                  pl.BlockSpec((1, L, Dh), lambda h, i: (h, 0, 0))],
        out_specs=[pl.BlockSpec((1, B, Dh), lambda h, i: (h, i, 0)), pl.BlockSpec((1, B, 1), lambda h, i: (h, i, 0))],
        out_shape=[jax.ShapeDtypeStruct((H, L, Dh), F32), jax.ShapeDtypeStruct((H, L, 1), F32)],
        compiler_params=_params(("parallel", "arbitrary")),
    )(q, k, v)


def _sb_bwd(q, k, v, do, ctot, name):
    H, L, Dh = q.shape
    B = SB_BLOCK
    scale = SB_HEAD_DIM ** -0.5

    def body(q_ref, k_ref, v_ref, do_ref, ct_ref, dq_ref, dk_ref, dv_ref):
        i = pl.program_id(1)

        @pl.when(i == 0)
        def _():
            dk_ref[...] = jnp.zeros_like(dk_ref)
            dv_ref[...] = jnp.zeros_like(dv_ref)

        qb, dob, ctot_b = q_ref[0], do_ref[0], ct_ref[0]
        row = lax.broadcasted_iota(jnp.int32, (B, B), 0)
        col = lax.broadcasted_iota(jnp.int32, (B, B), 1)
        after = (row > col).astype(MXU_DTYPE)
        before = (row < col).astype(MXU_DTYPE)
        strict = col < row

        def pair(j, p, e, dq, masked):
            rows = pl.ds(pl.multiple_of(j * B, B), B)
            kj, vj = k_ref[0, rows, :], v_ref[0, rows, :]
            z, t, lb = _sb_logits(qb, kj)
            r = 1.0 / (1.0 + t)
            sig = jnp.where(z >= 0, r, t * r)
            lom = lb - z
            if masked:
                lom = jnp.where(strict, lom, 0.0)
            tail = _split_dot(lom, after)
            p = p + jnp.sum(lom, axis=1, keepdims=True)
            a = jnp.exp(lb + tail + (ctot_b - p))
            if masked:
                a = jnp.where(strict, a, 0.0)
            da = lax.dot_general(dob, vj, _DIMS["nt"], preferred_element_type=F32)
            ea = da * a
            dlom = e + _split_dot(ea, before)
            if masked:
                dlom = jnp.where(strict, dlom, 0.0)
            e = e + jnp.sum(ea, axis=1, keepdims=True)
            dz = ((ea * (1.0 - sig) - dlom * sig) * scale).astype(MXU_DTYPE)
            dq = dq + jnp.dot(dz, kj, preferred_element_type=F32)
            dk_ref[0, rows, :] += lax.dot_general(dz, qb, _DIMS["tn"], preferred_element_type=F32)
            dv_ref[0, rows, :] += lax.dot_general(a.astype(MXU_DTYPE), dob, _DIMS["tn"], preferred_element_type=F32)
            return p, e, dq

        def step(j, carry):
            return pair(j, carry[0], carry[1], carry[2], False)

        zero = jnp.zeros((B, 1), F32)
        p, e, dq = lax.fori_loop(0, i, step, (zero, zero, jnp.zeros((B, Dh), F32)))
        _, _, dq = pair(i, p, e, dq, True)
        dq_ref[0] = dq

    blk = pl.BlockSpec((1, B, Dh), lambda h, i: (h, i, 0))
    full = pl.BlockSpec((1, L, Dh), lambda h, i: (h, 0, 0))
    out = jax.ShapeDtypeStruct((H, L, Dh), F32)
    return pl.pallas_call(
        body, name=name, grid=(H, L // B),
        in_specs=[blk, full, full, blk, pl.BlockSpec((1, B, 1), lambda h, i: (h, i, 0))],
        out_specs=[blk, full, full], out_shape=[out, out, out],
        compiler_params=_params(("parallel", "arbitrary")),
    )(q, k, v, do, ctot)


def _prep_w_in(w):
    sc, qkv, gz, gab, sb = w[:, 0:768], w[:, 768:2304], w[:, 2304:2816], w[:, 2816:2824], w[:, 2824:3592]
    pad = jnp.zeros((w.shape[0], P_END - D_IN_PROJ), w.dtype)
    return jnp.concatenate([qkv, gz, sc, sb, gab, pad], axis=1).astype(MXU_DTYPE)


def _unprep_dw_in(dw):
    qkv, gz, sc, sb, gab = dw[:, P_QKV:P_GZ], dw[:, P_GZ:P_SC], dw[:, P_SC:P_SB], dw[:, P_SB:P_GAB], dw[:, P_GAB:P_GAB + 8]
    return jnp.concatenate([sc, qkv, gz, gab, sb], axis=1)


def _prep_w_out(w):
    return jnp.concatenate([w[256:768], w[0:256], w[768:]], axis=0).astype(MXU_DTYPE)


def _unprep_dw_out(dw):
    return jnp.concatenate([dw[512:768], dw[0:512], dw[768:]], axis=0)


def _pad_lanes(v):
    return jnp.zeros((1, LANES), F32).at[0, :v.shape[0]].set(v)


def _to_heads(t):
    L = t.shape[0]
    return t.reshape(L, SB_HEADS, SB_HEAD_DIM).transpose(1, 0, 2)


def _from_heads(t):
    H, L, Dh = t.shape
    return t.transpose(1, 0, 2).reshape(L, H * Dh)


def _layer_fwd(x, p, l):
    L = x.shape[0]
    tm = min(1024, L)
    n = f"l{l}_"
    h = _rms_fwd(x, p["norm_mix"], n + "rms_mix")
    proj = _matmul(h, p["w_in"], "nn", F32, n + "mm_in", tm, 768, 1024)
    cb = SC_WIDTH
    y_sc = _conv_fwd((proj, P_SC // cb + 1), p["w_sconv"], SC_KERNEL, cb, 1, MXU_DTYPE, n + "sconv",
                     x2=(proj, P_SC // cb + 2), gate=(proj, P_SC // cb))
    cqkv = _conv_fwd((proj, 0), p["w_gconv"], GDN_CONV, 256, 6, F32, n + "gconv")
    y_gdn, s_all = _gdn_fwd(cqkv, proj, p["a_log"], p["dt_bias"], p["gdn_norm"], n + "gdn")
    q = _to_heads(proj[:, P_SB:P_SB + 256]).astype(MXU_DTYPE)
    k = _to_heads(proj[:, P_SB + 256:P_SB + 512]).astype(MXU_DTYPE)
    v = _to_heads(proj[:, P_SB + 512:P_SB + 768]).astype(MXU_DTYPE)
    o_sb, ctot = _sb_fwd(q, k, v, n + "sb")
    ycat = jnp.concatenate([y_gdn, y_sc, _from_heads(o_sb).astype(MXU_DTYPE)], axis=1)
    x1 = _matmul(ycat, p["w_out"], "nn", F32, n + "mm_out", tm, 512, 1024, resid=x)
    h2 = _rms_fwd(x1, p["norm_ffn"], n + "rms_ffn")
    up = _matmul(h2, p["w_up"], "nn", F32, n + "mm_up", tm, 512, 1024)
    u = _conv_fwd((up, 0), p["w_fconv"], FFN_CONV, 512, 11, F32, n + "fconv")
    act = _glu_fwd(u, n + "glu")
    x2 = _matmul(act, p["w_down"], "nn", F32, n + "mm_down", tm, 512, 1408, resid=x1)
    saved = dict(x=x, h=h, proj=proj, cqkv=cqkv, s_all=s_all, q=q, k=k, v=v, ctot=ctot, ycat=ycat, x1=x1, h2=h2,
                 up=up, u=u, act=act)
    return x2, saved


def _layer_bwd(dx2, p, s, l):
    L = dx2.shape[0]
    tm, tkl = min(1024, L), min(512, L)
    n = f"l{l}_"
    g = {}
    dx2b = dx2.astype(MXU_DTYPE)
    g["w_ffn_down"] = _matmul(s["act"], dx2b, "tn", F32, n + "mm_ddown", 1408, 512, tkl)
    dact = _matmul(dx2b, p["w_down"], "nt", F32, n + "mm_dact", tm, 1408, 1024)
    du = _glu_bwd(s["u"], dact, n + "dglu")
    dup, dwf = _conv_bwd((s["up"], 0), p["w_fconv"], (du, 0), FFN_CONV, 512, 11, MXU_DTYPE, n + "dfconv")
    g["w_ffn_conv"] = dwf[:FFN_CONV]
    g["w_ffn_up"] = _matmul(s["h2"], dup, "tn", F32, n + "mm_dup", 1024, 512, tkl)
    dh2 = _matmul(dup, p["w_up"], "nt", F32, n + "mm_dh2", tm, 512, 512)
    dx1, dwn = _rms_bwd(s["x1"], p["norm_ffn"], dh2, dx2, n + "drms_ffn")
    g["w_norm_ffn"] = dwn[0]

    dx1b = dx1.astype(MXU_DTYPE)
    g["w_mix_out"] = _unprep_dw_out(_matmul(s["ycat"], dx1b, "tn", F32, n + "mm_dout", 1024, 512, tkl))
    dycat = _matmul(dx1b, p["w_out"], "nt", F32, n + "mm_dycat", tm, 512, 1024)
    proj = s["proj"]
    cb = SC_WIDTH
    dsc_c, dsc_h, dsc_b, dws = _conv_bwd((proj, P_SC // cb + 1), p["w_sconv"], (dycat, Y_SC // cb), SC_KERNEL, cb, 1,
                                         MXU_DTYPE, n + "dsconv", x2=(proj, P_SC // cb + 2), gate=(proj, P_SC // cb))
    g["w_sconv"] = dws[:SC_KERNEL]
    dcqkv, dgz, dgab, dal, ddb, dgn = _gdn_bwd(s["cqkv"], proj, p["a_log"], p["dt_bias"], p["gdn_norm"], s["s_all"], dycat,
                                               n + "dgdn")
    g["gdn_a_log"], g["gdn_dt_bias"], g["w_gdn_norm"] = dal[0, :GDN_HEADS], ddb[0, :GDN_HEADS], dgn[0]
    dqkv, dwg = _conv_bwd((proj, 0), p["w_gconv"], (dcqkv, 0), GDN_CONV, 256, 6, MXU_DTYPE, n + "dgconv")
    g["w_gdn_conv"] = dwg[:GDN_CONV]
    do = _to_heads(dycat[:, Y_SB:]).astype(MXU_DTYPE)
    dq, dk, dv = _sb_bwd(s["q"], s["k"], s["v"], do, s["ctot"], n + "dsb")
    dproj = jnp.concatenate(
        [dqkv, dgz, dsc_b, dsc_c, dsc_h, _from_heads(dq).astype(MXU_DTYPE), _from_heads(dk).astype(MXU_DTYPE),
         _from_heads(dv).astype(MXU_DTYPE), dgab.astype(MXU_DTYPE), jnp.zeros((L, P_END - P_GAB - LANES), MXU_DTYPE)], axis=1)
    g["w_mix_in"] = _unprep_dw_in(_matmul(s["h"], dproj, "tn", F32, n + "mm_din", 1024, 768, tkl))
    dh = _matmul(dproj, p["w_in"], "nt", F32, n + "mm_dh", tm, 512, 768)
    dx, dwm = _rms_bwd(s["x"], p["norm_mix"], dh, dx1, n + "drms_mix")
    g["w_norm_mix"] = dwm[0]
    return dx, g


WEIGHTS = ["w_norm_mix", "w_mix_in", "w_sconv", "w_gdn_conv", "gdn_a_log", "gdn_dt_bias", "w_gdn_norm", "w_mix_out",
           "w_norm_ffn", "w_ffn_up", "w_ffn_conv", "w_ffn_down", "w_norm_final"]


def _local_step(x, w, target):
    layers = []
    for l in range(DEPTH):
        layers.append(dict(
            norm_mix=w["w_norm_mix"][l][None], w_in=_prep_w_in(w["w_mix_in"][l]), w_sconv=w["w_sconv"][l],
            w_gconv=w["w_gdn_conv"][l], a_log=_pad_lanes(w["gdn_a_log"][l]), dt_bias=_pad_lanes(w["gdn_dt_bias"][l]),
            gdn_norm=w["w_gdn_norm"][l][None], w_out=_prep_w_out(w["w_mix_out"][l]), norm_ffn=w["w_norm_ffn"][l][None],
            w_up=w["w_ffn_up"][l].astype(MXU_DTYPE), w_fconv=w["w_ffn_conv"][l], w_down=w["w_ffn_down"][l].astype(MXU_DTYPE)))
    saved = []
    for l in range(DEPTH):
        x, s = _layer_fwd(x, layers[l], l)
        saved.append(s)
    loss, dx, dwf = _loss_head(x, w["w_norm_final"][None], target, "loss_head")
    grads = [None] * DEPTH
    for l in reversed(range(DEPTH)):
        dx, grads[l] = _layer_bwd(dx, layers[l], saved[l], l)
    out = {k: jnp.stack([grads[l][k] for l in range(DEPTH)]) for k in WEIGHTS if k != "w_norm_final"}
    out["w_norm_final"] = dwf[0]
    return loss, dx, out


def _exchange(bufs, scatter, name):
    nb = len(bufs)

    def body(*refs):
        ins, outs = refs[:nb], refs[nb:2 * nb]
        send_sems, recv_sems, local_sems = refs[2 * nb:]
        x, y, c = lax.axis_index("x"), lax.axis_index("y"), lax.axis_index("c")
        me = 4 * x + 2 * y + c
        local = []
        for b in range(nb):
            cp = pltpu.make_async_copy(ins[b].at[me] if scatter[b] else ins[b], outs[b].at[me], local_sems.at[b])
            cp.start()
            local.append(cp)
        remote = []
        for b in range(nb):
            for kk in range(1, N_DEV):
                px = 1 - x if kk & 4 else x
                py = 1 - y if kk & 2 else y
                pc = 1 - c if kk & 1 else c
                src = ins[b].at[4 * px + 2 * py + pc] if scatter[b] else ins[b]
                cp = pltpu.make_async_remote_copy(
                    src_ref=src, dst_ref=outs[b].at[me], send_sem=send_sems.at[b, kk - 1], recv_sem=recv_sems.at[b, kk - 1],
                    device_id=(px, py, pc), device_id_type=pl.DeviceIdType.MESH)
                cp.start()
                remote.append(cp)
        for cp in remote:
            cp.wait()
        for cp in local:
            cp.wait()

    out_shape = [jax.ShapeDtypeStruct((N_DEV,) + (b.shape[1:] if s else b.shape), b.dtype) for b, s in zip(bufs, scatter)]
    hbm = pl.BlockSpec(memory_space=pl.ANY)
    return pl.pallas_call(
        body, name=name, in_specs=[hbm] * nb, out_specs=[hbm] * nb, out_shape=out_shape,
        scratch_shapes=[pltpu.SemaphoreType.DMA((nb, N_DEV - 1)), pltpu.SemaphoreType.DMA((nb, N_DEV - 1)),
                        pltpu.SemaphoreType.DMA((nb,))],
    )(*bufs)


def _sum_sources(recv, row_tile, name):
    _, R, _ = recv.shape

    def body(r_ref, o_ref):
        acc = r_ref[0].astype(F32)
        for s in range(1, N_DEV):
            acc = acc + r_ref[s].astype(F32)
        o_ref[...] = acc

    return pl.pallas_call(
        body, name=name, grid=(R // row_tile,),
        in_specs=[pl.BlockSpec((N_DEV, row_tile, LANES), lambda i: (0, i, 0))],
        out_specs=pl.BlockSpec((row_tile, LANES), lambda i: (i, 0)),
        out_shape=jax.ShapeDtypeStruct((R, LANES), F32),
        compiler_params=_params(("parallel",)),
    )(recv)


def _adamw(g, w, m, v, row_tile, name):
    R = g.shape[0]

    def body(g_ref, w_ref, m_ref, v_ref, d_ref, nm_ref, nv_ref):
        gg = g_ref[...]
        nm = ADAM_B1 * m_ref[...] + (1.0 - ADAM_B1) * gg
        nv = ADAM_B2 * v_ref[...] + (1.0 - ADAM_B2) * (gg * gg)
        m_hat = nm / (1.0 - ADAM_B1 ** ADAM_STEP)
        v_hat = nv / (1.0 - ADAM_B2 ** ADAM_STEP)
        d_ref[...] = -ADAM_LR * (m_hat / (jnp.sqrt(v_hat) + ADAM_EPS) + ADAM_WD * w_ref[...])
        nm_ref[...] = nm
        nv_ref[...] = nv

    blk = pl.BlockSpec((row_tile, LANES), lambda i: (i, 0))
    out = jax.ShapeDtypeStruct((R, LANES), F32)
    return pl.pallas_call(
        body, name=name, grid=(R // row_tile,), in_specs=[blk] * 4, out_specs=[blk] * 3, out_shape=[out] * 3,
        compiler_params=_params(("parallel",)),
    )(g, w, m, v)


def _pack(arrs, rows, dtype):
    flat = jnp.concatenate([a.reshape(-1).astype(dtype) for a in arrs])
    return jnp.pad(flat, (0, rows * LANES - flat.shape[0])).reshape(rows, LANES)


def _unpack(buf, shapes):
    lead = buf.shape[:-2]
    flat = buf.reshape(lead + (-1,))
    out, off = [], 0
    for shp in shapes:
        n = 1
        for d in shp:
            n *= d
        out.append(flat[..., off:off + n].reshape(lead + tuple(shp)))
        off += n
    return out


BIG = ["w_mix_in", "w_mix_out", "w_ffn_up", "w_ffn_down"]
BIG_AXIS = {"w_mix_in": 2, "w_mix_out": 1, "w_ffn_up": 2, "w_ffn_down": 1}
CONV = ["w_sconv", "w_gdn_conv", "w_ffn_conv"]
REPL = ["w_norm_mix", "gdn_a_log", "gdn_dt_bias", "w_gdn_norm", "w_norm_ffn", "w_norm_final"]
BIG_ROW_TILE = 1136
SMALL_ROWS = 416
CONV_ROWS = 48


def _gather_axis(g, axis):
    g = jnp.moveaxis(g, 0, axis)
    shp = list(g.shape)
    return g.reshape(shp[:axis] + [shp[axis] * shp[axis + 1]] + shp[axis + 2:])


def _scatter_axis(full, axis):
    shp = list(full.shape)
    g = full.reshape(shp[:axis] + [N_DEV, shp[axis] // N_DEV] + shp[axis + 1:])
    return jnp.moveaxis(g, axis, 0)


def kernel(x, w_norm_mix, w_mix_in, w_sconv, w_gdn_conv, gdn_a_log, gdn_dt_bias, w_gdn_norm, w_mix_out, w_norm_ffn, w_ffn_up, w_ffn_conv, w_ffn_down, w_norm_final, loss_target, m_w_norm_mix, m_w_mix_in, m_w_sconv, m_w_gdn_conv, m_gdn_a_log, m_gdn_dt_bias, m_w_gdn_norm, m_w_mix_out, m_w_norm_ffn, m_w_ffn_up, m_w_ffn_conv, m_w_ffn_down, m_w_norm_final, v_w_norm_mix, v_w_mix_in, v_w_sconv, v_w_gdn_conv, v_gdn_a_log, v_gdn_dt_bias, v_w_gdn_norm, v_w_mix_out, v_w_norm_ffn, v_w_ffn_up, v_w_ffn_conv, v_w_ffn_down, v_w_norm_final):
    w = dict(w_norm_mix=w_norm_mix, w_mix_in=w_mix_in, w_sconv=w_sconv, w_gdn_conv=w_gdn_conv, gdn_a_log=gdn_a_log,
             gdn_dt_bias=gdn_dt_bias, w_gdn_norm=w_gdn_norm, w_mix_out=w_mix_out, w_norm_ffn=w_norm_ffn, w_ffn_up=w_ffn_up,
             w_ffn_conv=w_ffn_conv, w_ffn_down=w_ffn_down, w_norm_final=w_norm_final)
    m = dict(w_norm_mix=m_w_norm_mix, w_mix_in=m_w_mix_in, w_sconv=m_w_sconv, w_gdn_conv=m_w_gdn_conv, gdn_a_log=m_gdn_a_log,
             gdn_dt_bias=m_gdn_dt_bias, w_gdn_norm=m_w_gdn_norm, w_mix_out=m_w_mix_out, w_norm_ffn=m_w_norm_ffn,
             w_ffn_up=m_w_ffn_up, w_ffn_conv=m_w_ffn_conv, w_ffn_down=m_w_ffn_down, w_norm_final=m_w_norm_final)
    v = dict(w_norm_mix=v_w_norm_mix, w_mix_in=v_w_mix_in, w_sconv=v_w_sconv, w_gdn_conv=v_w_gdn_conv, gdn_a_log=v_gdn_a_log,
             gdn_dt_bias=v_gdn_dt_bias, w_gdn_norm=v_w_gdn_norm, w_mix_out=v_w_mix_out, w_norm_ffn=v_w_norm_ffn,
             w_ffn_up=v_w_ffn_up, w_ffn_conv=v_w_ffn_conv, w_ffn_down=v_w_ffn_down, w_norm_final=v_w_norm_final)
    me = 4 * lax.axis_index("x") + 2 * lax.axis_index("y") + lax.axis_index("c")
    big_shapes = [w[k].shape for k in BIG]
    conv_shapes = [w[k].shape for k in CONV]
    big_rows = sum(w[k].size for k in BIG) // LANES

    wbig, wconv = _exchange([_pack([w[k] for k in BIG], big_rows, MXU_DTYPE), _pack([w[k] for k in CONV], CONV_ROWS, F32)],
                            [False, False], "gather_weights")
    full = dict(w)
    for k, g in zip(BIG, _unpack(wbig, big_shapes)):
        full[k] = _gather_axis(g, BIG_AXIS[k])
    for k, g in zip(CONV, _unpack(wconv, conv_shapes)):
        full[k] = _gather_axis(g, 2)

    loss, dx, grads = _local_step(x[0], full, loss_target[0])

    gbig = jnp.concatenate([_scatter_axis(grads[k], BIG_AXIS[k]).reshape(N_DEV, -1) for k in BIG], axis=1)
    gbig = gbig.astype(MXU_DTYPE).reshape(N_DEV, big_rows, LANES)
    small = CONV + REPL
    gsmall = _pack([grads[k] for k in small], SMALL_ROWS, F32)
    rbig, rsmall = _exchange([gbig, gsmall], [True, False], "exchange_grads")
    g_big = _unpack(_sum_sources(rbig, BIG_ROW_TILE, "sum_big"), big_shapes)
    g_small = _unpack(_sum_sources(rsmall, SMALL_ROWS, "sum_small"), [grads[k].shape for k in small])
    g = dict(zip(BIG, g_big))
    for k, gs in zip(small, g_small):
        g[k] = lax.dynamic_slice_in_dim(gs, me * w[k].shape[2], w[k].shape[2], axis=2) if k in CONV else gs

    d_big, m_big, v_big = _adamw(_pack(g_big, big_rows, F32), _pack([w[k] for k in BIG], big_rows, F32),
                                 _pack([m[k] for k in BIG], big_rows, F32), _pack([v[k] for k in BIG], big_rows, F32),
                                 BIG_ROW_TILE, "adamw_big")
    small_shapes = [w[k].shape for k in small]
    small_rows = -(-sum(w[k].size for k in small) // (SUBLANES * LANES)) * SUBLANES
    d_sm, m_sm, v_sm = _adamw(_pack([g[k] for k in small], small_rows, F32), _pack([w[k] for k in small], small_rows, F32),
                              _pack([m[k] for k in small], small_rows, F32), _pack([v[k] for k in small], small_rows, F32),
                              small_rows, "adamw_small")
    delta, new_m, new_v = {}, {}, {}
    for dst, big_buf, small_buf in ((delta, d_big, d_sm), (new_m, m_big, m_sm), (new_v, v_big, v_sm)):
        dst.update(zip(BIG, _unpack(big_buf, big_shapes)))
        dst.update(zip(small, _unpack(small_buf, small_shapes)))

    loss_all = lax.psum(loss[0, 0], ("x", "y", "c"))
    return (loss_all, dx[None], *[g[k] for k in WEIGHTS], *[delta[k] for k in WEIGHTS], *[new_m[k] for k in WEIGHTS],
            *[new_v[k] for k in WEIGHTS])
```

```python
import functools

import jax
import jax.numpy as jnp
from jax import lax
from jax.experimental import pallas as pl
from jax.experimental.pallas import tpu as pltpu

F32 = jnp.float32
MXU_DTYPE = jnp.bfloat16
HIGHEST = lax.Precision.HIGHEST

D_MODEL = 1024
DEPTH = 2
SC_WIDTH = 256
SC_KERNEL = 3
GDN_WIDTH = 512
GDN_HEADS = 4
GDN_HEAD_DIM = 128
GDN_CONV = 4
GDN_CHUNK = 64
SB_WIDTH = 256
SB_HEADS = 4
SB_HEAD_DIM = 64
SB_BLOCK = 128
D_FF = 2816
FFN_CONV = 3
NORM_EPS = 1e-6
D_IN_PROJ = 3592
ADAM_LR, ADAM_B1, ADAM_B2, ADAM_EPS, ADAM_WD, ADAM_STEP = 0.001, 0.9, 0.999, 1e-08, 0.01, 10

N_DEV = 8
LANES = 128
SUBLANES = 8
VMEM_LIMIT = 48 * 1024 * 1024

P_QKV, P_GZ, P_SC, P_SB, P_GAB, P_END = 0, 1536, 2048, 2816, 3584, 3840
Y_GDN, Y_SC, Y_SB = 0, 512, 768


def _params(semantics):
    return pltpu.CompilerParams(dimension_semantics=semantics, vmem_limit_bytes=VMEM_LIMIT)


_DIMS = {"nn": (((1,), (0,)), ((), ())), "nt": (((1,), (1,)), ((), ())), "tn": (((0,), (0,)), ((), ()))}


def _matmul(a, b, mode, out_dtype, name, tm, tn, tk, resid=None):
    if mode == "tn":
        (K, M), (K2, N) = a.shape, b.shape
    elif mode == "nt":
        (M, K), (N, K2) = a.shape, b.shape
    else:
        (M, K), (K2, N) = a.shape, b.shape
    assert K == K2 and M % tm == 0 and N % tn == 0 and K % tk == 0, (name, a.shape, b.shape, tm, tn, tk)
    nk = K // tk
    has_resid = resid is not None

    def body(*refs):
        if has_resid:
            a_ref, b_ref, r_ref, o_ref, acc = refs
        else:
            a_ref, b_ref, o_ref, acc = refs
        k = pl.program_id(2)

        @pl.when(k == 0)
        def _():
            acc[...] = jnp.zeros_like(acc)

        acc[...] += lax.dot_general(a_ref[...], b_ref[...], _DIMS[mode], preferred_element_type=F32)

        @pl.when(k == nk - 1)
        def _():
            r = acc[...]
            if has_resid:
                r = r + r_ref[...]
            o_ref[...] = r.astype(out_dtype)

    a_spec = pl.BlockSpec((tk, tm), lambda i, j, k: (k, i)) if mode == "tn" else pl.BlockSpec((tm, tk), lambda i, j, k: (i, k))
    b_spec = pl.BlockSpec((tn, tk), lambda i, j, k: (j, k)) if mode == "nt" else pl.BlockSpec((tk, tn), lambda i, j, k: (k, j))
    o_spec = pl.BlockSpec((tm, tn), lambda i, j, k: (i, j))
    in_specs = [a_spec, b_spec] + ([o_spec] if has_resid else [])
    args = (a, b) + ((resid,) if has_resid else ())
    return pl.pallas_call(
        body, name=name, grid=(M // tm, N // tn, nk), in_specs=in_specs, out_specs=o_spec,
        out_shape=jax.ShapeDtypeStruct((M, N), out_dtype),
        scratch_shapes=[pltpu.VMEM((tm, tn), F32)],
        compiler_params=_params(("parallel", "parallel", "arbitrary")),
    )(*args)


def _rms(x, w):
    return x * lax.rsqrt(jnp.mean(x * x, axis=-1, keepdims=True) + NORM_EPS) * w


ROW_TILE = 512


def _rms_fwd(x, w, name):
    L, Dm = x.shape

    def body(x_ref, w_ref, h_ref):
        h_ref[...] = _rms(x_ref[...], w_ref[...]).astype(h_ref.dtype)

    return pl.pallas_call(
        body, name=name, grid=(L // ROW_TILE,),
        in_specs=[pl.BlockSpec((ROW_TILE, Dm), lambda i: (i, 0)), pl.BlockSpec((1, Dm), lambda i: (0, 0))],
        out_specs=pl.BlockSpec((ROW_TILE, Dm), lambda i: (i, 0)),
        out_shape=jax.ShapeDtypeStruct((L, Dm), MXU_DTYPE),
        compiler_params=_params(("parallel",)),
    )(x, w)


def _rms_bwd(x, w, dh, dres, name):
    L, Dm = x.shape

    def body(x_ref, w_ref, dh_ref, dres_ref, dx_ref, dw_ref):
        _, vjp = jax.vjp(_rms, x_ref[...], w_ref[...])
        dx, dw = vjp(dh_ref[...])
        dx_ref[...] = dres_ref[...] + dx

        @pl.when(pl.program_id(0) == 0)
        def _():
            dw_ref[...] = jnp.zeros_like(dw_ref)

        dw_ref[...] += dw

    row = pl.BlockSpec((ROW_TILE, Dm), lambda i: (i, 0))
    vec = pl.BlockSpec((1, Dm), lambda i: (0, 0))
    return pl.pallas_call(
        body, name=name, grid=(L // ROW_TILE,), in_specs=[row, vec, row, row], out_specs=[row, vec],
        out_shape=[jax.ShapeDtypeStruct((L, Dm), F32), jax.ShapeDtypeStruct((1, Dm), F32)],
        compiler_params=_params(("arbitrary",)),
    )(x, w, dh, dres)


def _loss_head(x, w, target, name):
    L, Dm = x.shape

    def block_loss(xb, wb, tb):
        err = _rms(xb, wb) - tb
        return 0.5 * jnp.sum(jnp.sum(err * err, axis=-1, keepdims=True) * (1.0 / Dm), axis=0, keepdims=True)

    def body(x_ref, w_ref, t_ref, loss_ref, dx_ref, dw_ref):
        val, vjp = jax.vjp(lambda xb, wb: block_loss(xb, wb, t_ref[...]), x_ref[...], w_ref[...])
        dx, dw = vjp(jnp.ones_like(val))
        dx_ref[...] = dx

        @pl.when(pl.program_id(0) == 0)
        def _():
            dw_ref[...] = jnp.zeros_like(dw_ref)
            loss_ref[...] = jnp.zeros_like(loss_ref)

        dw_ref[...] += dw
        loss_ref[...] += val

    row = pl.BlockSpec((ROW_TILE, Dm), lambda i: (i, 0))
    vec = pl.BlockSpec((1, Dm), lambda i: (0, 0))
    one = pl.BlockSpec((1, 1), lambda i: (0, 0))
    return pl.pallas_call(
        body, name=name, grid=(L // ROW_TILE,), in_specs=[row, vec, row], out_specs=[one, row, vec],
        out_shape=[jax.ShapeDtypeStruct((1, 1), F32), jax.ShapeDtypeStruct((L, Dm), F32), jax.ShapeDtypeStruct((1, Dm), F32)],
        compiler_params=_params(("arbitrary",)),
    )(x, w, target)


HALO = SUBLANES


def _conv_specs(L, T, Cb, off):
    main = pl.BlockSpec((T, Cb), lambda j, i: (i, off + j))
    prev = pl.BlockSpec((HALO, Cb), lambda j, i: (jnp.maximum(i * (T // HALO) - 1, 0), off + j))
    nxt = pl.BlockSpec((HALO, Cb), lambda j, i: (jnp.minimum((i + 1) * (T // HALO), L // HALO - 1), off + j))
    return main, prev, nxt


def _conv_fwd(x1, w, K, Cb, ncol, out_dtype, name, x2=None, gate=None):
    (x1a, o1) = x1
    L = x1a.shape[0]
    T = min(ROW_TILE, L)
    has_mul, has_gate = x2 is not None, gate is not None

    def body(*refs):
        it = iter(refs)
        x1m, x1p = next(it), next(it)
        if has_mul:
            x2m, x2p = next(it), next(it)
        if has_gate:
            gm = next(it)
        w_ref, y_ref, scr = next(it), next(it), next(it)
        i = pl.program_id(1)
        p, pp = x1m[...].astype(F32), x1p[...].astype(F32)
        if has_mul:
            p, pp = p * x2m[...], pp * x2p[...]
        scr[0:HALO, :] = jnp.where(i > 0, pp, 0.0)
        scr[HALO:HALO + T, :] = p
        acc = w_ref[K - 1:K, :] * p
        for k in range(K - 1):
            s = K - 1 - k
            acc = acc + w_ref[k:k + 1, :] * scr[HALO - s:HALO - s + T, :]
        if has_gate:
            acc = acc * gm[...]
        y_ref[...] = acc.astype(out_dtype)

    in_specs, args = [], []
    m, p_, _ = _conv_specs(L, T, Cb, o1)
    in_specs += [m, p_]
    args += [x1a, x1a]
    if has_mul:
        m, p_, _ = _conv_specs(L, T, Cb, x2[1])
        in_specs += [m, p_]
        args += [x2[0], x2[0]]
    if has_gate:
        m, _, _ = _conv_specs(L, T, Cb, gate[1])
        in_specs += [m]
        args += [gate[0]]
    in_specs.append(pl.BlockSpec((K, Cb), lambda j, i: (0, j)))
    args.append(w)
    return pl.pallas_call(
        body, name=name, grid=(ncol, L // T), in_specs=in_specs,
        out_specs=pl.BlockSpec((T, Cb), lambda j, i: (i, j)),
        out_shape=jax.ShapeDtypeStruct((L, ncol * Cb), out_dtype),
        scratch_shapes=[pltpu.VMEM((T + HALO, Cb), F32)],
        compiler_params=_params(("parallel", "arbitrary")),
    )(*args)


def _conv_bwd(x1, w, dy, K, Cb, ncol, out_dtype, name, x2=None, gate=None):
    (x1a, o1) = x1
    L = x1a.shape[0]
    T = min(ROW_TILE, L)
    nrow = L // T
    has_mul, has_gate = x2 is not None, gate is not None

    def body(*refs):
        it = iter(refs)
        x1m, x1p = next(it), next(it)
        if has_mul:
            x2m, x2p = next(it), next(it)
        if has_gate:
            gm, gn = next(it), next(it)
        dym, dyn, w_ref = next(it), next(it), next(it)
        dx1_ref = next(it)
        if has_mul:
            dx2_ref = next(it)
        if has_gate:
            dg_ref = next(it)
        dw_ref, scr_p, scr_d = next(it), next(it), next(it)
        i = pl.program_id(1)
        p, pp = x1m[...].astype(F32), x1p[...].astype(F32)
        if has_mul:
            p, pp = p * x2m[...], pp * x2p[...]
        scr_p[0:HALO, :] = jnp.where(i > 0, pp, 0.0)
        scr_p[HALO:HALO + T, :] = p
        dcv, dcn = dym[...].astype(F32), dyn[...].astype(F32)
        if has_gate:
            dcv, dcn = dcv * gm[...], dcn * gn[...]
        scr_d[0:T, :] = dcv
        scr_d[T:T + HALO, :] = jnp.where(i < nrow - 1, dcn, 0.0)

        @pl.when(i == 0)
        def _():
            dw_ref[...] = jnp.zeros_like(dw_ref)

        dp = w_ref[K - 1:K, :] * dcv
        cv = w_ref[K - 1:K, :] * p
        dw_ref[K - 1:K, :] += jnp.sum(dcv * p, axis=0, keepdims=True)
        for k in range(K - 1):
            s = K - 1 - k
            dp = dp + w_ref[k:k + 1, :] * scr_d[s:s + T, :]
            sh = scr_p[HALO - s:HALO - s + T, :]
            dw_ref[k:k + 1, :] += jnp.sum(dcv * sh, axis=0, keepdims=True)
            if has_gate:
                cv = cv + w_ref[k:k + 1, :] * sh
        if has_gate:
            dg_ref[...] = (dym[...].astype(F32) * cv).astype(out_dtype)
        if has_mul:
            dx1_ref[...] = (dp * x2m[...]).astype(out_dtype)
            dx2_ref[...] = (dp * x1m[...]).astype(out_dtype)
        else:
            dx1_ref[...] = dp.astype(out_dtype)

    in_specs, args = [], []
    m, p_, _ = _conv_specs(L, T, Cb, o1)
    in_specs += [m, p_]
    args += [x1a, x1a]
    if has_mul:
        m, p_, _ = _conv_specs(L, T, Cb, x2[1])
        in_specs += [m, p_]
        args += [x2[0], x2[0]]
    if has_gate:
        m, _, n_ = _conv_specs(L, T, Cb, gate[1])
        in_specs += [m, n_]
        args += [gate[0], gate[0]]
    m, _, n_ = _conv_specs(L, T, Cb, dy[1])
    in_specs += [m, n_, pl.BlockSpec((K, Cb), lambda j, i: (0, j))]
    args += [dy[0], dy[0], w]
    out = pl.BlockSpec((T, Cb), lambda j, i: (i, j))
    full = jax.ShapeDtypeStruct((L, ncol * Cb), out_dtype)
    n_out = 1 + int(has_mul) + int(has_gate)
    return pl.pallas_call(
        body, name=name, grid=(ncol, nrow), in_specs=in_specs,
        out_specs=[out] * n_out + [pl.BlockSpec((SUBLANES, Cb), lambda j, i: (0, j))],
        out_shape=[full] * n_out + [jax.ShapeDtypeStruct((SUBLANES, ncol * Cb), F32)],
        scratch_shapes=[pltpu.VMEM((T + HALO, Cb), F32), pltpu.VMEM((T + HALO, Cb), F32)],
        compiler_params=_params(("parallel", "arbitrary")),
    )(*args)


GLU_COLS = 256


def _silu(x):
    return x * (1.0 / (1.0 + jnp.exp(-x)))


def _glu(g, v):
    return _silu(g) * v


def _causal_taps(w_ref, scr, first, rows, K):
    acc = w_ref[K - 1:K, :] * scr[first:first + rows, :]
    for k in range(K - 1):
        s = K - 1 - k
        acc = acc + w_ref[k:k + 1, :] * scr[first - s:first - s + rows, :]
    return acc


def _ffn_act_fwd(up, w, name):
    L = up.shape[0]
    T, Cb, K = min(ROW_TILE, L), GLU_COLS, FFN_CONV
    nb = D_FF // Cb

    def body(gm, gp, vm, vp, wg, wv, a_ref, sg, sv):
        i = pl.program_id(1)
        for main, prev, scr in ((gm, gp, sg), (vm, vp, sv)):
            scr[0:HALO, :] = jnp.where(i > 0, prev[...], 0.0)
            scr[HALO:HALO + T, :] = main[...]
        a_ref[...] = _glu(_causal_taps(wg, sg, HALO, T, K), _causal_taps(wv, sv, HALO, T, K)).astype(a_ref.dtype)

    gmain, gprev, _ = _conv_specs(L, T, Cb, 0)
    vmain, vprev, _ = _conv_specs(L, T, Cb, nb)
    return pl.pallas_call(
        body, name=name, grid=(nb, L // T),
        in_specs=[gmain, gprev, vmain, vprev, pl.BlockSpec((K, Cb), lambda j, i: (0, j)), pl.BlockSpec((K, Cb), lambda j, i: (0, nb + j))],
        out_specs=pl.BlockSpec((T, Cb), lambda j, i: (i, j)),
        out_shape=jax.ShapeDtypeStruct((L, D_FF), MXU_DTYPE),
        scratch_shapes=[pltpu.VMEM((T + HALO, Cb), F32), pltpu.VMEM((T + HALO, Cb), F32)],
        compiler_params=_params(("parallel", "arbitrary")),
    )(up, up, up, up, w, w)


def _ffn_act_bwd(up, w, dact, name):
    L = up.shape[0]
    T, Cb, K = min(ROW_TILE, L), GLU_COLS, FFN_CONV
    nb, nrow = D_FF // Cb, L // T

    def body(gm, gp, gn, vm, vp, vn, dam, dan, wg, wv, dg_ref, dv_ref, dwg_ref, dwv_ref, sg, sv, sdg, sdv):
        i = pl.program_id(1)
        for main, prev, nxt, scr in ((gm, gp, gn, sg), (vm, vp, vn, sv)):
            scr[0:HALO, :] = jnp.where(i > 0, prev[...], 0.0)
            scr[HALO:HALO + T, :] = main[...]
            scr[HALO + T:2 * HALO + T, :] = nxt[...]
        ug, uv = _causal_taps(wg, sg, HALO, T + HALO, K), _causal_taps(wv, sv, HALO, T + HALO, K)
        da = jnp.concatenate([dam[...], jnp.where(i < nrow - 1, dan[...], 0.0)], axis=0)
        _, vjp = jax.vjp(_glu, ug, uv)
        sdg[...], sdv[...] = vjp(da)

        @pl.when(i == 0)
        def _():
            dwg_ref[...] = jnp.zeros_like(dwg_ref)
            dwv_ref[...] = jnp.zeros_like(dwv_ref)

        for w_ref, scr, sd, d_ref, dw_ref in ((wg, sg, sdg, dg_ref, dwg_ref), (wv, sv, sdv, dv_ref, dwv_ref)):
            du = sd[0:T, :]
            dp = w_ref[K - 1:K, :] * du
            dw_ref[K - 1:K, :] += jnp.sum(du * scr[HALO:HALO + T, :], axis=0, keepdims=True)
            for k in range(K - 1):
                s = K - 1 - k
                dp = dp + w_ref[k:k + 1, :] * sd[s:s + T, :]
                dw_ref[k:k + 1, :] += jnp.sum(du * scr[HALO - s:HALO - s + T, :], axis=0, keepdims=True)
            d_ref[...] = dp.astype(d_ref.dtype)

    gmain, gprev, gnext = _conv_specs(L, T, Cb, 0)
    vmain, vprev, vnext = _conv_specs(L, T, Cb, nb)
    dmain, _, dnext = _conv_specs(L, T, Cb, 0)
    out = pl.BlockSpec((T, Cb), lambda j, i: (i, j))
    dwb = pl.BlockSpec((SUBLANES, Cb), lambda j, i: (0, j))
    half = jax.ShapeDtypeStruct((L, D_FF), MXU_DTYPE)
    dwh = jax.ShapeDtypeStruct((SUBLANES, D_FF), F32)
    dg, dv, dwg, dwv = pl.pallas_call(
        body, name=name, grid=(nb, nrow),
        in_specs=[gmain, gprev, gnext, vmain, vprev, vnext, dmain, dnext,
                  pl.BlockSpec((K, Cb), lambda j, i: (0, j)), pl.BlockSpec((K, Cb), lambda j, i: (0, nb + j))],
        out_specs=[out, out, dwb, dwb], out_shape=[half, half, dwh, dwh],
        scratch_shapes=[pltpu.VMEM((T + 2 * HALO, Cb), F32), pltpu.VMEM((T + 2 * HALO, Cb), F32),
                        pltpu.VMEM((T + HALO, Cb), F32), pltpu.VMEM((T + HALO, Cb), F32)],
        compiler_params=_params(("parallel", "arbitrary")),
    )(up, up, up, up, up, up, dact, dact, w, w)
    return jnp.concatenate([dg, dv], axis=1), jnp.concatenate([dwg, dwv], axis=1)


def _bdot_raw(a, b, mode):
    return lax.dot_general(a.astype(MXU_DTYPE), b.astype(MXU_DTYPE), _DIMS[mode], preferred_element_type=F32)


@functools.partial(jax.custom_vjp, nondiff_argnums=(2,))
def _bdot(a, b, mode):
    return _bdot_raw(a, b, mode)


def _bdot_fwd(a, b, mode):
    return _bdot_raw(a, b, mode), (a, b)


def _bdot_bwd(mode, res, ct):
    a, b = res
    if mode == "nn":
        return _bdot_raw(ct, b, "nt"), _bdot_raw(a, ct, "tn")
    if mode == "nt":
        return _bdot_raw(ct, b, "nn"), _bdot_raw(ct, a, "tn")
    return _bdot_raw(b, ct, "nt"), _bdot_raw(a, ct, "nn")


_bdot.defvjp(_bdot_fwd, _bdot_bwd)


def _hdot(a, b, mode="nn"):
    return lax.dot_general(a, b, _DIMS[mode], precision=lax.Precision.HIGH, preferred_element_type=F32)


@jax.custom_vjp
def _inv_unit_lower(a):
    C = a.shape[0]
    eye = (lax.broadcasted_iota(jnp.int32, (C, C), 0) == lax.broadcasted_iota(jnp.int32, (C, C), 1)).astype(F32)
    t = eye - a
    p = a
    n = 1
    while 2 * n < C:
        p = _hdot(p, p)
        t = t + _hdot(t, p)
        n *= 2
    return t


def _inv_fwd(a):
    t = _inv_unit_lower(a)
    return t, t


def _inv_bwd(t, ct):
    return (-_hdot(_hdot(t, ct, "tn"), t, "nt"),)


_inv_unit_lower.defvjp(_inv_fwd, _inv_bwd)


def _softplus(x):
    return jnp.maximum(x, 0.0) + jnp.log(1.0 + jnp.exp(-jnp.abs(x)))


def _sigmoid(x):
    return 1.0 / (1.0 + jnp.exp(-x))


def _pick_lane(blk, lane):
    ids = lax.broadcasted_iota(jnp.int32, blk.shape, 1)
    return jnp.sum(jnp.where(ids == lane, blk, 0.0), axis=1, keepdims=True)


def _gdn_chunk(cq, ck, cv, gz, gab, alog, dtb, wn, S, h):
    C, Dk = cq.shape
    ga, gb = _pick_lane(gab, h), _pick_lane(gab, GDN_HEADS + h)
    al, db = _pick_lane(alog, h), _pick_lane(dtb, h)
    q, k, v = _silu(cq), _silu(ck), _silu(cv)
    q = q * lax.rsqrt(jnp.sum(q * q, axis=-1, keepdims=True) + NORM_EPS) * (Dk ** -0.5)
    k = k * lax.rsqrt(jnp.sum(k * k, axis=-1, keepdims=True) + NORM_EPS)
    beta = _sigmoid(gb)
    g = -jnp.exp(al) * _softplus(ga + db)
    row = lax.broadcasted_iota(jnp.int32, (C, C), 0)
    col = lax.broadcasted_iota(jnp.int32, (C, C), 1)
    causal, strict = row >= col, row > col
    gcb = _hdot(causal.astype(F32), jnp.broadcast_to(g, (C, Dk)))
    first = (lax.broadcasted_iota(jnp.int32, (C, Dk), 1) == 0).astype(F32)
    gr = _hdot(first, gcb, "nt")
    gc = _pick_lane(gcb, 0)
    decay = jnp.where(causal, jnp.exp(jnp.where(causal, gc - gr, 0.0)), 0.0)
    kb = k * beta
    lower = jnp.where(strict, _bdot(kb, k, "nt") * decay, 0.0)
    t = _inv_unit_lower(lower)
    egc = jnp.exp(gc)
    u = _hdot(t, v * beta)
    w = _hdot(t, kb * egc)
    attn = jnp.where(causal, _bdot(q, k, "nt") * decay, 0.0)
    v_new = u - _bdot(w, S, "nn")
    o = _bdot(q * egc, S, "nn") + _bdot(attn, v_new, "nn")
    g_last = jnp.sum(jnp.where(lax.broadcasted_iota(jnp.int32, (C, 1), 0) == C - 1, gc, 0.0), axis=0, keepdims=True)
    S_new = S * jnp.exp(g_last) + _bdot(k * jnp.exp(g_last - gc), v_new, "tn")
    y = o * lax.rsqrt(jnp.mean(o * o, axis=-1, keepdims=True) + NORM_EPS) * wn * _silu(gz)
    return y, S_new


def _gdn_fwd(cqkv, proj, alog, dtb, wn, name):
    L = cqkv.shape[0]
    C, H, Dh = GDN_CHUNK, GDN_HEADS, GDN_HEAD_DIM
    N = L // C

    def body(c_ref, gz_ref, gab_ref, al_ref, db_ref, wn_ref, y_ref, sall_ref, S):
        n = pl.program_id(0)

        @pl.when(n == 0)
        def _():
            S[...] = jnp.zeros_like(S)

        gab = gab_ref[...]
        for h in range(H):
            s_in = S[h]
            sall_ref[0, h] = s_in
            y, s_new = _gdn_chunk(c_ref[:, h * Dh:(h + 1) * Dh], c_ref[:, (H + h) * Dh:(H + h + 1) * Dh],
                                  c_ref[:, (2 * H + h) * Dh:(2 * H + h + 1) * Dh], gz_ref[:, h * Dh:(h + 1) * Dh],
                                  gab, al_ref[...], db_ref[...], wn_ref[...], s_in, h)
            y_ref[:, h * Dh:(h + 1) * Dh] = y.astype(y_ref.dtype)
            S[h] = s_new

    vec = pl.BlockSpec((1, LANES), lambda n: (0, 0))
    return pl.pallas_call(
        body, name=name, grid=(N,),
        in_specs=[pl.BlockSpec((C, 3 * GDN_WIDTH), lambda n: (n, 0)),
                  pl.BlockSpec((C, GDN_WIDTH), lambda n: (n, P_GZ // GDN_WIDTH)),
                  pl.BlockSpec((C, LANES), lambda n: (n, P_GAB // LANES)), vec, vec, vec],
        out_specs=[pl.BlockSpec((C, GDN_WIDTH), lambda n: (n, 0)), pl.BlockSpec((1, H, Dh, Dh), lambda n: (n, 0, 0, 0))],
        out_shape=[jax.ShapeDtypeStruct((L, GDN_WIDTH), MXU_DTYPE), jax.ShapeDtypeStruct((N, H, Dh, Dh), F32)],
        scratch_shapes=[pltpu.VMEM((H, Dh, Dh), F32)],
        compiler_params=_params(("arbitrary",)),
    )(cqkv, proj, proj, alog, dtb, wn)


def _gdn_bwd(cqkv, proj, alog, dtb, wn, s_all, dy, name):
    L = cqkv.shape[0]
    C, H, Dh = GDN_CHUNK, GDN_HEADS, GDN_HEAD_DIM
    N = L // C

    def body(c_ref, gz_ref, gab_ref, al_ref, db_ref, wn_ref, sall_ref, dy_ref,
             dc_ref, dgz_ref, dgab_ref, dal_ref, ddb_ref, dwn_ref, dS):
        n = pl.program_id(0)

        @pl.when(n == 0)
        def _():
            dS[...] = jnp.zeros_like(dS)
            dal_ref[...] = jnp.zeros_like(dal_ref)
            ddb_ref[...] = jnp.zeros_like(ddb_ref)
            dwn_ref[...] = jnp.zeros_like(dwn_ref)

        gab = gab_ref[...]
        dgab = jnp.zeros_like(gab)
        dal, ddb, dwn = jnp.zeros((1, LANES), F32), jnp.zeros((1, LANES), F32), jnp.zeros((1, LANES), F32)
        for h in range(H):
            f = functools.partial(_gdn_chunk, h=h)
            _, vjp = jax.vjp(f, c_ref[:, h * Dh:(h + 1) * Dh], c_ref[:, (H + h) * Dh:(H + h + 1) * Dh],
                             c_ref[:, (2 * H + h) * Dh:(2 * H + h + 1) * Dh], gz_ref[:, h * Dh:(h + 1) * Dh],
                             gab, al_ref[...], db_ref[...], wn_ref[...], sall_ref[0, h])
            dq, dk, dv, dgz, dgab_h, dal_h, ddb_h, dwn_h, ds = vjp((dy_ref[:, h * Dh:(h + 1) * Dh], dS[h]))
            dc_ref[:, h * Dh:(h + 1) * Dh] = dq
            dc_ref[:, (H + h) * Dh:(H + h + 1) * Dh] = dk
            dc_ref[:, (2 * H + h) * Dh:(2 * H + h + 1) * Dh] = dv
            dgz_ref[:, h * Dh:(h + 1) * Dh] = dgz.astype(dgz_ref.dtype)
            dS[h] = ds
            dgab, dal, ddb, dwn = dgab + dgab_h, dal + dal_h, ddb + ddb_h, dwn + dwn_h
        dgab_ref[...] = dgab
        dal_ref[...] += dal
        ddb_ref[...] += ddb
        dwn_ref[...] += dwn

    vec = pl.BlockSpec((1, LANES), lambda n: (0, 0))
    rev = lambda n: N - 1 - n
    return pl.pallas_call(
        body, name=name, grid=(N,),
        in_specs=[pl.BlockSpec((C, 3 * GDN_WIDTH), lambda n: (rev(n), 0)),
                  pl.BlockSpec((C, GDN_WIDTH), lambda n: (rev(n), P_GZ // GDN_WIDTH)),
                  pl.BlockSpec((C, LANES), lambda n: (rev(n), P_GAB // LANES)), vec, vec, vec,
                  pl.BlockSpec((1, H, Dh, Dh), lambda n: (rev(n), 0, 0, 0)),
                  pl.BlockSpec((C, GDN_WIDTH), lambda n: (rev(n), Y_GDN // GDN_WIDTH))],
        out_specs=[pl.BlockSpec((C, 3 * GDN_WIDTH), lambda n: (rev(n), 0)),
                   pl.BlockSpec((C, GDN_WIDTH), lambda n: (rev(n), 0)),
                   pl.BlockSpec((C, LANES), lambda n: (rev(n), 0)), vec, vec, vec],
        out_shape=[jax.ShapeDtypeStruct((L, 3 * GDN_WIDTH), F32), jax.ShapeDtypeStruct((L, GDN_WIDTH), MXU_DTYPE),
                   jax.ShapeDtypeStruct((L, LANES), F32)] + [jax.ShapeDtypeStruct((1, LANES), F32)] * 3,
        scratch_shapes=[pltpu.VMEM((H, Dh, Dh), F32)],
        compiler_params=_params(("arbitrary",)),
    )(cqkv, proj, proj, alog, dtb, wn, s_all, dy)


def _split_dot(x, m):
    R = x.shape[0]
    hi = x.astype(MXU_DTYPE)
    lo = (x - hi.astype(F32)).astype(MXU_DTYPE)
    both = jnp.dot(jnp.concatenate([hi, lo], axis=0), m, preferred_element_type=F32)
    return both[:R] + both[R:]


def _sb_logits(qx, kps):
    B = SB_BLOCK
    z = jnp.concatenate([lax.dot_general(qx[2 * p * B:(2 * p + 2) * B], kp, _DIMS["nt"], preferred_element_type=F32)
                         for p, kp in enumerate(kps)], axis=0) * (SB_HEAD_DIM ** -0.5)
    t = jnp.exp(-jnp.abs(z))
    lb = jnp.minimum(z, 0.0) - jnp.log(1.0 + t)
    return z, t, lb


def _sb_head_masks():
    low = lax.broadcasted_iota(jnp.int32, (SB_BLOCK, LANES), 1) < SB_HEAD_DIM
    return [low if h % 2 == 0 else jnp.logical_not(low) for h in range(SB_HEADS)]


def _sb_stack_heads(ref):
    mine = _sb_head_masks()
    return jnp.concatenate([jnp.where(mine[h], ref[:, (h // 2) * LANES:(h // 2 + 1) * LANES], 0.0).astype(MXU_DTYPE)
                            for h in range(SB_HEADS)], axis=0)


def _sb_block_masks():
    B = SB_BLOCK
    row = lax.broadcasted_iota(jnp.int32, (B, B), 0)
    col = lax.broadcasted_iota(jnp.int32, (B, B), 1)
    row4 = lax.broadcasted_iota(jnp.int32, (SB_HEADS * B, B), 0) & (B - 1)
    col4 = lax.broadcasted_iota(jnp.int32, (SB_HEADS * B, B), 1)
    return (row > col).astype(MXU_DTYPE), (row < col).astype(MXU_DTYPE), col4 < row4


def _sb_fwd(proj, kv, name):
    L = proj.shape[0]
    B, H = SB_BLOCK, SB_HEADS

    def body(q_ref, kv_ref, y_ref, c_ref):
        i = pl.program_id(0)
        low = _sb_head_masks()[0]
        after, _, strict = _sb_block_masks()
        qx = _sb_stack_heads(q_ref)

        def pair(j, c, accs, masked):
            rows = pl.ds(pl.multiple_of(j * B, B), B)
            kps = [kv_ref[rows, p * LANES:(p + 1) * LANES] for p in range(H // 2)]
            vps = [kv_ref[rows, SB_WIDTH + p * LANES:SB_WIDTH + (p + 1) * LANES] for p in range(H // 2)]
            z, _, lb = _sb_logits(qx, kps)
            lom = lb - z
            if masked:
                lom = jnp.where(strict, lom, 0.0)
            a = jnp.exp(lb + _split_dot(lom, after) + c)
            if masked:
                a = jnp.where(strict, a, 0.0)
            a = a.astype(MXU_DTYPE)
            new_accs = []
            for p in range(H // 2):
                o = jnp.dot(a[2 * p * B:(2 * p + 2) * B], vps[p], preferred_element_type=F32)
                new_accs.append(accs[p] + jnp.where(low, o[:B], o[B:]))
            return c + jnp.sum(lom, axis=1, keepdims=True), new_accs

        c, accs = pair(i, jnp.zeros((H * B, 1), F32), [jnp.zeros((B, LANES), F32)] * (H // 2), True)

        def step(it, carry):
            c, accs = pair(i - 1 - it, carry[0], list(carry[1:]), False)
            return (c,) + tuple(accs)

        carry = lax.fori_loop(0, i, step, (c,) + tuple(accs))
        for p in range(H // 2):
            y_ref[:, p * LANES:(p + 1) * LANES] = carry[1 + p].astype(y_ref.dtype)
        lane = lax.broadcasted_iota(jnp.int32, (B, LANES), 1)
        ct = jnp.zeros((B, LANES), F32)
        for h in range(H):
            ct = jnp.where(lane == h, carry[0][h * B:(h + 1) * B], ct)
        c_ref[...] = ct

    return pl.pallas_call(
        body, name=name, grid=(L // B,),
        in_specs=[pl.BlockSpec((B, SB_WIDTH), lambda i: (i, P_SB // SB_WIDTH)), pl.BlockSpec((L, 2 * SB_WIDTH), lambda i: (0, 0))],
        out_specs=[pl.BlockSpec((B, SB_WIDTH), lambda i: (i, 0)), pl.BlockSpec((B, LANES), lambda i: (i, 0))],
        out_shape=[jax.ShapeDtypeStruct((L, SB_WIDTH), MXU_DTYPE), jax.ShapeDtypeStruct((L, LANES), F32)],
        compiler_params=_params(("arbitrary",)),
    )(proj, kv)


def _sb_bwd(proj, kv, dy, ctot, name):
    L = proj.shape[0]
    B, H = SB_BLOCK, SB_HEADS
    nblk = L // B
    scale = SB_HEAD_DIM ** -0.5

    def body(q_ref, kv_ref, do_ref, ct_ref, dq_ref, dk_hbm, dv_hbm, dk_acc, dv_acc):
        i = pl.program_id(0)

        @pl.when(i == 0)
        def _():
            dk_acc[...] = jnp.zeros_like(dk_acc)
            dv_acc[...] = jnp.zeros_like(dv_acc)

        low = _sb_head_masks()[0]
        after, before, strict = _sb_block_masks()
        qx, dox = _sb_stack_heads(q_ref), _sb_stack_heads(do_ref)
        ct = ct_ref[...]
        ctot = jnp.concatenate([_pick_lane(ct, h) for h in range(H)], axis=0)

        def pair(j, p, e, dqs, masked):
            rows = pl.ds(pl.multiple_of(j * B, B), B)
            kps = [kv_ref[rows, pr * LANES:(pr + 1) * LANES] for pr in range(H // 2)]
            vps = [kv_ref[rows, SB_WIDTH + pr * LANES:SB_WIDTH + (pr + 1) * LANES] for pr in range(H // 2)]
            z, t, lb = _sb_logits(qx, kps)
            r = 1.0 / (1.0 + t)
            sig = jnp.where(z >= 0, r, t * r)
            lom = lb - z
            if masked:
                lom = jnp.where(strict, lom, 0.0)
            p = p + jnp.sum(lom, axis=1, keepdims=True)
            a = jnp.exp(lb + _split_dot(lom, after) + (ctot - p))
            if masked:
                a = jnp.where(strict, a, 0.0)
            da = jnp.concatenate([lax.dot_general(dox[2 * pr * B:(2 * pr + 2) * B], vps[pr], _DIMS["nt"], preferred_element_type=F32)
                                  for pr in range(H // 2)], axis=0)
            ea = da * a
            dlom = e + _split_dot(ea, before)
            if masked:
                dlom = jnp.where(strict, dlom, 0.0)
            e = e + jnp.sum(ea, axis=1, keepdims=True)
            dz = ((ea * (1.0 - sig) - dlom * sig) * scale).astype(MXU_DTYPE)
            ab = a.astype(MXU_DTYPE)
            new_dq = []
            for pr in range(H // 2):
                both, cols = slice(2 * pr * B, (2 * pr + 2) * B), slice(pr * LANES, (pr + 1) * LANES)
                dqp = jnp.dot(dz[both], kps[pr], preferred_element_type=F32)
                new_dq.append(dqs[pr] + jnp.where(low, dqp[:B], dqp[B:]))
                dk_acc[rows, cols] += lax.dot_general(dz[both], qx[both], _DIMS["tn"], preferred_element_type=F32)
                dv_acc[rows, cols] += lax.dot_general(ab[both], dox[both], _DIMS["tn"], preferred_element_type=F32)
            return p, e, new_dq

        def step(j, carry):
            p, e, dqs = pair(j, carry[0], carry[1], list(carry[2:]), False)
            return (p, e) + tuple(dqs)

        zero = jnp.zeros((H * B, 1), F32)
        carry = lax.fori_loop(0, i, step, (zero, zero) + (jnp.zeros((B, LANES), F32),) * (H // 2))
        _, _, dqs = pair(i, carry[0], carry[1], list(carry[2:]), True)
        for pr in range(H // 2):
            dq_ref[:, pr * LANES:(pr + 1) * LANES] = dqs[pr].astype(dq_ref.dtype)

        @pl.when(i == nblk - 1)
        def _():
            pltpu.sync_copy(dk_acc, dk_hbm)
            pltpu.sync_copy(dv_acc, dv_hbm)

    hbm = pl.BlockSpec(memory_space=pl.ANY)
    acc = jax.ShapeDtypeStruct((L, SB_WIDTH), F32)
    return pl.pallas_call(
        body, name=name, grid=(nblk,),
        in_specs=[pl.BlockSpec((B, SB_WIDTH), lambda i: (i, P_SB // SB_WIDTH)), pl.BlockSpec((L, 2 * SB_WIDTH), lambda i: (0, 0)),
                  pl.BlockSpec((B, SB_WIDTH), lambda i: (i, Y_SB // SB_WIDTH)), pl.BlockSpec((B, LANES), lambda i: (i, 0))],
        out_specs=[pl.BlockSpec((B, SB_WIDTH), lambda i: (i, 0)), hbm, hbm],
        out_shape=[jax.ShapeDtypeStruct((L, SB_WIDTH), MXU_DTYPE), acc, acc],
        scratch_shapes=[pltpu.VMEM((L, SB_WIDTH), F32), pltpu.VMEM((L, SB_WIDTH), F32)],
        compiler_params=_params(("arbitrary",)),
    )(proj, kv, dy, ctot)


def _prep_w_in(w):
    sc, qkv, gz, gab, sb = w[:, 0:768], w[:, 768:2304], w[:, 2304:2816], w[:, 2816:2824], w[:, 2824:3592]
    pad = jnp.zeros((w.shape[0], P_END - D_IN_PROJ), w.dtype)
    return jnp.concatenate([qkv, gz, sc, sb, gab, pad], axis=1).astype(MXU_DTYPE)


def _unprep_dw_in(dw):
    qkv, gz, sc, sb, gab = dw[:, P_QKV:P_GZ], dw[:, P_GZ:P_SC], dw[:, P_SC:P_SB], dw[:, P_SB:P_GAB], dw[:, P_GAB:P_GAB + 8]
    return jnp.concatenate([sc, qkv, gz, gab, sb], axis=1)


def _prep_w_out(w):
    return jnp.concatenate([w[256:768], w[0:256], w[768:]], axis=0).astype(MXU_DTYPE)


def _unprep_dw_out(dw):
    return jnp.concatenate([dw[512:768], dw[0:512], dw[768:]], axis=0)


def _pad_lanes(v):
    return jnp.zeros((1, LANES), F32).at[0, :v.shape[0]].set(v)


def _layer_fwd(x, p, l):
    L = x.shape[0]
    tm = min(1024, L)
    n = f"l{l}_"
    h = _rms_fwd(x, p["norm_mix"], n + "rms_mix")
    proj = _matmul(h, p["w_in"], "nn", F32, n + "mm_in", tm, 768, 1024)
    cb = SC_WIDTH
    y_sc = _conv_fwd((proj, P_SC // cb + 1), p["w_sconv"], SC_KERNEL, cb, 1, MXU_DTYPE, n + "sconv",
                     x2=(proj, P_SC // cb + 2), gate=(proj, P_SC // cb))
    cqkv = _conv_fwd((proj, 0), p["w_gconv"], GDN_CONV, 256, 6, F32, n + "gconv")
    y_gdn, s_all = _gdn_fwd(cqkv, proj, p["a_log"], p["dt_bias"], p["gdn_norm"], n + "gdn")
    kv = proj[:, P_SB + SB_WIDTH:P_SB + 3 * SB_WIDTH].astype(MXU_DTYPE)
    y_sb, ctot = _sb_fwd(proj, kv, n + "sb")
    ycat = jnp.concatenate([y_gdn, y_sc, y_sb], axis=1)
    x1 = _matmul(ycat, p["w_out"], "nn", F32, n + "mm_out", tm, 512, 1024, resid=x)
    h2 = _rms_fwd(x1, p["norm_ffn"], n + "rms_ffn")
    up = _matmul(h2, p["w_up"], "nn", F32, n + "mm_up", tm, 512, 1024)
    act = _ffn_act_fwd(up, p["w_fconv"], n + "ffn_act")
    x2 = _matmul(act, p["w_down"], "nn", F32, n + "mm_down", tm, 512, 1408, resid=x1)
    saved = dict(x=x, h=h, proj=proj, cqkv=cqkv, s_all=s_all, kv=kv, ctot=ctot, ycat=ycat, x1=x1, h2=h2, up=up, act=act)
    return x2, saved


def _layer_bwd(dx2, p, s, l):
    L = dx2.shape[0]
    tm, tkl = min(1024, L), min(512, L)
    n = f"l{l}_"
    g = {}
    dx2b = dx2.astype(MXU_DTYPE)
    g["w_ffn_down"] = _matmul(s["act"], dx2b, "tn", F32, n + "mm_ddown", 1408, 512, tkl)
    dact = _matmul(dx2b, p["w_down"], "nt", F32, n + "mm_dact", tm, 1408, 1024)
    dup, dwf = _ffn_act_bwd(s["up"], p["w_fconv"], dact, n + "dffn_act")
    g["w_ffn_conv"] = dwf[:FFN_CONV]
    g["w_ffn_up"] = _matmul(s["h2"], dup, "tn", F32, n + "mm_dup", 1024, 512, tkl)
    dh2 = _matmul(dup, p["w_up"], "nt", F32, n + "mm_dh2", tm, 512, 512)
    dx1, dwn = _rms_bwd(s["x1"], p["norm_ffn"], dh2, dx2, n + "drms_ffn")
    g["w_norm_ffn"] = dwn[0]

    dx1b = dx1.astype(MXU_DTYPE)
    g["w_mix_out"] = _unprep_dw_out(_matmul(s["ycat"], dx1b, "tn", F32, n + "mm_dout", 1024, 512, tkl))
    dycat = _matmul(dx1b, p["w_out"], "nt", F32, n + "mm_dycat", tm, 512, 1024)
    proj = s["proj"]
    cb = SC_WIDTH
    dsc_c, dsc_h, dsc_b, dws = _conv_bwd((proj, P_SC // cb + 1), p["w_sconv"], (dycat, Y_SC // cb), SC_KERNEL, cb, 1,
                                         MXU_DTYPE, n + "dsconv", x2=(proj, P_SC // cb + 2), gate=(proj, P_SC // cb))
    g["w_sconv"] = dws[:SC_KERNEL]
    dcqkv, dgz, dgab, dal, ddb, dgn = _gdn_bwd(s["cqkv"], proj, p["a_log"], p["dt_bias"], p["gdn_norm"], s["s_all"], dycat,
                                               n + "dgdn")
    g["gdn_a_log"], g["gdn_dt_bias"], g["w_gdn_norm"] = dal[0, :GDN_HEADS], ddb[0, :GDN_HEADS], dgn[0]
    dqkv, dwg = _conv_bwd((proj, 0), p["w_gconv"], (dcqkv, 0), GDN_CONV, 256, 6, MXU_DTYPE, n + "dgconv")
    g["w_gdn_conv"] = dwg[:GDN_CONV]
    dq, dk, dv = _sb_bwd(proj, s["kv"], dycat, s["ctot"], n + "dsb")
    dproj = jnp.concatenate(
        [dqkv, dgz, dsc_b, dsc_c, dsc_h, dq, dk.astype(MXU_DTYPE), dv.astype(MXU_DTYPE), dgab.astype(MXU_DTYPE),
         jnp.zeros((L, P_END - P_GAB - LANES), MXU_DTYPE)], axis=1)
    g["w_mix_in"] = _unprep_dw_in(_matmul(s["h"], dproj, "tn", F32, n + "mm_din", 1024, 768, tkl))
    dh = _matmul(dproj, p["w_in"], "nt", F32, n + "mm_dh", tm, 512, 768)
    dx, dwm = _rms_bwd(s["x"], p["norm_mix"], dh, dx1, n + "drms_mix")
    g["w_norm_mix"] = dwm[0]
    return dx, g


WEIGHTS = ["w_norm_mix", "w_mix_in", "w_sconv", "w_gdn_conv", "gdn_a_log", "gdn_dt_bias", "w_gdn_norm", "w_mix_out",
           "w_norm_ffn", "w_ffn_up", "w_ffn_conv", "w_ffn_down", "w_norm_final"]


def _local_step(x, w, target):
    layers = []
    for l in range(DEPTH):
        layers.append(dict(
            norm_mix=w["w_norm_mix"][l][None], w_in=_prep_w_in(w["w_mix_in"][l]), w_sconv=w["w_sconv"][l],
            w_gconv=w["w_gdn_conv"][l], a_log=_pad_lanes(w["gdn_a_log"][l]), dt_bias=_pad_lanes(w["gdn_dt_bias"][l]),
            gdn_norm=w["w_gdn_norm"][l][None], w_out=_prep_w_out(w["w_mix_out"][l]), norm_ffn=w["w_norm_ffn"][l][None],
            w_up=w["w_ffn_up"][l].astype(MXU_DTYPE), w_fconv=w["w_ffn_conv"][l], w_down=w["w_ffn_down"][l].astype(MXU_DTYPE)))
    saved = []
    for l in range(DEPTH):
        x, s = _layer_fwd(x, layers[l], l)
        saved.append(s)
    loss, dx, dwf = _loss_head(x, w["w_norm_final"][None], target, "loss_head")
    grads = [None] * DEPTH
    for l in reversed(range(DEPTH)):
        dx, grads[l] = _layer_bwd(dx, layers[l], saved[l], l)
    out = {k: jnp.stack([grads[l][k] for l in range(DEPTH)]) for k in WEIGHTS if k != "w_norm_final"}
    out["w_norm_final"] = dwf[0]
    return loss, dx, out


def _exchange(bufs, scatter, name):
    nb = len(bufs)

    def body(*refs):
        ins, outs = refs[:nb], refs[nb:2 * nb]
        send_sems, recv_sems, local_sems = refs[2 * nb:]
        x, y, c = lax.axis_index("x"), lax.axis_index("y"), lax.axis_index("c")
        me = 4 * x + 2 * y + c
        local = []
        for b in range(nb):
            cp = pltpu.make_async_copy(ins[b].at[me] if scatter[b] else ins[b], outs[b].at[me], local_sems.at[b])
            cp.start()
            local.append(cp)
        remote = []
        for b in range(nb):
            for kk in range(1, N_DEV):
                px = 1 - x if kk & 4 else x
                py = 1 - y if kk & 2 else y
                pc = 1 - c if kk & 1 else c
                src = ins[b].at[4 * px + 2 * py + pc] if scatter[b] else ins[b]
                cp = pltpu.make_async_remote_copy(
                    src_ref=src, dst_ref=outs[b].at[me], send_sem=send_sems.at[b, kk - 1], recv_sem=recv_sems.at[b, kk - 1],
                    device_id=(px, py, pc), device_id_type=pl.DeviceIdType.MESH)
                cp.start()
                remote.append(cp)
        for cp in remote:
            cp.wait()
        for cp in local:
            cp.wait()

    out_shape = [jax.ShapeDtypeStruct((N_DEV,) + (b.shape[1:] if s else b.shape), b.dtype) for b, s in zip(bufs, scatter)]
    hbm = pl.BlockSpec(memory_space=pl.ANY)
    return pl.pallas_call(
        body, name=name, in_specs=[hbm] * nb, out_specs=[hbm] * nb, out_shape=out_shape,
        scratch_shapes=[pltpu.SemaphoreType.DMA((nb, N_DEV - 1)), pltpu.SemaphoreType.DMA((nb, N_DEV - 1)),
                        pltpu.SemaphoreType.DMA((nb,))],
    )(*bufs)


def _sum_sources(recv, row_tile, name):
    _, R, _ = recv.shape

    def body(r_ref, o_ref):
        acc = r_ref[0].astype(F32)
        for s in range(1, N_DEV):
            acc = acc + r_ref[s].astype(F32)
        o_ref[...] = acc

    return pl.pallas_call(
        body, name=name, grid=(R // row_tile,),
        in_specs=[pl.BlockSpec((N_DEV, row_tile, LANES), lambda i: (0, i, 0))],
        out_specs=pl.BlockSpec((row_tile, LANES), lambda i: (i, 0)),
        out_shape=jax.ShapeDtypeStruct((R, LANES), F32),
        compiler_params=_params(("parallel",)),
    )(recv)


def _adamw_math(g, w, m, v):
    nm = ADAM_B1 * m + (1.0 - ADAM_B1) * g
    nv = ADAM_B2 * v + (1.0 - ADAM_B2) * (g * g)
    m_hat = nm / (1.0 - ADAM_B1 ** ADAM_STEP)
    v_hat = nv / (1.0 - ADAM_B2 ** ADAM_STEP)
    return -ADAM_LR * (m_hat / (jnp.sqrt(v_hat) + ADAM_EPS) + ADAM_WD * w), nm, nv


def _sum_adamw(recv, w, m, v, row_tile, name):
    D0, R, C = w.shape

    def body(r_ref, w_ref, m_ref, v_ref, g_ref, d_ref, nm_ref, nv_ref):
        g = r_ref[0, 0].astype(F32)
        for s in range(1, N_DEV):
            g = g + r_ref[s, 0].astype(F32)
        g_ref[0] = g
        d_ref[0], nm_ref[0], nv_ref[0] = _adamw_math(g, w_ref[0], m_ref[0], v_ref[0])

    blk = pl.BlockSpec((1, row_tile, C), lambda l, i: (l, i, 0))
    out = jax.ShapeDtypeStruct((D0, R, C), F32)
    return pl.pallas_call(
        body, name=name, grid=(D0, R // row_tile),
        in_specs=[pl.BlockSpec((N_DEV, 1, row_tile, C), lambda l, i: (0, l, i, 0)), blk, blk, blk],
        out_specs=[blk] * 4, out_shape=[out] * 4,
        compiler_params=_params(("parallel", "parallel")),
    )(recv, w, m, v)


def _adamw(g, w, m, v, row_tile, name):
    R = g.shape[0]

    def body(g_ref, w_ref, m_ref, v_ref, d_ref, nm_ref, nv_ref):
        d_ref[...], nm_ref[...], nv_ref[...] = _adamw_math(g_ref[...], w_ref[...], m_ref[...], v_ref[...])

    blk = pl.BlockSpec((row_tile, LANES), lambda i: (i, 0))
    out = jax.ShapeDtypeStruct((R, LANES), F32)
    return pl.pallas_call(
        body, name=name, grid=(R // row_tile,), in_specs=[blk] * 4, out_specs=[blk] * 3, out_shape=[out] * 3,
        compiler_params=_params(("parallel",)),
    )(g, w, m, v)


def _pack(arrs, rows, dtype):
    flat = jnp.concatenate([a.reshape(-1).astype(dtype) for a in arrs])
    return jnp.pad(flat, (0, rows * LANES - flat.shape[0])).reshape(rows, LANES)


def _unpack(buf, shapes):
    lead = buf.shape[:-2]
    flat = buf.reshape(lead + (-1,))
    out, off = [], 0
    for shp in shapes:
        n = 1
        for d in shp:
            n *= d
        out.append(flat[..., off:off + n].reshape(lead + tuple(shp)))
        off += n
    return out


BIG = ["w_mix_in", "w_mix_out", "w_ffn_up", "w_ffn_down"]
BIG_AXIS = {"w_mix_in": 2, "w_mix_out": 1, "w_ffn_up": 2, "w_ffn_down": 1}
CONV = ["w_sconv", "w_gdn_conv", "w_ffn_conv"]
REPL = ["w_norm_mix", "gdn_a_log", "gdn_dt_bias", "w_gdn_norm", "w_norm_ffn", "w_norm_final"]
BIG_ROW_TILE = {"w_mix_in": 512, "w_mix_out": 128, "w_ffn_up": 512, "w_ffn_down": 352}
SMALL_ROWS = 416
CONV_ROWS = 48


def kernel(x, w_norm_mix, w_mix_in, w_sconv, w_gdn_conv, gdn_a_log, gdn_dt_bias, w_gdn_norm, w_mix_out, w_norm_ffn, w_ffn_up, w_ffn_conv, w_ffn_down, w_norm_final, loss_target, m_w_norm_mix, m_w_mix_in, m_w_sconv, m_w_gdn_conv, m_gdn_a_log, m_gdn_dt_bias, m_w_gdn_norm, m_w_mix_out, m_w_norm_ffn, m_w_ffn_up, m_w_ffn_conv, m_w_ffn_down, m_w_norm_final, v_w_norm_mix, v_w_mix_in, v_w_sconv, v_w_gdn_conv, v_gdn_a_log, v_gdn_dt_bias, v_w_gdn_norm, v_w_mix_out, v_w_norm_ffn, v_w_ffn_up, v_w_ffn_conv, v_w_ffn_down, v_w_norm_final):
    w = dict(w_norm_mix=w_norm_mix, w_mix_in=w_mix_in, w_sconv=w_sconv, w_gdn_conv=w_gdn_conv, gdn_a_log=gdn_a_log,
             gdn_dt_bias=gdn_dt_bias, w_gdn_norm=w_gdn_norm, w_mix_out=w_mix_out, w_norm_ffn=w_norm_ffn, w_ffn_up=w_ffn_up,
             w_ffn_conv=w_ffn_conv, w_ffn_down=w_ffn_down, w_norm_final=w_norm_final)
    m = dict(w_norm_mix=m_w_norm_mix, w_mix_in=m_w_mix_in, w_sconv=m_w_sconv, w_gdn_conv=m_w_gdn_conv, gdn_a_log=m_gdn_a_log,
             gdn_dt_bias=m_gdn_dt_bias, w_gdn_norm=m_w_gdn_norm, w_mix_out=m_w_mix_out, w_norm_ffn=m_w_norm_ffn,
             w_ffn_up=m_w_ffn_up, w_ffn_conv=m_w_ffn_conv, w_ffn_down=m_w_ffn_down, w_norm_final=m_w_norm_final)
    v = dict(w_norm_mix=v_w_norm_mix, w_mix_in=v_w_mix_in, w_sconv=v_w_sconv, w_gdn_conv=v_w_gdn_conv, gdn_a_log=v_gdn_a_log,
             gdn_dt_bias=v_gdn_dt_bias, w_gdn_norm=v_w_gdn_norm, w_mix_out=v_w_mix_out, w_norm_ffn=v_w_norm_ffn,
             w_ffn_up=v_w_ffn_up, w_ffn_conv=v_w_ffn_conv, w_ffn_down=v_w_ffn_down, w_norm_final=v_w_norm_final)
    me = 4 * lax.axis_index("x") + 2 * lax.axis_index("y") + lax.axis_index("c")
    conv_shapes = [w[k].shape for k in CONV]

    gathered = _exchange([w[k].astype(MXU_DTYPE) for k in BIG] + [_pack([w[k] for k in CONV], CONV_ROWS, F32)],
                         [False] * (len(BIG) + 1), "gather_weights")
    full = dict(w)
    for k, got in zip(BIG, gathered):
        full[k] = jnp.concatenate([got[s] for s in range(N_DEV)], axis=BIG_AXIS[k])
    for k, got in zip(CONV, _unpack(gathered[-1], conv_shapes)):
        full[k] = jnp.concatenate([got[s] for s in range(N_DEV)], axis=2)

    loss, dx, grads = _local_step(x[0], full, loss_target[0])

    small = CONV + REPL
    to_owner = [jnp.stack(jnp.split(grads[k], N_DEV, axis=BIG_AXIS[k])).astype(MXU_DTYPE) for k in BIG]
    received = _exchange(to_owner + [_pack([grads[k] for k in small], SMALL_ROWS, F32)], [True] * len(BIG) + [False],
                         "exchange_grads")
    g, delta, new_m, new_v = {}, {}, {}, {}
    for k, got in zip(BIG, received):
        g[k], delta[k], new_m[k], new_v[k] = _sum_adamw(got, w[k], m[k], v[k], BIG_ROW_TILE[k], "adamw_" + k)
    g_small = _unpack(_sum_sources(received[-1], SMALL_ROWS, "sum_small"), [grads[k].shape for k in small])
    for k, gs in zip(small, g_small):
        g[k] = lax.dynamic_slice_in_dim(gs, me * w[k].shape[2], w[k].shape[2], axis=2) if k in CONV else gs

    small_shapes = [w[k].shape for k in small]
    small_rows = -(-sum(w[k].size for k in small) // (SUBLANES * LANES)) * SUBLANES
    d_sm, m_sm, v_sm = _adamw(_pack([g[k] for k in small], small_rows, F32), _pack([w[k] for k in small], small_rows, F32),
                              _pack([m[k] for k in small], small_rows, F32), _pack([v[k] for k in small], small_rows, F32),
                              small_rows, "adamw_small")
    for dst, small_buf in ((delta, d_sm), (new_m, m_sm), (new_v, v_sm)):
        dst.update(zip(small, _unpack(small_buf, small_shapes)))

    loss_all = lax.psum(loss[0, 0], ("x", "y", "c"))
    return (loss_all, dx[None], *[g[k] for k in WEIGHTS], *[delta[k] for k in WEIGHTS], *[new_m[k] for k in WEIGHTS],
            *[new_v[k] for k in WEIGHTS])
```

```python
import functools

import jax
import jax.numpy as jnp
from jax import lax
from jax.experimental import pallas as pl
from jax.experimental.pallas import tpu as pltpu

F32 = jnp.float32
MXU_DTYPE = jnp.bfloat16
HIGHEST = lax.Precision.HIGHEST

D_MODEL = 1024
DEPTH = 2
SC_WIDTH = 256
SC_KERNEL = 3
GDN_WIDTH = 512
GDN_HEADS = 4
GDN_HEAD_DIM = 128
GDN_CONV = 4
GDN_CHUNK = 64
SB_WIDTH = 256
SB_HEADS = 4
SB_HEAD_DIM = 64
SB_BLOCK = 128
SB_SWEEP = 4
D_FF = 2816
FFN_CONV = 3
NORM_EPS = 1e-6
D_IN_PROJ = 3592
ADAM_LR, ADAM_B1, ADAM_B2, ADAM_EPS, ADAM_WD, ADAM_STEP = 0.001, 0.9, 0.999, 1e-08, 0.01, 10

N_DEV = 8
LANES = 128
SUBLANES = 8
VMEM_LIMIT = 48 * 1024 * 1024

P_QKV, P_GZ, P_SC, P_SB, P_GAB, P_END = 0, 1536, 2048, 2816, 3584, 3840
Y_GDN, Y_SC, Y_SB = 0, 512, 768


def _params(semantics):
    return pltpu.CompilerParams(dimension_semantics=semantics, vmem_limit_bytes=VMEM_LIMIT)


_DIMS = {"nn": (((1,), (0,)), ((), ())), "nt": (((1,), (1,)), ((), ())), "tn": (((0,), (0,)), ((), ()))}


def _matmul(a, b, mode, out_dtype, name, tm, tn, tk, resid=None):
    if mode == "tn":
        (K, M), (K2, N) = a.shape, b.shape
    elif mode == "nt":
        (M, K), (N, K2) = a.shape, b.shape
    else:
        (M, K), (K2, N) = a.shape, b.shape
    assert K == K2 and M % tm == 0 and N % tn == 0 and K % tk == 0, (name, a.shape, b.shape, tm, tn, tk)
    nk = K // tk
    has_resid = resid is not None

    def body(*refs):
        if has_resid:
            a_ref, b_ref, r_ref, o_ref, acc = refs
        else:
            a_ref, b_ref, o_ref, acc = refs
        k = pl.program_id(2)

        @pl.when(k == 0)
        def _():
            acc[...] = jnp.zeros_like(acc)

        acc[...] += lax.dot_general(a_ref[...], b_ref[...], _DIMS[mode], preferred_element_type=F32)

        @pl.when(k == nk - 1)
        def _():
            r = acc[...]
            if has_resid:
                r = r + r_ref[...]
            o_ref[...] = r.astype(out_dtype)

    a_spec = pl.BlockSpec((tk, tm), lambda i, j, k: (k, i)) if mode == "tn" else pl.BlockSpec((tm, tk), lambda i, j, k: (i, k))
    b_spec = pl.BlockSpec((tn, tk), lambda i, j, k: (j, k)) if mode == "nt" else pl.BlockSpec((tk, tn), lambda i, j, k: (k, j))
    o_spec = pl.BlockSpec((tm, tn), lambda i, j, k: (i, j))
    in_specs = [a_spec, b_spec] + ([o_spec] if has_resid else [])
    args = (a, b) + ((resid,) if has_resid else ())
    return pl.pallas_call(
        body, name=name, grid=(M // tm, N // tn, nk), in_specs=in_specs, out_specs=o_spec,
        out_shape=jax.ShapeDtypeStruct((M, N), out_dtype),
        scratch_shapes=[pltpu.VMEM((tm, tn), F32)],
        compiler_params=_params(("parallel", "parallel", "arbitrary")),
    )(*args)


def _rms(x, w):
    return x * lax.rsqrt(jnp.mean(x * x, axis=-1, keepdims=True) + NORM_EPS) * w


ROW_TILE = 512


def _rms_fwd(x, w, name):
    L, Dm = x.shape

    def body(x_ref, w_ref, h_ref):
        h_ref[...] = _rms(x_ref[...], w_ref[...]).astype(h_ref.dtype)

    return pl.pallas_call(
        body, name=name, grid=(L // ROW_TILE,),
        in_specs=[pl.BlockSpec((ROW_TILE, Dm), lambda i: (i, 0)), pl.BlockSpec((1, Dm), lambda i: (0, 0))],
        out_specs=pl.BlockSpec((ROW_TILE, Dm), lambda i: (i, 0)),
        out_shape=jax.ShapeDtypeStruct((L, Dm), MXU_DTYPE),
        compiler_params=_params(("parallel",)),
    )(x, w)


def _rms_bwd(x, w, dh, dres, name):
    L, Dm = x.shape

    def body(x_ref, w_ref, dh_ref, dres_ref, dx_ref, dw_ref):
        _, vjp = jax.vjp(_rms, x_ref[...], w_ref[...])
        dx, dw = vjp(dh_ref[...])
        dx_ref[...] = dres_ref[...] + dx

        @pl.when(pl.program_id(0) == 0)
        def _():
            dw_ref[...] = jnp.zeros_like(dw_ref)

        dw_ref[...] += dw

    row = pl.BlockSpec((ROW_TILE, Dm), lambda i: (i, 0))
    vec = pl.BlockSpec((1, Dm), lambda i: (0, 0))
    return pl.pallas_call(
        body, name=name, grid=(L // ROW_TILE,), in_specs=[row, vec, row, row], out_specs=[row, vec],
        out_shape=[jax.ShapeDtypeStruct((L, Dm), F32), jax.ShapeDtypeStruct((1, Dm), F32)],
        compiler_params=_params(("arbitrary",)),
    )(x, w, dh, dres)


def _loss_head(x, w, target, name):
    L, Dm = x.shape

    def block_loss(xb, wb, tb):
        err = _rms(xb, wb) - tb
        return 0.5 * jnp.sum(jnp.sum(err * err, axis=-1, keepdims=True) * (1.0 / Dm), axis=0, keepdims=True)

    def body(x_ref, w_ref, t_ref, loss_ref, dx_ref, dw_ref):
        val, vjp = jax.vjp(lambda xb, wb: block_loss(xb, wb, t_ref[...]), x_ref[...], w_ref[...])
        dx, dw = vjp(jnp.ones_like(val))
        dx_ref[...] = dx

        @pl.when(pl.program_id(0) == 0)
        def _():
            dw_ref[...] = jnp.zeros_like(dw_ref)
            loss_ref[...] = jnp.zeros_like(loss_ref)

        dw_ref[...] += dw
        loss_ref[...] += val

    row = pl.BlockSpec((ROW_TILE, Dm), lambda i: (i, 0))
    vec = pl.BlockSpec((1, Dm), lambda i: (0, 0))
    one = pl.BlockSpec((1, 1), lambda i: (0, 0))
    return pl.pallas_call(
        body, name=name, grid=(L // ROW_TILE,), in_specs=[row, vec, row], out_specs=[one, row, vec],
        out_shape=[jax.ShapeDtypeStruct((1, 1), F32), jax.ShapeDtypeStruct((L, Dm), F32), jax.ShapeDtypeStruct((1, Dm), F32)],
        compiler_params=_params(("arbitrary",)),
    )(x, w, target)


HALO = SUBLANES


def _conv_specs(L, T, Cb, off):
    main = pl.BlockSpec((T, Cb), lambda j, i: (i, off + j))
    prev = pl.BlockSpec((HALO, Cb), lambda j, i: (jnp.maximum(i * (T // HALO) - 1, 0), off + j))
    nxt = pl.BlockSpec((HALO, Cb), lambda j, i: (jnp.minimum((i + 1) * (T // HALO), L // HALO - 1), off + j))
    return main, prev, nxt


def _conv_fwd(x1, w, K, Cb, ncol, out_dtype, name, x2=None, gate=None):
    (x1a, o1) = x1
    L = x1a.shape[0]
    T = min(ROW_TILE, L)
    has_mul, has_gate = x2 is not None, gate is not None

    def body(*refs):
        it = iter(refs)
        x1m, x1p = next(it), next(it)
        if has_mul:
            x2m, x2p = next(it), next(it)
        if has_gate:
            gm = next(it)
        w_ref, y_ref, scr = next(it), next(it), next(it)
        i = pl.program_id(1)
        p, pp = x1m[...].astype(F32), x1p[...].astype(F32)
        if has_mul:
            p, pp = p * x2m[...], pp * x2p[...]
        scr[0:HALO, :] = jnp.where(i > 0, pp, 0.0)
        scr[HALO:HALO + T, :] = p
        acc = w_ref[K - 1:K, :] * p
        for k in range(K - 1):
            s = K - 1 - k
            acc = acc + w_ref[k:k + 1, :] * scr[HALO - s:HALO - s + T, :]
        if has_gate:
            acc = acc * gm[...]
        y_ref[...] = acc.astype(out_dtype)

    in_specs, args = [], []
    m, p_, _ = _conv_specs(L, T, Cb, o1)
    in_specs += [m, p_]
    args += [x1a, x1a]
    if has_mul:
        m, p_, _ = _conv_specs(L, T, Cb, x2[1])
        in_specs += [m, p_]
        args += [x2[0], x2[0]]
    if has_gate:
        m, _, _ = _conv_specs(L, T, Cb, gate[1])
        in_specs += [m]
        args += [gate[0]]
    in_specs.append(pl.BlockSpec((K, Cb), lambda j, i: (0, j)))
    args.append(w)
    return pl.pallas_call(
        body, name=name, grid=(ncol, L // T), in_specs=in_specs,
        out_specs=pl.BlockSpec((T, Cb), lambda j, i: (i, j)),
        out_shape=jax.ShapeDtypeStruct((L, ncol * Cb), out_dtype),
        scratch_shapes=[pltpu.VMEM((T + HALO, Cb), F32)],
        compiler_params=_params(("parallel", "arbitrary")),
    )(*args)


def _conv_bwd(x1, w, dy, K, Cb, ncol, out_dtype, name, x2=None, gate=None):
    (x1a, o1) = x1
    L = x1a.shape[0]
    T = min(ROW_TILE, L)
    nrow = L // T
    has_mul, has_gate = x2 is not None, gate is not None

    def body(*refs):
        it = iter(refs)
        x1m, x1p = next(it), next(it)
        if has_mul:
            x2m, x2p = next(it), next(it)
        if has_gate:
            gm, gn = next(it), next(it)
        dym, dyn, w_ref = next(it), next(it), next(it)
        dx1_ref = next(it)
        if has_mul:
            dx2_ref = next(it)
        if has_gate:
            dg_ref = next(it)
        dw_ref, scr_p, scr_d = next(it), next(it), next(it)
        i = pl.program_id(1)
        p, pp = x1m[...].astype(F32), x1p[...].astype(F32)
        if has_mul:
            p, pp = p * x2m[...], pp * x2p[...]
        scr_p[0:HALO, :] = jnp.where(i > 0, pp, 0.0)
        scr_p[HALO:HALO + T, :] = p
        dcv, dcn = dym[...].astype(F32), dyn[...].astype(F32)
        if has_gate:
            dcv, dcn = dcv * gm[...], dcn * gn[...]
        scr_d[0:T, :] = dcv
        scr_d[T:T + HALO, :] = jnp.where(i < nrow - 1, dcn, 0.0)

        @pl.when(i == 0)
        def _():
            dw_ref[...] = jnp.zeros_like(dw_ref)

        dp = w_ref[K - 1:K, :] * dcv
        cv = w_ref[K - 1:K, :] * p
        dw_ref[K - 1:K, :] += jnp.sum(dcv * p, axis=0, keepdims=True)
        for k in range(K - 1):
            s = K - 1 - k
            dp = dp + w_ref[k:k + 1, :] * scr_d[s:s + T, :]
            sh = scr_p[HALO - s:HALO - s + T, :]
            dw_ref[k:k + 1, :] += jnp.sum(dcv * sh, axis=0, keepdims=True)
            if has_gate:
                cv = cv + w_ref[k:k + 1, :] * sh
        if has_gate:
            dg_ref[...] = (dym[...].astype(F32) * cv).astype(out_dtype)
        if has_mul:
            dx1_ref[...] = (dp * x2m[...]).astype(out_dtype)
            dx2_ref[...] = (dp * x1m[...]).astype(out_dtype)
        else:
            dx1_ref[...] = dp.astype(out_dtype)

    in_specs, args = [], []
    m, p_, _ = _conv_specs(L, T, Cb, o1)
    in_specs += [m, p_]
    args += [x1a, x1a]
    if has_mul:
        m, p_, _ = _conv_specs(L, T, Cb, x2[1])
        in_specs += [m, p_]
        args += [x2[0], x2[0]]
    if has_gate:
        m, _, n_ = _conv_specs(L, T, Cb, gate[1])
        in_specs += [m, n_]
        args += [gate[0], gate[0]]
    m, _, n_ = _conv_specs(L, T, Cb, dy[1])
    in_specs += [m, n_, pl.BlockSpec((K, Cb), lambda j, i: (0, j))]
    args += [dy[0], dy[0], w]
    out = pl.BlockSpec((T, Cb), lambda j, i: (i, j))
    full = jax.ShapeDtypeStruct((L, ncol * Cb), out_dtype)
    n_out = 1 + int(has_mul) + int(has_gate)
    return pl.pallas_call(
        body, name=name, grid=(ncol, nrow), in_specs=in_specs,
        out_specs=[out] * n_out + [pl.BlockSpec((SUBLANES, Cb), lambda j, i: (0, j))],
        out_shape=[full] * n_out + [jax.ShapeDtypeStruct((SUBLANES, ncol * Cb), F32)],
        scratch_shapes=[pltpu.VMEM((T + HALO, Cb), F32), pltpu.VMEM((T + HALO, Cb), F32)],
        compiler_params=_params(("parallel", "arbitrary")),
    )(*args)


GLU_COLS = 256


def _silu(x):
    return x * (1.0 / (1.0 + jnp.exp(-x)))


def _glu(g, v):
    return _silu(g) * v


def _causal_taps(w_ref, scr, first, rows, K):
    acc = w_ref[K - 1:K, :] * scr[first:first + rows, :]
    for k in range(K - 1):
        s = K - 1 - k
        acc = acc + w_ref[k:k + 1, :] * scr[first - s:first - s + rows, :]
    return acc


def _ffn_act_fwd(up, w, name):
    L = up.shape[0]
    T, Cb, K = min(ROW_TILE, L), GLU_COLS, FFN_CONV
    nb = D_FF // Cb

    def body(gm, gp, vm, vp, wg, wv, a_ref, sg, sv):
        i = pl.program_id(1)
        for main, prev, scr in ((gm, gp, sg), (vm, vp, sv)):
            scr[0:HALO, :] = jnp.where(i > 0, prev[...], 0.0)
            scr[HALO:HALO + T, :] = main[...]
        a_ref[...] = _glu(_causal_taps(wg, sg, HALO, T, K), _causal_taps(wv, sv, HALO, T, K)).astype(a_ref.dtype)

    gmain, gprev, _ = _conv_specs(L, T, Cb, 0)
    vmain, vprev, _ = _conv_specs(L, T, Cb, nb)
    return pl.pallas_call(
        body, name=name, grid=(nb, L // T),
        in_specs=[gmain, gprev, vmain, vprev, pl.BlockSpec((K, Cb), lambda j, i: (0, j)), pl.BlockSpec((K, Cb), lambda j, i: (0, nb + j))],
        out_specs=pl.BlockSpec((T, Cb), lambda j, i: (i, j)),
        out_shape=jax.ShapeDtypeStruct((L, D_FF), MXU_DTYPE),
        scratch_shapes=[pltpu.VMEM((T + HALO, Cb), F32), pltpu.VMEM((T + HALO, Cb), F32)],
        compiler_params=_params(("parallel", "arbitrary")),
    )(up, up, up, up, w, w)


def _ffn_act_bwd(up, w, dact, name):
    L = up.shape[0]
    T, Cb, K = min(ROW_TILE, L), GLU_COLS, FFN_CONV
    nb, nrow = D_FF // Cb, L // T

    def body(gm, gp, gn, vm, vp, vn, dam, dan, wg, wv, dg_ref, dv_ref, dwg_ref, dwv_ref, sg, sv, sdg, sdv):
        i = pl.program_id(1)
        for main, prev, nxt, scr in ((gm, gp, gn, sg), (vm, vp, vn, sv)):
            scr[0:HALO, :] = jnp.where(i > 0, prev[...], 0.0)
            scr[HALO:HALO + T, :] = main[...]
            scr[HALO + T:2 * HALO + T, :] = nxt[...]
        ug, uv = _causal_taps(wg, sg, HALO, T + HALO, K), _causal_taps(wv, sv, HALO, T + HALO, K)
        da = jnp.concatenate([dam[...], jnp.where(i < nrow - 1, dan[...], 0.0)], axis=0)
        _, vjp = jax.vjp(_glu, ug, uv)
        sdg[...], sdv[...] = vjp(da)

        @pl.when(i == 0)
        def _():
            dwg_ref[...] = jnp.zeros_like(dwg_ref)
            dwv_ref[...] = jnp.zeros_like(dwv_ref)

        for w_ref, scr, sd, d_ref, dw_ref in ((wg, sg, sdg, dg_ref, dwg_ref), (wv, sv, sdv, dv_ref, dwv_ref)):
            du = sd[0:T, :]
            dp = w_ref[K - 1:K, :] * du
            dw_ref[K - 1:K, :] += jnp.sum(du * scr[HALO:HALO + T, :], axis=0, keepdims=True)
            for k in range(K - 1):
                s = K - 1 - k
                dp = dp + w_ref[k:k + 1, :] * sd[s:s + T, :]
                dw_ref[k:k + 1, :] += jnp.sum(du * scr[HALO - s:HALO - s + T, :], axis=0, keepdims=True)
            d_ref[...] = dp.astype(d_ref.dtype)

    gmain, gprev, gnext = _conv_specs(L, T, Cb, 0)
    vmain, vprev, vnext = _conv_specs(L, T, Cb, nb)
    dmain, _, dnext = _conv_specs(L, T, Cb, 0)
    out = pl.BlockSpec((T, Cb), lambda j, i: (i, j))
    dwb = pl.BlockSpec((SUBLANES, Cb), lambda j, i: (0, j))
    half = jax.ShapeDtypeStruct((L, D_FF), MXU_DTYPE)
    dwh = jax.ShapeDtypeStruct((SUBLANES, D_FF), F32)
    dg, dv, dwg, dwv = pl.pallas_call(
        body, name=name, grid=(nb, nrow),
        in_specs=[gmain, gprev, gnext, vmain, vprev, vnext, dmain, dnext,
                  pl.BlockSpec((K, Cb), lambda j, i: (0, j)), pl.BlockSpec((K, Cb), lambda j, i: (0, nb + j))],
        out_specs=[out, out, dwb, dwb], out_shape=[half, half, dwh, dwh],
        scratch_shapes=[pltpu.VMEM((T + 2 * HALO, Cb), F32), pltpu.VMEM((T + 2 * HALO, Cb), F32),
                        pltpu.VMEM((T + HALO, Cb), F32), pltpu.VMEM((T + HALO, Cb), F32)],
        compiler_params=_params(("parallel", "arbitrary")),
    )(up, up, up, up, up, up, dact, dact, w, w)
    return jnp.concatenate([dg, dv], axis=1), jnp.concatenate([dwg, dwv], axis=1)


def _bdot_raw(a, b, mode):
    return lax.dot_general(a.astype(MXU_DTYPE), b.astype(MXU_DTYPE), _DIMS[mode], preferred_element_type=F32)


@functools.partial(jax.custom_vjp, nondiff_argnums=(2,))
def _bdot(a, b, mode):
    return _bdot_raw(a, b, mode)


def _bdot_fwd(a, b, mode):
    return _bdot_raw(a, b, mode), (a, b)


def _bdot_bwd(mode, res, ct):
    a, b = res
    if mode == "nn":
        return _bdot_raw(ct, b, "nt"), _bdot_raw(a, ct, "tn")
    if mode == "nt":
        return _bdot_raw(ct, b, "nn"), _bdot_raw(ct, a, "tn")
    return _bdot_raw(b, ct, "nt"), _bdot_raw(a, ct, "nn")


_bdot.defvjp(_bdot_fwd, _bdot_bwd)


def _hdot(a, b, mode="nn"):
    return lax.dot_general(a, b, _DIMS[mode], precision=lax.Precision.HIGH, preferred_element_type=F32)


@jax.custom_vjp
def _inv_unit_lower(a):
    C = a.shape[0]
    eye = (lax.broadcasted_iota(jnp.int32, (C, C), 0) == lax.broadcasted_iota(jnp.int32, (C, C), 1)).astype(F32)
    t = eye - a
    p = a
    n = 1
    while 2 * n < C:
        p = _hdot(p, p)
        t = t + _hdot(t, p)
        n *= 2
    return t


def _inv_fwd(a):
    t = _inv_unit_lower(a)
    return t, t


def _inv_bwd(t, ct):
    return (-_hdot(_hdot(t, ct, "tn"), t, "nt"),)


_inv_unit_lower.defvjp(_inv_fwd, _inv_bwd)


def _softplus(x):
    return jnp.maximum(x, 0.0) + jnp.log(1.0 + jnp.exp(-jnp.abs(x)))


def _sigmoid(x):
    return 1.0 / (1.0 + jnp.exp(-x))


def _pick_lane(blk, lane):
    ids = lax.broadcasted_iota(jnp.int32, blk.shape, 1)
    return jnp.sum(jnp.where(ids == lane, blk, 0.0), axis=1, keepdims=True)


def _gdn_chunk(cq, ck, cv, gz, gab, alog, dtb, wn, S, h):
    C, Dk = cq.shape
    ga, gb = _pick_lane(gab, h), _pick_lane(gab, GDN_HEADS + h)
    al, db = _pick_lane(alog, h), _pick_lane(dtb, h)
    q, k, v = _silu(cq), _silu(ck), _silu(cv)
    q = q * lax.rsqrt(jnp.sum(q * q, axis=-1, keepdims=True) + NORM_EPS) * (Dk ** -0.5)
    k = k * lax.rsqrt(jnp.sum(k * k, axis=-1, keepdims=True) + NORM_EPS)
    beta = _sigmoid(gb)
    g = -jnp.exp(al) * _softplus(ga + db)
    row = lax.broadcasted_iota(jnp.int32, (C, C), 0)
    col = lax.broadcasted_iota(jnp.int32, (C, C), 1)
    causal, strict = row >= col, row > col
    gcb = _hdot(causal.astype(F32), jnp.broadcast_to(g, (C, Dk)))
    first = (lax.broadcasted_iota(jnp.int32, (C, Dk), 1) == 0).astype(F32)
    gr = _hdot(first, gcb, "nt")
    gc = _pick_lane(gcb, 0)
    decay = jnp.where(causal, jnp.exp(jnp.where(causal, gc - gr, 0.0)), 0.0)
    kb = k * beta
    lower = jnp.where(strict, _bdot(kb, k, "nt") * decay, 0.0)
    t = _inv_unit_lower(lower)
    egc = jnp.exp(gc)
    u = _hdot(t, v * beta)
    w = _hdot(t, kb * egc)
    attn = jnp.where(causal, _bdot(q, k, "nt") * decay, 0.0)
    v_new = u - _bdot(w, S, "nn")
    o = _bdot(q * egc, S, "nn") + _bdot(attn, v_new, "nn")
    g_last = jnp.sum(jnp.where(lax.broadcasted_iota(jnp.int32, (C, 1), 0) == C - 1, gc, 0.0), axis=0, keepdims=True)
    S_new = S * jnp.exp(g_last) + _bdot(k * jnp.exp(g_last - gc), v_new, "tn")
    y = o * lax.rsqrt(jnp.mean(o * o, axis=-1, keepdims=True) + NORM_EPS) * wn * _silu(gz)
    return y, S_new


def _gdn_fwd(cqkv, proj, alog, dtb, wn, name):
    L = cqkv.shape[0]
    C, H, Dh = GDN_CHUNK, GDN_HEADS, GDN_HEAD_DIM
    N = L // C

    def body(c_ref, gz_ref, gab_ref, al_ref, db_ref, wn_ref, y_ref, sall_ref, S):
        n = pl.program_id(0)

        @pl.when(n == 0)
        def _():
            S[...] = jnp.zeros_like(S)

        gab = gab_ref[...]
        for h in range(H):
            s_in = S[h]
            sall_ref[0, h] = s_in
            y, s_new = _gdn_chunk(c_ref[:, h * Dh:(h + 1) * Dh], c_ref[:, (H + h) * Dh:(H + h + 1) * Dh],
                                  c_ref[:, (2 * H + h) * Dh:(2 * H + h + 1) * Dh], gz_ref[:, h * Dh:(h + 1) * Dh],
                                  gab, al_ref[...], db_ref[...], wn_ref[...], s_in, h)
            y_ref[:, h * Dh:(h + 1) * Dh] = y.astype(y_ref.dtype)
            S[h] = s_new

    vec = pl.BlockSpec((1, LANES), lambda n: (0, 0))
    return pl.pallas_call(
        body, name=name, grid=(N,),
        in_specs=[pl.BlockSpec((C, 3 * GDN_WIDTH), lambda n: (n, 0)),
                  pl.BlockSpec((C, GDN_WIDTH), lambda n: (n, P_GZ // GDN_WIDTH)),
                  pl.BlockSpec((C, LANES), lambda n: (n, P_GAB // LANES)), vec, vec, vec],
        out_specs=[pl.BlockSpec((C, GDN_WIDTH), lambda n: (n, 0)), pl.BlockSpec((1, H, Dh, Dh), lambda n: (n, 0, 0, 0))],
        out_shape=[jax.ShapeDtypeStruct((L, GDN_WIDTH), MXU_DTYPE), jax.ShapeDtypeStruct((N, H, Dh, Dh), F32)],
        scratch_shapes=[pltpu.VMEM((H, Dh, Dh), F32)],
        compiler_params=_params(("arbitrary",)),
    )(cqkv, proj, proj, alog, dtb, wn)


def _gdn_bwd(cqkv, proj, alog, dtb, wn, s_all, dy, name):
    L = cqkv.shape[0]
    C, H, Dh = GDN_CHUNK, GDN_HEADS, GDN_HEAD_DIM
    N = L // C

    def body(c_ref, gz_ref, gab_ref, al_ref, db_ref, wn_ref, sall_ref, dy_ref,
             dc_ref, dgz_ref, dgab_ref, dal_ref, ddb_ref, dwn_ref, dS):
        n = pl.program_id(0)

        @pl.when(n == 0)
        def _():
            dS[...] = jnp.zeros_like(dS)
            dal_ref[...] = jnp.zeros_like(dal_ref)
            ddb_ref[...] = jnp.zeros_like(ddb_ref)
            dwn_ref[...] = jnp.zeros_like(dwn_ref)

        gab = gab_ref[...]
        dgab = jnp.zeros_like(gab)
        dal, ddb, dwn = jnp.zeros((1, LANES), F32), jnp.zeros((1, LANES), F32), jnp.zeros((1, LANES), F32)
        for h in range(H):
            f = functools.partial(_gdn_chunk, h=h)
            _, vjp = jax.vjp(f, c_ref[:, h * Dh:(h + 1) * Dh], c_ref[:, (H + h) * Dh:(H + h + 1) * Dh],
                             c_ref[:, (2 * H + h) * Dh:(2 * H + h + 1) * Dh], gz_ref[:, h * Dh:(h + 1) * Dh],
                             gab, al_ref[...], db_ref[...], wn_ref[...], sall_ref[0, h])
            dq, dk, dv, dgz, dgab_h, dal_h, ddb_h, dwn_h, ds = vjp((dy_ref[:, h * Dh:(h + 1) * Dh], dS[h]))
            dc_ref[:, h * Dh:(h + 1) * Dh] = dq
            dc_ref[:, (H + h) * Dh:(H + h + 1) * Dh] = dk
            dc_ref[:, (2 * H + h) * Dh:(2 * H + h + 1) * Dh] = dv
            dgz_ref[:, h * Dh:(h + 1) * Dh] = dgz.astype(dgz_ref.dtype)
            dS[h] = ds
            dgab, dal, ddb, dwn = dgab + dgab_h, dal + dal_h, ddb + ddb_h, dwn + dwn_h
        dgab_ref[...] = dgab
        dal_ref[...] += dal
        ddb_ref[...] += ddb
        dwn_ref[...] += dwn

    vec = pl.BlockSpec((1, LANES), lambda n: (0, 0))
    rev = lambda n: N - 1 - n
    return pl.pallas_call(
        body, name=name, grid=(N,),
        in_specs=[pl.BlockSpec((C, 3 * GDN_WIDTH), lambda n: (rev(n), 0)),
                  pl.BlockSpec((C, GDN_WIDTH), lambda n: (rev(n), P_GZ // GDN_WIDTH)),
                  pl.BlockSpec((C, LANES), lambda n: (rev(n), P_GAB // LANES)), vec, vec, vec,
                  pl.BlockSpec((1, H, Dh, Dh), lambda n: (rev(n), 0, 0, 0)),
                  pl.BlockSpec((C, GDN_WIDTH), lambda n: (rev(n), Y_GDN // GDN_WIDTH))],
        out_specs=[pl.BlockSpec((C, 3 * GDN_WIDTH), lambda n: (rev(n), 0)),
                   pl.BlockSpec((C, GDN_WIDTH), lambda n: (rev(n), 0)),
                   pl.BlockSpec((C, LANES), lambda n: (rev(n), 0)), vec, vec, vec],
        out_shape=[jax.ShapeDtypeStruct((L, 3 * GDN_WIDTH), F32), jax.ShapeDtypeStruct((L, GDN_WIDTH), MXU_DTYPE),
                   jax.ShapeDtypeStruct((L, LANES), F32)] + [jax.ShapeDtypeStruct((1, LANES), F32)] * 3,
        scratch_shapes=[pltpu.VMEM((H, Dh, Dh), F32)],
        compiler_params=_params(("arbitrary",)),
    )(cqkv, proj, proj, alog, dtb, wn, s_all, dy)


def _split_dot(x, m):
    R = x.shape[0]
    hi = x.astype(MXU_DTYPE)
    lo = (x - hi.astype(F32)).astype(MXU_DTYPE)
    both = jnp.dot(jnp.concatenate([hi, lo], axis=0), m, preferred_element_type=F32)
    return both[:R] + both[R:]


def _sb_kv_blocks(kv_ref, js):
    B = SB_BLOCK
    rows = [pl.ds(pl.multiple_of(j * B, B), B) for j in js]
    kps = [[kv_ref[r, p * LANES:(p + 1) * LANES] for p in range(SB_HEADS // 2)] for r in rows]
    vps = [[kv_ref[r, SB_WIDTH + p * LANES:SB_WIDTH + (p + 1) * LANES] for p in range(SB_HEADS // 2)] for r in rows]
    return rows, kps, vps


def _sb_pair_dots(x, mats, mode):
    B = SB_BLOCK
    return jnp.concatenate([lax.dot_general(x[2 * p * B:(2 * p + 2) * B], m, _DIMS[mode], preferred_element_type=F32)
                            for mp in mats for p, m in enumerate(mp)], axis=0)


def _sb_logits(qx, kps):
    z = _sb_pair_dots(qx, kps, "nt") * (SB_HEAD_DIM ** -0.5)
    t = jnp.exp(-jnp.abs(z))
    lb = jnp.minimum(z, 0.0) - jnp.log(1.0 + t)
    return z, t, lb


def _sb_running(start, sums, inclusive):
    R = start.shape[0]
    n = sums.shape[0] // R
    vals, cur = [], start
    for b in range(n):
        nxt = cur + sums[b * R:(b + 1) * R]
        vals.append(nxt if inclusive else cur)
        cur = nxt
    return (vals[0] if n == 1 else jnp.concatenate(vals, axis=0)), cur


def _sb_head_masks():
    low = lax.broadcasted_iota(jnp.int32, (SB_BLOCK, LANES), 1) < SB_HEAD_DIM
    return [low if h % 2 == 0 else jnp.logical_not(low) for h in range(SB_HEADS)]


def _sb_stack_heads(ref):
    mine = _sb_head_masks()
    return jnp.concatenate([jnp.where(mine[h], ref[:, (h // 2) * LANES:(h // 2 + 1) * LANES], 0.0).astype(MXU_DTYPE)
                            for h in range(SB_HEADS)], axis=0)


def _sb_block_masks():
    B = SB_BLOCK
    row = lax.broadcasted_iota(jnp.int32, (B, B), 0)
    col = lax.broadcasted_iota(jnp.int32, (B, B), 1)
    row4 = lax.broadcasted_iota(jnp.int32, (SB_HEADS * B, B), 0) & (B - 1)
    col4 = lax.broadcasted_iota(jnp.int32, (SB_HEADS * B, B), 1)
    return (row > col).astype(MXU_DTYPE), (row < col).astype(MXU_DTYPE), col4 < row4


def _sb_fwd(proj, kv, name):
    L = proj.shape[0]
    B, H = SB_BLOCK, SB_HEADS

    def body(q_ref, kv_ref, y_ref, c_ref):
        i = pl.program_id(0)
        low = _sb_head_masks()[0]
        after, _, strict = _sb_block_masks()
        qx = _sb_stack_heads(q_ref)

        def sweep(js, c, accs, masked):
            _, kps, vps = _sb_kv_blocks(kv_ref, js)
            z, _, lb = _sb_logits(qx, kps)
            lom = lb - z
            if masked:
                lom = jnp.where(strict, lom, 0.0)
            before_block, c = _sb_running(c, jnp.sum(lom, axis=1, keepdims=True), False)
            a = jnp.exp(lb + _split_dot(lom, after) + before_block)
            if masked:
                a = jnp.where(strict, a, 0.0)
            a = a.astype(MXU_DTYPE)
            new_accs = list(accs)
            for b in range(len(js)):
                o = _sb_pair_dots(a[b * H * B:(b + 1) * H * B], [vps[b]], "nn")
                for p in range(H // 2):
                    new_accs[p] = new_accs[p] + jnp.where(low, o[2 * p * B:(2 * p + 1) * B], o[(2 * p + 1) * B:(2 * p + 2) * B])
            return c, new_accs

        c, accs = sweep([i], jnp.zeros((H * B, 1), F32), [jnp.zeros((B, LANES), F32)] * (H // 2), True)

        W = SB_SWEEP

        def wide(it, carry):
            j = i - 1 - W * it
            c, accs = sweep([j - b for b in range(W)], carry[0], list(carry[1:]), False)
            return (c,) + tuple(accs)

        def one(it, carry):
            c, accs = sweep([i % W - 1 - it], carry[0], list(carry[1:]), False)
            return (c,) + tuple(accs)

        carry = lax.fori_loop(0, i // W, wide, (c,) + tuple(accs))
        carry = lax.fori_loop(0, i % W, one, carry)
        for p in range(H // 2):
            y_ref[:, p * LANES:(p + 1) * LANES] = carry[1 + p].astype(y_ref.dtype)
        lane = lax.broadcasted_iota(jnp.int32, (B, LANES), 1)
        ct = jnp.zeros((B, LANES), F32)
        for h in range(H):
            ct = jnp.where(lane == h, carry[0][h * B:(h + 1) * B], ct)
        c_ref[...] = ct

    return pl.pallas_call(
        body, name=name, grid=(L // B,),
        in_specs=[pl.BlockSpec((B, SB_WIDTH), lambda i: (i, P_SB // SB_WIDTH)), pl.BlockSpec((L, 2 * SB_WIDTH), lambda i: (0, 0))],
        out_specs=[pl.BlockSpec((B, SB_WIDTH), lambda i: (i, 0)), pl.BlockSpec((B, LANES), lambda i: (i, 0))],
        out_shape=[jax.ShapeDtypeStruct((L, SB_WIDTH), MXU_DTYPE), jax.ShapeDtypeStruct((L, LANES), F32)],
        compiler_params=_params(("arbitrary",)),
    )(proj, kv)


def _sb_bwd(proj, kv, dy, ctot, name):
    L = proj.shape[0]
    B, H = SB_BLOCK, SB_HEADS
    nblk = L // B
    scale = SB_HEAD_DIM ** -0.5

    def body(q_ref, kv_ref, do_ref, ct_ref, dq_ref, dk_hbm, dv_hbm, dk_acc, dv_acc):
        i = pl.program_id(0)

        @pl.when(i == 0)
        def _():
            dk_acc[...] = jnp.zeros_like(dk_acc)
            dv_acc[...] = jnp.zeros_like(dv_acc)

        low = _sb_head_masks()[0]
        after, before, strict = _sb_block_masks()
        qx, dox = _sb_stack_heads(q_ref), _sb_stack_heads(do_ref)
        ct = ct_ref[...]
        ctot = jnp.concatenate([_pick_lane(ct, h) for h in range(H)], axis=0)

        def sweep(js, p, e, dqs, masked):
            n = len(js)
            rows, kps, vps = _sb_kv_blocks(kv_ref, js)
            z, t, lb = _sb_logits(qx, kps)
            r = 1.0 / (1.0 + t)
            sig = jnp.where(z >= 0, r, t * r)
            lom = lb - z
            if masked:
                lom = jnp.where(strict, lom, 0.0)
            through_block, p = _sb_running(p, jnp.sum(lom, axis=1, keepdims=True), True)
            right_of_block = (ctot if n == 1 else jnp.concatenate([ctot] * n, axis=0)) - through_block
            a = jnp.exp(lb + _split_dot(lom, after) + right_of_block)
            if masked:
                a = jnp.where(strict, a, 0.0)
            ea = _sb_pair_dots(dox, vps, "nt") * a
            left_of_block, e = _sb_running(e, jnp.sum(ea, axis=1, keepdims=True), False)
            dlom = left_of_block + _split_dot(ea, before)
            if masked:
                dlom = jnp.where(strict, dlom, 0.0)
            dz = ((ea * (1.0 - sig) - dlom * sig) * scale).astype(MXU_DTYPE)
            ab = a.astype(MXU_DTYPE)
            new_dq = list(dqs)
            for b in range(n):
                for pr in range(H // 2):
                    heads = slice(2 * pr * B, (2 * pr + 2) * B)
                    both, cols = slice((b * H + 2 * pr) * B, (b * H + 2 * pr + 2) * B), slice(pr * LANES, (pr + 1) * LANES)
                    dqp = jnp.dot(dz[both], kps[b][pr], preferred_element_type=F32)
                    new_dq[pr] = new_dq[pr] + jnp.where(low, dqp[:B], dqp[B:])
                    dk_acc[rows[b], cols] += lax.dot_general(dz[both], qx[heads], _DIMS["tn"], preferred_element_type=F32)
                    dv_acc[rows[b], cols] += lax.dot_general(ab[both], dox[heads], _DIMS["tn"], preferred_element_type=F32)
            return p, e, new_dq

        W = SB_SWEEP

        def wide(it, carry):
            p, e, dqs = sweep([W * it + b for b in range(W)], carry[0], carry[1], list(carry[2:]), False)
            return (p, e) + tuple(dqs)

        def one(it, carry):
            p, e, dqs = sweep([i - i % W + it], carry[0], carry[1], list(carry[2:]), False)
            return (p, e) + tuple(dqs)

        zero = jnp.zeros((H * B, 1), F32)
        carry = lax.fori_loop(0, i // W, wide, (zero, zero) + (jnp.zeros((B, LANES), F32),) * (H // 2))
        carry = lax.fori_loop(0, i % W, one, carry)
        _, _, dqs = sweep([i], carry[0], carry[1], list(carry[2:]), True)
        for pr in range(H // 2):
            dq_ref[:, pr * LANES:(pr + 1) * LANES] = dqs[pr].astype(dq_ref.dtype)

        @pl.when(i == nblk - 1)
        def _():
            pltpu.sync_copy(dk_acc, dk_hbm)
            pltpu.sync_copy(dv_acc, dv_hbm)

    hbm = pl.BlockSpec(memory_space=pl.ANY)
    acc = jax.ShapeDtypeStruct((L, SB_WIDTH), F32)
    return pl.pallas_call(
        body, name=name, grid=(nblk,),
        in_specs=[pl.BlockSpec((B, SB_WIDTH), lambda i: (i, P_SB // SB_WIDTH)), pl.BlockSpec((L, 2 * SB_WIDTH), lambda i: (0, 0)),
                  pl.BlockSpec((B, SB_WIDTH), lambda i: (i, Y_SB // SB_WIDTH)), pl.BlockSpec((B, LANES), lambda i: (i, 0))],
        out_specs=[pl.BlockSpec((B, SB_WIDTH), lambda i: (i, 0)), hbm, hbm],
        out_shape=[jax.ShapeDtypeStruct((L, SB_WIDTH), MXU_DTYPE), acc, acc],
        scratch_shapes=[pltpu.VMEM((L, SB_WIDTH), F32), pltpu.VMEM((L, SB_WIDTH), F32)],
        compiler_params=_params(("arbitrary",)),
    )(proj, kv, dy, ctot)


def _prep_w_in(w):
    sc, qkv, gz, gab, sb = w[:, 0:768], w[:, 768:2304], w[:, 2304:2816], w[:, 2816:2824], w[:, 2824:3592]
    pad = jnp.zeros((w.shape[0], P_END - D_IN_PROJ), w.dtype)
    return jnp.concatenate([qkv, gz, sc, sb, gab, pad], axis=1).astype(MXU_DTYPE)


def _unprep_dw_in(dw):
    qkv, gz, sc, sb, gab = dw[:, P_QKV:P_GZ], dw[:, P_GZ:P_SC], dw[:, P_SC:P_SB], dw[:, P_SB:P_GAB], dw[:, P_GAB:P_GAB + 8]
    return jnp.concatenate([sc, qkv, gz, gab, sb], axis=1)


def _prep_w_out(w):
    return jnp.concatenate([w[256:768], w[0:256], w[768:]], axis=0).astype(MXU_DTYPE)


def _unprep_dw_out(dw):
    return jnp.concatenate([dw[512:768], dw[0:512], dw[768:]], axis=0)


def _pad_lanes(v):
    return jnp.zeros((1, LANES), F32).at[0, :v.shape[0]].set(v)


def _layer_fwd(x, p, l):
    L = x.shape[0]
    tm = min(1024, L)
    n = f"l{l}_"
    h = _rms_fwd(x, p["norm_mix"], n + "rms_mix")
    proj = _matmul(h, p["w_in"], "nn", F32, n + "mm_in", tm, 768, 1024)
    cb = SC_WIDTH
    y_sc = _conv_fwd((proj, P_SC // cb + 1), p["w_sconv"], SC_KERNEL, cb, 1, MXU_DTYPE, n + "sconv",
                     x2=(proj, P_SC // cb + 2), gate=(proj, P_SC // cb))
    cqkv = _conv_fwd((proj, 0), p["w_gconv"], GDN_CONV, 256, 6, F32, n + "gconv")
    y_gdn, s_all = _gdn_fwd(cqkv, proj, p["a_log"], p["dt_bias"], p["gdn_norm"], n + "gdn")
    kv = proj[:, P_SB + SB_WIDTH:P_SB + 3 * SB_WIDTH].astype(MXU_DTYPE)
    y_sb, ctot = _sb_fwd(proj, kv, n + "sb")
    ycat = jnp.concatenate([y_gdn, y_sc, y_sb], axis=1)
    x1 = _matmul(ycat, p["w_out"], "nn", F32, n + "mm_out", tm, 512, 1024, resid=x)
    h2 = _rms_fwd(x1, p["norm_ffn"], n + "rms_ffn")
    up = _matmul(h2, p["w_up"], "nn", F32, n + "mm_up", tm, 512, 1024)
    act = _ffn_act_fwd(up, p["w_fconv"], n + "ffn_act")
    x2 = _matmul(act, p["w_down"], "nn", F32, n + "mm_down", tm, 512, 1408, resid=x1)
    saved = dict(x=x, h=h, proj=proj, cqkv=cqkv, s_all=s_all, kv=kv, ctot=ctot, ycat=ycat, x1=x1, h2=h2, up=up, act=act)
    return x2, saved


def _layer_bwd(dx2, p, s, l):
    L = dx2.shape[0]
    tm, tkl = min(1024, L), min(1024, L)
    n = f"l{l}_"
    g = {}
    dx2b = dx2.astype(MXU_DTYPE)
    g["w_ffn_down"] = _matmul(s["act"], dx2b, "tn", F32, n + "mm_ddown", 1408, 512, tkl)
    dact = _matmul(dx2b, p["w_down"], "nt", F32, n + "mm_dact", tm, 1408, 1024)
    dup, dwf = _ffn_act_bwd(s["up"], p["w_fconv"], dact, n + "dffn_act")
    g["w_ffn_conv"] = dwf[:FFN_CONV]
    g["w_ffn_up"] = _matmul(s["h2"], dup, "tn", F32, n + "mm_dup", 1024, 512, tkl)
    dh2 = _matmul(dup, p["w_up"], "nt", F32, n + "mm_dh2", tm, 512, 1408)
    dx1, dwn = _rms_bwd(s["x1"], p["norm_ffn"], dh2, dx2, n + "drms_ffn")
    g["w_norm_ffn"] = dwn[0]

    dx1b = dx1.astype(MXU_DTYPE)
    g["w_mix_out"] = _unprep_dw_out(_matmul(s["ycat"], dx1b, "tn", F32, n + "mm_dout", 1024, 512, tkl))
    dycat = _matmul(dx1b, p["w_out"], "nt", F32, n + "mm_dycat", tm, 512, 1024)
    proj = s["proj"]
    cb = SC_WIDTH
    dsc_c, dsc_h, dsc_b, dws = _conv_bwd((proj, P_SC // cb + 1), p["w_sconv"], (dycat, Y_SC // cb), SC_KERNEL, cb, 1,
                                         MXU_DTYPE, n + "dsconv", x2=(proj, P_SC // cb + 2), gate=(proj, P_SC // cb))
    g["w_sconv"] = dws[:SC_KERNEL]
    dcqkv, dgz, dgab, dal, ddb, dgn = _gdn_bwd(s["cqkv"], proj, p["a_log"], p["dt_bias"], p["gdn_norm"], s["s_all"], dycat,
                                               n + "dgdn")
    g["gdn_a_log"], g["gdn_dt_bias"], g["w_gdn_norm"] = dal[0, :GDN_HEADS], ddb[0, :GDN_HEADS], dgn[0]
    dqkv, dwg = _conv_bwd((proj, 0), p["w_gconv"], (dcqkv, 0), GDN_CONV, 256, 6, MXU_DTYPE, n + "dgconv")
    g["w_gdn_conv"] = dwg[:GDN_CONV]
    dq, dk, dv = _sb_bwd(proj, s["kv"], dycat, s["ctot"], n + "dsb")
    dproj = jnp.concatenate(
        [dqkv, dgz, dsc_b, dsc_c, dsc_h, dq, dk.astype(MXU_DTYPE), dv.astype(MXU_DTYPE), dgab.astype(MXU_DTYPE),
         jnp.zeros((L, P_END - P_GAB - LANES), MXU_DTYPE)], axis=1)
    g["w_mix_in"] = _unprep_dw_in(_matmul(s["h"], dproj, "tn", F32, n + "mm_din", 1024, 768, tkl))
    dh = _matmul(dproj, p["w_in"], "nt", F32, n + "mm_dh", tm, 512, 768)
    dx, dwm = _rms_bwd(s["x"], p["norm_mix"], dh, dx1, n + "drms_mix")
    g["w_norm_mix"] = dwm[0]
    return dx, g


WEIGHTS = ["w_norm_mix", "w_mix_in", "w_sconv", "w_gdn_conv", "gdn_a_log", "gdn_dt_bias", "w_gdn_norm", "w_mix_out",
           "w_norm_ffn", "w_ffn_up", "w_ffn_conv", "w_ffn_down", "w_norm_final"]


def _local_step(x, w, target):
    layers = []
    for l in range(DEPTH):
        layers.append(dict(
            norm_mix=w["w_norm_mix"][l][None], w_in=_prep_w_in(w["w_mix_in"][l]), w_sconv=w["w_sconv"][l],
            w_gconv=w["w_gdn_conv"][l], a_log=_pad_lanes(w["gdn_a_log"][l]), dt_bias=_pad_lanes(w["gdn_dt_bias"][l]),
            gdn_norm=w["w_gdn_norm"][l][None], w_out=_prep_w_out(w["w_mix_out"][l]), norm_ffn=w["w_norm_ffn"][l][None],
            w_up=w["w_ffn_up"][l].astype(MXU_DTYPE), w_fconv=w["w_ffn_conv"][l], w_down=w["w_ffn_down"][l].astype(MXU_DTYPE)))
    saved = []
    for l in range(DEPTH):
        x, s = _layer_fwd(x, layers[l], l)
        saved.append(s)
    loss, dx, dwf = _loss_head(x, w["w_norm_final"][None], target, "loss_head")
    grads = [None] * DEPTH
    for l in reversed(range(DEPTH)):
        dx, grads[l] = _layer_bwd(dx, layers[l], saved[l], l)
    out = {k: jnp.stack([grads[l][k] for l in range(DEPTH)]) for k in WEIGHTS if k != "w_norm_final"}
    out["w_norm_final"] = dwf[0]
    return loss, dx, out


def _exchange(bufs, scatter, name):
    nb = len(bufs)

    def body(*refs):
        ins, outs = refs[:nb], refs[nb:2 * nb]
        send_sems, recv_sems, local_sems = refs[2 * nb:]
        x, y, c = lax.axis_index("x"), lax.axis_index("y"), lax.axis_index("c")
        me = 4 * x + 2 * y + c
        local = []
        for b in range(nb):
            cp = pltpu.make_async_copy(ins[b].at[me] if scatter[b] else ins[b], outs[b].at[me], local_sems.at[b])
            cp.start()
            local.append(cp)
        remote = []
        for b in range(nb):
            for kk in range(1, N_DEV):
                px = 1 - x if kk & 4 else x
                py = 1 - y if kk & 2 else y
                pc = 1 - c if kk & 1 else c
                src = ins[b].at[4 * px + 2 * py + pc] if scatter[b] else ins[b]
                cp = pltpu.make_async_remote_copy(
                    src_ref=src, dst_ref=outs[b].at[me], send_sem=send_sems.at[b, kk - 1], recv_sem=recv_sems.at[b, kk - 1],
                    device_id=(px, py, pc), device_id_type=pl.DeviceIdType.MESH)
                cp.start()
                remote.append(cp)
        for cp in remote:
            cp.wait()
        for cp in local:
            cp.wait()

    out_shape = [jax.ShapeDtypeStruct((N_DEV,) + (b.shape[1:] if s else b.shape), b.dtype) for b, s in zip(bufs, scatter)]
    hbm = pl.BlockSpec(memory_space=pl.ANY)
    return pl.pallas_call(
        body, name=name, in_specs=[hbm] * nb, out_specs=[hbm] * nb, out_shape=out_shape,
        scratch_shapes=[pltpu.SemaphoreType.DMA((nb, N_DEV - 1)), pltpu.SemaphoreType.DMA((nb, N_DEV - 1)),
                        pltpu.SemaphoreType.DMA((nb,))],
    )(*bufs)


def _sum_sources(recv, row_tile, name):
    _, R, _ = recv.shape

    def body(r_ref, o_ref):
        acc = r_ref[0].astype(F32)
        for s in range(1, N_DEV):
            acc = acc + r_ref[s].astype(F32)
        o_ref[...] = acc

    return pl.pallas_call(
        body, name=name, grid=(R // row_tile,),
        in_specs=[pl.BlockSpec((N_DEV, row_tile, LANES), lambda i: (0, i, 0))],
        out_specs=pl.BlockSpec((row_tile, LANES), lambda i: (i, 0)),
        out_shape=jax.ShapeDtypeStruct((R, LANES), F32),
        compiler_params=_params(("parallel",)),
    )(recv)


def _adamw_math(g, w, m, v):
    nm = ADAM_B1 * m + (1.0 - ADAM_B1) * g
    nv = ADAM_B2 * v + (1.0 - ADAM_B2) * (g * g)
    m_hat = nm / (1.0 - ADAM_B1 ** ADAM_STEP)
    v_hat = nv / (1.0 - ADAM_B2 ** ADAM_STEP)
    return -ADAM_LR * (m_hat / (jnp.sqrt(v_hat) + ADAM_EPS) + ADAM_WD * w), nm, nv


def _sum_adamw(recv, w, m, v, row_tile, name):
    D0, R, C = w.shape

    def body(r_ref, w_ref, m_ref, v_ref, g_ref, d_ref, nm_ref, nv_ref):
        g = r_ref[0, 0].astype(F32)
        for s in range(1, N_DEV):
            g = g + r_ref[s, 0].astype(F32)
        g_ref[0] = g
        d_ref[0], nm_ref[0], nv_ref[0] = _adamw_math(g, w_ref[0], m_ref[0], v_ref[0])

    blk = pl.BlockSpec((1, row_tile, C), lambda l, i: (l, i, 0))
    out = jax.ShapeDtypeStruct((D0, R, C), F32)
    return pl.pallas_call(
        body, name=name, grid=(D0, R // row_tile),
        in_specs=[pl.BlockSpec((N_DEV, 1, row_tile, C), lambda l, i: (0, l, i, 0)), blk, blk, blk],
        out_specs=[blk] * 4, out_shape=[out] * 4,
        compiler_params=_params(("parallel", "parallel")),
    )(recv, w, m, v)


def _adamw(g, w, m, v, row_tile, name):
    R = g.shape[0]

    def body(g_ref, w_ref, m_ref, v_ref, d_ref, nm_ref, nv_ref):
        d_ref[...], nm_ref[...], nv_ref[...] = _adamw_math(g_ref[...], w_ref[...], m_ref[...], v_ref[...])

    blk = pl.BlockSpec((row_tile, LANES), lambda i: (i, 0))
    out = jax.ShapeDtypeStruct((R, LANES), F32)
    return pl.pallas_call(
        body, name=name, grid=(R // row_tile,), in_specs=[blk] * 4, out_specs=[blk] * 3, out_shape=[out] * 3,
        compiler_params=_params(("parallel",)),
    )(g, w, m, v)


def _pack(arrs, rows, dtype):
    flat = jnp.concatenate([a.reshape(-1).astype(dtype) for a in arrs])
    return jnp.pad(flat, (0, rows * LANES - flat.shape[0])).reshape(rows, LANES)


def _unpack(buf, shapes):
    lead = buf.shape[:-2]
    flat = buf.reshape(lead + (-1,))
    out, off = [], 0
    for shp in shapes:
        n = 1
        for d in shp:
            n *= d
        out.append(flat[..., off:off + n].reshape(lead + tuple(shp)))
        off += n
    return out


BIG = ["w_mix_in", "w_mix_out", "w_ffn_up", "w_ffn_down"]
BIG_AXIS = {"w_mix_in": 2, "w_mix_out": 1, "w_ffn_up": 2, "w_ffn_down": 1}
CONV = ["w_sconv", "w_gdn_conv", "w_ffn_conv"]
REPL = ["w_norm_mix", "gdn_a_log", "gdn_dt_bias", "w_gdn_norm", "w_norm_ffn", "w_norm_final"]
BIG_ROW_TILE = {"w_mix_in": 512, "w_mix_out": 128, "w_ffn_up": 512, "w_ffn_down": 352}
SMALL_ROWS = 416
CONV_ROWS = 48


def kernel(x, w_norm_mix, w_mix_in, w_sconv, w_gdn_conv, gdn_a_log, gdn_dt_bias, w_gdn_norm, w_mix_out, w_norm_ffn, w_ffn_up, w_ffn_conv, w_ffn_down, w_norm_final, loss_target, m_w_norm_mix, m_w_mix_in, m_w_sconv, m_w_gdn_conv, m_gdn_a_log, m_gdn_dt_bias, m_w_gdn_norm, m_w_mix_out, m_w_norm_ffn, m_w_ffn_up, m_w_ffn_conv, m_w_ffn_down, m_w_norm_final, v_w_norm_mix, v_w_mix_in, v_w_sconv, v_w_gdn_conv, v_gdn_a_log, v_gdn_dt_bias, v_w_gdn_norm, v_w_mix_out, v_w_norm_ffn, v_w_ffn_up, v_w_ffn_conv, v_w_ffn_down, v_w_norm_final):
    w = dict(w_norm_mix=w_norm_mix, w_mix_in=w_mix_in, w_sconv=w_sconv, w_gdn_conv=w_gdn_conv, gdn_a_log=gdn_a_log,
             gdn_dt_bias=gdn_dt_bias, w_gdn_norm=w_gdn_norm, w_mix_out=w_mix_out, w_norm_ffn=w_norm_ffn, w_ffn_up=w_ffn_up,
             w_ffn_conv=w_ffn_conv, w_ffn_down=w_ffn_down, w_norm_final=w_norm_final)
    m = dict(w_norm_mix=m_w_norm_mix, w_mix_in=m_w_mix_in, w_sconv=m_w_sconv, w_gdn_conv=m_w_gdn_conv, gdn_a_log=m_gdn_a_log,
             gdn_dt_bias=m_gdn_dt_bias, w_gdn_norm=m_w_gdn_norm, w_mix_out=m_w_mix_out, w_norm_ffn=m_w_norm_ffn,
             w_ffn_up=m_w_ffn_up, w_ffn_conv=m_w_ffn_conv, w_ffn_down=m_w_ffn_down, w_norm_final=m_w_norm_final)
    v = dict(w_norm_mix=v_w_norm_mix, w_mix_in=v_w_mix_in, w_sconv=v_w_sconv, w_gdn_conv=v_w_gdn_conv, gdn_a_log=v_gdn_a_log,
             gdn_dt_bias=v_gdn_dt_bias, w_gdn_norm=v_w_gdn_norm, w_mix_out=v_w_mix_out, w_norm_ffn=v_w_norm_ffn,
             w_ffn_up=v_w_ffn_up, w_ffn_conv=v_w_ffn_conv, w_ffn_down=v_w_ffn_down, w_norm_final=v_w_norm_final)
    me = 4 * lax.axis_index("x") + 2 * lax.axis_index("y") + lax.axis_index("c")
    conv_shapes = [w[k].shape for k in CONV]

    gathered = _exchange([w[k].astype(MXU_DTYPE) for k in BIG] + [_pack([w[k] for k in CONV], CONV_ROWS, F32)],
                         [False] * (len(BIG) + 1), "gather_weights")
    full = dict(w)
    for k, got in zip(BIG, gathered):
        full[k] = jnp.concatenate([got[s] for s in range(N_DEV)], axis=BIG_AXIS[k])
    for k, got in zip(CONV, _unpack(gathered[-1], conv_shapes)):
        full[k] = jnp.concatenate([got[s] for s in range(N_DEV)], axis=2)

    loss, dx, grads = _local_step(x[0], full, loss_target[0])

    small = CONV + REPL
    to_owner = [jnp.stack(jnp.split(grads[k], N_DEV, axis=BIG_AXIS[k])).astype(MXU_DTYPE) for k in BIG]
    received = _exchange(to_owner + [_pack([grads[k] for k in small], SMALL_ROWS, F32)], [True] * len(BIG) + [False],
                         "exchange_grads")
    g, delta, new_m, new_v = {}, {}, {}, {}
    for k, got in zip(BIG, received):
        g[k], delta[k], new_m[k], new_v[k] = _sum_adamw(got, w[k], m[k], v[k], BIG_ROW_TILE[k], "adamw_" + k)
    g_small = _unpack(_sum_sources(received[-1], SMALL_ROWS, "sum_small"), [grads[k].shape for k in small])
    for k, gs in zip(small, g_small):
        g[k] = lax.dynamic_slice_in_dim(gs, me * w[k].shape[2], w[k].shape[2], axis=2) if k in CONV else gs

    small_shapes = [w[k].shape for k in small]
    small_rows = -(-sum(w[k].size for k in small) // (SUBLANES * LANES)) * SUBLANES
    d_sm, m_sm, v_sm = _adamw(_pack([g[k] for k in small], small_rows, F32), _pack([w[k] for k in small], small_rows, F32),
                              _pack([m[k] for k in small], small_rows, F32), _pack([v[k] for k in small], small_rows, F32),
                              small_rows, "adamw_small")
    for dst, small_buf in ((delta, d_sm), (new_m, m_sm), (new_v, v_sm)):
        dst.update(zip(small, _unpack(small_buf, small_shapes)))

    loss_all = lax.psum(loss[0, 0], ("x", "y", "c"))
    return (loss_all, dx[None], *[g[k] for k in WEIGHTS], *[delta[k] for k in WEIGHTS], *[new_m[k] for k in WEIGHTS],
            *[new_v[k] for k in WEIGHTS])
```

```python
import functools

import jax
import jax.numpy as jnp
from jax import lax
from jax.experimental import pallas as pl
from jax.experimental.pallas import tpu as pltpu

F32 = jnp.float32
MXU_DTYPE = jnp.bfloat16
HIGHEST = lax.Precision.HIGHEST

D_MODEL = 1024
DEPTH = 2
SC_WIDTH = 256
SC_KERNEL = 3
GDN_WIDTH = 512
GDN_HEADS = 4
GDN_HEAD_DIM = 128
GDN_CONV = 4
GDN_CHUNK = 64
SB_WIDTH = 256
SB_HEADS = 4
SB_HEAD_DIM = 64
SB_BLOCK = 128
SB_SWEEP = 4
D_FF = 2816
FFN_CONV = 3
NORM_EPS = 1e-6
D_IN_PROJ = 3592
ADAM_LR, ADAM_B1, ADAM_B2, ADAM_EPS, ADAM_WD, ADAM_STEP = 0.001, 0.9, 0.999, 1e-08, 0.01, 10

N_DEV = 8
LANES = 128
SUBLANES = 8
VMEM_LIMIT = 48 * 1024 * 1024

P_QKV, P_GZ, P_SC, P_SB, P_GAB, P_END = 0, 1536, 2048, 2816, 3584, 3840
Y_GDN, Y_SC, Y_SB = 0, 512, 768


def _params(semantics):
    return pltpu.CompilerParams(dimension_semantics=semantics, vmem_limit_bytes=VMEM_LIMIT)


_DIMS = {"nn": (((1,), (0,)), ((), ())), "nt": (((1,), (1,)), ((), ())), "tn": (((0,), (0,)), ((), ()))}


def _matmul(a, b, mode, out_dtype, name, tm, tn, tk, resid=None):
    if mode == "tn":
        (K, M), (K2, N) = a.shape, b.shape
    elif mode == "nt":
        (M, K), (N, K2) = a.shape, b.shape
    else:
        (M, K), (K2, N) = a.shape, b.shape
    assert K == K2 and M % tm == 0 and N % tn == 0 and K % tk == 0, (name, a.shape, b.shape, tm, tn, tk)
    nk = K // tk
    has_resid = resid is not None

    def body(*refs):
        if has_resid:
            a_ref, b_ref, r_ref, o_ref, acc = refs
        else:
            a_ref, b_ref, o_ref, acc = refs
        k = pl.program_id(2)

        @pl.when(k == 0)
        def _():
            acc[...] = jnp.zeros_like(acc)

        acc[...] += lax.dot_general(a_ref[...], b_ref[...], _DIMS[mode], preferred_element_type=F32)

        @pl.when(k == nk - 1)
        def _():
            r = acc[...]
            if has_resid:
                r = r + r_ref[...]
            o_ref[...] = r.astype(out_dtype)

    a_spec = pl.BlockSpec((tk, tm), lambda i, j, k: (k, i)) if mode == "tn" else pl.BlockSpec((tm, tk), lambda i, j, k: (i, k))
    b_spec = pl.BlockSpec((tn, tk), lambda i, j, k: (j, k)) if mode == "nt" else pl.BlockSpec((tk, tn), lambda i, j, k: (k, j))
    o_spec = pl.BlockSpec((tm, tn), lambda i, j, k: (i, j))
    in_specs = [a_spec, b_spec] + ([o_spec] if has_resid else [])
    args = (a, b) + ((resid,) if has_resid else ())
    return pl.pallas_call(
        body, name=name, grid=(M // tm, N // tn, nk), in_specs=in_specs, out_specs=o_spec,
        out_shape=jax.ShapeDtypeStruct((M, N), out_dtype),
        scratch_shapes=[pltpu.VMEM((tm, tn), F32)],
        compiler_params=_params(("parallel", "parallel", "arbitrary")),
    )(*args)


def _rms(x, w):
    return x * lax.rsqrt(jnp.mean(x * x, axis=-1, keepdims=True) + NORM_EPS) * w


ROW_TILE = 512


def _rms_fwd(x, w, name):
    L, Dm = x.shape

    def body(x_ref, w_ref, h_ref):
        h_ref[...] = _rms(x_ref[...], w_ref[...]).astype(h_ref.dtype)

    return pl.pallas_call(
        body, name=name, grid=(L // ROW_TILE,),
        in_specs=[pl.BlockSpec((ROW_TILE, Dm), lambda i: (i, 0)), pl.BlockSpec((1, Dm), lambda i: (0, 0))],
        out_specs=pl.BlockSpec((ROW_TILE, Dm), lambda i: (i, 0)),
        out_shape=jax.ShapeDtypeStruct((L, Dm), MXU_DTYPE),
        compiler_params=_params(("parallel",)),
    )(x, w)


def _rms_bwd(x, w, dh, dres, name):
    L, Dm = x.shape

    def body(x_ref, w_ref, dh_ref, dres_ref, dx_ref, dw_ref):
        _, vjp = jax.vjp(_rms, x_ref[...], w_ref[...])
        dx, dw = vjp(dh_ref[...])
        dx_ref[...] = dres_ref[...] + dx

        @pl.when(pl.program_id(0) == 0)
        def _():
            dw_ref[...] = jnp.zeros_like(dw_ref)

        dw_ref[...] += dw

    row = pl.BlockSpec((ROW_TILE, Dm), lambda i: (i, 0))
    vec = pl.BlockSpec((1, Dm), lambda i: (0, 0))
    return pl.pallas_call(
        body, name=name, grid=(L // ROW_TILE,), in_specs=[row, vec, row, row], out_specs=[row, vec],
        out_shape=[jax.ShapeDtypeStruct((L, Dm), F32), jax.ShapeDtypeStruct((1, Dm), F32)],
        compiler_params=_params(("arbitrary",)),
    )(x, w, dh, dres)


def _loss_head(x, w, target, name):
    L, Dm = x.shape

    def block_loss(xb, wb, tb):
        err = _rms(xb, wb) - tb
        return 0.5 * jnp.sum(jnp.sum(err * err, axis=-1, keepdims=True) * (1.0 / Dm), axis=0, keepdims=True)

    def body(x_ref, w_ref, t_ref, loss_ref, dx_ref, dw_ref):
        val, vjp = jax.vjp(lambda xb, wb: block_loss(xb, wb, t_ref[...]), x_ref[...], w_ref[...])
        dx, dw = vjp(jnp.ones_like(val))
        dx_ref[...] = dx

        @pl.when(pl.program_id(0) == 0)
        def _():
            dw_ref[...] = jnp.zeros_like(dw_ref)
            loss_ref[...] = jnp.zeros_like(loss_ref)

        dw_ref[...] += dw
        loss_ref[...] += val

    row = pl.BlockSpec((ROW_TILE, Dm), lambda i: (i, 0))
    vec = pl.BlockSpec((1, Dm), lambda i: (0, 0))
    one = pl.BlockSpec((1, 1), lambda i: (0, 0))
    return pl.pallas_call(
        body, name=name, grid=(L // ROW_TILE,), in_specs=[row, vec, row], out_specs=[one, row, vec],
        out_shape=[jax.ShapeDtypeStruct((1, 1), F32), jax.ShapeDtypeStruct((L, Dm), F32), jax.ShapeDtypeStruct((1, Dm), F32)],
        compiler_params=_params(("arbitrary",)),
    )(x, w, target)


HALO = SUBLANES


def _conv_specs(L, T, Cb, off):
    main = pl.BlockSpec((T, Cb), lambda j, i: (i, off + j))
    prev = pl.BlockSpec((HALO, Cb), lambda j, i: (jnp.maximum(i * (T // HALO) - 1, 0), off + j))
    nxt = pl.BlockSpec((HALO, Cb), lambda j, i: (jnp.minimum((i + 1) * (T // HALO), L // HALO - 1), off + j))
    return main, prev, nxt


def _conv_fwd(x1, w, K, Cb, ncol, out_dtype, name, x2=None, gate=None):
    (x1a, o1) = x1
    L = x1a.shape[0]
    T = min(ROW_TILE, L)
    has_mul, has_gate = x2 is not None, gate is not None

    def body(*refs):
        it = iter(refs)
        x1m, x1p = next(it), next(it)
        if has_mul:
            x2m, x2p = next(it), next(it)
        if has_gate:
            gm = next(it)
        w_ref, y_ref, scr = next(it), next(it), next(it)
        i = pl.program_id(1)
        p, pp = x1m[...].astype(F32), x1p[...].astype(F32)
        if has_mul:
            p, pp = p * x2m[...], pp * x2p[...]
        scr[0:HALO, :] = jnp.where(i > 0, pp, 0.0)
        scr[HALO:HALO + T, :] = p
        acc = w_ref[K - 1:K, :] * p
        for k in range(K - 1):
            s = K - 1 - k
            acc = acc + w_ref[k:k + 1, :] * scr[HALO - s:HALO - s + T, :]
        if has_gate:
            acc = acc * gm[...]
        y_ref[...] = acc.astype(out_dtype)

    in_specs, args = [], []
    m, p_, _ = _conv_specs(L, T, Cb, o1)
    in_specs += [m, p_]
    args += [x1a, x1a]
    if has_mul:
        m, p_, _ = _conv_specs(L, T, Cb, x2[1])
        in_specs += [m, p_]
        args += [x2[0], x2[0]]
    if has_gate:
        m, _, _ = _conv_specs(L, T, Cb, gate[1])
        in_specs += [m]
        args += [gate[0]]
    in_specs.append(pl.BlockSpec((K, Cb), lambda j, i: (0, j)))
    args.append(w)
    return pl.pallas_call(
        body, name=name, grid=(ncol, L // T), in_specs=in_specs,
        out_specs=pl.BlockSpec((T, Cb), lambda j, i: (i, j)),
        out_shape=jax.ShapeDtypeStruct((L, ncol * Cb), out_dtype),
        scratch_shapes=[pltpu.VMEM((T + HALO, Cb), F32)],
        compiler_params=_params(("parallel", "arbitrary")),
    )(*args)


def _conv_bwd(x1, w, dy, K, Cb, ncol, out_dtype, name, x2=None, gate=None):
    (x1a, o1) = x1
    L = x1a.shape[0]
    T = min(ROW_TILE, L)
    nrow = L // T
    has_mul, has_gate = x2 is not None, gate is not None

    def body(*refs):
        it = iter(refs)
        x1m, x1p = next(it), next(it)
        if has_mul:
            x2m, x2p = next(it), next(it)
        if has_gate:
            gm, gn = next(it), next(it)
        dym, dyn, w_ref = next(it), next(it), next(it)
        dx1_ref = next(it)
        if has_mul:
            dx2_ref = next(it)
        if has_gate:
            dg_ref = next(it)
        dw_ref, scr_p, scr_d = next(it), next(it), next(it)
        i = pl.program_id(1)
        p, pp = x1m[...].astype(F32), x1p[...].astype(F32)
        if has_mul:
            p, pp = p * x2m[...], pp * x2p[...]
        scr_p[0:HALO, :] = jnp.where(i > 0, pp, 0.0)
        scr_p[HALO:HALO + T, :] = p
        dcv, dcn = dym[...].astype(F32), dyn[...].astype(F32)
        if has_gate:
            dcv, dcn = dcv * gm[...], dcn * gn[...]
        scr_d[0:T, :] = dcv
        scr_d[T:T + HALO, :] = jnp.where(i < nrow - 1, dcn, 0.0)

        @pl.when(i == 0)
        def _():
            dw_ref[...] = jnp.zeros_like(dw_ref)

        dp = w_ref[K - 1:K, :] * dcv
        cv = w_ref[K - 1:K, :] * p
        dw_ref[K - 1:K, :] += jnp.sum(dcv * p, axis=0, keepdims=True)
        for k in range(K - 1):
            s = K - 1 - k
            dp = dp + w_ref[k:k + 1, :] * scr_d[s:s + T, :]
            sh = scr_p[HALO - s:HALO - s + T, :]
            dw_ref[k:k + 1, :] += jnp.sum(dcv * sh, axis=0, keepdims=True)
            if has_gate:
                cv = cv + w_ref[k:k + 1, :] * sh
        if has_gate:
            dg_ref[...] = (dym[...].astype(F32) * cv).astype(out_dtype)
        if has_mul:
            dx1_ref[...] = (dp * x2m[...]).astype(out_dtype)
            dx2_ref[...] = (dp * x1m[...]).astype(out_dtype)
        else:
            dx1_ref[...] = dp.astype(out_dtype)

    in_specs, args = [], []
    m, p_, _ = _conv_specs(L, T, Cb, o1)
    in_specs += [m, p_]
    args += [x1a, x1a]
    if has_mul:
        m, p_, _ = _conv_specs(L, T, Cb, x2[1])
        in_specs += [m, p_]
        args += [x2[0], x2[0]]
    if has_gate:
        m, _, n_ = _conv_specs(L, T, Cb, gate[1])
        in_specs += [m, n_]
        args += [gate[0], gate[0]]
    m, _, n_ = _conv_specs(L, T, Cb, dy[1])
    in_specs += [m, n_, pl.BlockSpec((K, Cb), lambda j, i: (0, j))]
    args += [dy[0], dy[0], w]
    out = pl.BlockSpec((T, Cb), lambda j, i: (i, j))
    full = jax.ShapeDtypeStruct((L, ncol * Cb), out_dtype)
    n_out = 1 + int(has_mul) + int(has_gate)
    return pl.pallas_call(
        body, name=name, grid=(ncol, nrow), in_specs=in_specs,
        out_specs=[out] * n_out + [pl.BlockSpec((SUBLANES, Cb), lambda j, i: (0, j))],
        out_shape=[full] * n_out + [jax.ShapeDtypeStruct((SUBLANES, ncol * Cb), F32)],
        scratch_shapes=[pltpu.VMEM((T + HALO, Cb), F32), pltpu.VMEM((T + HALO, Cb), F32)],
        compiler_params=_params(("parallel", "arbitrary")),
    )(*args)


GLU_COLS = 256


def _silu(x):
    return x * (1.0 / (1.0 + jnp.exp(-x)))


def _glu(g, v):
    return _silu(g) * v


def _causal_taps(w_ref, scr, first, rows, K):
    acc = w_ref[K - 1:K, :] * scr[first:first + rows, :]
    for k in range(K - 1):
        s = K - 1 - k
        acc = acc + w_ref[k:k + 1, :] * scr[first - s:first - s + rows, :]
    return acc


def _ffn_act_fwd(up, w, name):
    L = up.shape[0]
    T, Cb, K = min(ROW_TILE, L), GLU_COLS, FFN_CONV
    nb = D_FF // Cb

    def body(gm, gp, vm, vp, wg, wv, a_ref, sg, sv):
        i = pl.program_id(1)
        for main, prev, scr in ((gm, gp, sg), (vm, vp, sv)):
            scr[0:HALO, :] = jnp.where(i > 0, prev[...], 0.0)
            scr[HALO:HALO + T, :] = main[...]
        a_ref[...] = _glu(_causal_taps(wg, sg, HALO, T, K), _causal_taps(wv, sv, HALO, T, K)).astype(a_ref.dtype)

    gmain, gprev, _ = _conv_specs(L, T, Cb, 0)
    vmain, vprev, _ = _conv_specs(L, T, Cb, nb)
    return pl.pallas_call(
        body, name=name, grid=(nb, L // T),
        in_specs=[gmain, gprev, vmain, vprev, pl.BlockSpec((K, Cb), lambda j, i: (0, j)), pl.BlockSpec((K, Cb), lambda j, i: (0, nb + j))],
        out_specs=pl.BlockSpec((T, Cb), lambda j, i: (i, j)),
        out_shape=jax.ShapeDtypeStruct((L, D_FF), MXU_DTYPE),
        scratch_shapes=[pltpu.VMEM((T + HALO, Cb), F32), pltpu.VMEM((T + HALO, Cb), F32)],
        compiler_params=_params(("parallel", "arbitrary")),
    )(up, up, up, up, w, w)


def _ffn_act_bwd(up, w, dact, name):
    L = up.shape[0]
    T, Cb, K = min(ROW_TILE, L), GLU_COLS, FFN_CONV
    nb, nrow = D_FF // Cb, L // T

    def body(gm, gp, gn, vm, vp, vn, dam, dan, wg, wv, dg_ref, dv_ref, dwg_ref, dwv_ref, sg, sv, sdg, sdv):
        i = pl.program_id(1)
        for main, prev, nxt, scr in ((gm, gp, gn, sg), (vm, vp, vn, sv)):
            scr[0:HALO, :] = jnp.where(i > 0, prev[...], 0.0)
            scr[HALO:HALO + T, :] = main[...]
            scr[HALO + T:2 * HALO + T, :] = nxt[...]
        ug, uv = _causal_taps(wg, sg, HALO, T + HALO, K), _causal_taps(wv, sv, HALO, T + HALO, K)
        da = jnp.concatenate([dam[...], jnp.where(i < nrow - 1, dan[...], 0.0)], axis=0)
        _, vjp = jax.vjp(_glu, ug, uv)
        sdg[...], sdv[...] = vjp(da)

        @pl.when(i == 0)
        def _():
            dwg_ref[...] = jnp.zeros_like(dwg_ref)
            dwv_ref[...] = jnp.zeros_like(dwv_ref)

        for w_ref, scr, sd, d_ref, dw_ref in ((wg, sg, sdg, dg_ref, dwg_ref), (wv, sv, sdv, dv_ref, dwv_ref)):
            du = sd[0:T, :]
            dp = w_ref[K - 1:K, :] * du
            dw_ref[K - 1:K, :] += jnp.sum(du * scr[HALO:HALO + T, :], axis=0, keepdims=True)
            for k in range(K - 1):
                s = K - 1 - k
                dp = dp + w_ref[k:k + 1, :] * sd[s:s + T, :]
                dw_ref[k:k + 1, :] += jnp.sum(du * scr[HALO - s:HALO - s + T, :], axis=0, keepdims=True)
            d_ref[...] = dp.astype(d_ref.dtype)

    gmain, gprev, gnext = _conv_specs(L, T, Cb, 0)
    vmain, vprev, vnext = _conv_specs(L, T, Cb, nb)
    dmain, _, dnext = _conv_specs(L, T, Cb, 0)
    out = pl.BlockSpec((T, Cb), lambda j, i: (i, j))
    dwb = pl.BlockSpec((SUBLANES, Cb), lambda j, i: (0, j))
    half = jax.ShapeDtypeStruct((L, D_FF), MXU_DTYPE)
    dwh = jax.ShapeDtypeStruct((SUBLANES, D_FF), F32)
    dg, dv, dwg, dwv = pl.pallas_call(
        body, name=name, grid=(nb, nrow),
        in_specs=[gmain, gprev, gnext, vmain, vprev, vnext, dmain, dnext,
                  pl.BlockSpec((K, Cb), lambda j, i: (0, j)), pl.BlockSpec((K, Cb), lambda j, i: (0, nb + j))],
        out_specs=[out, out, dwb, dwb], out_shape=[half, half, dwh, dwh],
        scratch_shapes=[pltpu.VMEM((T + 2 * HALO, Cb), F32), pltpu.VMEM((T + 2 * HALO, Cb), F32),
                        pltpu.VMEM((T + HALO, Cb), F32), pltpu.VMEM((T + HALO, Cb), F32)],
        compiler_params=_params(("parallel", "arbitrary")),
    )(up, up, up, up, up, up, dact, dact, w, w)
    return jnp.concatenate([dg, dv], axis=1), jnp.concatenate([dwg, dwv], axis=1)


def _bdot_raw(a, b, mode):
    return lax.dot_general(a.astype(MXU_DTYPE), b.astype(MXU_DTYPE), _DIMS[mode], preferred_element_type=F32)


@functools.partial(jax.custom_vjp, nondiff_argnums=(2,))
def _bdot(a, b, mode):
    return _bdot_raw(a, b, mode)


def _bdot_fwd(a, b, mode):
    return _bdot_raw(a, b, mode), (a, b)


def _bdot_bwd(mode, res, ct):
    a, b = res
    if mode == "nn":
        return _bdot_raw(ct, b, "nt"), _bdot_raw(a, ct, "tn")
    if mode == "nt":
        return _bdot_raw(ct, b, "nn"), _bdot_raw(ct, a, "tn")
    return _bdot_raw(b, ct, "nt"), _bdot_raw(a, ct, "nn")


_bdot.defvjp(_bdot_fwd, _bdot_bwd)


def _hdot(a, b, mode="nn"):
    return lax.dot_general(a, b, _DIMS[mode], precision=lax.Precision.HIGH, preferred_element_type=F32)


@jax.custom_vjp
def _inv_unit_lower(a):
    R = a.shape[0]
    eye = (lax.broadcasted_iota(jnp.int32, (R, R), 0) == lax.broadcasted_iota(jnp.int32, (R, R), 1)).astype(F32)
    t = eye - a
    p = a
    n = 1
    while 2 * n < GDN_CHUNK:
        p = _hdot(p, p)
        t = t + _hdot(t, p)
        n *= 2
    return t


def _inv_fwd(a):
    t = _inv_unit_lower(a)
    return t, t


def _inv_bwd(t, ct):
    return (-_hdot(_hdot(t, ct, "tn"), t, "nt"),)


_inv_unit_lower.defvjp(_inv_fwd, _inv_bwd)


def _softplus(x):
    return jnp.maximum(x, 0.0) + jnp.log(1.0 + jnp.exp(-jnp.abs(x)))


def _sigmoid(x):
    return 1.0 / (1.0 + jnp.exp(-x))


def _pick_lane(blk, lane):
    ids = lax.broadcasted_iota(jnp.int32, blk.shape, 1)
    return jnp.sum(jnp.where(ids == lane, blk, 0.0), axis=1, keepdims=True)


def _gdn_chunk(cq, ck, cv, gz, gab, alog, dtb, wn, S):
    C, H, Dk = GDN_CHUNK, GDN_HEADS, GDN_HEAD_DIM
    R = H * C
    rows_of = lambda vals, n: jnp.concatenate([jnp.broadcast_to(x, (n, 1)) for x in vals], axis=0)
    ga = jnp.concatenate([_pick_lane(gab, h) for h in range(H)], axis=0)
    gb = jnp.concatenate([_pick_lane(gab, H + h) for h in range(H)], axis=0)
    al = rows_of([_pick_lane(alog, h) for h in range(H)], C)
    db = rows_of([_pick_lane(dtb, h) for h in range(H)], C)
    q, k, v = _silu(cq), _silu(ck), _silu(cv)
    q = q * lax.rsqrt(jnp.sum(q * q, axis=-1, keepdims=True) + NORM_EPS) * (Dk ** -0.5)
    k = k * lax.rsqrt(jnp.sum(k * k, axis=-1, keepdims=True) + NORM_EPS)
    beta = _sigmoid(gb)
    g = -jnp.exp(al) * _softplus(ga + db)
    row = lax.broadcasted_iota(jnp.int32, (R, R), 0)
    col = lax.broadcasted_iota(jnp.int32, (R, R), 1)
    same_head = (row // C) == (col // C)
    causal, strict = same_head & (row >= col), same_head & (row > col)
    gcb = _hdot(causal.astype(F32), jnp.broadcast_to(g, (R, Dk)))
    first = (lax.broadcasted_iota(jnp.int32, (R, Dk), 1) == 0).astype(F32)
    gr = _hdot(first, gcb, "nt")
    gc = _pick_lane(gcb, 0)
    decay = jnp.where(causal, jnp.exp(jnp.where(causal, gc - gr, 0.0)), 0.0)
    kb = k * beta
    lower = jnp.where(strict, _bdot(kb, k, "nt") * decay, 0.0)
    t = _inv_unit_lower(lower)
    egc = jnp.exp(gc)
    u = _hdot(t, v * beta)
    w = _hdot(t, kb * egc)
    attn = jnp.where(causal, _bdot(q, k, "nt") * decay, 0.0)
    own = (lax.broadcasted_iota(jnp.int32, (R, H * Dk), 0) // C) == (lax.broadcasted_iota(jnp.int32, (R, H * Dk), 1) // Dk)
    spread = lambda x: jnp.where(own, jnp.concatenate([x] * H, axis=1), 0.0)
    v_new = u - _bdot(spread(w), S, "nn")
    o = _bdot(spread(q * egc), S, "nn") + _bdot(attn, v_new, "nn")
    last = lax.broadcasted_iota(jnp.int32, (R, 1), 0)
    g_last = [jnp.sum(jnp.where(last == h * C + C - 1, gc, 0.0), axis=0, keepdims=True) for h in range(H)]
    S_new = S * jnp.exp(rows_of(g_last, Dk)) + _bdot(spread(k * jnp.exp(rows_of(g_last, C) - gc)), v_new, "tn")
    y = o * lax.rsqrt(jnp.mean(o * o, axis=-1, keepdims=True) + NORM_EPS) * wn * _silu(gz)
    return y, S_new


def _stack_heads(ref, first, width=GDN_HEAD_DIM):
    return jnp.concatenate([ref[:, first + h * width:first + (h + 1) * width] for h in range(GDN_HEADS)], axis=0)


def _gdn_fwd(cqkv, proj, alog, dtb, wn, name):
    L = cqkv.shape[0]
    C, H, Dh = GDN_CHUNK, GDN_HEADS, GDN_HEAD_DIM
    N = L // C

    def body(c_ref, gz_ref, gab_ref, al_ref, db_ref, wn_ref, y_ref, sall_ref, S):
        n = pl.program_id(0)

        @pl.when(n == 0)
        def _():
            S[...] = jnp.zeros_like(S)

        s_in = S[...]
        sall_ref[0] = s_in
        y, s_new = _gdn_chunk(_stack_heads(c_ref, 0), _stack_heads(c_ref, GDN_WIDTH), _stack_heads(c_ref, 2 * GDN_WIDTH),
                              _stack_heads(gz_ref, 0), gab_ref[...], al_ref[...], db_ref[...], wn_ref[...], s_in)
        for h in range(H):
            y_ref[:, h * Dh:(h + 1) * Dh] = y[h * C:(h + 1) * C].astype(y_ref.dtype)
        S[...] = s_new

    vec = pl.BlockSpec((1, LANES), lambda n: (0, 0))
    return pl.pallas_call(
        body, name=name, grid=(N,),
        in_specs=[pl.BlockSpec((C, 3 * GDN_WIDTH), lambda n: (n, 0)),
                  pl.BlockSpec((C, GDN_WIDTH), lambda n: (n, P_GZ // GDN_WIDTH)),
                  pl.BlockSpec((C, LANES), lambda n: (n, P_GAB // LANES)), vec, vec, vec],
        out_specs=[pl.BlockSpec((C, GDN_WIDTH), lambda n: (n, 0)), pl.BlockSpec((1, H * Dh, Dh), lambda n: (n, 0, 0))],
        out_shape=[jax.ShapeDtypeStruct((L, GDN_WIDTH), MXU_DTYPE), jax.ShapeDtypeStruct((N, H * Dh, Dh), F32)],
        scratch_shapes=[pltpu.VMEM((H * Dh, Dh), F32)],
        compiler_params=_params(("arbitrary",)),
    )(cqkv, proj, proj, alog, dtb, wn)


def _gdn_bwd(cqkv, proj, alog, dtb, wn, s_all, dy, name):
    L = cqkv.shape[0]
    C, H, Dh = GDN_CHUNK, GDN_HEADS, GDN_HEAD_DIM
    N = L // C

    def body(c_ref, gz_ref, gab_ref, al_ref, db_ref, wn_ref, sall_ref, dy_ref,
             dc_ref, dgz_ref, dgab_ref, dal_ref, ddb_ref, dwn_ref, dS):
        n = pl.program_id(0)

        @pl.when(n == 0)
        def _():
            dS[...] = jnp.zeros_like(dS)
            dal_ref[...] = jnp.zeros_like(dal_ref)
            ddb_ref[...] = jnp.zeros_like(ddb_ref)
            dwn_ref[...] = jnp.zeros_like(dwn_ref)

        _, vjp = jax.vjp(_gdn_chunk, _stack_heads(c_ref, 0), _stack_heads(c_ref, GDN_WIDTH), _stack_heads(c_ref, 2 * GDN_WIDTH),
                         _stack_heads(gz_ref, 0), gab_ref[...], al_ref[...], db_ref[...], wn_ref[...], sall_ref[0])
        dq, dk, dv, dgz, dgab, dal, ddb, dwn, ds = vjp((_stack_heads(dy_ref, 0), dS[...]))
        for h in range(H):
            rows = slice(h * C, (h + 1) * C)
            dc_ref[:, h * Dh:(h + 1) * Dh] = dq[rows]
            dc_ref[:, (H + h) * Dh:(H + h + 1) * Dh] = dk[rows]
            dc_ref[:, (2 * H + h) * Dh:(2 * H + h + 1) * Dh] = dv[rows]
            dgz_ref[:, h * Dh:(h + 1) * Dh] = dgz[rows].astype(dgz_ref.dtype)
        dS[...] = ds
        dgab_ref[...] = dgab
        dal_ref[...] += dal
        ddb_ref[...] += ddb
        dwn_ref[...] += dwn

    vec = pl.BlockSpec((1, LANES), lambda n: (0, 0))
    rev = lambda n: N - 1 - n
    return pl.pallas_call(
        body, name=name, grid=(N,),
        in_specs=[pl.BlockSpec((C, 3 * GDN_WIDTH), lambda n: (rev(n), 0)),
                  pl.BlockSpec((C, GDN_WIDTH), lambda n: (rev(n), P_GZ // GDN_WIDTH)),
                  pl.BlockSpec((C, LANES), lambda n: (rev(n), P_GAB // LANES)), vec, vec, vec,
                  pl.BlockSpec((1, H * Dh, Dh), lambda n: (rev(n), 0, 0)),
                  pl.BlockSpec((C, GDN_WIDTH), lambda n: (rev(n), Y_GDN // GDN_WIDTH))],
        out_specs=[pl.BlockSpec((C, 3 * GDN_WIDTH), lambda n: (rev(n), 0)),
                   pl.BlockSpec((C, GDN_WIDTH), lambda n: (rev(n), 0)),
                   pl.BlockSpec((C, LANES), lambda n: (rev(n), 0)), vec, vec, vec],
        out_shape=[jax.ShapeDtypeStruct((L, 3 * GDN_WIDTH), F32), jax.ShapeDtypeStruct((L, GDN_WIDTH), MXU_DTYPE),
                   jax.ShapeDtypeStruct((L, LANES), F32)] + [jax.ShapeDtypeStruct((1, LANES), F32)] * 3,
        scratch_shapes=[pltpu.VMEM((H * Dh, Dh), F32)],
        compiler_params=_params(("arbitrary",)),
    )(cqkv, proj, proj, alog, dtb, wn, s_all, dy)


def _split_dot(x, m):
    R = x.shape[0]
    hi = x.astype(MXU_DTYPE)
    lo = (x - hi.astype(F32)).astype(MXU_DTYPE)
    both = jnp.dot(jnp.concatenate([hi, lo], axis=0), m, preferred_element_type=F32)
    return both[:R] + both[R:]


def _sb_kv_blocks(kv_ref, js):
    B = SB_BLOCK
    rows = [pl.ds(pl.multiple_of(j * B, B), B) for j in js]
    kps = [[kv_ref[r, p * LANES:(p + 1) * LANES] for p in range(SB_HEADS // 2)] for r in rows]
    vps = [[kv_ref[r, SB_WIDTH + p * LANES:SB_WIDTH + (p + 1) * LANES] for p in range(SB_HEADS // 2)] for r in rows]
    return rows, kps, vps


def _sb_pair_dots(x, mats, mode):
    B = SB_BLOCK
    return jnp.concatenate([lax.dot_general(x[2 * p * B:(2 * p + 2) * B], m, _DIMS[mode], preferred_element_type=F32)
                            for mp in mats for p, m in enumerate(mp)], axis=0)


def _sb_logits(qx, kps):
    z = _sb_pair_dots(qx, kps, "nt") * (SB_HEAD_DIM ** -0.5)
    t = jnp.exp(-jnp.abs(z))
    lb = jnp.minimum(z, 0.0) - jnp.log(1.0 + t)
    return z, t, lb


def _sb_running(start, sums, inclusive):
    R = start.shape[0]
    n = sums.shape[0] // R
    vals, cur = [], start
    for b in range(n):
        nxt = cur + sums[b * R:(b + 1) * R]
        vals.append(nxt if inclusive else cur)
        cur = nxt
    return (vals[0] if n == 1 else jnp.concatenate(vals, axis=0)), cur


def _sb_head_masks():
    low = lax.broadcasted_iota(jnp.int32, (SB_BLOCK, LANES), 1) < SB_HEAD_DIM
    return [low if h % 2 == 0 else jnp.logical_not(low) for h in range(SB_HEADS)]


def _sb_stack_heads(ref):
    mine = _sb_head_masks()
    return jnp.concatenate([jnp.where(mine[h], ref[:, (h // 2) * LANES:(h // 2 + 1) * LANES], 0.0).astype(MXU_DTYPE)
                            for h in range(SB_HEADS)], axis=0)


def _sb_block_masks():
    B = SB_BLOCK
    row = lax.broadcasted_iota(jnp.int32, (B, B), 0)
    col = lax.broadcasted_iota(jnp.int32, (B, B), 1)
    row4 = lax.broadcasted_iota(jnp.int32, (SB_HEADS * B, B), 0) & (B - 1)
    col4 = lax.broadcasted_iota(jnp.int32, (SB_HEADS * B, B), 1)
    return (row > col).astype(MXU_DTYPE), (row < col).astype(MXU_DTYPE), col4 < row4


def _sb_fwd(proj, kv, name):
    L = proj.shape[0]
    B, H = SB_BLOCK, SB_HEADS

    def body(q_ref, kv_ref, y_ref, c_ref):
        i = pl.program_id(0)
        low = _sb_head_masks()[0]
        after, _, strict = _sb_block_masks()
        qx = _sb_stack_heads(q_ref)

        def sweep(js, c, accs, masked):
            _, kps, vps = _sb_kv_blocks(kv_ref, js)
            z, _, lb = _sb_logits(qx, kps)
            lom = lb - z
            if masked:
                lom = jnp.where(strict, lom, 0.0)
            before_block, c = _sb_running(c, jnp.sum(lom, axis=1, keepdims=True), False)
            a = jnp.exp(lb + _split_dot(lom, after) + before_block)
            if masked:
                a = jnp.where(strict, a, 0.0)
            a = a.astype(MXU_DTYPE)
            new_accs = list(accs)
            for b in range(len(js)):
                o = _sb_pair_dots(a[b * H * B:(b + 1) * H * B], [vps[b]], "nn")
                for p in range(H // 2):
                    new_accs[p] = new_accs[p] + jnp.where(low, o[2 * p * B:(2 * p + 1) * B], o[(2 * p + 1) * B:(2 * p + 2) * B])
            return c, new_accs

        c, accs = sweep([i], jnp.zeros((H * B, 1), F32), [jnp.zeros((B, LANES), F32)] * (H // 2), True)

        W = SB_SWEEP

        def wide(it, carry):
            j = i - 1 - W * it
            c, accs = sweep([j - b for b in range(W)], carry[0], list(carry[1:]), False)
            return (c,) + tuple(accs)

        def one(it, carry):
            c, accs = sweep([i % W - 1 - it], carry[0], list(carry[1:]), False)
            return (c,) + tuple(accs)

        carry = lax.fori_loop(0, i // W, wide, (c,) + tuple(accs))
        carry = lax.fori_loop(0, i % W, one, carry)
        for p in range(H // 2):
            y_ref[:, p * LANES:(p + 1) * LANES] = carry[1 + p].astype(y_ref.dtype)
        lane = lax.broadcasted_iota(jnp.int32, (B, LANES), 1)
        ct = jnp.zeros((B, LANES), F32)
        for h in range(H):
            ct = jnp.where(lane == h, carry[0][h * B:(h + 1) * B], ct)
        c_ref[...] = ct

    return pl.pallas_call(
        body, name=name, grid=(L // B,),
        in_specs=[pl.BlockSpec((B, SB_WIDTH), lambda i: (i, P_SB // SB_WIDTH)), pl.BlockSpec((L, 2 * SB_WIDTH), lambda i: (0, 0))],
        out_specs=[pl.BlockSpec((B, SB_WIDTH), lambda i: (i, 0)), pl.BlockSpec((B, LANES), lambda i: (i, 0))],
        out_shape=[jax.ShapeDtypeStruct((L, SB_WIDTH), MXU_DTYPE), jax.ShapeDtypeStruct((L, LANES), F32)],
        compiler_params=_params(("arbitrary",)),
    )(proj, kv)


def _sb_bwd(proj, kv, dy, ctot, name):
    L = proj.shape[0]
    B, H = SB_BLOCK, SB_HEADS
    nblk = L // B
    scale = SB_HEAD_DIM ** -0.5

    def body(q_ref, kv_ref, do_ref, ct_ref, dq_ref, dk_hbm, dv_hbm, dk_acc, dv_acc):
        i = pl.program_id(0)

        @pl.when(i == 0)
        def _():
            dk_acc[...] = jnp.zeros_like(dk_acc)
            dv_acc[...] = jnp.zeros_like(dv_acc)

        low = _sb_head_masks()[0]
        after, before, strict = _sb_block_masks()
        qx, dox = _sb_stack_heads(q_ref), _sb_stack_heads(do_ref)
        ct = ct_ref[...]
        ctot = jnp.concatenate([_pick_lane(ct, h) for h in range(H)], axis=0)

        def sweep(js, p, e, dqs, masked):
            n = len(js)
            rows, kps, vps = _sb_kv_blocks(kv_ref, js)
            z, t, lb = _sb_logits(qx, kps)
            r = 1.0 / (1.0 + t)
            sig = jnp.where(z >= 0, r, t * r)
            lom = lb - z
            if masked:
                lom = jnp.where(strict, lom, 0.0)
            through_block, p = _sb_running(p, jnp.sum(lom, axis=1, keepdims=True), True)
            right_of_block = (ctot if n == 1 else jnp.concatenate([ctot] * n, axis=0)) - through_block
            a = jnp.exp(lb + _split_dot(lom, after) + right_of_block)
            if masked:
                a = jnp.where(strict, a, 0.0)
            ea = _sb_pair_dots(dox, vps, "nt") * a
            left_of_block, e = _sb_running(e, jnp.sum(ea, axis=1, keepdims=True), False)
            dlom = left_of_block + _split_dot(ea, before)
            if masked:
                dlom = jnp.where(strict, dlom, 0.0)
            dz = ((ea * (1.0 - sig) - dlom * sig) * scale).astype(MXU_DTYPE)
            ab = a.astype(MXU_DTYPE)
            new_dq = list(dqs)
            for b in range(n):
                for pr in range(H // 2):
                    heads = slice(2 * pr * B, (2 * pr + 2) * B)
                    both, cols = slice((b * H + 2 * pr) * B, (b * H + 2 * pr + 2) * B), slice(pr * LANES, (pr + 1) * LANES)
                    dqp = jnp.dot(dz[both], kps[b][pr], preferred_element_type=F32)
                    new_dq[pr] = new_dq[pr] + jnp.where(low, dqp[:B], dqp[B:])
                    dk_acc[rows[b], cols] += lax.dot_general(dz[both], qx[heads], _DIMS["tn"], preferred_element_type=F32)
                    dv_acc[rows[b], cols] += lax.dot_general(ab[both], dox[heads], _DIMS["tn"], preferred_element_type=F32)
            return p, e, new_dq

        W = SB_SWEEP

        def wide(it, carry):
            p, e, dqs = sweep([W * it + b for b in range(W)], carry[0], carry[1], list(carry[2:]), False)
            return (p, e) + tuple(dqs)

        def one(it, carry):
            p, e, dqs = sweep([i - i % W + it], carry[0], carry[1], list(carry[2:]), False)
            return (p, e) + tuple(dqs)

        zero = jnp.zeros((H * B, 1), F32)
        carry = lax.fori_loop(0, i // W, wide, (zero, zero) + (jnp.zeros((B, LANES), F32),) * (H // 2))
        carry = lax.fori_loop(0, i % W, one, carry)
        _, _, dqs = sweep([i], carry[0], carry[1], list(carry[2:]), True)
        for pr in range(H // 2):
            dq_ref[:, pr * LANES:(pr + 1) * LANES] = dqs[pr].astype(dq_ref.dtype)

        @pl.when(i == nblk - 1)
        def _():
            pltpu.sync_copy(dk_acc, dk_hbm)
            pltpu.sync_copy(dv_acc, dv_hbm)

    hbm = pl.BlockSpec(memory_space=pl.ANY)
    acc = jax.ShapeDtypeStruct((L, SB_WIDTH), F32)
    return pl.pallas_call(
        body, name=name, grid=(nblk,),
        in_specs=[pl.BlockSpec((B, SB_WIDTH), lambda i: (i, P_SB // SB_WIDTH)), pl.BlockSpec((L, 2 * SB_WIDTH), lambda i: (0, 0)),
                  pl.BlockSpec((B, SB_WIDTH), lambda i: (i, Y_SB // SB_WIDTH)), pl.BlockSpec((B, LANES), lambda i: (i, 0))],
        out_specs=[pl.BlockSpec((B, SB_WIDTH), lambda i: (i, 0)), hbm, hbm],
        out_shape=[jax.ShapeDtypeStruct((L, SB_WIDTH), MXU_DTYPE), acc, acc],
        scratch_shapes=[pltpu.VMEM((L, SB_WIDTH), F32), pltpu.VMEM((L, SB_WIDTH), F32)],
        compiler_params=_params(("arbitrary",)),
    )(proj, kv, dy, ctot)


def _prep_w_in(w):
    sc, qkv, gz, gab, sb = w[:, 0:768], w[:, 768:2304], w[:, 2304:2816], w[:, 2816:2824], w[:, 2824:3592]
    pad = jnp.zeros((w.shape[0], P_END - D_IN_PROJ), w.dtype)
    return jnp.concatenate([qkv, gz, sc, sb, gab, pad], axis=1).astype(MXU_DTYPE)


def _unprep_dw_in(dw):
    qkv, gz, sc, sb, gab = dw[:, P_QKV:P_GZ], dw[:, P_GZ:P_SC], dw[:, P_SC:P_SB], dw[:, P_SB:P_GAB], dw[:, P_GAB:P_GAB + 8]
    return jnp.concatenate([sc, qkv, gz, gab, sb], axis=1)


def _prep_w_out(w):
    return jnp.concatenate([w[256:768], w[0:256], w[768:]], axis=0).astype(MXU_DTYPE)


def _unprep_dw_out(dw):
    return jnp.concatenate([dw[512:768], dw[0:512], dw[768:]], axis=0)


def _pad_lanes(v):
    return jnp.zeros((1, LANES), F32).at[0, :v.shape[0]].set(v)


def _layer_fwd(x, p, l):
    L = x.shape[0]
    tm = min(1024, L)
    n = f"l{l}_"
    h = _rms_fwd(x, p["norm_mix"], n + "rms_mix")
    proj = _matmul(h, p["w_in"], "nn", F32, n + "mm_in", tm, 768, 1024)
    cb = SC_WIDTH
    y_sc = _conv_fwd((proj, P_SC // cb + 1), p["w_sconv"], SC_KERNEL, cb, 1, MXU_DTYPE, n + "sconv",
                     x2=(proj, P_SC // cb + 2), gate=(proj, P_SC // cb))
    cqkv = _conv_fwd((proj, 0), p["w_gconv"], GDN_CONV, 256, 6, F32, n + "gconv")
    y_gdn, s_all = _gdn_fwd(cqkv, proj, p["a_log"], p["dt_bias"], p["gdn_norm"], n + "gdn")
    kv = proj[:, P_SB + SB_WIDTH:P_SB + 3 * SB_WIDTH].astype(MXU_DTYPE)
    y_sb, ctot = _sb_fwd(proj, kv, n + "sb")
    ycat = jnp.concatenate([y_gdn, y_sc, y_sb], axis=1)
    x1 = _matmul(ycat, p["w_out"], "nn", F32, n + "mm_out", tm, 512, 1024, resid=x)
    h2 = _rms_fwd(x1, p["norm_ffn"], n + "rms_ffn")
    up = _matmul(h2, p["w_up"], "nn", F32, n + "mm_up", tm, 512, 1024)
    act = _ffn_act_fwd(up, p["w_fconv"], n + "ffn_act")
    x2 = _matmul(act, p["w_down"], "nn", F32, n + "mm_down", tm, 512, 1408, resid=x1)
    saved = dict(x=x, h=h, proj=proj, cqkv=cqkv, s_all=s_all, kv=kv, ctot=ctot, ycat=ycat, x1=x1, h2=h2, up=up, act=act)
    return x2, saved


def _layer_bwd(dx2, p, s, l):
    L = dx2.shape[0]
    tm, tkl = min(1024, L), min(1024, L)
    n = f"l{l}_"
    g = {}
    dx2b = dx2.astype(MXU_DTYPE)
    g["w_ffn_down"] = _matmul(s["act"], dx2b, "tn", F32, n + "mm_ddown", 1408, 512, tkl)
    dact = _matmul(dx2b, p["w_down"], "nt", F32, n + "mm_dact", tm, 1408, 1024)
    dup, dwf = _ffn_act_bwd(s["up"], p["w_fconv"], dact, n + "dffn_act")
    g["w_ffn_conv"] = dwf[:FFN_CONV]
    g["w_ffn_up"] = _matmul(s["h2"], dup, "tn", F32, n + "mm_dup", 1024, 512, tkl)
    dh2 = _matmul(dup, p["w_up"], "nt", F32, n + "mm_dh2", tm, 512, 1408)
    dx1, dwn = _rms_bwd(s["x1"], p["norm_ffn"], dh2, dx2, n + "drms_ffn")
    g["w_norm_ffn"] = dwn[0]

    dx1b = dx1.astype(MXU_DTYPE)
    g["w_mix_out"] = _unprep_dw_out(_matmul(s["ycat"], dx1b, "tn", F32, n + "mm_dout", 1024, 512, tkl))
    dycat = _matmul(dx1b, p["w_out"], "nt", F32, n + "mm_dycat", tm, 512, 1024)
    proj = s["proj"]
    cb = SC_WIDTH
    dsc_c, dsc_h, dsc_b, dws = _conv_bwd((proj, P_SC // cb + 1), p["w_sconv"], (dycat, Y_SC // cb), SC_KERNEL, cb, 1,
                                         MXU_DTYPE, n + "dsconv", x2=(proj, P_SC // cb + 2), gate=(proj, P_SC // cb))
    g["w_sconv"] = dws[:SC_KERNEL]
    dcqkv, dgz, dgab, dal, ddb, dgn = _gdn_bwd(s["cqkv"], proj, p["a_log"], p["dt_bias"], p["gdn_norm"], s["s_all"], dycat,
                                               n + "dgdn")
    g["gdn_a_log"], g["gdn_dt_bias"], g["w_gdn_norm"] = dal[0, :GDN_HEADS], ddb[0, :GDN_HEADS], dgn[0]
    dqkv, dwg = _conv_bwd((proj, 0), p["w_gconv"], (dcqkv, 0), GDN_CONV, 256, 6, MXU_DTYPE, n + "dgconv")
    g["w_gdn_conv"] = dwg[:GDN_CONV]
    dq, dk, dv = _sb_bwd(proj, s["kv"], dycat, s["ctot"], n + "dsb")
    dproj = jnp.concatenate(
        [dqkv, dgz, dsc_b, dsc_c, dsc_h, dq, dk.astype(MXU_DTYPE), dv.astype(MXU_DTYPE), dgab.astype(MXU_DTYPE),
         jnp.zeros((L, P_END - P_GAB - LANES), MXU_DTYPE)], axis=1)
    g["w_mix_in"] = _unprep_dw_in(_matmul(s["h"], dproj, "tn", F32, n + "mm_din", 1024, 768, tkl))
    dh = _matmul(dproj, p["w_in"], "nt", F32, n + "mm_dh", tm, 512, 768)
    dx, dwm = _rms_bwd(s["x"], p["norm_mix"], dh, dx1, n + "drms_mix")
    g["w_norm_mix"] = dwm[0]
    return dx, g


WEIGHTS = ["w_norm_mix", "w_mix_in", "w_sconv", "w_gdn_conv", "gdn_a_log", "gdn_dt_bias", "w_gdn_norm", "w_mix_out",
           "w_norm_ffn", "w_ffn_up", "w_ffn_conv", "w_ffn_down", "w_norm_final"]


def _local_step(x, w, target):
    layers = []
    for l in range(DEPTH):
        layers.append(dict(
            norm_mix=w["w_norm_mix"][l][None], w_in=_prep_w_in(w["w_mix_in"][l]), w_sconv=w["w_sconv"][l],
            w_gconv=w["w_gdn_conv"][l], a_log=_pad_lanes(w["gdn_a_log"][l]), dt_bias=_pad_lanes(w["gdn_dt_bias"][l]),
            gdn_norm=w["w_gdn_norm"][l][None], w_out=_prep_w_out(w["w_mix_out"][l]), norm_ffn=w["w_norm_ffn"][l][None],
            w_up=w["w_ffn_up"][l].astype(MXU_DTYPE), w_fconv=w["w_ffn_conv"][l], w_down=w["w_ffn_down"][l].astype(MXU_DTYPE)))
    saved = []
    for l in range(DEPTH):
        x, s = _layer_fwd(x, layers[l], l)
        saved.append(s)
    loss, dx, dwf = _loss_head(x, w["w_norm_final"][None], target, "loss_head")
    grads = [None] * DEPTH
    for l in reversed(range(DEPTH)):
        dx, grads[l] = _layer_bwd(dx, layers[l], saved[l], l)
    out = {k: jnp.stack([grads[l][k] for l in range(DEPTH)]) for k in WEIGHTS if k != "w_norm_final"}
    out["w_norm_final"] = dwf[0]
    return loss, dx, out


def _exchange(bufs, scatter, name):
    nb = len(bufs)

    def body(*refs):
        ins, outs = refs[:nb], refs[nb:2 * nb]
        send_sems, recv_sems, local_sems = refs[2 * nb:]
        x, y, c = lax.axis_index("x"), lax.axis_index("y"), lax.axis_index("c")
        me = 4 * x + 2 * y + c
        local = []
        for b in range(nb):
            cp = pltpu.make_async_copy(ins[b].at[me] if scatter[b] else ins[b], outs[b].at[me], local_sems.at[b])
            cp.start()
            local.append(cp)
        remote = []
        for b in range(nb):
            for kk in range(1, N_DEV):
                px = 1 - x if kk & 4 else x
                py = 1 - y if kk & 2 else y
                pc = 1 - c if kk & 1 else c
                src = ins[b].at[4 * px + 2 * py + pc] if scatter[b] else ins[b]
                cp = pltpu.make_async_remote_copy(
                    src_ref=src, dst_ref=outs[b].at[me], send_sem=send_sems.at[b, kk - 1], recv_sem=recv_sems.at[b, kk - 1],
                    device_id=(px, py, pc), device_id_type=pl.DeviceIdType.MESH)
                cp.start()
                remote.append(cp)
        for cp in remote:
            cp.wait()
        for cp in local:
            cp.wait()

    out_shape = [jax.ShapeDtypeStruct((N_DEV,) + (b.shape[1:] if s else b.shape), b.dtype) for b, s in zip(bufs, scatter)]
    hbm = pl.BlockSpec(memory_space=pl.ANY)
    return pl.pallas_call(
        body, name=name, in_specs=[hbm] * nb, out_specs=[hbm] * nb, out_shape=out_shape,
        scratch_shapes=[pltpu.SemaphoreType.DMA((nb, N_DEV - 1)), pltpu.SemaphoreType.DMA((nb, N_DEV - 1)),
                        pltpu.SemaphoreType.DMA((nb,))],
    )(*bufs)


def _sum_sources(recv, row_tile, name):
    _, R, _ = recv.shape

    def body(r_ref, o_ref):
        acc = r_ref[0].astype(F32)
        for s in range(1, N_DEV):
            acc = acc + r_ref[s].astype(F32)
        o_ref[...] = acc

    return pl.pallas_call(
        body, name=name, grid=(R // row_tile,),
        in_specs=[pl.BlockSpec((N_DEV, row_tile, LANES), lambda i: (0, i, 0))],
        out_specs=pl.BlockSpec((row_tile, LANES), lambda i: (i, 0)),
        out_shape=jax.ShapeDtypeStruct((R, LANES), F32),
        compiler_params=_params(("parallel",)),
    )(recv)


def _adamw_math(g, w, m, v):
    nm = ADAM_B1 * m + (1.0 - ADAM_B1) * g
    nv = ADAM_B2 * v + (1.0 - ADAM_B2) * (g * g)
    m_hat = nm / (1.0 - ADAM_B1 ** ADAM_STEP)
    v_hat = nv / (1.0 - ADAM_B2 ** ADAM_STEP)
    return -ADAM_LR * (m_hat / (jnp.sqrt(v_hat) + ADAM_EPS) + ADAM_WD * w), nm, nv


def _sum_adamw(recv, w, m, v, row_tile, name):
    D0, R, C = w.shape

    def body(r_ref, w_ref, m_ref, v_ref, g_ref, d_ref, nm_ref, nv_ref):
        g = r_ref[0, 0].astype(F32)
        for s in range(1, N_DEV):
            g = g + r_ref[s, 0].astype(F32)
        g_ref[0] = g
        d_ref[0], nm_ref[0], nv_ref[0] = _adamw_math(g, w_ref[0], m_ref[0], v_ref[0])

    blk = pl.BlockSpec((1, row_tile, C), lambda l, i: (l, i, 0))
    out = jax.ShapeDtypeStruct((D0, R, C), F32)
    return pl.pallas_call(
        body, name=name, grid=(D0, R // row_tile),
        in_specs=[pl.BlockSpec((N_DEV, 1, row_tile, C), lambda l, i: (0, l, i, 0)), blk, blk, blk],
        out_specs=[blk] * 4, out_shape=[out] * 4,
        compiler_params=_params(("parallel", "parallel")),
    )(recv, w, m, v)


def _adamw(g, w, m, v, row_tile, name):
    R = g.shape[0]

    def body(g_ref, w_ref, m_ref, v_ref, d_ref, nm_ref, nv_ref):
        d_ref[...], nm_ref[...], nv_ref[...] = _adamw_math(g_ref[...], w_ref[...], m_ref[...], v_ref[...])

    blk = pl.BlockSpec((row_tile, LANES), lambda i: (i, 0))
    out = jax.ShapeDtypeStruct((R, LANES), F32)
    return pl.pallas_call(
        body, name=name, grid=(R // row_tile,), in_specs=[blk] * 4, out_specs=[blk] * 3, out_shape=[out] * 3,
        compiler_params=_params(("parallel",)),
    )(g, w, m, v)


def _pack(arrs, rows, dtype):
    flat = jnp.concatenate([a.reshape(-1).astype(dtype) for a in arrs])
    return jnp.pad(flat, (0, rows * LANES - flat.shape[0])).reshape(rows, LANES)


def _unpack(buf, shapes):
    lead = buf.shape[:-2]
    flat = buf.reshape(lead + (-1,))
    out, off = [], 0
    for shp in shapes:
        n = 1
        for d in shp:
            n *= d
        out.append(flat[..., off:off + n].reshape(lead + tuple(shp)))
        off += n
    return out


BIG = ["w_mix_in", "w_mix_out", "w_ffn_up", "w_ffn_down"]
BIG_AXIS = {"w_mix_in": 2, "w_mix_out": 1, "w_ffn_up": 2, "w_ffn_down": 1}
CONV = ["w_sconv", "w_gdn_conv", "w_ffn_conv"]
REPL = ["w_norm_mix", "gdn_a_log", "gdn_dt_bias", "w_gdn_norm", "w_norm_ffn", "w_norm_final"]
BIG_ROW_TILE = {"w_mix_in": 512, "w_mix_out": 128, "w_ffn_up": 512, "w_ffn_down": 352}
SMALL_ROWS = 416
CONV_ROWS = 48


def kernel(x, w_norm_mix, w_mix_in, w_sconv, w_gdn_conv, gdn_a_log, gdn_dt_bias, w_gdn_norm, w_mix_out, w_norm_ffn, w_ffn_up, w_ffn_conv, w_ffn_down, w_norm_final, loss_target, m_w_norm_mix, m_w_mix_in, m_w_sconv, m_w_gdn_conv, m_gdn_a_log, m_gdn_dt_bias, m_w_gdn_norm, m_w_mix_out, m_w_norm_ffn, m_w_ffn_up, m_w_ffn_conv, m_w_ffn_down, m_w_norm_final, v_w_norm_mix, v_w_mix_in, v_w_sconv, v_w_gdn_conv, v_gdn_a_log, v_gdn_dt_bias, v_w_gdn_norm, v_w_mix_out, v_w_norm_ffn, v_w_ffn_up, v_w_ffn_conv, v_w_ffn_down, v_w_norm_final):
    w = dict(w_norm_mix=w_norm_mix, w_mix_in=w_mix_in, w_sconv=w_sconv, w_gdn_conv=w_gdn_conv, gdn_a_log=gdn_a_log,
             gdn_dt_bias=gdn_dt_bias, w_gdn_norm=w_gdn_norm, w_mix_out=w_mix_out, w_norm_ffn=w_norm_ffn, w_ffn_up=w_ffn_up,
             w_ffn_conv=w_ffn_conv, w_ffn_down=w_ffn_down, w_norm_final=w_norm_final)
    m = dict(w_norm_mix=m_w_norm_mix, w_mix_in=m_w_mix_in, w_sconv=m_w_sconv, w_gdn_conv=m_w_gdn_conv, gdn_a_log=m_gdn_a_log,
             gdn_dt_bias=m_gdn_dt_bias, w_gdn_norm=m_w_gdn_norm, w_mix_out=m_w_mix_out, w_norm_ffn=m_w_norm_ffn,
             w_ffn_up=m_w_ffn_up, w_ffn_conv=m_w_ffn_conv, w_ffn_down=m_w_ffn_down, w_norm_final=m_w_norm_final)
    v = dict(w_norm_mix=v_w_norm_mix, w_mix_in=v_w_mix_in, w_sconv=v_w_sconv, w_gdn_conv=v_w_gdn_conv, gdn_a_log=v_gdn_a_log,
             gdn_dt_bias=v_gdn_dt_bias, w_gdn_norm=v_w_gdn_norm, w_mix_out=v_w_mix_out, w_norm_ffn=v_w_norm_ffn,
             w_ffn_up=v_w_ffn_up, w_ffn_conv=v_w_ffn_conv, w_ffn_down=v_w_ffn_down, w_norm_final=v_w_norm_final)
    me = 4 * lax.axis_index("x") + 2 * lax.axis_index("y") + lax.axis_index("c")
    conv_shapes = [w[k].shape for k in CONV]

    gathered = _exchange([w[k].astype(MXU_DTYPE) for k in BIG] + [_pack([w[k] for k in CONV], CONV_ROWS, F32)],
                         [False] * (len(BIG) + 1), "gather_weights")
    full = dict(w)
    for k, got in zip(BIG, gathered):
        full[k] = jnp.concatenate([got[s] for s in range(N_DEV)], axis=BIG_AXIS[k])
    for k, got in zip(CONV, _unpack(gathered[-1], conv_shapes)):
        full[k] = jnp.concatenate([got[s] for s in range(N_DEV)], axis=2)

    loss, dx, grads = _local_step(x[0], full, loss_target[0])

    small = CONV + REPL
    to_owner = [jnp.stack(jnp.split(grads[k], N_DEV, axis=BIG_AXIS[k])).astype(MXU_DTYPE) for k in BIG]
    received = _exchange(to_owner + [_pack([grads[k] for k in small], SMALL_ROWS, F32)], [True] * len(BIG) + [False],
                         "exchange_grads")
    g, delta, new_m, new_v = {}, {}, {}, {}
    for k, got in zip(BIG, received):
        g[k], delta[k], new_m[k], new_v[k] = _sum_adamw(got, w[k], m[k], v[k], BIG_ROW_TILE[k], "adamw_" + k)
    g_small = _unpack(_sum_sources(received[-1], SMALL_ROWS, "sum_small"), [grads[k].shape for k in small])
    for k, gs in zip(small, g_small):
        g[k] = lax.dynamic_slice_in_dim(gs, me * w[k].shape[2], w[k].shape[2], axis=2) if k in CONV else gs

    small_shapes = [w[k].shape for k in small]
    small_rows = -(-sum(w[k].size for k in small) // (SUBLANES * LANES)) * SUBLANES
    d_sm, m_sm, v_sm = _adamw(_pack([g[k] for k in small], small_rows, F32), _pack([w[k] for k in small], small_rows, F32),
                              _pack([m[k] for k in small], small_rows, F32), _pack([v[k] for k in small], small_rows, F32),
                              small_rows, "adamw_small")
    for dst, small_buf in ((delta, d_sm), (new_m, m_sm), (new_v, v_sm)):
        dst.update(zip(small, _unpack(small_buf, small_shapes)))

    loss_all = lax.psum(loss[0, 0], ("x", "y", "c"))
    return (loss_all, dx[None], *[g[k] for k in WEIGHTS], *[delta[k] for k in WEIGHTS], *[new_m[k] for k in WEIGHTS],
            *[new_v[k] for k in WEIGHTS])
```

```python
import functools

import jax
import jax.numpy as jnp
from jax import lax
from jax.experimental import pallas as pl
from jax.experimental.pallas import tpu as pltpu

F32 = jnp.float32
MXU_DTYPE = jnp.bfloat16
HIGHEST = lax.Precision.HIGHEST

D_MODEL = 1024
DEPTH = 2
SC_WIDTH = 256
SC_KERNEL = 3
GDN_WIDTH = 512
GDN_HEADS = 4
GDN_HEAD_DIM = 128
GDN_CONV = 4
GDN_CHUNK = 64
SB_WIDTH = 256
SB_HEADS = 4
SB_HEAD_DIM = 64
SB_BLOCK = 128
SB_SWEEP = 4
D_FF = 2816
FFN_CONV = 3
NORM_EPS = 1e-6
D_IN_PROJ = 3592
ADAM_LR, ADAM_B1, ADAM_B2, ADAM_EPS, ADAM_WD, ADAM_STEP = 0.001, 0.9, 0.999, 1e-08, 0.01, 10

N_DEV = 8
LANES = 128
SUBLANES = 8
VMEM_LIMIT = 48 * 1024 * 1024

P_QKV, P_GZ, P_SC, P_SB, P_GAB, P_END = 0, 1536, 2048, 2816, 3584, 3840
Y_GDN, Y_SC, Y_SB = 0, 512, 768


def _params(semantics):
    return pltpu.CompilerParams(dimension_semantics=semantics, vmem_limit_bytes=VMEM_LIMIT)


_DIMS = {"nn": (((1,), (0,)), ((), ())), "nt": (((1,), (1,)), ((), ())), "tn": (((0,), (0,)), ((), ()))}


def _matmul(a, b, mode, out_dtype, name, tm, tn, tk, resid=None):
    if mode == "tn":
        (K, M), (K2, N) = a.shape, b.shape
    elif mode == "nt":
        (M, K), (N, K2) = a.shape, b.shape
    else:
        (M, K), (K2, N) = a.shape, b.shape
    assert K == K2 and M % tm == 0 and N % tn == 0 and K % tk == 0, (name, a.shape, b.shape, tm, tn, tk)
    nk = K // tk
    has_resid = resid is not None

    def body(*refs):
        if has_resid:
            a_ref, b_ref, r_ref, o_ref, acc = refs
        else:
            a_ref, b_ref, o_ref, acc = refs
        k = pl.program_id(2)

        @pl.when(k == 0)
        def _():
            acc[...] = jnp.zeros_like(acc)

        acc[...] += lax.dot_general(a_ref[...], b_ref[...], _DIMS[mode], preferred_element_type=F32)

        @pl.when(k == nk - 1)
        def _():
            r = acc[...]
            if has_resid:
                r = r + r_ref[...]
            o_ref[...] = r.astype(out_dtype)

    a_spec = pl.BlockSpec((tk, tm), lambda i, j, k: (k, i)) if mode == "tn" else pl.BlockSpec((tm, tk), lambda i, j, k: (i, k))
    b_spec = pl.BlockSpec((tn, tk), lambda i, j, k: (j, k)) if mode == "nt" else pl.BlockSpec((tk, tn), lambda i, j, k: (k, j))
    o_spec = pl.BlockSpec((tm, tn), lambda i, j, k: (i, j))
    in_specs = [a_spec, b_spec] + ([o_spec] if has_resid else [])
    args = (a, b) + ((resid,) if has_resid else ())
    return pl.pallas_call(
        body, name=name, grid=(M // tm, N // tn, nk), in_specs=in_specs, out_specs=o_spec,
        out_shape=jax.ShapeDtypeStruct((M, N), out_dtype),
        scratch_shapes=[pltpu.VMEM((tm, tn), F32)],
        compiler_params=_params(("parallel", "parallel", "arbitrary")),
    )(*args)


def _rms(x, w):
    return x * lax.rsqrt(jnp.mean(x * x, axis=-1, keepdims=True) + NORM_EPS) * w


ROW_TILE = 512


def _rms_fwd(x, w, name):
    L, Dm = x.shape

    def body(x_ref, w_ref, h_ref):
        h_ref[...] = _rms(x_ref[...], w_ref[...]).astype(h_ref.dtype)

    return pl.pallas_call(
        body, name=name, grid=(L // ROW_TILE,),
        in_specs=[pl.BlockSpec((ROW_TILE, Dm), lambda i: (i, 0)), pl.BlockSpec((1, Dm), lambda i: (0, 0))],
        out_specs=pl.BlockSpec((ROW_TILE, Dm), lambda i: (i, 0)),
        out_shape=jax.ShapeDtypeStruct((L, Dm), MXU_DTYPE),
        compiler_params=_params(("parallel",)),
    )(x, w)


def _rms_bwd(x, w, dh, dres, name):
    L, Dm = x.shape

    def body(x_ref, w_ref, dh_ref, dres_ref, dx_ref, dw_ref):
        _, vjp = jax.vjp(_rms, x_ref[...], w_ref[...])
        dx, dw = vjp(dh_ref[...])
        dx_ref[...] = dres_ref[...] + dx

        @pl.when(pl.program_id(0) == 0)
        def _():
            dw_ref[...] = jnp.zeros_like(dw_ref)

        dw_ref[...] += dw

    row = pl.BlockSpec((ROW_TILE, Dm), lambda i: (i, 0))
    vec = pl.BlockSpec((1, Dm), lambda i: (0, 0))
    return pl.pallas_call(
        body, name=name, grid=(L // ROW_TILE,), in_specs=[row, vec, row, row], out_specs=[row, vec],
        out_shape=[jax.ShapeDtypeStruct((L, Dm), F32), jax.ShapeDtypeStruct((1, Dm), F32)],
        compiler_params=_params(("arbitrary",)),
    )(x, w, dh, dres)


def _loss_head(x, w, target, name):
    L, Dm = x.shape

    def block_loss(xb, wb, tb):
        err = _rms(xb, wb) - tb
        return 0.5 * jnp.sum(jnp.sum(err * err, axis=-1, keepdims=True) * (1.0 / Dm), axis=0, keepdims=True)

    def body(x_ref, w_ref, t_ref, loss_ref, dx_ref, dw_ref):
        val, vjp = jax.vjp(lambda xb, wb: block_loss(xb, wb, t_ref[...]), x_ref[...], w_ref[...])
        dx, dw = vjp(jnp.ones_like(val))
        dx_ref[...] = dx

        @pl.when(pl.program_id(0) == 0)
        def _():
            dw_ref[...] = jnp.zeros_like(dw_ref)
            loss_ref[...] = jnp.zeros_like(loss_ref)

        dw_ref[...] += dw
        loss_ref[...] += val

    row = pl.BlockSpec((ROW_TILE, Dm), lambda i: (i, 0))
    vec = pl.BlockSpec((1, Dm), lambda i: (0, 0))
    one = pl.BlockSpec((1, 1), lambda i: (0, 0))
    return pl.pallas_call(
        body, name=name, grid=(L // ROW_TILE,), in_specs=[row, vec, row], out_specs=[one, row, vec],
        out_shape=[jax.ShapeDtypeStruct((1, 1), F32), jax.ShapeDtypeStruct((L, Dm), F32), jax.ShapeDtypeStruct((1, Dm), F32)],
        compiler_params=_params(("arbitrary",)),
    )(x, w, target)


HALO = SUBLANES


def _conv_specs(L, T, Cb, off):
    main = pl.BlockSpec((T, Cb), lambda j, i: (i, off + j))
    prev = pl.BlockSpec((HALO, Cb), lambda j, i: (jnp.maximum(i * (T // HALO) - 1, 0), off + j))
    nxt = pl.BlockSpec((HALO, Cb), lambda j, i: (jnp.minimum((i + 1) * (T // HALO), L // HALO - 1), off + j))
    return main, prev, nxt


def _conv_fwd(x1, w, K, Cb, ncol, out_dtype, name, x2=None, gate=None):
    (x1a, o1) = x1
    L = x1a.shape[0]
    T = min(ROW_TILE, L)
    has_mul, has_gate = x2 is not None, gate is not None

    def body(*refs):
        it = iter(refs)
        x1m, x1p = next(it), next(it)
        if has_mul:
            x2m, x2p = next(it), next(it)
        if has_gate:
            gm = next(it)
        w_ref, y_ref, scr = next(it), next(it), next(it)
        i = pl.program_id(1)
        p, pp = x1m[...].astype(F32), x1p[...].astype(F32)
        if has_mul:
            p, pp = p * x2m[...], pp * x2p[...]
        scr[0:HALO, :] = jnp.where(i > 0, pp, 0.0)
        scr[HALO:HALO + T, :] = p
        acc = w_ref[K - 1:K, :] * p
        for k in range(K - 1):
            s = K - 1 - k
            acc = acc + w_ref[k:k + 1, :] * scr[HALO - s:HALO - s + T, :]
        if has_gate:
            acc = acc * gm[...]
        y_ref[...] = acc.astype(out_dtype)

    in_specs, args = [], []
    m, p_, _ = _conv_specs(L, T, Cb, o1)
    in_specs += [m, p_]
    args += [x1a, x1a]
    if has_mul:
        m, p_, _ = _conv_specs(L, T, Cb, x2[1])
        in_specs += [m, p_]
        args += [x2[0], x2[0]]
    if has_gate:
        m, _, _ = _conv_specs(L, T, Cb, gate[1])
        in_specs += [m]
        args += [gate[0]]
    in_specs.append(pl.BlockSpec((K, Cb), lambda j, i: (0, j)))
    args.append(w)
    return pl.pallas_call(
        body, name=name, grid=(ncol, L // T), in_specs=in_specs,
        out_specs=pl.BlockSpec((T, Cb), lambda j, i: (i, j)),
        out_shape=jax.ShapeDtypeStruct((L, ncol * Cb), out_dtype),
        scratch_shapes=[pltpu.VMEM((T + HALO, Cb), F32)],
        compiler_params=_params(("parallel", "arbitrary")),
    )(*args)


def _conv_bwd(x1, w, dy, K, Cb, ncol, out_dtype, name, x2=None, gate=None):
    (x1a, o1) = x1
    L = x1a.shape[0]
    T = min(ROW_TILE, L)
    nrow = L // T
    has_mul, has_gate = x2 is not None, gate is not None

    def body(*refs):
        it = iter(refs)
        x1m, x1p = next(it), next(it)
        if has_mul:
            x2m, x2p = next(it), next(it)
        if has_gate:
            gm, gn = next(it), next(it)
        dym, dyn, w_ref = next(it), next(it), next(it)
        dx1_ref = next(it)
        if has_mul:
            dx2_ref = next(it)
        if has_gate:
            dg_ref = next(it)
        dw_ref, scr_p, scr_d = next(it), next(it), next(it)
        i = pl.program_id(1)
        p, pp = x1m[...].astype(F32), x1p[...].astype(F32)
        if has_mul:
            p, pp = p * x2m[...], pp * x2p[...]
        scr_p[0:HALO, :] = jnp.where(i > 0, pp, 0.0)
        scr_p[HALO:HALO + T, :] = p
        dcv, dcn = dym[...].astype(F32), dyn[...].astype(F32)
        if has_gate:
            dcv, dcn = dcv * gm[...], dcn * gn[...]
        scr_d[0:T, :] = dcv
        scr_d[T:T + HALO, :] = jnp.where(i < nrow - 1, dcn, 0.0)

        @pl.when(i == 0)
        def _():
            dw_ref[...] = jnp.zeros_like(dw_ref)

        dp = w_ref[K - 1:K, :] * dcv
        cv = w_ref[K - 1:K, :] * p
        dw_ref[K - 1:K, :] += jnp.sum(dcv * p, axis=0, keepdims=True)
        for k in range(K - 1):
            s = K - 1 - k
            dp = dp + w_ref[k:k + 1, :] * scr_d[s:s + T, :]
            sh = scr_p[HALO - s:HALO - s + T, :]
            dw_ref[k:k + 1, :] += jnp.sum(dcv * sh, axis=0, keepdims=True)
            if has_gate:
                cv = cv + w_ref[k:k + 1, :] * sh
        if has_gate:
            dg_ref[...] = (dym[...].astype(F32) * cv).astype(out_dtype)
        if has_mul:
            dx1_ref[...] = (dp * x2m[...]).astype(out_dtype)
            dx2_ref[...] = (dp * x1m[...]).astype(out_dtype)
        else:
            dx1_ref[...] = dp.astype(out_dtype)

    in_specs, args = [], []
    m, p_, _ = _conv_specs(L, T, Cb, o1)
    in_specs += [m, p_]
    args += [x1a, x1a]
    if has_mul:
        m, p_, _ = _conv_specs(L, T, Cb, x2[1])
        in_specs += [m, p_]
        args += [x2[0], x2[0]]
    if has_gate:
        m, _, n_ = _conv_specs(L, T, Cb, gate[1])
        in_specs += [m, n_]
        args += [gate[0], gate[0]]
    m, _, n_ = _conv_specs(L, T, Cb, dy[1])
    in_specs += [m, n_, pl.BlockSpec((K, Cb), lambda j, i: (0, j))]
    args += [dy[0], dy[0], w]
    out = pl.BlockSpec((T, Cb), lambda j, i: (i, j))
    full = jax.ShapeDtypeStruct((L, ncol * Cb), out_dtype)
    n_out = 1 + int(has_mul) + int(has_gate)
    return pl.pallas_call(
        body, name=name, grid=(ncol, nrow), in_specs=in_specs,
        out_specs=[out] * n_out + [pl.BlockSpec((SUBLANES, Cb), lambda j, i: (0, j))],
        out_shape=[full] * n_out + [jax.ShapeDtypeStruct((SUBLANES, ncol * Cb), F32)],
        scratch_shapes=[pltpu.VMEM((T + HALO, Cb), F32), pltpu.VMEM((T + HALO, Cb), F32)],
        compiler_params=_params(("parallel", "arbitrary")),
    )(*args)


GLU_COLS = 256


def _silu(x):
    return x * (1.0 / (1.0 + jnp.exp(-x)))


def _glu(g, v):
    return _silu(g) * v


def _causal_taps(w_ref, scr, first, rows, K):
    acc = w_ref[K - 1:K, :] * scr[first:first + rows, :]
    for k in range(K - 1):
        s = K - 1 - k
        acc = acc + w_ref[k:k + 1, :] * scr[first - s:first - s + rows, :]
    return acc


def _ffn_act_fwd(up, w, name):
    L = up.shape[0]
    T, Cb, K = min(ROW_TILE, L), GLU_COLS, FFN_CONV
    nb = D_FF // Cb

    def body(gm, gp, vm, vp, wg, wv, a_ref, sg, sv):
        i = pl.program_id(1)
        for main, prev, scr in ((gm, gp, sg), (vm, vp, sv)):
            scr[0:HALO, :] = jnp.where(i > 0, prev[...], 0.0)
            scr[HALO:HALO + T, :] = main[...]
        a_ref[...] = _glu(_causal_taps(wg, sg, HALO, T, K), _causal_taps(wv, sv, HALO, T, K)).astype(a_ref.dtype)

    gmain, gprev, _ = _conv_specs(L, T, Cb, 0)
    vmain, vprev, _ = _conv_specs(L, T, Cb, nb)
    return pl.pallas_call(
        body, name=name, grid=(nb, L // T),
        in_specs=[gmain, gprev, vmain, vprev, pl.BlockSpec((K, Cb), lambda j, i: (0, j)), pl.BlockSpec((K, Cb), lambda j, i: (0, nb + j))],
        out_specs=pl.BlockSpec((T, Cb), lambda j, i: (i, j)),
        out_shape=jax.ShapeDtypeStruct((L, D_FF), MXU_DTYPE),
        scratch_shapes=[pltpu.VMEM((T + HALO, Cb), F32), pltpu.VMEM((T + HALO, Cb), F32)],
        compiler_params=_params(("parallel", "arbitrary")),
    )(up, up, up, up, w, w)


def _ffn_act_bwd(up, w, dact, name):
    L = up.shape[0]
    T, Cb, K = min(ROW_TILE, L), GLU_COLS, FFN_CONV
    nb, nrow = D_FF // Cb, L // T

    def body(gm, gp, gn, vm, vp, vn, dam, dan, wg, wv, dg_ref, dv_ref, dwg_ref, dwv_ref, sg, sv, sdg, sdv):
        i = pl.program_id(1)
        for main, prev, nxt, scr in ((gm, gp, gn, sg), (vm, vp, vn, sv)):
            scr[0:HALO, :] = jnp.where(i > 0, prev[...], 0.0)
            scr[HALO:HALO + T, :] = main[...]
            scr[HALO + T:2 * HALO + T, :] = nxt[...]
        ug, uv = _causal_taps(wg, sg, HALO, T + HALO, K), _causal_taps(wv, sv, HALO, T + HALO, K)
        da = jnp.concatenate([dam[...], jnp.where(i < nrow - 1, dan[...], 0.0)], axis=0)
        _, vjp = jax.vjp(_glu, ug, uv)
        sdg[...], sdv[...] = vjp(da)

        @pl.when(i == 0)
        def _():
            dwg_ref[...] = jnp.zeros_like(dwg_ref)
            dwv_ref[...] = jnp.zeros_like(dwv_ref)

        for w_ref, scr, sd, d_ref, dw_ref in ((wg, sg, sdg, dg_ref, dwg_ref), (wv, sv, sdv, dv_ref, dwv_ref)):
            du = sd[0:T, :]
            dp = w_ref[K - 1:K, :] * du
            dw_ref[K - 1:K, :] += jnp.sum(du * scr[HALO:HALO + T, :], axis=0, keepdims=True)
            for k in range(K - 1):
                s = K - 1 - k
                dp = dp + w_ref[k:k + 1, :] * sd[s:s + T, :]
                dw_ref[k:k + 1, :] += jnp.sum(du * scr[HALO - s:HALO - s + T, :], axis=0, keepdims=True)
            d_ref[...] = dp.astype(d_ref.dtype)

    gmain, gprev, gnext = _conv_specs(L, T, Cb, 0)
    vmain, vprev, vnext = _conv_specs(L, T, Cb, nb)
    dmain, _, dnext = _conv_specs(L, T, Cb, 0)
    out = pl.BlockSpec((T, Cb), lambda j, i: (i, j))
    dwb = pl.BlockSpec((SUBLANES, Cb), lambda j, i: (0, j))
    half = jax.ShapeDtypeStruct((L, D_FF), MXU_DTYPE)
    dwh = jax.ShapeDtypeStruct((SUBLANES, D_FF), F32)
    dg, dv, dwg, dwv = pl.pallas_call(
        body, name=name, grid=(nb, nrow),
        in_specs=[gmain, gprev, gnext, vmain, vprev, vnext, dmain, dnext,
                  pl.BlockSpec((K, Cb), lambda j, i: (0, j)), pl.BlockSpec((K, Cb), lambda j, i: (0, nb + j))],
        out_specs=[out, out, dwb, dwb], out_shape=[half, half, dwh, dwh],
        scratch_shapes=[pltpu.VMEM((T + 2 * HALO, Cb), F32), pltpu.VMEM((T + 2 * HALO, Cb), F32),
                        pltpu.VMEM((T + HALO, Cb), F32), pltpu.VMEM((T + HALO, Cb), F32)],
        compiler_params=_params(("parallel", "arbitrary")),
    )(up, up, up, up, up, up, dact, dact, w, w)
    return jnp.concatenate([dg, dv], axis=1), jnp.concatenate([dwg, dwv], axis=1)


def _bdot_raw(a, b, mode):
    return lax.dot_general(a.astype(MXU_DTYPE), b.astype(MXU_DTYPE), _DIMS[mode], preferred_element_type=F32)


@functools.partial(jax.custom_vjp, nondiff_argnums=(2,))
def _bdot(a, b, mode):
    return _bdot_raw(a, b, mode)


def _bdot_fwd(a, b, mode):
    return _bdot_raw(a, b, mode), (a, b)


def _bdot_bwd(mode, res, ct):
    a, b = res
    if mode == "nn":
        return _bdot_raw(ct, b, "nt"), _bdot_raw(a, ct, "tn")
    if mode == "nt":
        return _bdot_raw(ct, b, "nn"), _bdot_raw(ct, a, "tn")
    return _bdot_raw(b, ct, "nt"), _bdot_raw(a, ct, "nn")


_bdot.defvjp(_bdot_fwd, _bdot_bwd)


def _hdot(a, b, mode="nn"):
    return lax.dot_general(a, b, _DIMS[mode], precision=lax.Precision.HIGH, preferred_element_type=F32)


@jax.custom_vjp
def _inv_unit_lower(a):
    R = a.shape[0]
    eye = (lax.broadcasted_iota(jnp.int32, (R, R), 0) == lax.broadcasted_iota(jnp.int32, (R, R), 1)).astype(F32)
    t = eye - a
    p = a
    n = 1
    while 2 * n < GDN_CHUNK:
        p = _hdot(p, p)
        t = t + _hdot(t, p)
        n *= 2
    return t


def _inv_fwd(a):
    t = _inv_unit_lower(a)
    return t, t


def _inv_bwd(t, ct):
    return (-_hdot(_hdot(t, ct, "tn"), t, "nt"),)


_inv_unit_lower.defvjp(_inv_fwd, _inv_bwd)


def _softplus(x):
    return jnp.maximum(x, 0.0) + jnp.log(1.0 + jnp.exp(-jnp.abs(x)))


def _sigmoid(x):
    return 1.0 / (1.0 + jnp.exp(-x))


def _pick_lane(blk, lane):
    ids = lax.broadcasted_iota(jnp.int32, blk.shape, 1)
    return jnp.sum(jnp.where(ids == lane, blk, 0.0), axis=1, keepdims=True)


def _gdn_chunk(cq, ck, cv, gz, gab, alog, dtb, wn, S):
    C, H, Dk = GDN_CHUNK, GDN_HEADS, GDN_HEAD_DIM
    R = H * C
    rows_of = lambda vals, n: jnp.concatenate([jnp.broadcast_to(x, (n, 1)) for x in vals], axis=0)
    ga = jnp.concatenate([_pick_lane(gab, h) for h in range(H)], axis=0)
    gb = jnp.concatenate([_pick_lane(gab, H + h) for h in range(H)], axis=0)
    al = rows_of([_pick_lane(alog, h) for h in range(H)], C)
    db = rows_of([_pick_lane(dtb, h) for h in range(H)], C)
    q, k, v = _silu(cq), _silu(ck), _silu(cv)
    q = q * lax.rsqrt(jnp.sum(q * q, axis=-1, keepdims=True) + NORM_EPS) * (Dk ** -0.5)
    k = k * lax.rsqrt(jnp.sum(k * k, axis=-1, keepdims=True) + NORM_EPS)
    beta = _sigmoid(gb)
    g = -jnp.exp(al) * _softplus(ga + db)
    row = lax.broadcasted_iota(jnp.int32, (R, R), 0)
    col = lax.broadcasted_iota(jnp.int32, (R, R), 1)
    same_head = (row // C) == (col // C)
    causal, strict = same_head & (row >= col), same_head & (row > col)
    gcb = _hdot(causal.astype(F32), jnp.broadcast_to(g, (R, Dk)))
    first = (lax.broadcasted_iota(jnp.int32, (R, Dk), 1) == 0).astype(F32)
    gr = _hdot(first, gcb, "nt")
    gc = _pick_lane(gcb, 0)
    decay = jnp.where(causal, jnp.exp(jnp.where(causal, gc - gr, 0.0)), 0.0)
    kb = k * beta
    lower = jnp.where(strict, _bdot(kb, k, "nt") * decay, 0.0)
    t = _inv_unit_lower(lower)
    egc = jnp.exp(gc)
    u = _hdot(t, v * beta)
    w = _hdot(t, kb * egc)
    attn = jnp.where(causal, _bdot(q, k, "nt") * decay, 0.0)
    own = (lax.broadcasted_iota(jnp.int32, (R, H * Dk), 0) // C) == (lax.broadcasted_iota(jnp.int32, (R, H * Dk), 1) // Dk)
    spread = lambda x: jnp.where(own, jnp.concatenate([x] * H, axis=1), 0.0)
    v_new = u - _bdot(spread(w), S, "nn")
    o = _bdot(spread(q * egc), S, "nn") + _bdot(attn, v_new, "nn")
    last = lax.broadcasted_iota(jnp.int32, (R, 1), 0)
    g_last = [jnp.sum(jnp.where(last == h * C + C - 1, gc, 0.0), axis=0, keepdims=True) for h in range(H)]
    S_new = S * jnp.exp(rows_of(g_last, Dk)) + _bdot(spread(k * jnp.exp(rows_of(g_last, C) - gc)), v_new, "tn")
    y = o * lax.rsqrt(jnp.mean(o * o, axis=-1, keepdims=True) + NORM_EPS) * wn * _silu(gz)
    return y, S_new


def _stack_heads(ref, first, width=GDN_HEAD_DIM):
    return jnp.concatenate([ref[:, first + h * width:first + (h + 1) * width] for h in range(GDN_HEADS)], axis=0)


def _gdn_fwd(cqkv, proj, alog, dtb, wn, name):
    L = cqkv.shape[0]
    C, H, Dh = GDN_CHUNK, GDN_HEADS, GDN_HEAD_DIM
    N = L // C

    def body(c_ref, gz_ref, gab_ref, al_ref, db_ref, wn_ref, y_ref, sall_ref, S):
        n = pl.program_id(0)

        @pl.when(n == 0)
        def _():
            S[...] = jnp.zeros_like(S)

        s_in = S[...]
        sall_ref[0] = s_in
        y, s_new = _gdn_chunk(_stack_heads(c_ref, 0), _stack_heads(c_ref, GDN_WIDTH), _stack_heads(c_ref, 2 * GDN_WIDTH),
                              _stack_heads(gz_ref, 0), gab_ref[...], al_ref[...], db_ref[...], wn_ref[...], s_in)
        for h in range(H):
            y_ref[:, h * Dh:(h + 1) * Dh] = y[h * C:(h + 1) * C].astype(y_ref.dtype)
        S[...] = s_new

    vec = pl.BlockSpec((1, LANES), lambda n: (0, 0))
    return pl.pallas_call(
        body, name=name, grid=(N,),
        in_specs=[pl.BlockSpec((C, 3 * GDN_WIDTH), lambda n: (n, 0)),
                  pl.BlockSpec((C, GDN_WIDTH), lambda n: (n, P_GZ // GDN_WIDTH)),
                  pl.BlockSpec((C, LANES), lambda n: (n, P_GAB // LANES)), vec, vec, vec],
        out_specs=[pl.BlockSpec((C, GDN_WIDTH), lambda n: (n, 0)), pl.BlockSpec((1, H * Dh, Dh), lambda n: (n, 0, 0))],
        out_shape=[jax.ShapeDtypeStruct((L, GDN_WIDTH), MXU_DTYPE), jax.ShapeDtypeStruct((N, H * Dh, Dh), F32)],
        scratch_shapes=[pltpu.VMEM((H * Dh, Dh), F32)],
        compiler_params=_params(("arbitrary",)),
    )(cqkv, proj, proj, alog, dtb, wn)


def _gdn_bwd(cqkv, proj, alog, dtb, wn, s_all, dy, name):
    L = cqkv.shape[0]
    C, H, Dh = GDN_CHUNK, GDN_HEADS, GDN_HEAD_DIM
    N = L // C

    def body(c_ref, gz_ref, gab_ref, al_ref, db_ref, wn_ref, sall_ref, dy_ref,
             dc_ref, dgz_ref, dgab_ref, dal_ref, ddb_ref, dwn_ref, dS):
        n = pl.program_id(0)

        @pl.when(n == 0)
        def _():
            dS[...] = jnp.zeros_like(dS)
            dal_ref[...] = jnp.zeros_like(dal_ref)
            ddb_ref[...] = jnp.zeros_like(ddb_ref)
            dwn_ref[...] = jnp.zeros_like(dwn_ref)

        _, vjp = jax.vjp(_gdn_chunk, _stack_heads(c_ref, 0), _stack_heads(c_ref, GDN_WIDTH), _stack_heads(c_ref, 2 * GDN_WIDTH),
                         _stack_heads(gz_ref, 0), gab_ref[...], al_ref[...], db_ref[...], wn_ref[...], sall_ref[0])
        dq, dk, dv, dgz, dgab, dal, ddb, dwn, ds = vjp((_stack_heads(dy_ref, 0), dS[...]))
        for h in range(H):
            rows = slice(h * C, (h + 1) * C)
            dc_ref[:, h * Dh:(h + 1) * Dh] = dq[rows]
            dc_ref[:, (H + h) * Dh:(H + h + 1) * Dh] = dk[rows]
            dc_ref[:, (2 * H + h) * Dh:(2 * H + h + 1) * Dh] = dv[rows]
            dgz_ref[:, h * Dh:(h + 1) * Dh] = dgz[rows].astype(dgz_ref.dtype)
        dS[...] = ds
        dgab_ref[...] = dgab
        dal_ref[...] += dal
        ddb_ref[...] += ddb
        dwn_ref[...] += dwn

    vec = pl.BlockSpec((1, LANES), lambda n: (0, 0))
    rev = lambda n: N - 1 - n
    return pl.pallas_call(
        body, name=name, grid=(N,),
        in_specs=[pl.BlockSpec((C, 3 * GDN_WIDTH), lambda n: (rev(n), 0)),
                  pl.BlockSpec((C, GDN_WIDTH), lambda n: (rev(n), P_GZ // GDN_WIDTH)),
                  pl.BlockSpec((C, LANES), lambda n: (rev(n), P_GAB // LANES)), vec, vec, vec,
                  pl.BlockSpec((1, H * Dh, Dh), lambda n: (rev(n), 0, 0)),
                  pl.BlockSpec((C, GDN_WIDTH), lambda n: (rev(n), Y_GDN // GDN_WIDTH))],
        out_specs=[pl.BlockSpec((C, 3 * GDN_WIDTH), lambda n: (rev(n), 0)),
                   pl.BlockSpec((C, GDN_WIDTH), lambda n: (rev(n), 0)),
                   pl.BlockSpec((C, LANES), lambda n: (rev(n), 0)), vec, vec, vec],
        out_shape=[jax.ShapeDtypeStruct((L, 3 * GDN_WIDTH), F32), jax.ShapeDtypeStruct((L, GDN_WIDTH), MXU_DTYPE),
                   jax.ShapeDtypeStruct((L, LANES), F32)] + [jax.ShapeDtypeStruct((1, LANES), F32)] * 3,
        scratch_shapes=[pltpu.VMEM((H * Dh, Dh), F32)],
        compiler_params=_params(("arbitrary",)),
    )(cqkv, proj, proj, alog, dtb, wn, s_all, dy)


def _split_dot(x, m):
    R = x.shape[0]
    hi = x.astype(MXU_DTYPE)
    lo = (x - hi.astype(F32)).astype(MXU_DTYPE)
    both = jnp.dot(jnp.concatenate([hi, lo], axis=0), m, preferred_element_type=F32)
    return both[:R] + both[R:]


def _sb_kv_blocks(kv_ref, js):
    B = SB_BLOCK
    rows = [pl.ds(pl.multiple_of(j * B, B), B) for j in js]
    kps = [[kv_ref[r, p * LANES:(p + 1) * LANES] for p in range(SB_HEADS // 2)] for r in rows]
    vps = [[kv_ref[r, SB_WIDTH + p * LANES:SB_WIDTH + (p + 1) * LANES] for p in range(SB_HEADS // 2)] for r in rows]
    return rows, kps, vps


def _sb_pair_dots(x, mats, mode):
    B = SB_BLOCK
    return jnp.concatenate([lax.dot_general(x[2 * p * B:(2 * p + 2) * B], m, _DIMS[mode], preferred_element_type=F32)
                            for mp in mats for p, m in enumerate(mp)], axis=0)


SB_SCALE = SB_HEAD_DIM ** -0.5


def _sb_logits(qx, kps):
    z = _sb_pair_dots(qx, kps, "nt")
    return z, jnp.minimum(z, 0.0) - jnp.log(1.0 + jnp.exp(-jnp.abs(z)))


def _sb_running(start, sums, inclusive):
    R = start.shape[0]
    n = sums.shape[0] // R
    vals, cur = [], start
    for b in range(n):
        nxt = cur + sums[b * R:(b + 1) * R]
        vals.append(nxt if inclusive else cur)
        cur = nxt
    return (vals[0] if n == 1 else jnp.concatenate(vals, axis=0)), cur


def _sb_head_masks():
    low = lax.broadcasted_iota(jnp.int32, (SB_BLOCK, LANES), 1) < SB_HEAD_DIM
    return [low if h % 2 == 0 else jnp.logical_not(low) for h in range(SB_HEADS)]


def _sb_stack_heads(ref, scale=1.0):
    mine = _sb_head_masks()
    return jnp.concatenate([jnp.where(mine[h], ref[:, (h // 2) * LANES:(h // 2 + 1) * LANES] * scale, 0.0).astype(MXU_DTYPE)
                            for h in range(SB_HEADS)], axis=0)


def _sb_block_masks():
    B = SB_BLOCK
    row = lax.broadcasted_iota(jnp.int32, (B, B), 0)
    col = lax.broadcasted_iota(jnp.int32, (B, B), 1)
    row4 = lax.broadcasted_iota(jnp.int32, (SB_HEADS * B, B), 0) & (B - 1)
    col4 = lax.broadcasted_iota(jnp.int32, (SB_HEADS * B, B), 1)
    return (row > col).astype(MXU_DTYPE), (row < col).astype(MXU_DTYPE), col4 < row4


def _sb_fwd(proj, kv, name):
    L = proj.shape[0]
    B, H = SB_BLOCK, SB_HEADS

    def body(q_ref, kv_ref, y_ref, c_ref):
        i = pl.program_id(0)
        low = _sb_head_masks()[0]
        after, _, strict = _sb_block_masks()
        qx = _sb_stack_heads(q_ref, SB_SCALE)

        def sweep(js, c, accs, masked):
            _, kps, vps = _sb_kv_blocks(kv_ref, js)
            z, lb = _sb_logits(qx, kps)
            lom = lb - z
            if masked:
                lom = jnp.where(strict, lom, 0.0)
            before_block, c = _sb_running(c, jnp.sum(lom, axis=1, keepdims=True), False)
            a = jnp.exp(lb + _split_dot(lom, after) + before_block)
            if masked:
                a = jnp.where(strict, a, 0.0)
            a = a.astype(MXU_DTYPE)
            new_accs = list(accs)
            for b in range(len(js)):
                o = _sb_pair_dots(a[b * H * B:(b + 1) * H * B], [vps[b]], "nn")
                for p in range(H // 2):
                    new_accs[p] = new_accs[p] + jnp.where(low, o[2 * p * B:(2 * p + 1) * B], o[(2 * p + 1) * B:(2 * p + 2) * B])
            return c, new_accs

        c, accs = sweep([i], jnp.zeros((H * B, 1), F32), [jnp.zeros((B, LANES), F32)] * (H // 2), True)

        W = SB_SWEEP

        def wide(it, carry):
            j = i - 1 - W * it
            c, accs = sweep([j - b for b in range(W)], carry[0], list(carry[1:]), False)
            return (c,) + tuple(accs)

        def one(it, carry):
            c, accs = sweep([i % W - 1 - it], carry[0], list(carry[1:]), False)
            return (c,) + tuple(accs)

        carry = lax.fori_loop(0, i // W, wide, (c,) + tuple(accs))
        carry = lax.fori_loop(0, i % W, one, carry)
        for p in range(H // 2):
            y_ref[:, p * LANES:(p + 1) * LANES] = carry[1 + p].astype(y_ref.dtype)
        lane = lax.broadcasted_iota(jnp.int32, (B, LANES), 1)
        ct = jnp.zeros((B, LANES), F32)
        for h in range(H):
            ct = jnp.where(lane == h, carry[0][h * B:(h + 1) * B], ct)
        c_ref[...] = ct

    return pl.pallas_call(
        body, name=name, grid=(L // B,),
        in_specs=[pl.BlockSpec((B, SB_WIDTH), lambda i: (i, P_SB // SB_WIDTH)), pl.BlockSpec((L, 2 * SB_WIDTH), lambda i: (0, 0))],
        out_specs=[pl.BlockSpec((B, SB_WIDTH), lambda i: (i, 0)), pl.BlockSpec((B, LANES), lambda i: (i, 0))],
        out_shape=[jax.ShapeDtypeStruct((L, SB_WIDTH), MXU_DTYPE), jax.ShapeDtypeStruct((L, LANES), F32)],
        compiler_params=_params(("arbitrary",)),
    )(proj, kv)


def _sb_bwd(proj, kv, dy, ctot, name):
    L = proj.shape[0]
    B, H = SB_BLOCK, SB_HEADS
    nblk = L // B

    def body(q_ref, kv_ref, do_ref, ct_ref, dq_ref, dk_hbm, dv_hbm, dk_acc, dv_acc):
        i = pl.program_id(0)

        @pl.when(i == 0)
        def _():
            dk_acc[...] = jnp.zeros_like(dk_acc)
            dv_acc[...] = jnp.zeros_like(dv_acc)

        low = _sb_head_masks()[0]
        after, before, strict = _sb_block_masks()
        qx, dox = _sb_stack_heads(q_ref, SB_SCALE), _sb_stack_heads(do_ref)
        ct = ct_ref[...]
        ctot = jnp.concatenate([_pick_lane(ct, h) for h in range(H)], axis=0)

        def sweep(js, p, e, dqs, masked):
            n = len(js)
            rows, kps, vps = _sb_kv_blocks(kv_ref, js)
            z, lb = _sb_logits(qx, kps)
            sig = jnp.exp(lb)
            lom = lb - z
            if masked:
                lom = jnp.where(strict, lom, 0.0)
            through_block, p = _sb_running(p, jnp.sum(lom, axis=1, keepdims=True), True)
            right_of_block = (ctot if n == 1 else jnp.concatenate([ctot] * n, axis=0)) - through_block
            a = jnp.exp(lb + _split_dot(lom, after) + right_of_block)
            if masked:
                a = jnp.where(strict, a, 0.0)
            ea = _sb_pair_dots(dox, vps, "nt") * a
            left_of_block, e = _sb_running(e, jnp.sum(ea, axis=1, keepdims=True), False)
            dlom = left_of_block + jnp.dot(ea.astype(MXU_DTYPE), before, preferred_element_type=F32)
            if masked:
                dlom = jnp.where(strict, dlom, 0.0)
            dz = (ea * (1.0 - sig) - dlom * sig).astype(MXU_DTYPE)
            ab = a.astype(MXU_DTYPE)
            new_dq = list(dqs)
            for b in range(n):
                for pr in range(H // 2):
                    heads = slice(2 * pr * B, (2 * pr + 2) * B)
                    both, cols = slice((b * H + 2 * pr) * B, (b * H + 2 * pr + 2) * B), slice(pr * LANES, (pr + 1) * LANES)
                    dqp = jnp.dot(dz[both], kps[b][pr], preferred_element_type=F32)
                    new_dq[pr] = new_dq[pr] + jnp.where(low, dqp[:B], dqp[B:])
                    dk_acc[rows[b], cols] += lax.dot_general(dz[both], qx[heads], _DIMS["tn"], preferred_element_type=F32)
                    dv_acc[rows[b], cols] += lax.dot_general(ab[both], dox[heads], _DIMS["tn"], preferred_element_type=F32)
            return p, e, new_dq

        W = SB_SWEEP

        def wide(it, carry):
            p, e, dqs = sweep([W * it + b for b in range(W)], carry[0], carry[1], list(carry[2:]), False)
            return (p, e) + tuple(dqs)

        def one(it, carry):
            p, e, dqs = sweep([i - i % W + it], carry[0], carry[1], list(carry[2:]), False)
            return (p, e) + tuple(dqs)

        zero = jnp.zeros((H * B, 1), F32)
        carry = lax.fori_loop(0, i // W, wide, (zero, zero) + (jnp.zeros((B, LANES), F32),) * (H // 2))
        carry = lax.fori_loop(0, i % W, one, carry)
        _, _, dqs = sweep([i], carry[0], carry[1], list(carry[2:]), True)
        for pr in range(H // 2):
            dq_ref[:, pr * LANES:(pr + 1) * LANES] = (dqs[pr] * SB_SCALE).astype(dq_ref.dtype)

        @pl.when(i == nblk - 1)
        def _():
            pltpu.sync_copy(dk_acc, dk_hbm)
            pltpu.sync_copy(dv_acc, dv_hbm)

    hbm = pl.BlockSpec(memory_space=pl.ANY)
    acc = jax.ShapeDtypeStruct((L, SB_WIDTH), F32)
    return pl.pallas_call(
        body, name=name, grid=(nblk,),
        in_specs=[pl.BlockSpec((B, SB_WIDTH), lambda i: (i, P_SB // SB_WIDTH)), pl.BlockSpec((L, 2 * SB_WIDTH), lambda i: (0, 0)),
                  pl.BlockSpec((B, SB_WIDTH), lambda i: (i, Y_SB // SB_WIDTH)), pl.BlockSpec((B, LANES), lambda i: (i, 0))],
        out_specs=[pl.BlockSpec((B, SB_WIDTH), lambda i: (i, 0)), hbm, hbm],
        out_shape=[jax.ShapeDtypeStruct((L, SB_WIDTH), MXU_DTYPE), acc, acc],
        scratch_shapes=[pltpu.VMEM((L, SB_WIDTH), F32), pltpu.VMEM((L, SB_WIDTH), F32)],
        compiler_params=_params(("arbitrary",)),
    )(proj, kv, dy, ctot)


def _prep_w_in(w):
    sc, qkv, gz, gab, sb = w[:, 0:768], w[:, 768:2304], w[:, 2304:2816], w[:, 2816:2824], w[:, 2824:3592]
    pad = jnp.zeros((w.shape[0], P_END - D_IN_PROJ), w.dtype)
    return jnp.concatenate([qkv, gz, sc, sb, gab, pad], axis=1).astype(MXU_DTYPE)


def _unprep_dw_in(dw):
    qkv, gz, sc, sb, gab = dw[:, P_QKV:P_GZ], dw[:, P_GZ:P_SC], dw[:, P_SC:P_SB], dw[:, P_SB:P_GAB], dw[:, P_GAB:P_GAB + 8]
    return jnp.concatenate([sc, qkv, gz, gab, sb], axis=1)


def _prep_w_out(w):
    return jnp.concatenate([w[256:768], w[0:256], w[768:]], axis=0).astype(MXU_DTYPE)


def _unprep_dw_out(dw):
    return jnp.concatenate([dw[512:768], dw[0:512], dw[768:]], axis=0)


def _pad_lanes(v):
    return jnp.zeros((1, LANES), F32).at[0, :v.shape[0]].set(v)


def _layer_fwd(x, p, l):
    L = x.shape[0]
    tm = min(1024, L)
    n = f"l{l}_"
    h = _rms_fwd(x, p["norm_mix"], n + "rms_mix")
    proj = _matmul(h, p["w_in"], "nn", F32, n + "mm_in", tm, 768, 1024)
    cb = SC_WIDTH
    y_sc = _conv_fwd((proj, P_SC // cb + 1), p["w_sconv"], SC_KERNEL, cb, 1, MXU_DTYPE, n + "sconv",
                     x2=(proj, P_SC // cb + 2), gate=(proj, P_SC // cb))
    cqkv = _conv_fwd((proj, 0), p["w_gconv"], GDN_CONV, 256, 6, F32, n + "gconv")
    y_gdn, s_all = _gdn_fwd(cqkv, proj, p["a_log"], p["dt_bias"], p["gdn_norm"], n + "gdn")
    kv = proj[:, P_SB + SB_WIDTH:P_SB + 3 * SB_WIDTH].astype(MXU_DTYPE)
    y_sb, ctot = _sb_fwd(proj, kv, n + "sb")
    ycat = jnp.concatenate([y_gdn, y_sc, y_sb], axis=1)
    x1 = _matmul(ycat, p["w_out"], "nn", F32, n + "mm_out", tm, 512, 1024, resid=x)
    h2 = _rms_fwd(x1, p["norm_ffn"], n + "rms_ffn")
    up = _matmul(h2, p["w_up"], "nn", F32, n + "mm_up", tm, 512, 1024)
    act = _ffn_act_fwd(up, p["w_fconv"], n + "ffn_act")
    x2 = _matmul(act, p["w_down"], "nn", F32, n + "mm_down", tm, 512, 1408, resid=x1)
    saved = dict(x=x, h=h, proj=proj, cqkv=cqkv, s_all=s_all, kv=kv, ctot=ctot, ycat=ycat, x1=x1, h2=h2, up=up, act=act)
    return x2, saved


def _layer_bwd(dx2, p, s, l):
    L = dx2.shape[0]
    tm, tkl = min(1024, L), min(1024, L)
    n = f"l{l}_"
    g = {}
    dx2b = dx2.astype(MXU_DTYPE)
    g["w_ffn_down"] = _matmul(s["act"], dx2b, "tn", F32, n + "mm_ddown", 1408, 512, tkl)
    dact = _matmul(dx2b, p["w_down"], "nt", F32, n + "mm_dact", tm, 1408, 1024)
    dup, dwf = _ffn_act_bwd(s["up"], p["w_fconv"], dact, n + "dffn_act")
    g["w_ffn_conv"] = dwf[:FFN_CONV]
    g["w_ffn_up"] = _matmul(s["h2"], dup, "tn", F32, n + "mm_dup", 1024, 512, tkl)
    dh2 = _matmul(dup, p["w_up"], "nt", F32, n + "mm_dh2", tm, 512, 1408)
    dx1, dwn = _rms_bwd(s["x1"], p["norm_ffn"], dh2, dx2, n + "drms_ffn")
    g["w_norm_ffn"] = dwn[0]

    dx1b = dx1.astype(MXU_DTYPE)
    g["w_mix_out"] = _unprep_dw_out(_matmul(s["ycat"], dx1b, "tn", F32, n + "mm_dout", 1024, 512, tkl))
    dycat = _matmul(dx1b, p["w_out"], "nt", F32, n + "mm_dycat", tm, 512, 1024)
    proj = s["proj"]
    cb = SC_WIDTH
    dsc_c, dsc_h, dsc_b, dws = _conv_bwd((proj, P_SC // cb + 1), p["w_sconv"], (dycat, Y_SC // cb), SC_KERNEL, cb, 1,
                                         MXU_DTYPE, n + "dsconv", x2=(proj, P_SC // cb + 2), gate=(proj, P_SC // cb))
    g["w_sconv"] = dws[:SC_KERNEL]
    dcqkv, dgz, dgab, dal, ddb, dgn = _gdn_bwd(s["cqkv"], proj, p["a_log"], p["dt_bias"], p["gdn_norm"], s["s_all"], dycat,
                                               n + "dgdn")
    g["gdn_a_log"], g["gdn_dt_bias"], g["w_gdn_norm"] = dal[0, :GDN_HEADS], ddb[0, :GDN_HEADS], dgn[0]
    dqkv, dwg = _conv_bwd((proj, 0), p["w_gconv"], (dcqkv, 0), GDN_CONV, 256, 6, MXU_DTYPE, n + "dgconv")
    g["w_gdn_conv"] = dwg[:GDN_CONV]
    dq, dk, dv = _sb_bwd(proj, s["kv"], dycat, s["ctot"], n + "dsb")
    dproj = jnp.concatenate(
        [dqkv, dgz, dsc_b, dsc_c, dsc_h, dq, dk.astype(MXU_DTYPE), dv.astype(MXU_DTYPE), dgab.astype(MXU_DTYPE),
         jnp.zeros((L, P_END - P_GAB - LANES), MXU_DTYPE)], axis=1)
    g["w_mix_in"] = _unprep_dw_in(_matmul(s["h"], dproj, "tn", F32, n + "mm_din", 1024, 768, tkl))
    dh = _matmul(dproj, p["w_in"], "nt", F32, n + "mm_dh", tm, 512, 768)
    dx, dwm = _rms_bwd(s["x"], p["norm_mix"], dh, dx1, n + "drms_mix")
    g["w_norm_mix"] = dwm[0]
    return dx, g


WEIGHTS = ["w_norm_mix", "w_mix_in", "w_sconv", "w_gdn_conv", "gdn_a_log", "gdn_dt_bias", "w_gdn_norm", "w_mix_out",
           "w_norm_ffn", "w_ffn_up", "w_ffn_conv", "w_ffn_down", "w_norm_final"]


def _local_step(x, w, target):
    layers = []
    for l in range(DEPTH):
        layers.append(dict(
            norm_mix=w["w_norm_mix"][l][None], w_in=_prep_w_in(w["w_mix_in"][l]), w_sconv=w["w_sconv"][l],
            w_gconv=w["w_gdn_conv"][l], a_log=_pad_lanes(w["gdn_a_log"][l]), dt_bias=_pad_lanes(w["gdn_dt_bias"][l]),
            gdn_norm=w["w_gdn_norm"][l][None], w_out=_prep_w_out(w["w_mix_out"][l]), norm_ffn=w["w_norm_ffn"][l][None],
            w_up=w["w_ffn_up"][l].astype(MXU_DTYPE), w_fconv=w["w_ffn_conv"][l], w_down=w["w_ffn_down"][l].astype(MXU_DTYPE)))
    saved = []
    for l in range(DEPTH):
        x, s = _layer_fwd(x, layers[l], l)
        saved.append(s)
    loss, dx, dwf = _loss_head(x, w["w_norm_final"][None], target, "loss_head")
    grads = [None] * DEPTH
    for l in reversed(range(DEPTH)):
        dx, grads[l] = _layer_bwd(dx, layers[l], saved[l], l)
    out = {k: jnp.stack([grads[l][k] for l in range(DEPTH)]) for k in WEIGHTS if k != "w_norm_final"}
    out["w_norm_final"] = dwf[0]
    return loss, dx, out


N_CHIPS = 4
_HBM = pl.BlockSpec(memory_space=pl.ANY)


def _other_chips(x, y):
    return [(1 - x, y), (x, 1 - y), (1 - x, 1 - y)]


def _remote(src, dst, send_sems, recv_sems, k, to):
    return pltpu.make_async_remote_copy(src_ref=src, dst_ref=dst, send_sem=send_sems.at[k], recv_sem=recv_sems.at[k],
                                        device_id=to, device_id_type=pl.DeviceIdType.MESH)


def _all_gather(bufs, name):
    nb = len(bufs)

    def body(*refs):
        ins, outs = refs[:nb], refs[nb:2 * nb]
        send_sems, recv_sems, local_sems = refs[2 * nb:]
        x, y, c = lax.axis_index("x"), lax.axis_index("y"), lax.axis_index("c")
        me, sibling, chips = (x, y, c), (x, y, 1 - c), _other_chips(x, y)

        def copy(b, k, block, to, src=None):
            slot = outs[b].at[4 * block[0] + 2 * block[1] + block[2]]
            return _remote(slot if src is None else src, slot, send_sems.at[b], recv_sems.at[b], k, to)

        local = [pltpu.make_async_copy(ins[b], outs[b].at[4 * x + 2 * y + c], local_sems.at[b]) for b in range(nb)]
        first = [copy(b, 0, me, sibling, src=ins[b]) for b in range(nb)]
        first += [copy(b, 1 + j, me, (*chip, c), src=ins[b]) for j, chip in enumerate(chips) for b in range(nb)]
        for cp in local + first:
            cp.start()
        passed = []
        for j, chip in enumerate(chips):
            for b in range(nb):
                copy(b, 1 + j, (*chip, c), me).wait_recv()
                passed.append(copy(b, 4 + j, (*chip, c), sibling))
                passed[-1].start()
        for b in range(nb):
            copy(b, 0, sibling, me).wait_recv()
        for j, chip in enumerate(chips):
            for b in range(nb):
                copy(b, 4 + j, (*chip, 1 - c), me).wait_recv()
        for cp in first + passed:
            cp.wait_send()
        for cp in local:
            cp.wait()

    return pl.pallas_call(
        body, name=name, in_specs=[_HBM] * nb, out_specs=[_HBM] * nb,
        out_shape=[jax.ShapeDtypeStruct((N_DEV,) + b.shape, b.dtype) for b in bufs],
        scratch_shapes=[pltpu.SemaphoreType.DMA((nb, N_DEV - 1)), pltpu.SemaphoreType.DMA((nb, N_DEV - 1)),
                        pltpu.SemaphoreType.DMA((nb,))],
    )(*bufs)


def _to_sibling(bufs, name):
    nb = len(bufs)

    def body(*refs):
        ins, outs = refs[:nb], refs[nb:2 * nb]
        send_sems, recv_sems = refs[2 * nb:]
        x, y, c = lax.axis_index("x"), lax.axis_index("y"), lax.axis_index("c")
        copies = [_remote(ins[b].at[1 - c], outs[b], send_sems, recv_sems, b, (x, y, 1 - c)) for b in range(nb)]
        for cp in copies:
            cp.start()
        for cp in copies:
            cp.wait()

    return pl.pallas_call(
        body, name=name, in_specs=[_HBM] * nb, out_specs=[_HBM] * nb,
        out_shape=[jax.ShapeDtypeStruct(b.shape[1:], b.dtype) for b in bufs],
        scratch_shapes=[pltpu.SemaphoreType.DMA((nb,)), pltpu.SemaphoreType.DMA((nb,))],
    )(*bufs)


def _to_chips(bufs, name):
    nb = len(bufs)

    def body(*refs):
        ins, outs = refs[:nb], refs[nb:2 * nb]
        send_sems, recv_sems, local_sems = refs[2 * nb:]
        x, y, c = lax.axis_index("x"), lax.axis_index("y"), lax.axis_index("c")
        here = 2 * x + y
        local = [pltpu.make_async_copy(ins[b].at[here], outs[b].at[here], local_sems.at[b]) for b in range(nb)]
        remote = [_remote(ins[b].at[2 * px + py], outs[b].at[here], send_sems.at[b], recv_sems.at[b], j, (px, py, c))
                  for j, (px, py) in enumerate(_other_chips(x, y)) for b in range(nb)]
        for cp in local + remote:
            cp.start()
        for cp in remote:
            cp.wait()
        for cp in local:
            cp.wait()

    return pl.pallas_call(
        body, name=name, in_specs=[_HBM] * nb, out_specs=[_HBM] * nb,
        out_shape=[jax.ShapeDtypeStruct(b.shape, b.dtype) for b in bufs],
        scratch_shapes=[pltpu.SemaphoreType.DMA((nb, N_CHIPS - 1)), pltpu.SemaphoreType.DMA((nb, N_CHIPS - 1)),
                        pltpu.SemaphoreType.DMA((nb,))],
    )(*bufs)


def _pair_sum(a, b, row_tile, name):
    n, D0, R, C = a.shape

    def body(a_ref, b_ref, o_ref):
        o_ref[...] = (a_ref[...].astype(F32) + b_ref[...].astype(F32)).astype(o_ref.dtype)

    blk = pl.BlockSpec((1, 1, row_tile, C), lambda s, l, i: (s, l, i, 0))
    return pl.pallas_call(
        body, name=name, grid=(n, D0, R // row_tile), in_specs=[blk, blk], out_specs=blk,
        out_shape=jax.ShapeDtypeStruct(a.shape, a.dtype), compiler_params=_params(("parallel", "parallel", "parallel")),
    )(a, b)


def _sum_sources(recv, row_tile, name):
    n, R, _ = recv.shape

    def body(r_ref, o_ref):
        acc = r_ref[0].astype(F32)
        for s in range(1, n):
            acc = acc + r_ref[s].astype(F32)
        o_ref[...] = acc

    return pl.pallas_call(
        body, name=name, grid=(R // row_tile,),
        in_specs=[pl.BlockSpec((n, row_tile, LANES), lambda i: (0, i, 0))],
        out_specs=pl.BlockSpec((row_tile, LANES), lambda i: (i, 0)),
        out_shape=jax.ShapeDtypeStruct((R, LANES), F32),
        compiler_params=_params(("parallel",)),
    )(recv)


def _adamw_math(g, w, m, v):
    nm = ADAM_B1 * m + (1.0 - ADAM_B1) * g
    nv = ADAM_B2 * v + (1.0 - ADAM_B2) * (g * g)
    m_hat = nm / (1.0 - ADAM_B1 ** ADAM_STEP)
    v_hat = nv / (1.0 - ADAM_B2 ** ADAM_STEP)
    return -ADAM_LR * (m_hat / (jnp.sqrt(v_hat) + ADAM_EPS) + ADAM_WD * w), nm, nv


def _sum_adamw(recv, w, m, v, row_tile, name):
    D0, R, C = w.shape
    n = recv.shape[0]

    def body(r_ref, w_ref, m_ref, v_ref, g_ref, d_ref, nm_ref, nv_ref):
        g = r_ref[0, 0].astype(F32)
        for s in range(1, n):
            g = g + r_ref[s, 0].astype(F32)
        g_ref[0] = g
        d_ref[0], nm_ref[0], nv_ref[0] = _adamw_math(g, w_ref[0], m_ref[0], v_ref[0])

    blk = pl.BlockSpec((1, row_tile, C), lambda l, i: (l, i, 0))
    out = jax.ShapeDtypeStruct((D0, R, C), F32)
    return pl.pallas_call(
        body, name=name, grid=(D0, R // row_tile),
        in_specs=[pl.BlockSpec((n, 1, row_tile, C), lambda l, i: (0, l, i, 0)), blk, blk, blk],
        out_specs=[blk] * 4, out_shape=[out] * 4,
        compiler_params=_params(("parallel", "parallel")),
    )(recv, w, m, v)


def _adamw(g, w, m, v, row_tile, name):
    R = g.shape[0]

    def body(g_ref, w_ref, m_ref, v_ref, d_ref, nm_ref, nv_ref):
        d_ref[...], nm_ref[...], nv_ref[...] = _adamw_math(g_ref[...], w_ref[...], m_ref[...], v_ref[...])

    blk = pl.BlockSpec((row_tile, LANES), lambda i: (i, 0))
    out = jax.ShapeDtypeStruct((R, LANES), F32)
    return pl.pallas_call(
        body, name=name, grid=(R // row_tile,), in_specs=[blk] * 4, out_specs=[blk] * 3, out_shape=[out] * 3,
        compiler_params=_params(("parallel",)),
    )(g, w, m, v)


def _pack(arrs, rows, dtype):
    flat = jnp.concatenate([a.reshape(-1).astype(dtype) for a in arrs])
    return jnp.pad(flat, (0, rows * LANES - flat.shape[0])).reshape(rows, LANES)


def _unpack(buf, shapes):
    lead = buf.shape[:-2]
    flat = buf.reshape(lead + (-1,))
    out, off = [], 0
    for shp in shapes:
        n = 1
        for d in shp:
            n *= d
        out.append(flat[..., off:off + n].reshape(lead + tuple(shp)))
        off += n
    return out


BIG = ["w_mix_in", "w_mix_out", "w_ffn_up", "w_ffn_down"]
BIG_AXIS = {"w_mix_in": 2, "w_mix_out": 1, "w_ffn_up": 2, "w_ffn_down": 1}
CONV = ["w_sconv", "w_gdn_conv", "w_ffn_conv"]
REPL = ["w_norm_mix", "gdn_a_log", "gdn_dt_bias", "w_gdn_norm", "w_norm_ffn", "w_norm_final"]
BIG_ROW_TILE = {"w_mix_in": 512, "w_mix_out": 128, "w_ffn_up": 512, "w_ffn_down": 352}
SMALL_ROWS = 416
CONV_ROWS = 48


def kernel(x, w_norm_mix, w_mix_in, w_sconv, w_gdn_conv, gdn_a_log, gdn_dt_bias, w_gdn_norm, w_mix_out, w_norm_ffn, w_ffn_up, w_ffn_conv, w_ffn_down, w_norm_final, loss_target, m_w_norm_mix, m_w_mix_in, m_w_sconv, m_w_gdn_conv, m_gdn_a_log, m_gdn_dt_bias, m_w_gdn_norm, m_w_mix_out, m_w_norm_ffn, m_w_ffn_up, m_w_ffn_conv, m_w_ffn_down, m_w_norm_final, v_w_norm_mix, v_w_mix_in, v_w_sconv, v_w_gdn_conv, v_gdn_a_log, v_gdn_dt_bias, v_w_gdn_norm, v_w_mix_out, v_w_norm_ffn, v_w_ffn_up, v_w_ffn_conv, v_w_ffn_down, v_w_norm_final):
    w = dict(w_norm_mix=w_norm_mix, w_mix_in=w_mix_in, w_sconv=w_sconv, w_gdn_conv=w_gdn_conv, gdn_a_log=gdn_a_log,
             gdn_dt_bias=gdn_dt_bias, w_gdn_norm=w_gdn_norm, w_mix_out=w_mix_out, w_norm_ffn=w_norm_ffn, w_ffn_up=w_ffn_up,
             w_ffn_conv=w_ffn_conv, w_ffn_down=w_ffn_down, w_norm_final=w_norm_final)
    m = dict(w_norm_mix=m_w_norm_mix, w_mix_in=m_w_mix_in, w_sconv=m_w_sconv, w_gdn_conv=m_w_gdn_conv, gdn_a_log=m_gdn_a_log,
             gdn_dt_bias=m_gdn_dt_bias, w_gdn_norm=m_w_gdn_norm, w_mix_out=m_w_mix_out, w_norm_ffn=m_w_norm_ffn,
             w_ffn_up=m_w_ffn_up, w_ffn_conv=m_w_ffn_conv, w_ffn_down=m_w_ffn_down, w_norm_final=m_w_norm_final)
    v = dict(w_norm_mix=v_w_norm_mix, w_mix_in=v_w_mix_in, w_sconv=v_w_sconv, w_gdn_conv=v_w_gdn_conv, gdn_a_log=v_gdn_a_log,
             gdn_dt_bias=v_gdn_dt_bias, w_gdn_norm=v_w_gdn_norm, w_mix_out=v_w_mix_out, w_norm_ffn=v_w_norm_ffn,
             w_ffn_up=v_w_ffn_up, w_ffn_conv=v_w_ffn_conv, w_ffn_down=v_w_ffn_down, w_norm_final=v_w_norm_final)
    me = 4 * lax.axis_index("x") + 2 * lax.axis_index("y") + lax.axis_index("c")
    conv_shapes = [w[k].shape for k in CONV]

    gathered = _all_gather([w[k].astype(MXU_DTYPE) for k in BIG] + [_pack([w[k] for k in CONV], CONV_ROWS, F32)],
                           "gather_weights")
    full = dict(w)
    for k, got in zip(BIG, gathered):
        full[k] = jnp.concatenate([got[s] for s in range(N_DEV)], axis=BIG_AXIS[k])
    for k, got in zip(CONV, _unpack(gathered[-1], conv_shapes)):
        full[k] = jnp.concatenate([got[s] for s in range(N_DEV)], axis=2)

    loss, dx, grads = _local_step(x[0], full, loss_target[0])

    small = CONV + REPL
    core = lax.axis_index("c")
    by_core = []
    for k in BIG:
        piece = jnp.split(grads[k], N_DEV, axis=BIG_AXIS[k])
        by_core.append(jnp.stack([jnp.stack([piece[2 * ch + p] for ch in range(N_CHIPS)]) for p in range(2)]).astype(MXU_DTYPE))
    from_sibling = _to_sibling(by_core, "grads_to_sibling")
    chip_sums = [_pair_sum(lax.dynamic_index_in_dim(mine, core, 0, keepdims=False), theirs, BIG_ROW_TILE[k], "pair_sum_" + k)
                 for k, mine, theirs in zip(BIG, by_core, from_sibling)]
    received = _to_chips(chip_sums, "grads_to_chips")
    g, delta, new_m, new_v = {}, {}, {}, {}
    for k, got in zip(BIG, received):
        g[k], delta[k], new_m[k], new_v[k] = _sum_adamw(got, w[k], m[k], v[k], BIG_ROW_TILE[k], "adamw_" + k)
    small_parts = _all_gather([_pack([grads[k] for k in small], SMALL_ROWS, F32)], "gather_small_grads")[0]
    g_small = _unpack(_sum_sources(small_parts, SMALL_ROWS, "sum_small"), [grads[k].shape for k in small])
    for k, gs in zip(small, g_small):
        g[k] = lax.dynamic_slice_in_dim(gs, me * w[k].shape[2], w[k].shape[2], axis=2) if k in CONV else gs

    small_shapes = [w[k].shape for k in small]
    small_rows = -(-sum(w[k].size for k in small) // (SUBLANES * LANES)) * SUBLANES
    d_sm, m_sm, v_sm = _adamw(_pack([g[k] for k in small], small_rows, F32), _pack([w[k] for k in small], small_rows, F32),
                              _pack([m[k] for k in small], small_rows, F32), _pack([v[k] for k in small], small_rows, F32),
                              small_rows, "adamw_small")
    for dst, small_buf in ((delta, d_sm), (new_m, m_sm), (new_v, v_sm)):
        dst.update(zip(small, _unpack(small_buf, small_shapes)))

    loss_all = lax.psum(loss[0, 0], ("x", "y", "c"))
    return (loss_all, dx[None], *[g[k] for k in WEIGHTS], *[delta[k] for k in WEIGHTS], *[new_m[k] for k in WEIGHTS],
            *[new_v[k] for k in WEIGHTS])
```

```python
import functools

import jax
import jax.numpy as jnp
from jax import lax
from jax.experimental import pallas as pl
from jax.experimental.pallas import tpu as pltpu

F32 = jnp.float32
MXU_DTYPE = jnp.bfloat16
HIGHEST = lax.Precision.HIGHEST

D_MODEL = 1024
DEPTH = 2
SC_WIDTH = 256
SC_KERNEL = 3
GDN_WIDTH = 512
GDN_HEADS = 4
GDN_HEAD_DIM = 128
GDN_CONV = 4
GDN_CHUNK = 64
SB_WIDTH = 256
SB_HEADS = 4
SB_HEAD_DIM = 64
SB_BLOCK = 128
SB_SWEEP = 8
D_FF = 2816
FFN_CONV = 3
NORM_EPS = 1e-6
D_IN_PROJ = 3592
ADAM_LR, ADAM_B1, ADAM_B2, ADAM_EPS, ADAM_WD, ADAM_STEP = 0.001, 0.9, 0.999, 1e-08, 0.01, 10

N_DEV = 8
LANES = 128
SUBLANES = 8
VMEM_LIMIT = 48 * 1024 * 1024

P_QKV, P_GZ, P_SC, P_SB, P_GAB, P_END = 0, 1536, 2048, 2816, 3584, 3840
Y_GDN, Y_SC, Y_SB = 0, 512, 768


def _params(semantics):
    return pltpu.CompilerParams(dimension_semantics=semantics, vmem_limit_bytes=VMEM_LIMIT)


_DIMS = {"nn": (((1,), (0,)), ((), ())), "nt": (((1,), (1,)), ((), ())), "tn": (((0,), (0,)), ((), ()))}


def _matmul(a, b, mode, out_dtype, name, tm, tn, tk, resid=None):
    if mode == "tn":
        (K, M), (K2, N) = a.shape, b.shape
    elif mode == "nt":
        (M, K), (N, K2) = a.shape, b.shape
    else:
        (M, K), (K2, N) = a.shape, b.shape
    assert K == K2 and M % tm == 0 and N % tn == 0 and K % tk == 0, (name, a.shape, b.shape, tm, tn, tk)
    nk = K // tk
    has_resid = resid is not None
    assert nk == 1 or out_dtype == F32, name

    def body(*refs):
        if has_resid:
            a_ref, b_ref, r_ref, o_ref = refs
        else:
            a_ref, b_ref, o_ref = refs
        k = pl.program_id(2)
        part = lax.dot_general(a_ref[...], b_ref[...], _DIMS[mode], preferred_element_type=F32)
        if nk == 1:
            o_ref[...] = ((part + r_ref[...]) if has_resid else part).astype(out_dtype)
        else:
            @pl.when(k == 0)
            def _():
                o_ref[...] = (part + r_ref[...]) if has_resid else part

            @pl.when(k > 0)
            def _():
                o_ref[...] += part

    a_spec = pl.BlockSpec((tk, tm), lambda i, j, k: (k, i)) if mode == "tn" else pl.BlockSpec((tm, tk), lambda i, j, k: (i, k))
    b_spec = pl.BlockSpec((tn, tk), lambda i, j, k: (j, k)) if mode == "nt" else pl.BlockSpec((tk, tn), lambda i, j, k: (k, j))
    o_spec = pl.BlockSpec((tm, tn), lambda i, j, k: (i, j))
    in_specs = [a_spec, b_spec] + ([o_spec] if has_resid else [])
    args = (a, b) + ((resid,) if has_resid else ())
    return pl.pallas_call(
        body, name=name, grid=(M // tm, N // tn, nk), in_specs=in_specs, out_specs=o_spec,
        out_shape=jax.ShapeDtypeStruct((M, N), out_dtype),
        compiler_params=_params(("parallel", "parallel", "arbitrary")),
    )(*args)


def _rms(x, w):
    return x * lax.rsqrt(jnp.mean(x * x, axis=-1, keepdims=True) + NORM_EPS) * w


ROW_TILE = 512


def _rms_fwd(x, w, name):
    L, Dm = x.shape

    def body(x_ref, w_ref, h_ref):
        h_ref[...] = _rms(x_ref[...], w_ref[...]).astype(h_ref.dtype)

    return pl.pallas_call(
        body, name=name, grid=(L // ROW_TILE,),
        in_specs=[pl.BlockSpec((ROW_TILE, Dm), lambda i: (i, 0)), pl.BlockSpec((1, Dm), lambda i: (0, 0))],
        out_specs=pl.BlockSpec((ROW_TILE, Dm), lambda i: (i, 0)),
        out_shape=jax.ShapeDtypeStruct((L, Dm), MXU_DTYPE),
        compiler_params=_params(("parallel",)),
    )(x, w)


def _rms_bwd(x, w, dh, dres, name):
    L, Dm = x.shape

    def body(x_ref, w_ref, dh_ref, dres_ref, dx_ref, dw_ref):
        _, vjp = jax.vjp(_rms, x_ref[...], w_ref[...])
        dx, dw = vjp(dh_ref[...])
        dx_ref[...] = dres_ref[...] + dx

        @pl.when(pl.program_id(0) == 0)
        def _():
            dw_ref[...] = jnp.zeros_like(dw_ref)

        dw_ref[...] += dw

    row = pl.BlockSpec((ROW_TILE, Dm), lambda i: (i, 0))
    vec = pl.BlockSpec((1, Dm), lambda i: (0, 0))
    return pl.pallas_call(
        body, name=name, grid=(L // ROW_TILE,), in_specs=[row, vec, row, row], out_specs=[row, vec],
        out_shape=[jax.ShapeDtypeStruct((L, Dm), F32), jax.ShapeDtypeStruct((1, Dm), F32)],
        compiler_params=_params(("arbitrary",)),
    )(x, w, dh, dres)


def _loss_head(x, w, target, name):
    L, Dm = x.shape

    def block_loss(xb, wb, tb):
        err = _rms(xb, wb) - tb
        return 0.5 * jnp.sum(jnp.sum(err * err, axis=-1, keepdims=True) * (1.0 / Dm), axis=0, keepdims=True)

    def body(x_ref, w_ref, t_ref, loss_ref, dx_ref, dw_ref):
        val, vjp = jax.vjp(lambda xb, wb: block_loss(xb, wb, t_ref[...]), x_ref[...], w_ref[...])
        dx, dw = vjp(jnp.ones_like(val))
        dx_ref[...] = dx

        @pl.when(pl.program_id(0) == 0)
        def _():
            dw_ref[...] = jnp.zeros_like(dw_ref)
            loss_ref[...] = jnp.zeros_like(loss_ref)

        dw_ref[...] += dw
        loss_ref[...] += val

    row = pl.BlockSpec((ROW_TILE, Dm), lambda i: (i, 0))
    vec = pl.BlockSpec((1, Dm), lambda i: (0, 0))
    one = pl.BlockSpec((1, 1), lambda i: (0, 0))
    return pl.pallas_call(
        body, name=name, grid=(L // ROW_TILE,), in_specs=[row, vec, row], out_specs=[one, row, vec],
        out_shape=[jax.ShapeDtypeStruct((1, 1), F32), jax.ShapeDtypeStruct((L, Dm), F32), jax.ShapeDtypeStruct((1, Dm), F32)],
        compiler_params=_params(("arbitrary",)),
    )(x, w, target)


HALO = SUBLANES


def _conv_specs(L, T, Cb, off):
    main = pl.BlockSpec((T, Cb), lambda j, i: (i, off + j))
    prev = pl.BlockSpec((HALO, Cb), lambda j, i: (jnp.maximum(i * (T // HALO) - 1, 0), off + j))
    nxt = pl.BlockSpec((HALO, Cb), lambda j, i: (jnp.minimum((i + 1) * (T // HALO), L // HALO - 1), off + j))
    return main, prev, nxt


def _conv_fwd(x1, w, K, Cb, ncol, out_dtype, name, x2=None, gate=None):
    (x1a, o1) = x1
    L = x1a.shape[0]
    T = min(ROW_TILE, L)
    has_mul, has_gate = x2 is not None, gate is not None

    def body(*refs):
        it = iter(refs)
        x1m, x1p = next(it), next(it)
        if has_mul:
            x2m, x2p = next(it), next(it)
        if has_gate:
            gm = next(it)
        w_ref, y_ref, scr = next(it), next(it), next(it)
        i = pl.program_id(1)
        p, pp = x1m[...].astype(F32), x1p[...].astype(F32)
        if has_mul:
            p, pp = p * x2m[...], pp * x2p[...]
        scr[0:HALO, :] = jnp.where(i > 0, pp, 0.0)
        scr[HALO:HALO + T, :] = p
        acc = w_ref[K - 1:K, :] * p
        for k in range(K - 1):
            s = K - 1 - k
            acc = acc + w_ref[k:k + 1, :] * scr[HALO - s:HALO - s + T, :]
        if has_gate:
            acc = acc * gm[...]
        y_ref[...] = acc.astype(out_dtype)

    in_specs, args = [], []
    m, p_, _ = _conv_specs(L, T, Cb, o1)
    in_specs += [m, p_]
    args += [x1a, x1a]
    if has_mul:
        m, p_, _ = _conv_specs(L, T, Cb, x2[1])
        in_specs += [m, p_]
        args += [x2[0], x2[0]]
    if has_gate:
        m, _, _ = _conv_specs(L, T, Cb, gate[1])
        in_specs += [m]
        args += [gate[0]]
    in_specs.append(pl.BlockSpec((K, Cb), lambda j, i: (0, j)))
    args.append(w)
    return pl.pallas_call(
        body, name=name, grid=(ncol, L // T), in_specs=in_specs,
        out_specs=pl.BlockSpec((T, Cb), lambda j, i: (i, j)),
        out_shape=jax.ShapeDtypeStruct((L, ncol * Cb), out_dtype),
        scratch_shapes=[pltpu.VMEM((T + HALO, Cb), F32)],
        compiler_params=_params(("parallel", "arbitrary")),
    )(*args)


def _conv_bwd(x1, w, dy, K, Cb, ncol, out_dtype, name, x2=None, gate=None):
    (x1a, o1) = x1
    L = x1a.shape[0]
    T = min(ROW_TILE, L)
    nrow = L // T
    has_mul, has_gate = x2 is not None, gate is not None

    def body(*refs):
        it = iter(refs)
        x1m, x1p = next(it), next(it)
        if has_mul:
            x2m, x2p = next(it), next(it)
        if has_gate:
            gm, gn = next(it), next(it)
        dym, dyn, w_ref = next(it), next(it), next(it)
        dx1_ref = next(it)
        if has_mul:
            dx2_ref = next(it)
        if has_gate:
            dg_ref = next(it)
        dw_ref, scr_p, scr_d = next(it), next(it), next(it)
        i = pl.program_id(1)
        p, pp = x1m[...].astype(F32), x1p[...].astype(F32)
        if has_mul:
            p, pp = p * x2m[...], pp * x2p[...]
        scr_p[0:HALO, :] = jnp.where(i > 0, pp, 0.0)
        scr_p[HALO:HALO + T, :] = p
        dcv, dcn = dym[...].astype(F32), dyn[...].astype(F32)
        if has_gate:
            dcv, dcn = dcv * gm[...], dcn * gn[...]
        scr_d[0:T, :] = dcv
        scr_d[T:T + HALO, :] = jnp.where(i < nrow - 1, dcn, 0.0)

        @pl.when(i == 0)
        def _():
            dw_ref[...] = jnp.zeros_like(dw_ref)

        dp = w_ref[K - 1:K, :] * dcv
        cv = w_ref[K - 1:K, :] * p
        dw_ref[K - 1:K, :] += jnp.sum(dcv * p, axis=0, keepdims=True)
        for k in range(K - 1):
            s = K - 1 - k
            dp = dp + w_ref[k:k + 1, :] * scr_d[s:s + T, :]
            sh = scr_p[HALO - s:HALO - s + T, :]
            dw_ref[k:k + 1, :] += jnp.sum(dcv * sh, axis=0, keepdims=True)
            if has_gate:
                cv = cv + w_ref[k:k + 1, :] * sh
        if has_gate:
            dg_ref[...] = (dym[...].astype(F32) * cv).astype(out_dtype)
        if has_mul:
            dx1_ref[...] = (dp * x2m[...]).astype(out_dtype)
            dx2_ref[...] = (dp * x1m[...]).astype(out_dtype)
        else:
            dx1_ref[...] = dp.astype(out_dtype)

    in_specs, args = [], []
    m, p_, _ = _conv_specs(L, T, Cb, o1)
    in_specs += [m, p_]
    args += [x1a, x1a]
    if has_mul:
        m, p_, _ = _conv_specs(L, T, Cb, x2[1])
        in_specs += [m, p_]
        args += [x2[0], x2[0]]
    if has_gate:
        m, _, n_ = _conv_specs(L, T, Cb, gate[1])
        in_specs += [m, n_]
        args += [gate[0], gate[0]]
    m, _, n_ = _conv_specs(L, T, Cb, dy[1])
    in_specs += [m, n_, pl.BlockSpec((K, Cb), lambda j, i: (0, j))]
    args += [dy[0], dy[0], w]
    out = pl.BlockSpec((T, Cb), lambda j, i: (i, j))
    full = jax.ShapeDtypeStruct((L, ncol * Cb), out_dtype)
    n_out = 1 + int(has_mul) + int(has_gate)
    return pl.pallas_call(
        body, name=name, grid=(ncol, nrow), in_specs=in_specs,
        out_specs=[out] * n_out + [pl.BlockSpec((SUBLANES, Cb), lambda j, i: (0, j))],
        out_shape=[full] * n_out + [jax.ShapeDtypeStruct((SUBLANES, ncol * Cb), F32)],
        scratch_shapes=[pltpu.VMEM((T + HALO, Cb), F32), pltpu.VMEM((T + HALO, Cb), F32)],
        compiler_params=_params(("parallel", "arbitrary")),
    )(*args)


GLU_COLS = 256


def _silu(x):
    return x * (1.0 / (1.0 + jnp.exp(-x)))


def _glu(g, v):
    return _silu(g) * v


def _causal_taps(w_ref, scr, first, rows, K):
    acc = w_ref[K - 1:K, :] * scr[first:first + rows, :]
    for k in range(K - 1):
        s = K - 1 - k
        acc = acc + w_ref[k:k + 1, :] * scr[first - s:first - s + rows, :]
    return acc


def _ffn_act_fwd(up, w, name):
    L = up.shape[0]
    T, Cb, K = min(ROW_TILE, L), GLU_COLS, FFN_CONV
    nb = D_FF // Cb

    def body(gm, gp, vm, vp, wg, wv, a_ref, sg, sv):
        i = pl.program_id(1)
        for main, prev, scr in ((gm, gp, sg), (vm, vp, sv)):
            scr[0:HALO, :] = jnp.where(i > 0, prev[...], 0.0)
            scr[HALO:HALO + T, :] = main[...]
        a_ref[...] = _glu(_causal_taps(wg, sg, HALO, T, K), _causal_taps(wv, sv, HALO, T, K)).astype(a_ref.dtype)

    gmain, gprev, _ = _conv_specs(L, T, Cb, 0)
    vmain, vprev, _ = _conv_specs(L, T, Cb, nb)
    return pl.pallas_call(
        body, name=name, grid=(nb, L // T),
        in_specs=[gmain, gprev, vmain, vprev, pl.BlockSpec((K, Cb), lambda j, i: (0, j)), pl.BlockSpec((K, Cb), lambda j, i: (0, nb + j))],
        out_specs=pl.BlockSpec((T, Cb), lambda j, i: (i, j)),
        out_shape=jax.ShapeDtypeStruct((L, D_FF), MXU_DTYPE),
        scratch_shapes=[pltpu.VMEM((T + HALO, Cb), F32), pltpu.VMEM((T + HALO, Cb), F32)],
        compiler_params=_params(("parallel", "arbitrary")),
    )(up, up, up, up, w, w)


def _ffn_act_bwd(up, w, dact, name):
    L = up.shape[0]
    T, Cb, K = min(ROW_TILE, L), GLU_COLS, FFN_CONV
    nb, nrow = D_FF // Cb, L // T

    def body(gm, gp, gn, vm, vp, vn, dam, dan, wg, wv, dg_ref, dv_ref, dwg_ref, dwv_ref, sg, sv, sdg, sdv):
        i = pl.program_id(1)
        for main, prev, nxt, scr in ((gm, gp, gn, sg), (vm, vp, vn, sv)):
            scr[0:HALO, :] = jnp.where(i > 0, prev[...], 0.0)
            scr[HALO:HALO + T, :] = main[...]
            scr[HALO + T:2 * HALO + T, :] = nxt[...]
        ug, uv = _causal_taps(wg, sg, HALO, T + HALO, K), _causal_taps(wv, sv, HALO, T + HALO, K)
        da = jnp.concatenate([dam[...], jnp.where(i < nrow - 1, dan[...], 0.0)], axis=0)
        _, vjp = jax.vjp(_glu, ug, uv)
        sdg[...], sdv[...] = vjp(da)

        @pl.when(i == 0)
        def _():
            dwg_ref[...] = jnp.zeros_like(dwg_ref)
            dwv_ref[...] = jnp.zeros_like(dwv_ref)

        for w_ref, scr, sd, d_ref, dw_ref in ((wg, sg, sdg, dg_ref, dwg_ref), (wv, sv, sdv, dv_ref, dwv_ref)):
            du = sd[0:T, :]
            dp = w_ref[K - 1:K, :] * du
            dw_ref[K - 1:K, :] += jnp.sum(du * scr[HALO:HALO + T, :], axis=0, keepdims=True)
            for k in range(K - 1):
                s = K - 1 - k
                dp = dp + w_ref[k:k + 1, :] * sd[s:s + T, :]
                dw_ref[k:k + 1, :] += jnp.sum(du * scr[HALO - s:HALO - s + T, :], axis=0, keepdims=True)
            d_ref[...] = dp.astype(d_ref.dtype)

    gmain, gprev, gnext = _conv_specs(L, T, Cb, 0)
    vmain, vprev, vnext = _conv_specs(L, T, Cb, nb)
    dmain, _, dnext = _conv_specs(L, T, Cb, 0)
    out = pl.BlockSpec((T, Cb), lambda j, i: (i, j))
    dwb = pl.BlockSpec((SUBLANES, Cb), lambda j, i: (0, j))
    half = jax.ShapeDtypeStruct((L, D_FF), MXU_DTYPE)
    dwh = jax.ShapeDtypeStruct((SUBLANES, D_FF), F32)
    dg, dv, dwg, dwv = pl.pallas_call(
        body, name=name, grid=(nb, nrow),
        in_specs=[gmain, gprev, gnext, vmain, vprev, vnext, dmain, dnext,
                  pl.BlockSpec((K, Cb), lambda j, i: (0, j)), pl.BlockSpec((K, Cb), lambda j, i: (0, nb + j))],
        out_specs=[out, out, dwb, dwb], out_shape=[half, half, dwh, dwh],
        scratch_shapes=[pltpu.VMEM((T + 2 * HALO, Cb), F32), pltpu.VMEM((T + 2 * HALO, Cb), F32),
                        pltpu.VMEM((T + HALO, Cb), F32), pltpu.VMEM((T + HALO, Cb), F32)],
        compiler_params=_params(("parallel", "arbitrary")),
    )(up, up, up, up, up, up, dact, dact, w, w)
    return jnp.concatenate([dg, dv], axis=1), jnp.concatenate([dwg, dwv], axis=1)


def _bdot_raw(a, b, mode):
    return lax.dot_general(a.astype(MXU_DTYPE), b.astype(MXU_DTYPE), _DIMS[mode], preferred_element_type=F32)


@functools.partial(jax.custom_vjp, nondiff_argnums=(2,))
def _bdot(a, b, mode):
    return _bdot_raw(a, b, mode)


def _bdot_fwd(a, b, mode):
    return _bdot_raw(a, b, mode), (a, b)


def _bdot_bwd(mode, res, ct):
    a, b = res
    if mode == "nn":
        return _bdot_raw(ct, b, "nt"), _bdot_raw(a, ct, "tn")
    if mode == "nt":
        return _bdot_raw(ct, b, "nn"), _bdot_raw(ct, a, "tn")
    return _bdot_raw(b, ct, "nt"), _bdot_raw(a, ct, "nn")


_bdot.defvjp(_bdot_fwd, _bdot_bwd)


def _hdot(a, b, mode="nn"):
    return lax.dot_general(a, b, _DIMS[mode], precision=lax.Precision.HIGH, preferred_element_type=F32)


@jax.custom_vjp
def _inv_unit_lower(a):
    R = a.shape[0]
    eye = (lax.broadcasted_iota(jnp.int32, (R, R), 0) == lax.broadcasted_iota(jnp.int32, (R, R), 1)).astype(F32)
    t = eye - a
    p = a
    n = 1
    while 2 * n < GDN_CHUNK:
        p = _hdot(p, p)
        t = t + _hdot(t, p)
        n *= 2
    return t


def _inv_fwd(a):
    t = _inv_unit_lower(a)
    return t, t


def _inv_bwd(t, ct):
    return (-_hdot(_hdot(t, ct, "tn"), t, "nt"),)


_inv_unit_lower.defvjp(_inv_fwd, _inv_bwd)


@jax.custom_vjp
def _inv_saved(a, t):
    return t


def _inv_saved_fwd(a, t):
    return t, t


def _inv_saved_bwd(t, ct):
    return _inv_bwd(t, ct)[0], jnp.zeros_like(t)


_inv_saved.defvjp(_inv_saved_fwd, _inv_saved_bwd)


def _softplus(x):
    return jnp.maximum(x, 0.0) + jnp.log(1.0 + jnp.exp(-jnp.abs(x)))


def _sigmoid(x):
    return 1.0 / (1.0 + jnp.exp(-x))


def _pick_lane(blk, lane):
    ids = lax.broadcasted_iota(jnp.int32, blk.shape, 1)
    return jnp.sum(jnp.where(ids == lane, blk, 0.0), axis=1, keepdims=True)


def _gdn_chunk(cq, ck, cv, gz, gab, alog, dtb, wn, S, t_saved=None):
    C, H, Dk = GDN_CHUNK, GDN_HEADS, GDN_HEAD_DIM
    R = H * C
    rows_of = lambda vals, n: jnp.concatenate([jnp.broadcast_to(x, (n, 1)) for x in vals], axis=0)
    ga = jnp.concatenate([_pick_lane(gab, h) for h in range(H)], axis=0)
    gb = jnp.concatenate([_pick_lane(gab, H + h) for h in range(H)], axis=0)
    al = rows_of([_pick_lane(alog, h) for h in range(H)], C)
    db = rows_of([_pick_lane(dtb, h) for h in range(H)], C)
    q, k, v = _silu(cq), _silu(ck), _silu(cv)
    q = q * lax.rsqrt(jnp.sum(q * q, axis=-1, keepdims=True) + NORM_EPS) * (Dk ** -0.5)
    k = k * lax.rsqrt(jnp.sum(k * k, axis=-1, keepdims=True) + NORM_EPS)
    beta = _sigmoid(gb)
    g = -jnp.exp(al) * _softplus(ga + db)
    row = lax.broadcasted_iota(jnp.int32, (R, R), 0)
    col = lax.broadcasted_iota(jnp.int32, (R, R), 1)
    same_head = (row // C) == (col // C)
    causal, strict = same_head & (row >= col), same_head & (row > col)
    gcb = _hdot(causal.astype(F32), jnp.broadcast_to(g, (R, Dk)))
    first = (lax.broadcasted_iota(jnp.int32, (R, Dk), 1) == 0).astype(F32)
    gr = _hdot(first, gcb, "nt")
    gc = _pick_lane(gcb, 0)
    decay = jnp.where(causal, jnp.exp(jnp.where(causal, gc - gr, 0.0)), 0.0)
    kb = k * beta
    lower = jnp.where(strict, _bdot(kb, k, "nt") * decay, 0.0)
    t = _inv_unit_lower(lower) if t_saved is None else _inv_saved(lower, t_saved)
    egc = jnp.exp(gc)
    u = _hdot(t, v * beta)
    w = _hdot(t, kb * egc)
    attn = jnp.where(causal, _bdot(q, k, "nt") * decay, 0.0)
    own = (lax.broadcasted_iota(jnp.int32, (R, H * Dk), 0) // C) == (lax.broadcasted_iota(jnp.int32, (R, H * Dk), 1) // Dk)
    spread = lambda x: jnp.where(own, jnp.concatenate([x] * H, axis=1), 0.0)
    v_new = u - _bdot(spread(w), S, "nn")
    o = _bdot(spread(q * egc), S, "nn") + _bdot(attn, v_new, "nn")
    last = lax.broadcasted_iota(jnp.int32, (R, 1), 0)
    g_last = [jnp.sum(jnp.where(last == h * C + C - 1, gc, 0.0), axis=0, keepdims=True) for h in range(H)]
    S_new = S * jnp.exp(rows_of(g_last, Dk)) + _bdot(spread(k * jnp.exp(rows_of(g_last, C) - gc)), v_new, "tn")
    y = o * lax.rsqrt(jnp.mean(o * o, axis=-1, keepdims=True) + NORM_EPS) * wn * _silu(gz)
    return y, S_new, t


def _stack_heads(ref, first, width=GDN_HEAD_DIM):
    return jnp.concatenate([ref[:, first + h * width:first + (h + 1) * width] for h in range(GDN_HEADS)], axis=0)


def _gdn_fwd(cqkv, proj, alog, dtb, wn, name):
    L = cqkv.shape[0]
    C, H, Dh = GDN_CHUNK, GDN_HEADS, GDN_HEAD_DIM
    N = L // C

    def body(c_ref, gz_ref, gab_ref, al_ref, db_ref, wn_ref, y_ref, sall_ref, tall_ref, S):
        n = pl.program_id(0)

        @pl.when(n == 0)
        def _():
            S[...] = jnp.zeros_like(S)

        s_in = S[...]
        sall_ref[0] = s_in
        y, s_new, t = _gdn_chunk(_stack_heads(c_ref, 0), _stack_heads(c_ref, GDN_WIDTH), _stack_heads(c_ref, 2 * GDN_WIDTH),
                                 _stack_heads(gz_ref, 0), gab_ref[...], al_ref[...], db_ref[...], wn_ref[...], s_in)
        for h in range(H):
            y_ref[:, h * Dh:(h + 1) * Dh] = y[h * C:(h + 1) * C].astype(y_ref.dtype)
        S[...] = s_new
        tall_ref[0] = t

    vec = pl.BlockSpec((1, LANES), lambda n: (0, 0))
    return pl.pallas_call(
        body, name=name, grid=(N,),
        in_specs=[pl.BlockSpec((C, 3 * GDN_WIDTH), lambda n: (n, 0)),
                  pl.BlockSpec((C, GDN_WIDTH), lambda n: (n, P_GZ // GDN_WIDTH)),
                  pl.BlockSpec((C, LANES), lambda n: (n, P_GAB // LANES)), vec, vec, vec],
        out_specs=[pl.BlockSpec((C, GDN_WIDTH), lambda n: (n, 0)), pl.BlockSpec((1, H * Dh, Dh), lambda n: (n, 0, 0)),
                   pl.BlockSpec((1, H * C, H * C), lambda n: (n, 0, 0))],
        out_shape=[jax.ShapeDtypeStruct((L, GDN_WIDTH), MXU_DTYPE), jax.ShapeDtypeStruct((N, H * Dh, Dh), F32),
                   jax.ShapeDtypeStruct((N, H * C, H * C), F32)],
        scratch_shapes=[pltpu.VMEM((H * Dh, Dh), F32)],
        compiler_params=_params(("arbitrary",)),
    )(cqkv, proj, proj, alog, dtb, wn)


def _gdn_bwd(cqkv, proj, alog, dtb, wn, s_all, t_all, dy, name):
    L = cqkv.shape[0]
    C, H, Dh = GDN_CHUNK, GDN_HEADS, GDN_HEAD_DIM
    N = L // C

    def body(c_ref, gz_ref, gab_ref, al_ref, db_ref, wn_ref, sall_ref, tall_ref, dy_ref,
             dc_ref, dgz_ref, dgab_ref, dal_ref, ddb_ref, dwn_ref, dS):
        n = pl.program_id(0)

        @pl.when(n == 0)
        def _():
            dS[...] = jnp.zeros_like(dS)
            dal_ref[...] = jnp.zeros_like(dal_ref)
            ddb_ref[...] = jnp.zeros_like(ddb_ref)
            dwn_ref[...] = jnp.zeros_like(dwn_ref)

        t_saved = tall_ref[0]
        chunk = lambda *a: _gdn_chunk(*a, t_saved=t_saved)[:2]
        _, vjp = jax.vjp(chunk, _stack_heads(c_ref, 0), _stack_heads(c_ref, GDN_WIDTH), _stack_heads(c_ref, 2 * GDN_WIDTH),
                         _stack_heads(gz_ref, 0), gab_ref[...], al_ref[...], db_ref[...], wn_ref[...], sall_ref[0])
        dq, dk, dv, dgz, dgab, dal, ddb, dwn, ds = vjp((_stack_heads(dy_ref, 0), dS[...]))
        for h in range(H):
            rows = slice(h * C, (h + 1) * C)
            dc_ref[:, h * Dh:(h + 1) * Dh] = dq[rows]
            dc_ref[:, (H + h) * Dh:(H + h + 1) * Dh] = dk[rows]
            dc_ref[:, (2 * H + h) * Dh:(2 * H + h + 1) * Dh] = dv[rows]
            dgz_ref[:, h * Dh:(h + 1) * Dh] = dgz[rows].astype(dgz_ref.dtype)
        dS[...] = ds
        dgab_ref[...] = dgab
        dal_ref[...] += dal
        ddb_ref[...] += ddb
        dwn_ref[...] += dwn

    vec = pl.BlockSpec((1, LANES), lambda n: (0, 0))
    rev = lambda n: N - 1 - n
    return pl.pallas_call(
        body, name=name, grid=(N,),
        in_specs=[pl.BlockSpec((C, 3 * GDN_WIDTH), lambda n: (rev(n), 0)),
                  pl.BlockSpec((C, GDN_WIDTH), lambda n: (rev(n), P_GZ // GDN_WIDTH)),
                  pl.BlockSpec((C, LANES), lambda n: (rev(n), P_GAB // LANES)), vec, vec, vec,
                  pl.BlockSpec((1, H * Dh, Dh), lambda n: (rev(n), 0, 0)),
                  pl.BlockSpec((1, H * C, H * C), lambda n: (rev(n), 0, 0)),
                  pl.BlockSpec((C, GDN_WIDTH), lambda n: (rev(n), Y_GDN // GDN_WIDTH))],
        out_specs=[pl.BlockSpec((C, 3 * GDN_WIDTH), lambda n: (rev(n), 0)),
                   pl.BlockSpec((C, GDN_WIDTH), lambda n: (rev(n), 0)),
                   pl.BlockSpec((C, LANES), lambda n: (rev(n), 0)), vec, vec, vec],
        out_shape=[jax.ShapeDtypeStruct((L, 3 * GDN_WIDTH), F32), jax.ShapeDtypeStruct((L, GDN_WIDTH), MXU_DTYPE),
                   jax.ShapeDtypeStruct((L, LANES), F32)] + [jax.ShapeDtypeStruct((1, LANES), F32)] * 3,
        scratch_shapes=[pltpu.VMEM((H * Dh, Dh), F32)],
        compiler_params=_params(("arbitrary",)),
    )(cqkv, proj, proj, alog, dtb, wn, s_all, t_all, dy)


def _split_dot(x, m):
    R = x.shape[0]
    hi = x.astype(MXU_DTYPE)
    lo = (x - hi.astype(F32)).astype(MXU_DTYPE)
    both = jnp.dot(jnp.concatenate([hi, lo], axis=0), m, preferred_element_type=F32)
    return both[:R] + both[R:]


def _sb_kv_blocks(kv_ref, js):
    B = SB_BLOCK
    rows = [pl.ds(pl.multiple_of(j * B, B), B) for j in js]
    kps = [[kv_ref[r, p * LANES:(p + 1) * LANES] for p in range(SB_HEADS // 2)] for r in rows]
    vps = [[kv_ref[r, SB_WIDTH + p * LANES:SB_WIDTH + (p + 1) * LANES] for p in range(SB_HEADS // 2)] for r in rows]
    return rows, kps, vps


def _sb_pair_dots(x, mats, mode):
    B = SB_BLOCK
    return jnp.concatenate([lax.dot_general(x[2 * p * B:(2 * p + 2) * B], m, _DIMS[mode], preferred_element_type=F32)
                            for mp in mats for p, m in enumerate(mp)], axis=0)


SB_SCALE = SB_HEAD_DIM ** -0.5


def _sb_logits(qx, kps):
    z = _sb_pair_dots(qx, kps, "nt")
    return z, jnp.minimum(z, 0.0) - jnp.log(1.0 + jnp.exp(-jnp.abs(z)))


def _sb_running(start, sums, inclusive):
    R = start.shape[0]
    n = sums.shape[0] // R
    vals, cur = [], start
    for b in range(n):
        nxt = cur + sums[b * R:(b + 1) * R]
        vals.append(nxt if inclusive else cur)
        cur = nxt
    return (vals[0] if n == 1 else jnp.concatenate(vals, axis=0)), cur


def _sb_head_masks():
    low = lax.broadcasted_iota(jnp.int32, (SB_BLOCK, LANES), 1) < SB_HEAD_DIM
    return [low if h % 2 == 0 else jnp.logical_not(low) for h in range(SB_HEADS)]


def _sb_stack_heads(ref, scale=1.0):
    mine = _sb_head_masks()
    return jnp.concatenate([jnp.where(mine[h], ref[:, (h // 2) * LANES:(h // 2 + 1) * LANES] * scale, 0.0).astype(MXU_DTYPE)
                            for h in range(SB_HEADS)], axis=0)


def _sb_block_masks():
    B = SB_BLOCK
    row = lax.broadcasted_iota(jnp.int32, (B, B), 0)
    col = lax.broadcasted_iota(jnp.int32, (B, B), 1)
    row4 = lax.broadcasted_iota(jnp.int32, (SB_HEADS * B, B), 0) & (B - 1)
    col4 = lax.broadcasted_iota(jnp.int32, (SB_HEADS * B, B), 1)
    return (row > col).astype(MXU_DTYPE), (row < col).astype(MXU_DTYPE), col4 < row4


def _sb_fwd(proj, kv, name):
    L = proj.shape[0]
    B, H = SB_BLOCK, SB_HEADS

    def body(q_ref, kv_ref, y_ref, c_ref):
        i = pl.program_id(0)
        low = _sb_head_masks()[0]
        after, _, strict = _sb_block_masks()
        qx = _sb_stack_heads(q_ref, SB_SCALE)

        def sweep(js, c, accs, masked):
            _, kps, vps = _sb_kv_blocks(kv_ref, js)
            z, lb = _sb_logits(qx, kps)
            lom = lb - z
            if masked:
                lom = jnp.where(strict, lom, 0.0)
            before_block, c = _sb_running(c, jnp.sum(lom, axis=1, keepdims=True), False)
            a = jnp.exp(lb + _split_dot(lom, after) + before_block)
            if masked:
                a = jnp.where(strict, a, 0.0)
            a = a.astype(MXU_DTYPE)
            new_accs = list(accs)
            for b in range(len(js)):
                o = _sb_pair_dots(a[b * H * B:(b + 1) * H * B], [vps[b]], "nn")
                for p in range(H // 2):
                    new_accs[p] = new_accs[p] + jnp.where(low, o[2 * p * B:(2 * p + 1) * B], o[(2 * p + 1) * B:(2 * p + 2) * B])
            return c, new_accs

        c, accs = sweep([i], jnp.zeros((H * B, 1), F32), [jnp.zeros((B, LANES), F32)] * (H // 2), True)

        W, M = SB_SWEEP, SB_SWEEP // 2

        def wide(it, carry):
            j = i - 1 - W * it
            c, accs = sweep([j - b for b in range(W)], carry[0], list(carry[1:]), False)
            return (c,) + tuple(accs)

        def mid(it, carry):
            j = i % W - 1
            c, accs = sweep([j - b for b in range(M)], carry[0], list(carry[1:]), False)
            return (c,) + tuple(accs)

        def one(it, carry):
            c, accs = sweep([i % M - 1 - it], carry[0], list(carry[1:]), False)
            return (c,) + tuple(accs)

        carry = lax.fori_loop(0, i // W, wide, (c,) + tuple(accs))
        carry = lax.fori_loop(0, (i % W) // M, mid, carry)
        carry = lax.fori_loop(0, i % M, one, carry)
        for p in range(H // 2):
            y_ref[:, p * LANES:(p + 1) * LANES] = carry[1 + p].astype(y_ref.dtype)
        lane = lax.broadcasted_iota(jnp.int32, (B, LANES), 1)
        ct = jnp.zeros((B, LANES), F32)
        for h in range(H):
            ct = jnp.where(lane == h, carry[0][h * B:(h + 1) * B], ct)
        c_ref[...] = ct

    return pl.pallas_call(
        body, name=name, grid=(L // B,),
        in_specs=[pl.BlockSpec((B, SB_WIDTH), lambda i: (i, P_SB // SB_WIDTH)), pl.BlockSpec((L, 2 * SB_WIDTH), lambda i: (0, 0))],
        out_specs=[pl.BlockSpec((B, SB_WIDTH), lambda i: (i, 0)), pl.BlockSpec((B, LANES), lambda i: (i, 0))],
        out_shape=[jax.ShapeDtypeStruct((L, SB_WIDTH), MXU_DTYPE), jax.ShapeDtypeStruct((L, LANES), F32)],
        compiler_params=_params(("arbitrary",)),
    )(proj, kv)


def _sb_bwd(proj, kv, dy, ctot, name):
    L = proj.shape[0]
    B, H = SB_BLOCK, SB_HEADS
    nblk = L // B

    def body(q_ref, kv_ref, do_ref, ct_ref, dq_ref, dk_hbm, dv_hbm, dk_acc, dv_acc):
        i = pl.program_id(0)

        @pl.when(i == 0)
        def _():
            dk_acc[...] = jnp.zeros_like(dk_acc)
            dv_acc[...] = jnp.zeros_like(dv_acc)

        low = _sb_head_masks()[0]
        after, before, strict = _sb_block_masks()
        qx, dox = _sb_stack_heads(q_ref, SB_SCALE), _sb_stack_heads(do_ref)
        ct = ct_ref[...]
        ctot = jnp.concatenate([_pick_lane(ct, h) for h in range(H)], axis=0)

        def sweep(js, p, e, dqs, masked):
            n = len(js)
            rows, kps, vps = _sb_kv_blocks(kv_ref, js)
            z, lb = _sb_logits(qx, kps)
            sig = jnp.exp(lb)
            lom = lb - z
            if masked:
                lom = jnp.where(strict, lom, 0.0)
            through_block, p = _sb_running(p, jnp.sum(lom, axis=1, keepdims=True), True)
            right_of_block = (ctot if n == 1 else jnp.concatenate([ctot] * n, axis=0)) - through_block
            a = jnp.exp(lb + _split_dot(lom, after) + right_of_block)
            if masked:
                a = jnp.where(strict, a, 0.0)
            ea = _sb_pair_dots(dox, vps, "nt") * a
            left_of_block, e = _sb_running(e, jnp.sum(ea, axis=1, keepdims=True), False)
            dlom = left_of_block + jnp.dot(ea.astype(MXU_DTYPE), before, preferred_element_type=F32)
            if masked:
                dlom = jnp.where(strict, dlom, 0.0)
            dz = (ea * (1.0 - sig) - dlom * sig).astype(MXU_DTYPE)
            ab = a.astype(MXU_DTYPE)
            new_dq = list(dqs)
            for b in range(n):
                for pr in range(H // 2):
                    heads = slice(2 * pr * B, (2 * pr + 2) * B)
                    both, cols = slice((b * H + 2 * pr) * B, (b * H + 2 * pr + 2) * B), slice(pr * LANES, (pr + 1) * LANES)
                    dqp = jnp.dot(dz[both], kps[b][pr], preferred_element_type=F32)
                    new_dq[pr] = new_dq[pr] + jnp.where(low, dqp[:B], dqp[B:])
                    dk_acc[rows[b], cols] += lax.dot_general(dz[both], qx[heads], _DIMS["tn"], preferred_element_type=F32)
                    dv_acc[rows[b], cols] += lax.dot_general(ab[both], dox[heads], _DIMS["tn"], preferred_element_type=F32)
            return p, e, new_dq

        W, M = SB_SWEEP, SB_SWEEP // 2

        def wide(it, carry):
            p, e, dqs = sweep([W * it + b for b in range(W)], carry[0], carry[1], list(carry[2:]), False)
            return (p, e) + tuple(dqs)

        def mid(it, carry):
            p, e, dqs = sweep([i - i % W + b for b in range(M)], carry[0], carry[1], list(carry[2:]), False)
            return (p, e) + tuple(dqs)

        def one(it, carry):
            p, e, dqs = sweep([i - i % M + it], carry[0], carry[1], list(carry[2:]), False)
            return (p, e) + tuple(dqs)

        zero = jnp.zeros((H * B, 1), F32)
        carry = lax.fori_loop(0, i // W, wide, (zero, zero) + (jnp.zeros((B, LANES), F32),) * (H // 2))
        carry = lax.fori_loop(0, (i % W) // M, mid, carry)
        carry = lax.fori_loop(0, i % M, one, carry)
        _, _, dqs = sweep([i], carry[0], carry[1], list(carry[2:]), True)
        for pr in range(H // 2):
            dq_ref[:, pr * LANES:(pr + 1) * LANES] = (dqs[pr] * SB_SCALE).astype(dq_ref.dtype)

        @pl.when(i == nblk - 1)
        def _():
            pltpu.sync_copy(dk_acc, dk_hbm)
            pltpu.sync_copy(dv_acc, dv_hbm)

    hbm = pl.BlockSpec(memory_space=pl.ANY)
    acc = jax.ShapeDtypeStruct((L, SB_WIDTH), F32)
    return pl.pallas_call(
        body, name=name, grid=(nblk,),
        in_specs=[pl.BlockSpec((B, SB_WIDTH), lambda i: (i, P_SB // SB_WIDTH)), pl.BlockSpec((L, 2 * SB_WIDTH), lambda i: (0, 0)),
                  pl.BlockSpec((B, SB_WIDTH), lambda i: (i, Y_SB // SB_WIDTH)), pl.BlockSpec((B, LANES), lambda i: (i, 0))],
        out_specs=[pl.BlockSpec((B, SB_WIDTH), lambda i: (i, 0)), hbm, hbm],
        out_shape=[jax.ShapeDtypeStruct((L, SB_WIDTH), MXU_DTYPE), acc, acc],
        scratch_shapes=[pltpu.VMEM((L, SB_WIDTH), F32), pltpu.VMEM((L, SB_WIDTH), F32)],
        compiler_params=_params(("arbitrary",)),
    )(proj, kv, dy, ctot)


def _prep_w_in(w):
    sc, qkv, gz, gab, sb = w[:, 0:768], w[:, 768:2304], w[:, 2304:2816], w[:, 2816:2824], w[:, 2824:3592]
    pad = jnp.zeros((w.shape[0], P_END - D_IN_PROJ), w.dtype)
    return jnp.concatenate([qkv, gz, sc, sb, gab, pad], axis=1).astype(MXU_DTYPE)


def _unprep_dw_in(dw):
    qkv, gz, sc, sb, gab = dw[:, P_QKV:P_GZ], dw[:, P_GZ:P_SC], dw[:, P_SC:P_SB], dw[:, P_SB:P_GAB], dw[:, P_GAB:P_GAB + 8]
    return jnp.concatenate([sc, qkv, gz, gab, sb], axis=1)


def _prep_w_out(w):
    return jnp.concatenate([w[256:768], w[0:256], w[768:]], axis=0).astype(MXU_DTYPE)


def _unprep_dw_out(dw):
    return jnp.concatenate([dw[512:768], dw[0:512], dw[768:]], axis=0)


def _pad_lanes(v):
    return jnp.zeros((1, LANES), F32).at[0, :v.shape[0]].set(v)


def _layer_fwd(x, p, l):
    L = x.shape[0]
    tm = min(2048, L)
    n = f"l{l}_"
    h = _rms_fwd(x, p["norm_mix"], n + "rms_mix")
    proj = _matmul(h, p["w_in"], "nn", F32, n + "mm_in", tm, 768, 1024)
    cb = SC_WIDTH
    y_sc = _conv_fwd((proj, P_SC // cb + 1), p["w_sconv"], SC_KERNEL, cb, 1, MXU_DTYPE, n + "sconv",
                     x2=(proj, P_SC // cb + 2), gate=(proj, P_SC // cb))
    cqkv = _conv_fwd((proj, 0), p["w_gconv"], GDN_CONV, 256, 6, F32, n + "gconv")
    y_gdn, s_all, t_all = _gdn_fwd(cqkv, proj, p["a_log"], p["dt_bias"], p["gdn_norm"], n + "gdn")
    kv = proj[:, P_SB + SB_WIDTH:P_SB + 3 * SB_WIDTH].astype(MXU_DTYPE)
    y_sb, ctot = _sb_fwd(proj, kv, n + "sb")
    ycat = jnp.concatenate([y_gdn, y_sc, y_sb], axis=1)
    x1 = _matmul(ycat, p["w_out"], "nn", F32, n + "mm_out", tm, 512, 1024, resid=x)
    h2 = _rms_fwd(x1, p["norm_ffn"], n + "rms_ffn")
    up = _matmul(h2, p["w_up"], "nn", F32, n + "mm_up", tm, 512, 1024)
    act = _ffn_act_fwd(up, p["w_fconv"], n + "ffn_act")
    x2 = _matmul(act, p["w_down"], "nn", F32, n + "mm_down", tm, 512, 1408, resid=x1)
    saved = dict(x=x, h=h, proj=proj, cqkv=cqkv, s_all=s_all, t_all=t_all, kv=kv, ctot=ctot, ycat=ycat, x1=x1, h2=h2, up=up, act=act)
    return x2, saved


def _layer_bwd(dx2, p, s, l):
    L = dx2.shape[0]
    tm, tkl = min(1024, L), min(1024, L)
    n = f"l{l}_"
    g = {}
    dx2b = dx2.astype(MXU_DTYPE)
    g["w_ffn_down"] = _matmul(s["act"], dx2b, "tn", F32, n + "mm_ddown", 1408, 512, tkl)
    dact = _matmul(dx2b, p["w_down"], "nt", F32, n + "mm_dact", tm, 1408, 1024)
    dup, dwf = _ffn_act_bwd(s["up"], p["w_fconv"], dact, n + "dffn_act")
    g["w_ffn_conv"] = dwf[:FFN_CONV]
    g["w_ffn_up"] = _matmul(s["h2"], dup, "tn", F32, n + "mm_dup", 1024, 512, tkl)
    dh2 = _matmul(dup, p["w_up"], "nt", F32, n + "mm_dh2", tm, 512, 1408)
    dx1, dwn = _rms_bwd(s["x1"], p["norm_ffn"], dh2, dx2, n + "drms_ffn")
    g["w_norm_ffn"] = dwn[0]

    dx1b = dx1.astype(MXU_DTYPE)
    g["w_mix_out"] = _unprep_dw_out(_matmul(s["ycat"], dx1b, "tn", F32, n + "mm_dout", 1024, 512, tkl))
    dycat = _matmul(dx1b, p["w_out"], "nt", F32, n + "mm_dycat", tm, 512, 1024)
    proj = s["proj"]
    cb = SC_WIDTH
    dsc_c, dsc_h, dsc_b, dws = _conv_bwd((proj, P_SC // cb + 1), p["w_sconv"], (dycat, Y_SC // cb), SC_KERNEL, cb, 1,
                                         MXU_DTYPE, n + "dsconv", x2=(proj, P_SC // cb + 2), gate=(proj, P_SC // cb))
    g["w_sconv"] = dws[:SC_KERNEL]
    dcqkv, dgz, dgab, dal, ddb, dgn = _gdn_bwd(s["cqkv"], proj, p["a_log"], p["dt_bias"], p["gdn_norm"], s["s_all"], s["t_all"], dycat,
                                               n + "dgdn")
    g["gdn_a_log"], g["gdn_dt_bias"], g["w_gdn_norm"] = dal[0, :GDN_HEADS], ddb[0, :GDN_HEADS], dgn[0]
    dqkv, dwg = _conv_bwd((proj, 0), p["w_gconv"], (dcqkv, 0), GDN_CONV, 256, 6, MXU_DTYPE, n + "dgconv")
    g["w_gdn_conv"] = dwg[:GDN_CONV]
    dq, dk, dv = _sb_bwd(proj, s["kv"], dycat, s["ctot"], n + "dsb")
    dproj = jnp.concatenate(
        [dqkv, dgz, dsc_b, dsc_c, dsc_h, dq, dk.astype(MXU_DTYPE), dv.astype(MXU_DTYPE), dgab.astype(MXU_DTYPE),
         jnp.zeros((L, P_END - P_GAB - LANES), MXU_DTYPE)], axis=1)
    g["w_mix_in"] = _unprep_dw_in(_matmul(s["h"], dproj, "tn", F32, n + "mm_din", 1024, 768, tkl))
    dh = _matmul(dproj, p["w_in"], "nt", F32, n + "mm_dh", tm, 512, 768)
    dx, dwm = _rms_bwd(s["x"], p["norm_mix"], dh, dx1, n + "drms_mix")
    g["w_norm_mix"] = dwm[0]
    return dx, g


WEIGHTS = ["w_norm_mix", "w_mix_in", "w_sconv", "w_gdn_conv", "gdn_a_log", "gdn_dt_bias", "w_gdn_norm", "w_mix_out",
           "w_norm_ffn", "w_ffn_up", "w_ffn_conv", "w_ffn_down", "w_norm_final"]


def _local_step(x, w, target):
    layers = []
    for l in range(DEPTH):
        layers.append(dict(
            norm_mix=w["w_norm_mix"][l][None], w_in=_prep_w_in(w["w_mix_in"][l]), w_sconv=w["w_sconv"][l],
            w_gconv=w["w_gdn_conv"][l], a_log=_pad_lanes(w["gdn_a_log"][l]), dt_bias=_pad_lanes(w["gdn_dt_bias"][l]),
            gdn_norm=w["w_gdn_norm"][l][None], w_out=_prep_w_out(w["w_mix_out"][l]), norm_ffn=w["w_norm_ffn"][l][None],
            w_up=w["w_ffn_up"][l].astype(MXU_DTYPE), w_fconv=w["w_ffn_conv"][l], w_down=w["w_ffn_down"][l].astype(MXU_DTYPE)))
    saved = []
    for l in range(DEPTH):
        x, s = _layer_fwd(x, layers[l], l)
        saved.append(s)
    loss, dx, dwf = _loss_head(x, w["w_norm_final"][None], target, "loss_head")
    grads = [None] * DEPTH
    for l in reversed(range(DEPTH)):
        dx, grads[l] = _layer_bwd(dx, layers[l], saved[l], l)
    out = {k: jnp.stack([grads[l][k] for l in range(DEPTH)]) for k in WEIGHTS if k != "w_norm_final"}
    out["w_norm_final"] = dwf[0]
    return loss, dx, out


N_CHIPS = 4
_HBM = pl.BlockSpec(memory_space=pl.ANY)


def _other_chips(x, y):
    return [(1 - x, y), (x, 1 - y), (1 - x, 1 - y)]


def _remote(src, dst, send_sems, recv_sems, k, to):
    return pltpu.make_async_remote_copy(src_ref=src, dst_ref=dst, send_sem=send_sems.at[k], recv_sem=recv_sems.at[k],
                                        device_id=to, device_id_type=pl.DeviceIdType.MESH)


def _all_gather(bufs, name):
    nb = len(bufs)

    def body(*refs):
        ins, outs = refs[:nb], refs[nb:2 * nb]
        send_sems, recv_sems, local_sems = refs[2 * nb:]
        x, y, c = lax.axis_index("x"), lax.axis_index("y"), lax.axis_index("c")
        me, sibling, chips = (x, y, c), (x, y, 1 - c), _other_chips(x, y)

        def copy(b, k, block, to, src=None):
            slot = outs[b].at[4 * block[0] + 2 * block[1] + block[2]]
            return _remote(slot if src is None else src, slot, send_sems.at[b], recv_sems.at[b], k, to)

        local = [pltpu.make_async_copy(ins[b], outs[b].at[4 * x + 2 * y + c], local_sems.at[b]) for b in range(nb)]
        first = [copy(b, 0, me, sibling, src=ins[b]) for b in range(nb)]
        first += [copy(b, 1 + j, me, (*chip, c), src=ins[b]) for j, chip in enumerate(chips) for b in range(nb)]
        for cp in local + first:
            cp.start()
        passed = []
        for j, chip in enumerate(chips):
            for b in range(nb):
                copy(b, 1 + j, (*chip, c), me).wait_recv()
                passed.append(copy(b, 4 + j, (*chip, c), sibling))
                passed[-1].start()
        for b in range(nb):
            copy(b, 0, sibling, me).wait_recv()
        for j, chip in enumerate(chips):
            for b in range(nb):
                copy(b, 4 + j, (*chip, 1 - c), me).wait_recv()
        for cp in first + passed:
            cp.wait_send()
        for cp in local:
            cp.wait()

    return pl.pallas_call(
        body, name=name, in_specs=[_HBM] * nb, out_specs=[_HBM] * nb,
        out_shape=[jax.ShapeDtypeStruct((N_DEV,) + b.shape, b.dtype) for b in bufs],
        scratch_shapes=[pltpu.SemaphoreType.DMA((nb, N_DEV - 1)), pltpu.SemaphoreType.DMA((nb, N_DEV - 1)),
                        pltpu.SemaphoreType.DMA((nb,))],
    )(*bufs)


def _to_sibling(bufs, name):
    nb = len(bufs)

    def body(*refs):
        ins, outs = refs[:nb], refs[nb:2 * nb]
        send_sems, recv_sems = refs[2 * nb:]
        x, y, c = lax.axis_index("x"), lax.axis_index("y"), lax.axis_index("c")
        copies = [_remote(ins[b].at[1 - c], outs[b], send_sems, recv_sems, b, (x, y, 1 - c)) for b in range(nb)]
        for cp in copies:
            cp.start()
        for cp in copies:
            cp.wait()

    return pl.pallas_call(
        body, name=name, in_specs=[_HBM] * nb, out_specs=[_HBM] * nb,
        out_shape=[jax.ShapeDtypeStruct(b.shape[1:], b.dtype) for b in bufs],
        scratch_shapes=[pltpu.SemaphoreType.DMA((nb,)), pltpu.SemaphoreType.DMA((nb,))],
    )(*bufs)


def _to_chips(bufs, name):
    nb = len(bufs)

    def body(*refs):
        ins, outs = refs[:nb], refs[nb:2 * nb]
        send_sems, recv_sems, local_sems = refs[2 * nb:]
        x, y, c = lax.axis_index("x"), lax.axis_index("y"), lax.axis_index("c")
        here = 2 * x + y
        local = [pltpu.make_async_copy(ins[b].at[here], outs[b].at[here], local_sems.at[b]) for b in range(nb)]
        remote = [_remote(ins[b].at[2 * px + py], outs[b].at[here], send_sems.at[b], recv_sems.at[b], j, (px, py, c))
                  for j, (px, py) in enumerate(_other_chips(x, y)) for b in range(nb)]
        for cp in local + remote:
            cp.start()
        for cp in remote:
            cp.wait()
        for cp in local:
            cp.wait()

    return pl.pallas_call(
        body, name=name, in_specs=[_HBM] * nb, out_specs=[_HBM] * nb,
        out_shape=[jax.ShapeDtypeStruct(b.shape, b.dtype) for b in bufs],
        scratch_shapes=[pltpu.SemaphoreType.DMA((nb, N_CHIPS - 1)), pltpu.SemaphoreType.DMA((nb, N_CHIPS - 1)),
                        pltpu.SemaphoreType.DMA((nb,))],
    )(*bufs)


def _pair_sum(a, b, row_tile, name):
    n, D0, R, C = a.shape

    def body(a_ref, b_ref, o_ref):
        o_ref[...] = (a_ref[...].astype(F32) + b_ref[...].astype(F32)).astype(o_ref.dtype)

    blk = pl.BlockSpec((1, 1, row_tile, C), lambda s, l, i: (s, l, i, 0))
    return pl.pallas_call(
        body, name=name, grid=(n, D0, R // row_tile), in_specs=[blk, blk], out_specs=blk,
        out_shape=jax.ShapeDtypeStruct(a.shape, a.dtype), compiler_params=_params(("parallel", "parallel", "parallel")),
    )(a, b)


def _sum_sources(recv, row_tile, name):
    n, R, _ = recv.shape

    def body(r_ref, o_ref):
        acc = r_ref[0].astype(F32)
        for s in range(1, n):
            acc = acc + r_ref[s].astype(F32)
        o_ref[...] = acc

    return pl.pallas_call(
        body, name=name, grid=(R // row_tile,),
        in_specs=[pl.BlockSpec((n, row_tile, LANES), lambda i: (0, i, 0))],
        out_specs=pl.BlockSpec((row_tile, LANES), lambda i: (i, 0)),
        out_shape=jax.ShapeDtypeStruct((R, LANES), F32),
        compiler_params=_params(("parallel",)),
    )(recv)


def _adamw_math(g, w, m, v):
    nm = ADAM_B1 * m + (1.0 - ADAM_B1) * g
    nv = ADAM_B2 * v + (1.0 - ADAM_B2) * (g * g)
    m_hat = nm / (1.0 - ADAM_B1 ** ADAM_STEP)
    v_hat = nv / (1.0 - ADAM_B2 ** ADAM_STEP)
    return -ADAM_LR * (m_hat / (jnp.sqrt(v_hat) + ADAM_EPS) + ADAM_WD * w), nm, nv


def _sum_adamw(recv, w, m, v, row_tile, name):
    D0, R, C = w.shape
    n = recv.shape[0]

    def body(r_ref, w_ref, m_ref, v_ref, g_ref, d_ref, nm_ref, nv_ref):
        g = r_ref[0, 0].astype(F32)
        for s in range(1, n):
            g = g + r_ref[s, 0].astype(F32)
        g_ref[0] = g
        d_ref[0], nm_ref[0], nv_ref[0] = _adamw_math(g, w_ref[0], m_ref[0], v_ref[0])

    blk = pl.BlockSpec((1, row_tile, C), lambda l, i: (l, i, 0))
    out = jax.ShapeDtypeStruct((D0, R, C), F32)
    return pl.pallas_call(
        body, name=name, grid=(D0, R // row_tile),
        in_specs=[pl.BlockSpec((n, 1, row_tile, C), lambda l, i: (0, l, i, 0)), blk, blk, blk],
        out_specs=[blk] * 4, out_shape=[out] * 4,
        compiler_params=_params(("parallel", "parallel")),
    )(recv, w, m, v)


def _adamw(g, w, m, v, row_tile, name):
    R = g.shape[0]

    def body(g_ref, w_ref, m_ref, v_ref, d_ref, nm_ref, nv_ref):
        d_ref[...], nm_ref[...], nv_ref[...] = _adamw_math(g_ref[...], w_ref[...], m_ref[...], v_ref[...])

    blk = pl.BlockSpec((row_tile, LANES), lambda i: (i, 0))
    out = jax.ShapeDtypeStruct((R, LANES), F32)
    return pl.pallas_call(
        body, name=name, grid=(R // row_tile,), in_specs=[blk] * 4, out_specs=[blk] * 3, out_shape=[out] * 3,
        compiler_params=_params(("parallel",)),
    )(g, w, m, v)


def _pack(arrs, rows, dtype):
    flat = jnp.concatenate([a.reshape(-1).astype(dtype) for a in arrs])
    return jnp.pad(flat, (0, rows * LANES - flat.shape[0])).reshape(rows, LANES)


def _unpack(buf, shapes):
    lead = buf.shape[:-2]
    flat = buf.reshape(lead + (-1,))
    out, off = [], 0
    for shp in shapes:
        n = 1
        for d in shp:
            n *= d
        out.append(flat[..., off:off + n].reshape(lead + tuple(shp)))
        off += n
    return out


BIG = ["w_mix_in", "w_mix_out", "w_ffn_up", "w_ffn_down"]
BIG_AXIS = {"w_mix_in": 2, "w_mix_out": 1, "w_ffn_up": 2, "w_ffn_down": 1}
CONV = ["w_sconv", "w_gdn_conv", "w_ffn_conv"]
REPL = ["w_norm_mix", "gdn_a_log", "gdn_dt_bias", "w_gdn_norm", "w_norm_ffn", "w_norm_final"]
BIG_ROW_TILE = {"w_mix_in": 512, "w_mix_out": 128, "w_ffn_up": 512, "w_ffn_down": 352}
SMALL_ROWS = 416
CONV_ROWS = 48


def kernel(x, w_norm_mix, w_mix_in, w_sconv, w_gdn_conv, gdn_a_log, gdn_dt_bias, w_gdn_norm, w_mix_out, w_norm_ffn, w_ffn_up, w_ffn_conv, w_ffn_down, w_norm_final, loss_target, m_w_norm_mix, m_w_mix_in, m_w_sconv, m_w_gdn_conv, m_gdn_a_log, m_gdn_dt_bias, m_w_gdn_norm, m_w_mix_out, m_w_norm_ffn, m_w_ffn_up, m_w_ffn_conv, m_w_ffn_down, m_w_norm_final, v_w_norm_mix, v_w_mix_in, v_w_sconv, v_w_gdn_conv, v_gdn_a_log, v_gdn_dt_bias, v_w_gdn_norm, v_w_mix_out, v_w_norm_ffn, v_w_ffn_up, v_w_ffn_conv, v_w_ffn_down, v_w_norm_final):
    w = dict(w_norm_mix=w_norm_mix, w_mix_in=w_mix_in, w_sconv=w_sconv, w_gdn_conv=w_gdn_conv, gdn_a_log=gdn_a_log,
             gdn_dt_bias=gdn_dt_bias, w_gdn_norm=w_gdn_norm, w_mix_out=w_mix_out, w_norm_ffn=w_norm_ffn, w_ffn_up=w_ffn_up,
             w_ffn_conv=w_ffn_conv, w_ffn_down=w_ffn_down, w_norm_final=w_norm_final)
    m = dict(w_norm_mix=m_w_norm_mix, w_mix_in=m_w_mix_in, w_sconv=m_w_sconv, w_gdn_conv=m_w_gdn_conv, gdn_a_log=m_gdn_a_log,
             gdn_dt_bias=m_gdn_dt_bias, w_gdn_norm=m_w_gdn_norm, w_mix_out=m_w_mix_out, w_norm_ffn=m_w_norm_ffn,
             w_ffn_up=m_w_ffn_up, w_ffn_conv=m_w_ffn_conv, w_ffn_down=m_w_ffn_down, w_norm_final=m_w_norm_final)
    v = dict(w_norm_mix=v_w_norm_mix, w_mix_in=v_w_mix_in, w_sconv=v_w_sconv, w_gdn_conv=v_w_gdn_conv, gdn_a_log=v_gdn_a_log,
             gdn_dt_bias=v_gdn_dt_bias, w_gdn_norm=v_w_gdn_norm, w_mix_out=v_w_mix_out, w_norm_ffn=v_w_norm_ffn,
             w_ffn_up=v_w_ffn_up, w_ffn_conv=v_w_ffn_conv, w_ffn_down=v_w_ffn_down, w_norm_final=v_w_norm_final)
    me = 4 * lax.axis_index("x") + 2 * lax.axis_index("y") + lax.axis_index("c")
    conv_shapes = [w[k].shape for k in CONV]

    gathered = _all_gather([w[k].astype(MXU_DTYPE) for k in BIG] + [_pack([w[k] for k in CONV], CONV_ROWS, F32)],
                           "gather_weights")
    full = dict(w)
    for k, got in zip(BIG, gathered):
        full[k] = jnp.concatenate([got[s] for s in range(N_DEV)], axis=BIG_AXIS[k])
    for k, got in zip(CONV, _unpack(gathered[-1], conv_shapes)):
        full[k] = jnp.concatenate([got[s] for s in range(N_DEV)], axis=2)

    loss, dx, grads = _local_step(x[0], full, loss_target[0])

    small = CONV + REPL
    core = lax.axis_index("c")
    by_core = []
    for k in BIG:
        piece = jnp.split(grads[k], N_DEV, axis=BIG_AXIS[k])
        by_core.append(jnp.stack([jnp.stack([piece[2 * ch + p] for ch in range(N_CHIPS)]) for p in range(2)]).astype(MXU_DTYPE))
    from_sibling = _to_sibling(by_core, "grads_to_sibling")
    chip_sums = [_pair_sum(lax.dynamic_index_in_dim(mine, core, 0, keepdims=False), theirs, BIG_ROW_TILE[k], "pair_sum_" + k)
                 for k, mine, theirs in zip(BIG, by_core, from_sibling)]
    received = _to_chips(chip_sums, "grads_to_chips")
    g, delta, new_m, new_v = {}, {}, {}, {}
    for k, got in zip(BIG, received):
        g[k], delta[k], new_m[k], new_v[k] = _sum_adamw(got, w[k], m[k], v[k], BIG_ROW_TILE[k], "adamw_" + k)
    small_parts = _all_gather([_pack([grads[k] for k in small], SMALL_ROWS, F32)], "gather_small_grads")[0]
    g_small = _unpack(_sum_sources(small_parts, SMALL_ROWS, "sum_small"), [grads[k].shape for k in small])
    for k, gs in zip(small, g_small):
        g[k] = lax.dynamic_slice_in_dim(gs, me * w[k].shape[2], w[k].shape[2], axis=2) if k in CONV else gs

    small_shapes = [w[k].shape for k in small]
    small_rows = -(-sum(w[k].size for k in small) // (SUBLANES * LANES)) * SUBLANES
    d_sm, m_sm, v_sm = _adamw(_pack([g[k] for k in small], small_rows, F32), _pack([w[k] for k in small], small_rows, F32),
                              _pack([m[k] for k in small], small_rows, F32), _pack([v[k] for k in small], small_rows, F32),
                              small_rows, "adamw_small")
    for dst, small_buf in ((delta, d_sm), (new_m, m_sm), (new_v, v_sm)):
        dst.update(zip(small, _unpack(small_buf, small_shapes)))

    loss_all = lax.psum(loss[0, 0], ("x", "y", "c"))
    return (loss_all, dx[None], *[g[k] for k in WEIGHTS], *[delta[k] for k in WEIGHTS], *[new_m[k] for k in WEIGHTS],
            *[new_v[k] for k in WEIGHTS])
```

```python
import functools

import jax
import jax.numpy as jnp
from jax import lax
from jax.experimental import pallas as pl
from jax.experimental.pallas import tpu as pltpu

F32 = jnp.float32
MXU_DTYPE = jnp.bfloat16
HIGHEST = lax.Precision.HIGHEST

D_MODEL = 1024
DEPTH = 2
SC_WIDTH = 256
SC_KERNEL = 3
GDN_WIDTH = 512
GDN_HEADS = 4
GDN_HEAD_DIM = 128
GDN_CONV = 4
GDN_CHUNK = 64
SB_WIDTH = 256
SB_HEADS = 4
SB_HEAD_DIM = 64
SB_BLOCK = 128
SB_SWEEP = 8
D_FF = 2816
FFN_CONV = 3
NORM_EPS = 1e-6
D_IN_PROJ = 3592
ADAM_LR, ADAM_B1, ADAM_B2, ADAM_EPS, ADAM_WD, ADAM_STEP = 0.001, 0.9, 0.999, 1e-08, 0.01, 10

N_DEV = 8
LANES = 128
SUBLANES = 8
VMEM_LIMIT = 48 * 1024 * 1024

P_QKV, P_GZ, P_SC, P_SB, P_GAB, P_END = 0, 1536, 2048, 2816, 3584, 3840
Y_GDN, Y_SC, Y_SB = 0, 512, 768


def _params(semantics):
    return pltpu.CompilerParams(dimension_semantics=semantics, vmem_limit_bytes=VMEM_LIMIT)


_DIMS = {"nn": (((1,), (0,)), ((), ())), "nt": (((1,), (1,)), ((), ())), "tn": (((0,), (0,)), ((), ()))}


def _matmul(a, b, mode, out_dtype, name, tm, tn, tk, resid=None):
    if mode == "tn":
        (K, M), (K2, N) = a.shape, b.shape
    elif mode == "nt":
        (M, K), (N, K2) = a.shape, b.shape
    else:
        (M, K), (K2, N) = a.shape, b.shape
    assert K == K2 and M % tm == 0 and N % tn == 0 and K % tk == 0, (name, a.shape, b.shape, tm, tn, tk)
    nk = K // tk
    has_resid = resid is not None
    assert nk == 1 or out_dtype == F32, name

    def body(*refs):
        if has_resid:
            a_ref, b_ref, r_ref, o_ref = refs
        else:
            a_ref, b_ref, o_ref = refs
        k = pl.program_id(2)
        part = lax.dot_general(a_ref[...], b_ref[...], _DIMS[mode], preferred_element_type=F32)
        if nk == 1:
            o_ref[...] = ((part + r_ref[...]) if has_resid else part).astype(out_dtype)
        else:
            @pl.when(k == 0)
            def _():
                o_ref[...] = (part + r_ref[...]) if has_resid else part

            @pl.when(k > 0)
            def _():
                o_ref[...] += part

    a_spec = pl.BlockSpec((tk, tm), lambda i, j, k: (k, i)) if mode == "tn" else pl.BlockSpec((tm, tk), lambda i, j, k: (i, k))
    b_spec = pl.BlockSpec((tn, tk), lambda i, j, k: (j, k)) if mode == "nt" else pl.BlockSpec((tk, tn), lambda i, j, k: (k, j))
    o_spec = pl.BlockSpec((tm, tn), lambda i, j, k: (i, j))
    in_specs = [a_spec, b_spec] + ([o_spec] if has_resid else [])
    args = (a, b) + ((resid,) if has_resid else ())
    return pl.pallas_call(
        body, name=name, grid=(M // tm, N // tn, nk), in_specs=in_specs, out_specs=o_spec,
        out_shape=jax.ShapeDtypeStruct((M, N), out_dtype),
        compiler_params=_params(("parallel", "parallel", "arbitrary")),
    )(*args)


def _rms(x, w):
    return x * lax.rsqrt(jnp.mean(x * x, axis=-1, keepdims=True) + NORM_EPS) * w


ROW_TILE = 512


def _rms_fwd(x, w, name):
    L, Dm = x.shape

    def body(x_ref, w_ref, h_ref):
        h_ref[...] = _rms(x_ref[...], w_ref[...]).astype(h_ref.dtype)

    return pl.pallas_call(
        body, name=name, grid=(L // ROW_TILE,),
        in_specs=[pl.BlockSpec((ROW_TILE, Dm), lambda i: (i, 0)), pl.BlockSpec((1, Dm), lambda i: (0, 0))],
        out_specs=pl.BlockSpec((ROW_TILE, Dm), lambda i: (i, 0)),
        out_shape=jax.ShapeDtypeStruct((L, Dm), MXU_DTYPE),
        compiler_params=_params(("parallel",)),
    )(x, w)


def _rms_bwd(x, w, dh, dres, name):
    L, Dm = x.shape

    def body(x_ref, w_ref, dh_ref, dres_ref, dx_ref, dxb_ref, dw_ref):
        _, vjp = jax.vjp(_rms, x_ref[...], w_ref[...])
        dx, dw = vjp(dh_ref[...])
        dx = dres_ref[...] + dx
        dx_ref[...] = dx
        dxb_ref[...] = dx.astype(dxb_ref.dtype)

        @pl.when(pl.program_id(0) == 0)
        def _():
            dw_ref[...] = jnp.zeros_like(dw_ref)

        dw_ref[...] += dw

    row = pl.BlockSpec((ROW_TILE, Dm), lambda i: (i, 0))
    vec = pl.BlockSpec((1, Dm), lambda i: (0, 0))
    return pl.pallas_call(
        body, name=name, grid=(L // ROW_TILE,), in_specs=[row, vec, row, row], out_specs=[row, row, vec],
        out_shape=[jax.ShapeDtypeStruct((L, Dm), F32), jax.ShapeDtypeStruct((L, Dm), MXU_DTYPE), jax.ShapeDtypeStruct((1, Dm), F32)],
        compiler_params=_params(("arbitrary",)),
    )(x, w, dh, dres)


def _loss_head(x, w, target, name):
    L, Dm = x.shape

    def block_loss(xb, wb, tb):
        err = _rms(xb, wb) - tb
        return 0.5 * jnp.sum(jnp.sum(err * err, axis=-1, keepdims=True) * (1.0 / Dm), axis=0, keepdims=True)

    def body(x_ref, w_ref, t_ref, loss_ref, dx_ref, dxb_ref, dw_ref):
        val, vjp = jax.vjp(lambda xb, wb: block_loss(xb, wb, t_ref[...]), x_ref[...], w_ref[...])
        dx, dw = vjp(jnp.ones_like(val))
        dx_ref[...] = dx
        dxb_ref[...] = dx.astype(dxb_ref.dtype)

        @pl.when(pl.program_id(0) == 0)
        def _():
            dw_ref[...] = jnp.zeros_like(dw_ref)
            loss_ref[...] = jnp.zeros_like(loss_ref)

        dw_ref[...] += dw
        loss_ref[...] += val

    row = pl.BlockSpec((ROW_TILE, Dm), lambda i: (i, 0))
    vec = pl.BlockSpec((1, Dm), lambda i: (0, 0))
    one = pl.BlockSpec((1, 1), lambda i: (0, 0))
    return pl.pallas_call(
        body, name=name, grid=(L // ROW_TILE,), in_specs=[row, vec, row], out_specs=[one, row, row, vec],
        out_shape=[jax.ShapeDtypeStruct((1, 1), F32), jax.ShapeDtypeStruct((L, Dm), F32), jax.ShapeDtypeStruct((L, Dm), MXU_DTYPE),
                   jax.ShapeDtypeStruct((1, Dm), F32)],
        compiler_params=_params(("arbitrary",)),
    )(x, w, target)


HALO = SUBLANES


def _conv_specs(L, T, Cb, off):
    main = pl.BlockSpec((T, Cb), lambda j, i: (i, off + j))
    prev = pl.BlockSpec((HALO, Cb), lambda j, i: (jnp.maximum(i * (T // HALO) - 1, 0), off + j))
    nxt = pl.BlockSpec((HALO, Cb), lambda j, i: (jnp.minimum((i + 1) * (T // HALO), L // HALO - 1), off + j))
    return main, prev, nxt


def _conv_fwd(x1, w, K, Cb, ncol, out_dtype, name, x2=None, gate=None):
    (x1a, o1) = x1
    L = x1a.shape[0]
    T = min(ROW_TILE, L)
    has_mul, has_gate = x2 is not None, gate is not None

    def body(*refs):
        it = iter(refs)
        x1m, x1p = next(it), next(it)
        if has_mul:
            x2m, x2p = next(it), next(it)
        if has_gate:
            gm = next(it)
        w_ref, y_ref, scr = next(it), next(it), next(it)
        i = pl.program_id(1)
        p, pp = x1m[...].astype(F32), x1p[...].astype(F32)
        if has_mul:
            p, pp = p * x2m[...], pp * x2p[...]
        scr[0:HALO, :] = jnp.where(i > 0, pp, 0.0)
        scr[HALO:HALO + T, :] = p
        acc = w_ref[K - 1:K, :] * p
        for k in range(K - 1):
            s = K - 1 - k
            acc = acc + w_ref[k:k + 1, :] * scr[HALO - s:HALO - s + T, :]
        if has_gate:
            acc = acc * gm[...]
        y_ref[...] = acc.astype(out_dtype)

    in_specs, args = [], []
    m, p_, _ = _conv_specs(L, T, Cb, o1)
    in_specs += [m, p_]
    args += [x1a, x1a]
    if has_mul:
        m, p_, _ = _conv_specs(L, T, Cb, x2[1])
        in_specs += [m, p_]
        args += [x2[0], x2[0]]
    if has_gate:
        m, _, _ = _conv_specs(L, T, Cb, gate[1])
        in_specs += [m]
        args += [gate[0]]
    in_specs.append(pl.BlockSpec((K, Cb), lambda j, i: (0, j)))
    args.append(w)
    return pl.pallas_call(
        body, name=name, grid=(ncol, L // T), in_specs=in_specs,
        out_specs=pl.BlockSpec((T, Cb), lambda j, i: (i, j)),
        out_shape=jax.ShapeDtypeStruct((L, ncol * Cb), out_dtype),
        scratch_shapes=[pltpu.VMEM((T + HALO, Cb), F32)],
        compiler_params=_params(("parallel", "arbitrary")),
    )(*args)


def _conv_bwd(x1, w, dy, K, Cb, ncol, out_dtype, name, x2=None, gate=None):
    (x1a, o1) = x1
    L = x1a.shape[0]
    T = min(ROW_TILE, L)
    nrow = L // T
    has_mul, has_gate = x2 is not None, gate is not None

    def body(*refs):
        it = iter(refs)
        x1m, x1p = next(it), next(it)
        if has_mul:
            x2m, x2p = next(it), next(it)
        if has_gate:
            gm, gn = next(it), next(it)
        dym, dyn, w_ref = next(it), next(it), next(it)
        dx1_ref = next(it)
        if has_mul:
            dx2_ref = next(it)
        if has_gate:
            dg_ref = next(it)
        dw_ref, scr_p, scr_d = next(it), next(it), next(it)
        i = pl.program_id(1)
        p, pp = x1m[...].astype(F32), x1p[...].astype(F32)
        if has_mul:
            p, pp = p * x2m[...], pp * x2p[...]
        scr_p[0:HALO, :] = jnp.where(i > 0, pp, 0.0)
        scr_p[HALO:HALO + T, :] = p
        dcv, dcn = dym[...].astype(F32), dyn[...].astype(F32)
        if has_gate:
            dcv, dcn = dcv * gm[...], dcn * gn[...]
        scr_d[0:T, :] = dcv
        scr_d[T:T + HALO, :] = jnp.where(i < nrow - 1, dcn, 0.0)

        @pl.when(i == 0)
        def _():
            dw_ref[...] = jnp.zeros_like(dw_ref)

        dp = w_ref[K - 1:K, :] * dcv
        cv = w_ref[K - 1:K, :] * p
        dw_ref[K - 1:K, :] += jnp.sum(dcv * p, axis=0, keepdims=True)
        for k in range(K - 1):
            s = K - 1 - k
            dp = dp + w_ref[k:k + 1, :] * scr_d[s:s + T, :]
            sh = scr_p[HALO - s:HALO - s + T, :]
            dw_ref[k:k + 1, :] += jnp.sum(dcv * sh, axis=0, keepdims=True)
            if has_gate:
                cv = cv + w_ref[k:k + 1, :] * sh
        if has_gate:
            dg_ref[...] = (dym[...].astype(F32) * cv).astype(out_dtype)
        if has_mul:
            dx1_ref[...] = (dp * x2m[...]).astype(out_dtype)
            dx2_ref[...] = (dp * x1m[...]).astype(out_dtype)
        else:
            dx1_ref[...] = dp.astype(out_dtype)

    in_specs, args = [], []
    m, p_, _ = _conv_specs(L, T, Cb, o1)
    in_specs += [m, p_]
    args += [x1a, x1a]
    if has_mul:
        m, p_, _ = _conv_specs(L, T, Cb, x2[1])
        in_specs += [m, p_]
        args += [x2[0], x2[0]]
    if has_gate:
        m, _, n_ = _conv_specs(L, T, Cb, gate[1])
        in_specs += [m, n_]
        args += [gate[0], gate[0]]
    m, _, n_ = _conv_specs(L, T, Cb, dy[1])
    in_specs += [m, n_, pl.BlockSpec((K, Cb), lambda j, i: (0, j))]
    args += [dy[0], dy[0], w]
    out = pl.BlockSpec((T, Cb), lambda j, i: (i, j))
    full = jax.ShapeDtypeStruct((L, ncol * Cb), out_dtype)
    n_out = 1 + int(has_mul) + int(has_gate)
    return pl.pallas_call(
        body, name=name, grid=(ncol, nrow), in_specs=in_specs,
        out_specs=[out] * n_out + [pl.BlockSpec((SUBLANES, Cb), lambda j, i: (0, j))],
        out_shape=[full] * n_out + [jax.ShapeDtypeStruct((SUBLANES, ncol * Cb), F32)],
        scratch_shapes=[pltpu.VMEM((T + HALO, Cb), F32), pltpu.VMEM((T + HALO, Cb), F32)],
        compiler_params=_params(("parallel", "arbitrary")),
    )(*args)


GLU_COLS = 256


def _silu(x):
    return x * (1.0 / (1.0 + jnp.exp(-x)))


def _glu(g, v):
    return _silu(g) * v


def _causal_taps(w_ref, scr, first, rows, K):
    acc = w_ref[K - 1:K, :] * scr[first:first + rows, :]
    for k in range(K - 1):
        s = K - 1 - k
        acc = acc + w_ref[k:k + 1, :] * scr[first - s:first - s + rows, :]
    return acc


def _ffn_act_fwd(up, w, name):
    L = up.shape[0]
    T, Cb, K = min(ROW_TILE, L), GLU_COLS, FFN_CONV
    nb = D_FF // Cb

    def body(gm, gp, vm, vp, wg, wv, a_ref, sg, sv):
        i = pl.program_id(1)
        for main, prev, scr in ((gm, gp, sg), (vm, vp, sv)):
            scr[0:HALO, :] = jnp.where(i > 0, prev[...], 0.0)
            scr[HALO:HALO + T, :] = main[...]
        a_ref[...] = _glu(_causal_taps(wg, sg, HALO, T, K), _causal_taps(wv, sv, HALO, T, K)).astype(a_ref.dtype)

    gmain, gprev, _ = _conv_specs(L, T, Cb, 0)
    vmain, vprev, _ = _conv_specs(L, T, Cb, nb)
    return pl.pallas_call(
        body, name=name, grid=(nb, L // T),
        in_specs=[gmain, gprev, vmain, vprev, pl.BlockSpec((K, Cb), lambda j, i: (0, j)), pl.BlockSpec((K, Cb), lambda j, i: (0, nb + j))],
        out_specs=pl.BlockSpec((T, Cb), lambda j, i: (i, j)),
        out_shape=jax.ShapeDtypeStruct((L, D_FF), MXU_DTYPE),
        scratch_shapes=[pltpu.VMEM((T + HALO, Cb), F32), pltpu.VMEM((T + HALO, Cb), F32)],
        compiler_params=_params(("parallel", "arbitrary")),
    )(up, up, up, up, w, w)


def _ffn_act_bwd(up, w, dact, name):
    L = up.shape[0]
    T, Cb, K = min(ROW_TILE, L), GLU_COLS, FFN_CONV
    nb, nrow = D_FF // Cb, L // T

    def body(gm, gp, gn, vm, vp, vn, dam, dan, wg, wv, dg_ref, dv_ref, dwg_ref, dwv_ref, sg, sv, sdg, sdv):
        i = pl.program_id(1)
        for main, prev, nxt, scr in ((gm, gp, gn, sg), (vm, vp, vn, sv)):
            scr[0:HALO, :] = jnp.where(i > 0, prev[...], 0.0)
            scr[HALO:HALO + T, :] = main[...]
            scr[HALO + T:2 * HALO + T, :] = nxt[...]
        ug, uv = _causal_taps(wg, sg, HALO, T + HALO, K), _causal_taps(wv, sv, HALO, T + HALO, K)
        da = jnp.concatenate([dam[...], jnp.where(i < nrow - 1, dan[...], 0.0)], axis=0)
        _, vjp = jax.vjp(_glu, ug, uv)
        sdg[...], sdv[...] = vjp(da)

        @pl.when(i == 0)
        def _():
            dwg_ref[...] = jnp.zeros_like(dwg_ref)
            dwv_ref[...] = jnp.zeros_like(dwv_ref)

        for w_ref, scr, sd, d_ref, dw_ref in ((wg, sg, sdg, dg_ref, dwg_ref), (wv, sv, sdv, dv_ref, dwv_ref)):
            du = sd[0:T, :]
            dp = w_ref[K - 1:K, :] * du
            dw_ref[K - 1:K, :] += jnp.sum(du * scr[HALO:HALO + T, :], axis=0, keepdims=True)
            for k in range(K - 1):
                s = K - 1 - k
                dp = dp + w_ref[k:k + 1, :] * sd[s:s + T, :]
                dw_ref[k:k + 1, :] += jnp.sum(du * scr[HALO - s:HALO - s + T, :], axis=0, keepdims=True)
            d_ref[...] = dp.astype(d_ref.dtype)

    gmain, gprev, gnext = _conv_specs(L, T, Cb, 0)
    vmain, vprev, vnext = _conv_specs(L, T, Cb, nb)
    dmain, _, dnext = _conv_specs(L, T, Cb, 0)
    out = pl.BlockSpec((T, Cb), lambda j, i: (i, j))
    dwb = pl.BlockSpec((SUBLANES, Cb), lambda j, i: (0, j))
    half = jax.ShapeDtypeStruct((L, D_FF), MXU_DTYPE)
    dwh = jax.ShapeDtypeStruct((SUBLANES, D_FF), F32)
    dg, dv, dwg, dwv = pl.pallas_call(
        body, name=name, grid=(nb, nrow),
        in_specs=[gmain, gprev, gnext, vmain, vprev, vnext, dmain, dnext,
                  pl.BlockSpec((K, Cb), lambda j, i: (0, j)), pl.BlockSpec((K, Cb), lambda j, i: (0, nb + j))],
        out_specs=[out, out, dwb, dwb], out_shape=[half, half, dwh, dwh],
        scratch_shapes=[pltpu.VMEM((T + 2 * HALO, Cb), F32), pltpu.VMEM((T + 2 * HALO, Cb), F32),
                        pltpu.VMEM((T + HALO, Cb), F32), pltpu.VMEM((T + HALO, Cb), F32)],
        compiler_params=_params(("parallel", "arbitrary")),
    )(up, up, up, up, up, up, dact, dact, w, w)
    return jnp.concatenate([dg, dv], axis=1), jnp.concatenate([dwg, dwv], axis=1)


def _bdot_raw(a, b, mode):
    return lax.dot_general(a.astype(MXU_DTYPE), b.astype(MXU_DTYPE), _DIMS[mode], preferred_element_type=F32)


@functools.partial(jax.custom_vjp, nondiff_argnums=(2,))
def _bdot(a, b, mode):
    return _bdot_raw(a, b, mode)


def _bdot_fwd(a, b, mode):
    return _bdot_raw(a, b, mode), (a, b)


def _bdot_bwd(mode, res, ct):
    a, b = res
    if mode == "nn":
        return _bdot_raw(ct, b, "nt"), _bdot_raw(a, ct, "tn")
    if mode == "nt":
        return _bdot_raw(ct, b, "nn"), _bdot_raw(ct, a, "tn")
    return _bdot_raw(b, ct, "nt"), _bdot_raw(a, ct, "nn")


_bdot.defvjp(_bdot_fwd, _bdot_bwd)


def _hdot(a, b, mode="nn"):
    return lax.dot_general(a, b, _DIMS[mode], precision=lax.Precision.HIGH, preferred_element_type=F32)


@jax.custom_vjp
def _inv_unit_lower(a):
    R = a.shape[0]
    eye = (lax.broadcasted_iota(jnp.int32, (R, R), 0) == lax.broadcasted_iota(jnp.int32, (R, R), 1)).astype(F32)
    t = eye - a
    p = a
    n = 1
    while 2 * n < GDN_CHUNK:
        p = _hdot(p, p)
        t = t + _hdot(t, p)
        n *= 2
    return t


def _inv_fwd(a):
    t = _inv_unit_lower(a)
    return t, t


def _inv_bwd(t, ct):
    return (-_hdot(_hdot(t, ct, "tn"), t, "nt"),)


_inv_unit_lower.defvjp(_inv_fwd, _inv_bwd)


@jax.custom_vjp
def _inv_saved(a, t):
    return t


def _inv_saved_fwd(a, t):
    return t, t


def _inv_saved_bwd(t, ct):
    return _inv_bwd(t, ct)[0], jnp.zeros_like(t)


_inv_saved.defvjp(_inv_saved_fwd, _inv_saved_bwd)


def _softplus(x):
    return jnp.maximum(x, 0.0) + jnp.log(1.0 + jnp.exp(-jnp.abs(x)))


def _sigmoid(x):
    return 1.0 / (1.0 + jnp.exp(-x))


def _pick_lane(blk, lane):
    ids = lax.broadcasted_iota(jnp.int32, blk.shape, 1)
    return jnp.sum(jnp.where(ids == lane, blk, 0.0), axis=1, keepdims=True)


def _gdn_chunk(cq, ck, cv, gz, gab, alog, dtb, wn, S, t_saved=None):
    C, H, Dk = GDN_CHUNK, GDN_HEADS, GDN_HEAD_DIM
    R = H * C
    rows_of = lambda vals, n: jnp.concatenate([jnp.broadcast_to(x, (n, 1)) for x in vals], axis=0)
    ga = jnp.concatenate([_pick_lane(gab, h) for h in range(H)], axis=0)
    gb = jnp.concatenate([_pick_lane(gab, H + h) for h in range(H)], axis=0)
    al = rows_of([_pick_lane(alog, h) for h in range(H)], C)
    db = rows_of([_pick_lane(dtb, h) for h in range(H)], C)
    q, k, v = _silu(cq), _silu(ck), _silu(cv)
    q = q * lax.rsqrt(jnp.sum(q * q, axis=-1, keepdims=True) + NORM_EPS) * (Dk ** -0.5)
    k = k * lax.rsqrt(jnp.sum(k * k, axis=-1, keepdims=True) + NORM_EPS)
    beta = _sigmoid(gb)
    g = -jnp.exp(al) * _softplus(ga + db)
    row = lax.broadcasted_iota(jnp.int32, (R, R), 0)
    col = lax.broadcasted_iota(jnp.int32, (R, R), 1)
    same_head = (row // C) == (col // C)
    causal, strict = same_head & (row >= col), same_head & (row > col)
    gcb = _hdot(causal.astype(F32), jnp.broadcast_to(g, (R, Dk)))
    first = (lax.broadcasted_iota(jnp.int32, (R, Dk), 1) == 0).astype(F32)
    gr = _hdot(first, gcb, "nt")
    gc = _pick_lane(gcb, 0)
    decay = jnp.where(causal, jnp.exp(jnp.where(causal, gc - gr, 0.0)), 0.0)
    kb = k * beta
    lower = jnp.where(strict, _bdot(kb, k, "nt") * decay, 0.0)
    t = _inv_unit_lower(lower) if t_saved is None else _inv_saved(lower, t_saved)
    egc = jnp.exp(gc)
    u = _hdot(t, v * beta)
    w = _hdot(t, kb * egc)
    attn = jnp.where(causal, _bdot(q, k, "nt") * decay, 0.0)
    own = (lax.broadcasted_iota(jnp.int32, (R, H * Dk), 0) // C) == (lax.broadcasted_iota(jnp.int32, (R, H * Dk), 1) // Dk)
    spread = lambda x: jnp.where(own, jnp.concatenate([x] * H, axis=1), 0.0)
    v_new = u - _bdot(spread(w), S, "nn")
    o = _bdot(spread(q * egc), S, "nn") + _bdot(attn, v_new, "nn")
    last = lax.broadcasted_iota(jnp.int32, (R, 1), 0)
    g_last = [jnp.sum(jnp.where(last == h * C + C - 1, gc, 0.0), axis=0, keepdims=True) for h in range(H)]
    S_new = S * jnp.exp(rows_of(g_last, Dk)) + _bdot(spread(k * jnp.exp(rows_of(g_last, C) - gc)), v_new, "tn")
    y = o * lax.rsqrt(jnp.mean(o * o, axis=-1, keepdims=True) + NORM_EPS) * wn * _silu(gz)
    return y, S_new, t


def _stack_heads(ref, first, width=GDN_HEAD_DIM):
    return jnp.concatenate([ref[:, first + h * width:first + (h + 1) * width] for h in range(GDN_HEADS)], axis=0)


def _gdn_fwd(cqkv, proj, alog, dtb, wn, name):
    L = cqkv.shape[0]
    C, H, Dh = GDN_CHUNK, GDN_HEADS, GDN_HEAD_DIM
    N = L // C

    def body(c_ref, gz_ref, gab_ref, al_ref, db_ref, wn_ref, y_ref, sall_ref, tall_ref, S):
        n = pl.program_id(0)

        @pl.when(n == 0)
        def _():
            S[...] = jnp.zeros_like(S)

        s_in = S[...]
        sall_ref[0] = s_in
        y, s_new, t = _gdn_chunk(_stack_heads(c_ref, 0), _stack_heads(c_ref, GDN_WIDTH), _stack_heads(c_ref, 2 * GDN_WIDTH),
                                 _stack_heads(gz_ref, 0), gab_ref[...], al_ref[...], db_ref[...], wn_ref[...], s_in)
        for h in range(H):
            y_ref[:, h * Dh:(h + 1) * Dh] = y[h * C:(h + 1) * C].astype(y_ref.dtype)
        S[...] = s_new
        tall_ref[0] = t

    vec = pl.BlockSpec((1, LANES), lambda n: (0, 0))
    return pl.pallas_call(
        body, name=name, grid=(N,),
        in_specs=[pl.BlockSpec((C, 3 * GDN_WIDTH), lambda n: (n, 0)),
                  pl.BlockSpec((C, GDN_WIDTH), lambda n: (n, P_GZ // GDN_WIDTH)),
                  pl.BlockSpec((C, LANES), lambda n: (n, P_GAB // LANES)), vec, vec, vec],
        out_specs=[pl.BlockSpec((C, GDN_WIDTH), lambda n: (n, 0)), pl.BlockSpec((1, H * Dh, Dh), lambda n: (n, 0, 0)),
                   pl.BlockSpec((1, H * C, H * C), lambda n: (n, 0, 0))],
        out_shape=[jax.ShapeDtypeStruct((L, GDN_WIDTH), MXU_DTYPE), jax.ShapeDtypeStruct((N, H * Dh, Dh), F32),
                   jax.ShapeDtypeStruct((N, H * C, H * C), F32)],
        scratch_shapes=[pltpu.VMEM((H * Dh, Dh), F32)],
        compiler_params=_params(("arbitrary",)),
    )(cqkv, proj, proj, alog, dtb, wn)


def _gdn_bwd(cqkv, proj, alog, dtb, wn, s_all, t_all, dy, name):
    L = cqkv.shape[0]
    C, H, Dh = GDN_CHUNK, GDN_HEADS, GDN_HEAD_DIM
    N = L // C

    def body(c_ref, gz_ref, gab_ref, al_ref, db_ref, wn_ref, sall_ref, tall_ref, dy_ref,
             dc_ref, dgz_ref, dgab_ref, dal_ref, ddb_ref, dwn_ref, dS):
        n = pl.program_id(0)

        @pl.when(n == 0)
        def _():
            dS[...] = jnp.zeros_like(dS)
            dal_ref[...] = jnp.zeros_like(dal_ref)
            ddb_ref[...] = jnp.zeros_like(ddb_ref)
            dwn_ref[...] = jnp.zeros_like(dwn_ref)

        t_saved = tall_ref[0]
        chunk = lambda *a: _gdn_chunk(*a, t_saved=t_saved)[:2]
        _, vjp = jax.vjp(chunk, _stack_heads(c_ref, 0), _stack_heads(c_ref, GDN_WIDTH), _stack_heads(c_ref, 2 * GDN_WIDTH),
                         _stack_heads(gz_ref, 0), gab_ref[...], al_ref[...], db_ref[...], wn_ref[...], sall_ref[0])
        dq, dk, dv, dgz, dgab, dal, ddb, dwn, ds = vjp((_stack_heads(dy_ref, 0), dS[...]))
        for h in range(H):
            rows = slice(h * C, (h + 1) * C)
            dc_ref[:, h * Dh:(h + 1) * Dh] = dq[rows]
            dc_ref[:, (H + h) * Dh:(H + h + 1) * Dh] = dk[rows]
            dc_ref[:, (2 * H + h) * Dh:(2 * H + h + 1) * Dh] = dv[rows]
            dgz_ref[:, h * Dh:(h + 1) * Dh] = dgz[rows].astype(dgz_ref.dtype)
        dS[...] = ds
        dgab_ref[...] = dgab
        dal_ref[...] += dal
        ddb_ref[...] += ddb
        dwn_ref[...] += dwn

    vec = pl.BlockSpec((1, LANES), lambda n: (0, 0))
    rev = lambda n: N - 1 - n
    return pl.pallas_call(
        body, name=name, grid=(N,),
        in_specs=[pl.BlockSpec((C, 3 * GDN_WIDTH), lambda n: (rev(n), 0)),
                  pl.BlockSpec((C, GDN_WIDTH), lambda n: (rev(n), P_GZ // GDN_WIDTH)),
                  pl.BlockSpec((C, LANES), lambda n: (rev(n), P_GAB // LANES)), vec, vec, vec,
                  pl.BlockSpec((1, H * Dh, Dh), lambda n: (rev(n), 0, 0)),
                  pl.BlockSpec((1, H * C, H * C), lambda n: (rev(n), 0, 0)),
                  pl.BlockSpec((C, GDN_WIDTH), lambda n: (rev(n), Y_GDN // GDN_WIDTH))],
        out_specs=[pl.BlockSpec((C, 3 * GDN_WIDTH), lambda n: (rev(n), 0)),
                   pl.BlockSpec((C, GDN_WIDTH), lambda n: (rev(n), 0)),
                   pl.BlockSpec((C, LANES), lambda n: (rev(n), 0)), vec, vec, vec],
        out_shape=[jax.ShapeDtypeStruct((L, 3 * GDN_WIDTH), F32), jax.ShapeDtypeStruct((L, GDN_WIDTH), MXU_DTYPE),
                   jax.ShapeDtypeStruct((L, LANES), F32)] + [jax.ShapeDtypeStruct((1, LANES), F32)] * 3,
        scratch_shapes=[pltpu.VMEM((H * Dh, Dh), F32)],
        compiler_params=_params(("arbitrary",)),
    )(cqkv, proj, proj, alog, dtb, wn, s_all, t_all, dy)


def _split_dot(x, m):
    R = x.shape[0]
    hi = x.astype(MXU_DTYPE)
    lo = (x - hi.astype(F32)).astype(MXU_DTYPE)
    both = jnp.dot(jnp.concatenate([hi, lo], axis=0), m, preferred_element_type=F32)
    return both[:R] + both[R:]


def _sb_kv_blocks(kv_ref, js):
    B = SB_BLOCK
    rows = [pl.ds(pl.multiple_of(j * B, B), B) for j in js]
    kps = [[kv_ref[r, p * LANES:(p + 1) * LANES] for p in range(SB_HEADS // 2)] for r in rows]
    vps = [[kv_ref[r, SB_WIDTH + p * LANES:SB_WIDTH + (p + 1) * LANES] for p in range(SB_HEADS // 2)] for r in rows]
    return rows, kps, vps


def _sb_pair_dots(x, mats, mode):
    B = SB_BLOCK
    return jnp.concatenate([lax.dot_general(x[2 * p * B:(2 * p + 2) * B], m, _DIMS[mode], preferred_element_type=F32)
                            for mp in mats for p, m in enumerate(mp)], axis=0)


SB_SCALE = SB_HEAD_DIM ** -0.5


def _sb_logits(qx, kps):
    z = _sb_pair_dots(qx, kps, "nt")
    return z, jnp.minimum(z, 0.0) - jnp.log(1.0 + jnp.exp(-jnp.abs(z)))


def _sb_running(start, sums, inclusive):
    R = start.shape[0]
    n = sums.shape[0] // R
    vals, cur = [], start
    for b in range(n):
        nxt = cur + sums[b * R:(b + 1) * R]
        vals.append(nxt if inclusive else cur)
        cur = nxt
    return (vals[0] if n == 1 else jnp.concatenate(vals, axis=0)), cur


def _sb_head_masks():
    low = lax.broadcasted_iota(jnp.int32, (SB_BLOCK, LANES), 1) < SB_HEAD_DIM
    return [low if h % 2 == 0 else jnp.logical_not(low) for h in range(SB_HEADS)]


def _sb_stack_heads(ref, scale=1.0):
    mine = _sb_head_masks()
    return jnp.concatenate([jnp.where(mine[h], ref[:, (h // 2) * LANES:(h // 2 + 1) * LANES] * scale, 0.0).astype(MXU_DTYPE)
                            for h in range(SB_HEADS)], axis=0)


def _sb_block_masks():
    B = SB_BLOCK
    row = lax.broadcasted_iota(jnp.int32, (B, B), 0)
    col = lax.broadcasted_iota(jnp.int32, (B, B), 1)
    row4 = lax.broadcasted_iota(jnp.int32, (SB_HEADS * B, B), 0) & (B - 1)
    col4 = lax.broadcasted_iota(jnp.int32, (SB_HEADS * B, B), 1)
    return (row > col).astype(MXU_DTYPE), (row < col).astype(MXU_DTYPE), col4 < row4


def _sb_fwd(proj, kv, name):
    L = proj.shape[0]
    B, H = SB_BLOCK, SB_HEADS

    def body(q_ref, kv_ref, y_ref, c_ref):
        i = pl.program_id(0)
        low = _sb_head_masks()[0]
        after, _, strict = _sb_block_masks()
        qx = _sb_stack_heads(q_ref, SB_SCALE)

        def sweep(js, c, accs, masked):
            _, kps, vps = _sb_kv_blocks(kv_ref, js)
            z, lb = _sb_logits(qx, kps)
            lom = lb - z
            if masked:
                lom = jnp.where(strict, lom, 0.0)
            before_block, c = _sb_running(c, jnp.sum(lom, axis=1, keepdims=True), False)
            a = jnp.exp(lb + _split_dot(lom, after) + before_block)
            if masked:
                a = jnp.where(strict, a, 0.0)
            a = a.astype(MXU_DTYPE)
            new_accs = list(accs)
            for b in range(len(js)):
                o = _sb_pair_dots(a[b * H * B:(b + 1) * H * B], [vps[b]], "nn")
                for p in range(H // 2):
                    new_accs[p] = new_accs[p] + jnp.where(low, o[2 * p * B:(2 * p + 1) * B], o[(2 * p + 1) * B:(2 * p + 2) * B])
            return c, new_accs

        c, accs = sweep([i], jnp.zeros((H * B, 1), F32), [jnp.zeros((B, LANES), F32)] * (H // 2), True)

        W, M = SB_SWEEP, SB_SWEEP // 2

        def wide(it, carry):
            j = i - 1 - W * it
            c, accs = sweep([j - b for b in range(W)], carry[0], list(carry[1:]), False)
            return (c,) + tuple(accs)

        def mid(it, carry):
            j = i % W - 1
            c, accs = sweep([j - b for b in range(M)], carry[0], list(carry[1:]), False)
            return (c,) + tuple(accs)

        def one(it, carry):
            c, accs = sweep([i % M - 1 - it], carry[0], list(carry[1:]), False)
            return (c,) + tuple(accs)

        carry = lax.fori_loop(0, i // W, wide, (c,) + tuple(accs))
        carry = lax.fori_loop(0, (i % W) // M, mid, carry)
        carry = lax.fori_loop(0, i % M, one, carry)
        for p in range(H // 2):
            y_ref[:, p * LANES:(p + 1) * LANES] = carry[1 + p].astype(y_ref.dtype)
        lane = lax.broadcasted_iota(jnp.int32, (B, LANES), 1)
        ct = jnp.zeros((B, LANES), F32)
        for h in range(H):
            ct = jnp.where(lane == h, carry[0][h * B:(h + 1) * B], ct)
        c_ref[...] = ct

    return pl.pallas_call(
        body, name=name, grid=(L // B,),
        in_specs=[pl.BlockSpec((B, SB_WIDTH), lambda i: (i, P_SB // SB_WIDTH)), pl.BlockSpec((L, 2 * SB_WIDTH), lambda i: (0, 0))],
        out_specs=[pl.BlockSpec((B, SB_WIDTH), lambda i: (i, 0)), pl.BlockSpec((B, LANES), lambda i: (i, 0))],
        out_shape=[jax.ShapeDtypeStruct((L, SB_WIDTH), MXU_DTYPE), jax.ShapeDtypeStruct((L, LANES), F32)],
        compiler_params=_params(("arbitrary",)),
    )(proj, kv)


def _sb_bwd(proj, kv, dy, ctot, name):
    L = proj.shape[0]
    B, H = SB_BLOCK, SB_HEADS
    nblk = L // B

    def body(q_ref, kv_ref, do_ref, ct_ref, dq_ref, dk_hbm, dv_hbm, dk_acc, dv_acc):
        i = pl.program_id(0)

        @pl.when(i == 0)
        def _():
            dk_acc[...] = jnp.zeros_like(dk_acc)
            dv_acc[...] = jnp.zeros_like(dv_acc)

        low = _sb_head_masks()[0]
        after, before, strict = _sb_block_masks()
        qx, dox = _sb_stack_heads(q_ref, SB_SCALE), _sb_stack_heads(do_ref)
        ct = ct_ref[...]
        ctot = jnp.concatenate([_pick_lane(ct, h) for h in range(H)], axis=0)

        def sweep(js, p, e, dqs, masked):
            n = len(js)
            rows, kps, vps = _sb_kv_blocks(kv_ref, js)
            z, lb = _sb_logits(qx, kps)
            sig = jnp.exp(lb)
            lom = lb - z
            if masked:
                lom = jnp.where(strict, lom, 0.0)
            through_block, p = _sb_running(p, jnp.sum(lom, axis=1, keepdims=True), True)
            right_of_block = (ctot if n == 1 else jnp.concatenate([ctot] * n, axis=0)) - through_block
            a = jnp.exp(lb + _split_dot(lom, after) + right_of_block)
            if masked:
                a = jnp.where(strict, a, 0.0)
            ea = _sb_pair_dots(dox, vps, "nt") * a
            left_of_block, e = _sb_running(e, jnp.sum(ea, axis=1, keepdims=True), False)
            dlom = left_of_block + jnp.dot(ea.astype(MXU_DTYPE), before, preferred_element_type=F32)
            if masked:
                dlom = jnp.where(strict, dlom, 0.0)
            dz = (ea * (1.0 - sig) - dlom * sig).astype(MXU_DTYPE)
            ab = a.astype(MXU_DTYPE)
            new_dq = list(dqs)
            for b in range(n):
                for pr in range(H // 2):
                    heads = slice(2 * pr * B, (2 * pr + 2) * B)
                    both, cols = slice((b * H + 2 * pr) * B, (b * H + 2 * pr + 2) * B), slice(pr * LANES, (pr + 1) * LANES)
                    dqp = jnp.dot(dz[both], kps[b][pr], preferred_element_type=F32)
                    new_dq[pr] = new_dq[pr] + jnp.where(low, dqp[:B], dqp[B:])
                    dk_acc[rows[b], cols] += lax.dot_general(dz[both], qx[heads], _DIMS["tn"], preferred_element_type=F32)
                    dv_acc[rows[b], cols] += lax.dot_general(ab[both], dox[heads], _DIMS["tn"], preferred_element_type=F32)
            return p, e, new_dq

        W, M = SB_SWEEP, SB_SWEEP // 2

        def wide(it, carry):
            p, e, dqs = sweep([W * it + b for b in range(W)], carry[0], carry[1], list(carry[2:]), False)
            return (p, e) + tuple(dqs)

        def mid(it, carry):
            p, e, dqs = sweep([i - i % W + b for b in range(M)], carry[0], carry[1], list(carry[2:]), False)
            return (p, e) + tuple(dqs)

        def one(it, carry):
            p, e, dqs = sweep([i - i % M + it], carry[0], carry[1], list(carry[2:]), False)
            return (p, e) + tuple(dqs)

        zero = jnp.zeros((H * B, 1), F32)
        carry = lax.fori_loop(0, i // W, wide, (zero, zero) + (jnp.zeros((B, LANES), F32),) * (H // 2))
        carry = lax.fori_loop(0, (i % W) // M, mid, carry)
        carry = lax.fori_loop(0, i % M, one, carry)
        _, _, dqs = sweep([i], carry[0], carry[1], list(carry[2:]), True)
        for pr in range(H // 2):
            dq_ref[:, pr * LANES:(pr + 1) * LANES] = (dqs[pr] * SB_SCALE).astype(dq_ref.dtype)

        @pl.when(i == nblk - 1)
        def _():
            pltpu.sync_copy(dk_acc, dk_hbm)
            pltpu.sync_copy(dv_acc, dv_hbm)

    hbm = pl.BlockSpec(memory_space=pl.ANY)
    acc = jax.ShapeDtypeStruct((L, SB_WIDTH), F32)
    return pl.pallas_call(
        body, name=name, grid=(nblk,),
        in_specs=[pl.BlockSpec((B, SB_WIDTH), lambda i: (i, P_SB // SB_WIDTH)), pl.BlockSpec((L, 2 * SB_WIDTH), lambda i: (0, 0)),
                  pl.BlockSpec((B, SB_WIDTH), lambda i: (i, Y_SB // SB_WIDTH)), pl.BlockSpec((B, LANES), lambda i: (i, 0))],
        out_specs=[pl.BlockSpec((B, SB_WIDTH), lambda i: (i, 0)), hbm, hbm],
        out_shape=[jax.ShapeDtypeStruct((L, SB_WIDTH), MXU_DTYPE), acc, acc],
        scratch_shapes=[pltpu.VMEM((L, SB_WIDTH), F32), pltpu.VMEM((L, SB_WIDTH), F32)],
        compiler_params=_params(("arbitrary",)),
    )(proj, kv, dy, ctot)


def _prep_w_in(w):
    sc, qkv, gz, gab, sb = w[:, 0:768], w[:, 768:2304], w[:, 2304:2816], w[:, 2816:2824], w[:, 2824:3592]
    pad = jnp.zeros((w.shape[0], P_END - D_IN_PROJ), w.dtype)
    return jnp.concatenate([qkv, gz, sc, sb, gab, pad], axis=1).astype(MXU_DTYPE)


def _unprep_dw_in(dw):
    qkv, gz, sc, sb, gab = dw[:, P_QKV:P_GZ], dw[:, P_GZ:P_SC], dw[:, P_SC:P_SB], dw[:, P_SB:P_GAB], dw[:, P_GAB:P_GAB + 8]
    return jnp.concatenate([sc, qkv, gz, gab, sb], axis=1)


def _prep_w_out(w):
    return jnp.concatenate([w[256:768], w[0:256], w[768:]], axis=0).astype(MXU_DTYPE)


def _unprep_dw_out(dw):
    return jnp.concatenate([dw[512:768], dw[0:512], dw[768:]], axis=0)


def _pad_lanes(v):
    return jnp.zeros((1, LANES), F32).at[0, :v.shape[0]].set(v)


def _layer_fwd(x, p, l):
    L = x.shape[0]
    tm = min(2048, L)
    n = f"l{l}_"
    h = _rms_fwd(x, p["norm_mix"], n + "rms_mix")
    proj = _matmul(h, p["w_in"], "nn", F32, n + "mm_in", tm, 768, 1024)
    cb = SC_WIDTH
    y_sc = _conv_fwd((proj, P_SC // cb + 1), p["w_sconv"], SC_KERNEL, cb, 1, MXU_DTYPE, n + "sconv",
                     x2=(proj, P_SC // cb + 2), gate=(proj, P_SC // cb))
    cqkv = _conv_fwd((proj, 0), p["w_gconv"], GDN_CONV, 256, 6, F32, n + "gconv")
    y_gdn, s_all, t_all = _gdn_fwd(cqkv, proj, p["a_log"], p["dt_bias"], p["gdn_norm"], n + "gdn")
    kv = proj[:, P_SB + SB_WIDTH:P_SB + 3 * SB_WIDTH].astype(MXU_DTYPE)
    y_sb, ctot = _sb_fwd(proj, kv, n + "sb")
    ycat = jnp.concatenate([y_gdn, y_sc, y_sb], axis=1)
    x1 = _matmul(ycat, p["w_out"], "nn", F32, n + "mm_out", tm, 512, 1024, resid=x)
    h2 = _rms_fwd(x1, p["norm_ffn"], n + "rms_ffn")
    up = _matmul(h2, p["w_up"], "nn", F32, n + "mm_up", tm, 512, 1024)
    act = _ffn_act_fwd(up, p["w_fconv"], n + "ffn_act")
    x2 = _matmul(act, p["w_down"], "nn", F32, n + "mm_down", tm, 512, 1408, resid=x1)
    saved = dict(x=x, h=h, proj=proj, cqkv=cqkv, s_all=s_all, t_all=t_all, kv=kv, ctot=ctot, ycat=ycat, x1=x1, h2=h2, up=up, act=act)
    return x2, saved


def _layer_bwd(dx2, dx2b, p, s, l):
    L = dx2.shape[0]
    tm, tkl = min(1024, L), min(1024, L)
    n = f"l{l}_"
    g = {}
    g["w_ffn_down"] = _matmul(s["act"], dx2b, "tn", F32, n + "mm_ddown", 1408, 1024, tkl)
    dact = _matmul(dx2b, p["w_down"], "nt", F32, n + "mm_dact", tm, 1408, 1024)
    dup, dwf = _ffn_act_bwd(s["up"], p["w_fconv"], dact, n + "dffn_act")
    g["w_ffn_conv"] = dwf[:FFN_CONV]
    g["w_ffn_up"] = _matmul(s["h2"], dup, "tn", F32, n + "mm_dup", 1024, 2816, tkl)
    dh2 = _matmul(dup, p["w_up"], "nt", F32, n + "mm_dh2", tm, 512, 2 * D_FF)
    dx1, dx1b, dwn = _rms_bwd(s["x1"], p["norm_ffn"], dh2, dx2, n + "drms_ffn")
    g["w_norm_ffn"] = dwn[0]

    g["w_mix_out"] = _unprep_dw_out(_matmul(s["ycat"], dx1b, "tn", F32, n + "mm_dout", 1024, 1024, tkl))
    dycat = _matmul(dx1b, p["w_out"], "nt", F32, n + "mm_dycat", tm, 512, 1024)
    proj = s["proj"]
    cb = SC_WIDTH
    dsc_c, dsc_h, dsc_b, dws = _conv_bwd((proj, P_SC // cb + 1), p["w_sconv"], (dycat, Y_SC // cb), SC_KERNEL, cb, 1,
                                         MXU_DTYPE, n + "dsconv", x2=(proj, P_SC // cb + 2), gate=(proj, P_SC // cb))
    g["w_sconv"] = dws[:SC_KERNEL]
    dcqkv, dgz, dgab, dal, ddb, dgn = _gdn_bwd(s["cqkv"], proj, p["a_log"], p["dt_bias"], p["gdn_norm"], s["s_all"], s["t_all"], dycat,
                                               n + "dgdn")
    g["gdn_a_log"], g["gdn_dt_bias"], g["w_gdn_norm"] = dal[0, :GDN_HEADS], ddb[0, :GDN_HEADS], dgn[0]
    dqkv, dwg = _conv_bwd((proj, 0), p["w_gconv"], (dcqkv, 0), GDN_CONV, 256, 6, MXU_DTYPE, n + "dgconv")
    g["w_gdn_conv"] = dwg[:GDN_CONV]
    dq, dk, dv = _sb_bwd(proj, s["kv"], dycat, s["ctot"], n + "dsb")
    dproj = jnp.concatenate(
        [dqkv, dgz, dsc_b, dsc_c, dsc_h, dq, dk.astype(MXU_DTYPE), dv.astype(MXU_DTYPE), dgab.astype(MXU_DTYPE),
         jnp.zeros((L, P_END - P_GAB - LANES), MXU_DTYPE)], axis=1)
    g["w_mix_in"] = _unprep_dw_in(_matmul(s["h"], dproj, "tn", F32, n + "mm_din", 1024, 1920, tkl))
    dh = _matmul(dproj, p["w_in"], "nt", F32, n + "mm_dh", tm, 512, P_END)
    dx, dxb, dwm = _rms_bwd(s["x"], p["norm_mix"], dh, dx1, n + "drms_mix")
    g["w_norm_mix"] = dwm[0]
    return dx, dxb, g


WEIGHTS = ["w_norm_mix", "w_mix_in", "w_sconv", "w_gdn_conv", "gdn_a_log", "gdn_dt_bias", "w_gdn_norm", "w_mix_out",
           "w_norm_ffn", "w_ffn_up", "w_ffn_conv", "w_ffn_down", "w_norm_final"]


def _local_step(x, w, target):
    layers = []
    for l in range(DEPTH):
        layers.append(dict(
            norm_mix=w["w_norm_mix"][l][None], w_in=_prep_w_in(w["w_mix_in"][l]), w_sconv=w["w_sconv"][l],
            w_gconv=w["w_gdn_conv"][l], a_log=_pad_lanes(w["gdn_a_log"][l]), dt_bias=_pad_lanes(w["gdn_dt_bias"][l]),
            gdn_norm=w["w_gdn_norm"][l][None], w_out=_prep_w_out(w["w_mix_out"][l]), norm_ffn=w["w_norm_ffn"][l][None],
            w_up=w["w_ffn_up"][l].astype(MXU_DTYPE), w_fconv=w["w_ffn_conv"][l], w_down=w["w_ffn_down"][l].astype(MXU_DTYPE)))
    saved = []
    for l in range(DEPTH):
        x, s = _layer_fwd(x, layers[l], l)
        saved.append(s)
    loss, dx, dxb, dwf = _loss_head(x, w["w_norm_final"][None], target, "loss_head")
    grads = [None] * DEPTH
    for l in reversed(range(DEPTH)):
        dx, dxb, grads[l] = _layer_bwd(dx, dxb, layers[l], saved[l], l)
    out = {k: jnp.stack([grads[l][k] for l in range(DEPTH)]) for k in WEIGHTS if k != "w_norm_final"}
    out["w_norm_final"] = dwf[0]
    return loss, dx, out


N_CHIPS = 4
_HBM = pl.BlockSpec(memory_space=pl.ANY)


def _other_chips(x, y):
    return [(1 - x, y), (x, 1 - y), (1 - x, 1 - y)]


def _remote(src, dst, send_sems, recv_sems, k, to):
    return pltpu.make_async_remote_copy(src_ref=src, dst_ref=dst, send_sem=send_sems.at[k], recv_sem=recv_sems.at[k],
                                        device_id=to, device_id_type=pl.DeviceIdType.MESH)


def _all_gather(bufs, name):
    nb = len(bufs)

    def body(*refs):
        ins, outs = refs[:nb], refs[nb:2 * nb]
        send_sems, recv_sems, local_sems = refs[2 * nb:]
        x, y, c = lax.axis_index("x"), lax.axis_index("y"), lax.axis_index("c")
        me, sibling, chips = (x, y, c), (x, y, 1 - c), _other_chips(x, y)

        def copy(b, k, block, to, src=None):
            slot = outs[b].at[4 * block[0] + 2 * block[1] + block[2]]
            return _remote(slot if src is None else src, slot, send_sems.at[b], recv_sems.at[b], k, to)

        local = [pltpu.make_async_copy(ins[b], outs[b].at[4 * x + 2 * y + c], local_sems.at[b]) for b in range(nb)]
        first = [copy(b, 0, me, sibling, src=ins[b]) for b in range(nb)]
        first += [copy(b, 1 + j, me, (*chip, c), src=ins[b]) for j, chip in enumerate(chips) for b in range(nb)]
        for cp in local + first:
            cp.start()
        passed = []
        for j, chip in enumerate(chips):
            for b in range(nb):
                copy(b, 1 + j, (*chip, c), me).wait_recv()
                passed.append(copy(b, 4 + j, (*chip, c), sibling))
                passed[-1].start()
        for b in range(nb):
            copy(b, 0, sibling, me).wait_recv()
        for j, chip in enumerate(chips):
            for b in range(nb):
                copy(b, 4 + j, (*chip, 1 - c), me).wait_recv()
        for cp in first + passed:
            cp.wait_send()
        for cp in local:
            cp.wait()

    return pl.pallas_call(
        body, name=name, in_specs=[_HBM] * nb, out_specs=[_HBM] * nb,
        out_shape=[jax.ShapeDtypeStruct((N_DEV,) + b.shape, b.dtype) for b in bufs],
        scratch_shapes=[pltpu.SemaphoreType.DMA((nb, N_DEV - 1)), pltpu.SemaphoreType.DMA((nb, N_DEV - 1)),
                        pltpu.SemaphoreType.DMA((nb,))],
    )(*bufs)


def _to_sibling(bufs, name):
    nb = len(bufs)

    def body(*refs):
        ins, outs = refs[:nb], refs[nb:2 * nb]
        send_sems, recv_sems = refs[2 * nb:]
        x, y, c = lax.axis_index("x"), lax.axis_index("y"), lax.axis_index("c")
        copies = [_remote(ins[b].at[1 - c], outs[b], send_sems, recv_sems, b, (x, y, 1 - c)) for b in range(nb)]
        for cp in copies:
            cp.start()
        for cp in copies:
            cp.wait()

    return pl.pallas_call(
        body, name=name, in_specs=[_HBM] * nb, out_specs=[_HBM] * nb,
        out_shape=[jax.ShapeDtypeStruct(b.shape[1:], b.dtype) for b in bufs],
        scratch_shapes=[pltpu.SemaphoreType.DMA((nb,)), pltpu.SemaphoreType.DMA((nb,))],
    )(*bufs)


def _to_chips(bufs, name):
    nb = len(bufs)

    def body(*refs):
        ins, outs = refs[:nb], refs[nb:2 * nb]
        send_sems, recv_sems, local_sems = refs[2 * nb:]
        x, y, c = lax.axis_index("x"), lax.axis_index("y"), lax.axis_index("c")
        here = 2 * x + y
        local = [pltpu.make_async_copy(ins[b].at[here], outs[b].at[here], local_sems.at[b]) for b in range(nb)]
        remote = [_remote(ins[b].at[2 * px + py], outs[b].at[here], send_sems.at[b], recv_sems.at[b], j, (px, py, c))
                  for j, (px, py) in enumerate(_other_chips(x, y)) for b in range(nb)]
        for cp in local + remote:
            cp.start()
        for cp in remote:
            cp.wait()
        for cp in local:
            cp.wait()

    return pl.pallas_call(
        body, name=name, in_specs=[_HBM] * nb, out_specs=[_HBM] * nb,
        out_shape=[jax.ShapeDtypeStruct(b.shape, b.dtype) for b in bufs],
        scratch_shapes=[pltpu.SemaphoreType.DMA((nb, N_CHIPS - 1)), pltpu.SemaphoreType.DMA((nb, N_CHIPS - 1)),
                        pltpu.SemaphoreType.DMA((nb,))],
    )(*bufs)


def _pair_sum(a, b, row_tile, name):
    n, D0, R, C = a.shape

    def body(a_ref, b_ref, o_ref):
        o_ref[...] = (a_ref[...].astype(F32) + b_ref[...].astype(F32)).astype(o_ref.dtype)

    blk = pl.BlockSpec((1, 1, row_tile, C), lambda s, l, i: (s, l, i, 0))
    return pl.pallas_call(
        body, name=name, grid=(n, D0, R // row_tile), in_specs=[blk, blk], out_specs=blk,
        out_shape=jax.ShapeDtypeStruct(a.shape, a.dtype), compiler_params=_params(("parallel", "parallel", "parallel")),
    )(a, b)


def _sum_sources(recv, row_tile, name):
    n, R, _ = recv.shape

    def body(r_ref, o_ref):
        acc = r_ref[0].astype(F32)
        for s in range(1, n):
            acc = acc + r_ref[s].astype(F32)
        o_ref[...] = acc

    return pl.pallas_call(
        body, name=name, grid=(R // row_tile,),
        in_specs=[pl.BlockSpec((n, row_tile, LANES), lambda i: (0, i, 0))],
        out_specs=pl.BlockSpec((row_tile, LANES), lambda i: (i, 0)),
        out_shape=jax.ShapeDtypeStruct((R, LANES), F32),
        compiler_params=_params(("parallel",)),
    )(recv)


def _adamw_math(g, w, m, v):
    nm = ADAM_B1 * m + (1.0 - ADAM_B1) * g
    nv = ADAM_B2 * v + (1.0 - ADAM_B2) * (g * g)
    m_hat = nm / (1.0 - ADAM_B1 ** ADAM_STEP)
    v_hat = nv / (1.0 - ADAM_B2 ** ADAM_STEP)
    return -ADAM_LR * (m_hat / (jnp.sqrt(v_hat) + ADAM_EPS) + ADAM_WD * w), nm, nv


def _sum_adamw(recv, w, m, v, row_tile, name):
    D0, R, C = w.shape
    n = recv.shape[0]

    def body(r_ref, w_ref, m_ref, v_ref, g_ref, d_ref, nm_ref, nv_ref):
        g = r_ref[0, 0].astype(F32)
        for s in range(1, n):
            g = g + r_ref[s, 0].astype(F32)
        g_ref[0] = g
        d_ref[0], nm_ref[0], nv_ref[0] = _adamw_math(g, w_ref[0], m_ref[0], v_ref[0])

    blk = pl.BlockSpec((1, row_tile, C), lambda l, i: (l, i, 0))
    out = jax.ShapeDtypeStruct((D0, R, C), F32)
    return pl.pallas_call(
        body, name=name, grid=(D0, R // row_tile),
        in_specs=[pl.BlockSpec((n, 1, row_tile, C), lambda l, i: (0, l, i, 0)), blk, blk, blk],
        out_specs=[blk] * 4, out_shape=[out] * 4,
        compiler_params=_params(("parallel", "parallel")),
    )(recv, w, m, v)


def _adamw(g, w, m, v, row_tile, name):
    R = g.shape[0]

    def body(g_ref, w_ref, m_ref, v_ref, d_ref, nm_ref, nv_ref):
        d_ref[...], nm_ref[...], nv_ref[...] = _adamw_math(g_ref[...], w_ref[...], m_ref[...], v_ref[...])

    blk = pl.BlockSpec((row_tile, LANES), lambda i: (i, 0))
    out = jax.ShapeDtypeStruct((R, LANES), F32)
    return pl.pallas_call(
        body, name=name, grid=(R // row_tile,), in_specs=[blk] * 4, out_specs=[blk] * 3, out_shape=[out] * 3,
        compiler_params=_params(("parallel",)),
    )(g, w, m, v)


def _pack(arrs, rows, dtype):
    flat = jnp.concatenate([a.reshape(-1).astype(dtype) for a in arrs])
    return jnp.pad(flat, (0, rows * LANES - flat.shape[0])).reshape(rows, LANES)


def _unpack(buf, shapes):
    lead = buf.shape[:-2]
    flat = buf.reshape(lead + (-1,))
    out, off = [], 0
    for shp in shapes:
        n = 1
        for d in shp:
            n *= d
        out.append(flat[..., off:off + n].reshape(lead + tuple(shp)))
        off += n
    return out


BIG = ["w_mix_in", "w_mix_out", "w_ffn_up", "w_ffn_down"]
BIG_AXIS = {"w_mix_in": 2, "w_mix_out": 1, "w_ffn_up": 2, "w_ffn_down": 1}
CONV = ["w_sconv", "w_gdn_conv", "w_ffn_conv"]
REPL = ["w_norm_mix", "gdn_a_log", "gdn_dt_bias", "w_gdn_norm", "w_norm_ffn", "w_norm_final"]
BIG_ROW_TILE = {"w_mix_in": 512, "w_mix_out": 128, "w_ffn_up": 512, "w_ffn_down": 352}
SMALL_ROWS = 416
CONV_ROWS = 48


def kernel(x, w_norm_mix, w_mix_in, w_sconv, w_gdn_conv, gdn_a_log, gdn_dt_bias, w_gdn_norm, w_mix_out, w_norm_ffn, w_ffn_up, w_ffn_conv, w_ffn_down, w_norm_final, loss_target, m_w_norm_mix, m_w_mix_in, m_w_sconv, m_w_gdn_conv, m_gdn_a_log, m_gdn_dt_bias, m_w_gdn_norm, m_w_mix_out, m_w_norm_ffn, m_w_ffn_up, m_w_ffn_conv, m_w_ffn_down, m_w_norm_final, v_w_norm_mix, v_w_mix_in, v_w_sconv, v_w_gdn_conv, v_gdn_a_log, v_gdn_dt_bias, v_w_gdn_norm, v_w_mix_out, v_w_norm_ffn, v_w_ffn_up, v_w_ffn_conv, v_w_ffn_down, v_w_norm_final):
    w = dict(w_norm_mix=w_norm_mix, w_mix_in=w_mix_in, w_sconv=w_sconv, w_gdn_conv=w_gdn_conv, gdn_a_log=gdn_a_log,
             gdn_dt_bias=gdn_dt_bias, w_gdn_norm=w_gdn_norm, w_mix_out=w_mix_out, w_norm_ffn=w_norm_ffn, w_ffn_up=w_ffn_up,
             w_ffn_conv=w_ffn_conv, w_ffn_down=w_ffn_down, w_norm_final=w_norm_final)
    m = dict(w_norm_mix=m_w_norm_mix, w_mix_in=m_w_mix_in, w_sconv=m_w_sconv, w_gdn_conv=m_w_gdn_conv, gdn_a_log=m_gdn_a_log,
             gdn_dt_bias=m_gdn_dt_bias, w_gdn_norm=m_w_gdn_norm, w_mix_out=m_w_mix_out, w_norm_ffn=m_w_norm_ffn,
             w_ffn_up=m_w_ffn_up, w_ffn_conv=m_w_ffn_conv, w_ffn_down=m_w_ffn_down, w_norm_final=m_w_norm_final)
    v = dict(w_norm_mix=v_w_norm_mix, w_mix_in=v_w_mix_in, w_sconv=v_w_sconv, w_gdn_conv=v_w_gdn_conv, gdn_a_log=v_gdn_a_log,
             gdn_dt_bias=v_gdn_dt_bias, w_gdn_norm=v_w_gdn_norm, w_mix_out=v_w_mix_out, w_norm_ffn=v_w_norm_ffn,
             w_ffn_up=v_w_ffn_up, w_ffn_conv=v_w_ffn_conv, w_ffn_down=v_w_ffn_down, w_norm_final=v_w_norm_final)
    me = 4 * lax.axis_index("x") + 2 * lax.axis_index("y") + lax.axis_index("c")
    conv_shapes = [w[k].shape for k in CONV]

    gathered = _all_gather([w[k].astype(MXU_DTYPE) for k in BIG] + [_pack([w[k] for k in CONV], CONV_ROWS, F32)],
                           "gather_weights")
    full = dict(w)
    for k, got in zip(BIG, gathered):
        full[k] = jnp.concatenate([got[s] for s in range(N_DEV)], axis=BIG_AXIS[k])
    for k, got in zip(CONV, _unpack(gathered[-1], conv_shapes)):
        full[k] = jnp.concatenate([got[s] for s in range(N_DEV)], axis=2)

    loss, dx, grads = _local_step(x[0], full, loss_target[0])

    small = CONV + REPL
    core = lax.axis_index("c")
    by_core = []
    for k in BIG:
        piece = jnp.split(grads[k], N_DEV, axis=BIG_AXIS[k])
        by_core.append(jnp.stack([jnp.stack([piece[2 * ch + p] for ch in range(N_CHIPS)]) for p in range(2)]).astype(MXU_DTYPE))
    from_sibling = _to_sibling(by_core, "grads_to_sibling")
    chip_sums = [_pair_sum(lax.dynamic_index_in_dim(mine, core, 0, keepdims=False), theirs, BIG_ROW_TILE[k], "pair_sum_" + k)
                 for k, mine, theirs in zip(BIG, by_core, from_sibling)]
    received = _to_chips(chip_sums, "grads_to_chips")
    g, delta, new_m, new_v = {}, {}, {}, {}
    for k, got in zip(BIG, received):
        g[k], delta[k], new_m[k], new_v[k] = _sum_adamw(got, w[k], m[k], v[k], BIG_ROW_TILE[k], "adamw_" + k)
    small_parts = _all_gather([_pack([grads[k] for k in small], SMALL_ROWS, F32)], "gather_small_grads")[0]
    g_small = _unpack(_sum_sources(small_parts, SMALL_ROWS, "sum_small"), [grads[k].shape for k in small])
    for k, gs in zip(small, g_small):
        g[k] = lax.dynamic_slice_in_dim(gs, me * w[k].shape[2], w[k].shape[2], axis=2) if k in CONV else gs

    small_shapes = [w[k].shape for k in small]
    small_rows = -(-sum(w[k].size for k in small) // (SUBLANES * LANES)) * SUBLANES
    d_sm, m_sm, v_sm = _adamw(_pack([g[k] for k in small], small_rows, F32), _pack([w[k] for k in small], small_rows, F32),
                              _pack([m[k] for k in small], small_rows, F32), _pack([v[k] for k in small], small_rows, F32),
                              small_rows, "adamw_small")
    for dst, small_buf in ((delta, d_sm), (new_m, m_sm), (new_v, v_sm)):
        dst.update(zip(small, _unpack(small_buf, small_shapes)))

    loss_all = lax.psum(loss[0, 0], ("x", "y", "c"))
    return (loss_all, dx[None], *[g[k] for k in WEIGHTS], *[delta[k] for k in WEIGHTS], *[new_m[k] for k in WEIGHTS],
            *[new_v[k] for k in WEIGHTS])
```

```python
import functools

import jax
import jax.numpy as jnp
from jax import lax
from jax.experimental import pallas as pl
from jax.experimental.pallas import tpu as pltpu

F32 = jnp.float32
MXU_DTYPE = jnp.bfloat16
HIGHEST = lax.Precision.HIGHEST

D_MODEL = 1024
DEPTH = 2
SC_WIDTH = 256
SC_KERNEL = 3
GDN_WIDTH = 512
GDN_HEADS = 4
GDN_HEAD_DIM = 128
GDN_CONV = 4
GDN_CHUNK = 64
SB_WIDTH = 256
SB_HEADS = 4
SB_HEAD_DIM = 64
SB_BLOCK = 128
SB_SWEEP = 8
D_FF = 2816
FFN_CONV = 3
NORM_EPS = 1e-6
D_IN_PROJ = 3592
ADAM_LR, ADAM_B1, ADAM_B2, ADAM_EPS, ADAM_WD, ADAM_STEP = 0.001, 0.9, 0.999, 1e-08, 0.01, 10

N_DEV = 8
LANES = 128
SUBLANES = 8
VMEM_LIMIT = 48 * 1024 * 1024

P_QKV, P_GZ, P_SC, P_SB, P_GAB, P_END = 0, 1536, 2048, 2816, 3584, 3840
Y_GDN, Y_SC, Y_SB = 0, 512, 768


def _params(semantics):
    return pltpu.CompilerParams(dimension_semantics=semantics, vmem_limit_bytes=VMEM_LIMIT)


_DIMS = {"nn": (((1,), (0,)), ((), ())), "nt": (((1,), (1,)), ((), ())), "tn": (((0,), (0,)), ((), ()))}


def _matmul(a, b, mode, out_dtype, name, tm, tn, tk, resid=None):
    if mode == "tn":
        (K, M), (K2, N) = a.shape, b.shape
    elif mode == "nt":
        (M, K), (N, K2) = a.shape, b.shape
    else:
        (M, K), (K2, N) = a.shape, b.shape
    assert K == K2 and M % tm == 0 and N % tn == 0 and K % tk == 0, (name, a.shape, b.shape, tm, tn, tk)
    nk = K // tk
    has_resid = resid is not None
    assert nk == 1 or out_dtype == F32, name

    def body(*refs):
        if has_resid:
            a_ref, b_ref, r_ref, o_ref = refs
        else:
            a_ref, b_ref, o_ref = refs
        k = pl.program_id(2)
        part = lax.dot_general(a_ref[...], b_ref[...], _DIMS[mode], preferred_element_type=F32)
        if nk == 1:
            o_ref[...] = ((part + r_ref[...]) if has_resid else part).astype(out_dtype)
        else:
            @pl.when(k == 0)
            def _():
                o_ref[...] = (part + r_ref[...]) if has_resid else part

            @pl.when(k > 0)
            def _():
                o_ref[...] += part

    a_spec = pl.BlockSpec((tk, tm), lambda i, j, k: (k, i)) if mode == "tn" else pl.BlockSpec((tm, tk), lambda i, j, k: (i, k))
    b_spec = pl.BlockSpec((tn, tk), lambda i, j, k: (j, k)) if mode == "nt" else pl.BlockSpec((tk, tn), lambda i, j, k: (k, j))
    o_spec = pl.BlockSpec((tm, tn), lambda i, j, k: (i, j))
    in_specs = [a_spec, b_spec] + ([o_spec] if has_resid else [])
    args = (a, b) + ((resid,) if has_resid else ())
    return pl.pallas_call(
        body, name=name, grid=(M // tm, N // tn, nk), in_specs=in_specs, out_specs=o_spec,
        out_shape=jax.ShapeDtypeStruct((M, N), out_dtype),
        compiler_params=_params(("parallel", "parallel", "arbitrary")),
    )(*args)


def _rms(x, w):
    return x * lax.rsqrt(jnp.mean(x * x, axis=-1, keepdims=True) + NORM_EPS) * w


ROW_TILE = 512


def _rms_fwd(x, w, name):
    L, Dm = x.shape

    def body(x_ref, w_ref, h_ref):
        h_ref[...] = _rms(x_ref[...], w_ref[...]).astype(h_ref.dtype)

    return pl.pallas_call(
        body, name=name, grid=(L // ROW_TILE,),
        in_specs=[pl.BlockSpec((ROW_TILE, Dm), lambda i: (i, 0)), pl.BlockSpec((1, Dm), lambda i: (0, 0))],
        out_specs=pl.BlockSpec((ROW_TILE, Dm), lambda i: (i, 0)),
        out_shape=jax.ShapeDtypeStruct((L, Dm), MXU_DTYPE),
        compiler_params=_params(("parallel",)),
    )(x, w)


def _rms_bwd(x, w, dh, dres, name):
    L, Dm = x.shape

    def body(x_ref, w_ref, dh_ref, dres_ref, dx_ref, dxb_ref, dw_ref):
        _, vjp = jax.vjp(_rms, x_ref[...], w_ref[...])
        dx, dw = vjp(dh_ref[...])
        dx = dres_ref[...] + dx
        dx_ref[...] = dx
        dxb_ref[...] = dx.astype(dxb_ref.dtype)

        @pl.when(pl.program_id(0) == 0)
        def _():
            dw_ref[...] = jnp.zeros_like(dw_ref)

        dw_ref[...] += dw

    row = pl.BlockSpec((ROW_TILE, Dm), lambda i: (i, 0))
    vec = pl.BlockSpec((1, Dm), lambda i: (0, 0))
    return pl.pallas_call(
        body, name=name, grid=(L // ROW_TILE,), in_specs=[row, vec, row, row], out_specs=[row, row, vec],
        out_shape=[jax.ShapeDtypeStruct((L, Dm), F32), jax.ShapeDtypeStruct((L, Dm), MXU_DTYPE), jax.ShapeDtypeStruct((1, Dm), F32)],
        compiler_params=_params(("arbitrary",)),
    )(x, w, dh, dres)


def _loss_head(x, w, target, name):
    L, Dm = x.shape

    def block_loss(xb, wb, tb):
        err = _rms(xb, wb) - tb
        return 0.5 * jnp.sum(jnp.sum(err * err, axis=-1, keepdims=True) * (1.0 / Dm), axis=0, keepdims=True)

    def body(x_ref, w_ref, t_ref, loss_ref, dx_ref, dxb_ref, dw_ref):
        val, vjp = jax.vjp(lambda xb, wb: block_loss(xb, wb, t_ref[...]), x_ref[...], w_ref[...])
        dx, dw = vjp(jnp.ones_like(val))
        dx_ref[...] = dx
        dxb_ref[...] = dx.astype(dxb_ref.dtype)

        @pl.when(pl.program_id(0) == 0)
        def _():
            dw_ref[...] = jnp.zeros_like(dw_ref)
            loss_ref[...] = jnp.zeros_like(loss_ref)

        dw_ref[...] += dw
        loss_ref[...] += val

    row = pl.BlockSpec((ROW_TILE, Dm), lambda i: (i, 0))
    vec = pl.BlockSpec((1, Dm), lambda i: (0, 0))
    one = pl.BlockSpec((1, 1), lambda i: (0, 0))
    return pl.pallas_call(
        body, name=name, grid=(L // ROW_TILE,), in_specs=[row, vec, row], out_specs=[one, row, row, vec],
        out_shape=[jax.ShapeDtypeStruct((1, 1), F32), jax.ShapeDtypeStruct((L, Dm), F32), jax.ShapeDtypeStruct((L, Dm), MXU_DTYPE),
                   jax.ShapeDtypeStruct((1, Dm), F32)],
        compiler_params=_params(("arbitrary",)),
    )(x, w, target)


HALO = SUBLANES
CONV_ROW_TILE = 1024


def _conv_specs(L, T, Cb, off):
    main = pl.BlockSpec((T, Cb), lambda j, i: (i, off + j))
    prev = pl.BlockSpec((HALO, Cb), lambda j, i: (jnp.maximum(i * (T // HALO) - 1, 0), off + j))
    nxt = pl.BlockSpec((HALO, Cb), lambda j, i: (jnp.minimum((i + 1) * (T // HALO), L // HALO - 1), off + j))
    return main, prev, nxt


def _conv_fwd(x1, w, K, Cb, ncol, out_dtype, name, x2=None, gate=None):
    (x1a, o1) = x1
    L = x1a.shape[0]
    T = min(CONV_ROW_TILE, L)
    has_mul, has_gate = x2 is not None, gate is not None

    def body(*refs):
        it = iter(refs)
        x1m, x1p = next(it), next(it)
        if has_mul:
            x2m, x2p = next(it), next(it)
        if has_gate:
            gm = next(it)
        w_ref, y_ref, scr = next(it), next(it), next(it)
        i = pl.program_id(1)
        p, pp = x1m[...].astype(F32), x1p[...].astype(F32)
        if has_mul:
            p, pp = p * x2m[...], pp * x2p[...]
        scr[0:HALO, :] = jnp.where(i > 0, pp, 0.0)
        scr[HALO:HALO + T, :] = p
        acc = w_ref[K - 1:K, :] * p
        for k in range(K - 1):
            s = K - 1 - k
            acc = acc + w_ref[k:k + 1, :] * scr[HALO - s:HALO - s + T, :]
        if has_gate:
            acc = acc * gm[...]
        y_ref[...] = acc.astype(out_dtype)

    in_specs, args = [], []
    m, p_, _ = _conv_specs(L, T, Cb, o1)
    in_specs += [m, p_]
    args += [x1a, x1a]
    if has_mul:
        m, p_, _ = _conv_specs(L, T, Cb, x2[1])
        in_specs += [m, p_]
        args += [x2[0], x2[0]]
    if has_gate:
        m, _, _ = _conv_specs(L, T, Cb, gate[1])
        in_specs += [m]
        args += [gate[0]]
    in_specs.append(pl.BlockSpec((K, Cb), lambda j, i: (0, j)))
    args.append(w)
    return pl.pallas_call(
        body, name=name, grid=(ncol, L // T), in_specs=in_specs,
        out_specs=pl.BlockSpec((T, Cb), lambda j, i: (i, j)),
        out_shape=jax.ShapeDtypeStruct((L, ncol * Cb), out_dtype),
        scratch_shapes=[pltpu.VMEM((T + HALO, Cb), F32)],
        compiler_params=_params(("parallel", "arbitrary")),
    )(*args)


def _conv_bwd(x1, w, dy, K, Cb, ncol, out_dtype, name, x2=None, gate=None):
    (x1a, o1) = x1
    L = x1a.shape[0]
    T = min(CONV_ROW_TILE, L)
    nrow = L // T
    has_mul, has_gate = x2 is not None, gate is not None

    def body(*refs):
        it = iter(refs)
        x1m, x1p = next(it), next(it)
        if has_mul:
            x2m, x2p = next(it), next(it)
        if has_gate:
            gm, gn = next(it), next(it)
        dym, dyn, w_ref = next(it), next(it), next(it)
        dx1_ref = next(it)
        if has_mul:
            dx2_ref = next(it)
        if has_gate:
            dg_ref = next(it)
        dw_ref, scr_p, scr_d = next(it), next(it), next(it)
        i = pl.program_id(1)
        p, pp = x1m[...].astype(F32), x1p[...].astype(F32)
        if has_mul:
            p, pp = p * x2m[...], pp * x2p[...]
        scr_p[0:HALO, :] = jnp.where(i > 0, pp, 0.0)
        scr_p[HALO:HALO + T, :] = p
        dcv, dcn = dym[...].astype(F32), dyn[...].astype(F32)
        if has_gate:
            dcv, dcn = dcv * gm[...], dcn * gn[...]
        scr_d[0:T, :] = dcv
        scr_d[T:T + HALO, :] = jnp.where(i < nrow - 1, dcn, 0.0)

        @pl.when(i == 0)
        def _():
            dw_ref[...] = jnp.zeros_like(dw_ref)

        dp = w_ref[K - 1:K, :] * dcv
        cv = w_ref[K - 1:K, :] * p
        dw_ref[K - 1:K, :] += jnp.sum(dcv * p, axis=0, keepdims=True)
        for k in range(K - 1):
            s = K - 1 - k
            dp = dp + w_ref[k:k + 1, :] * scr_d[s:s + T, :]
            sh = scr_p[HALO - s:HALO - s + T, :]
            dw_ref[k:k + 1, :] += jnp.sum(dcv * sh, axis=0, keepdims=True)
            if has_gate:
                cv = cv + w_ref[k:k + 1, :] * sh
        if has_gate:
            dg_ref[...] = (dym[...].astype(F32) * cv).astype(out_dtype)
        if has_mul:
            dx1_ref[...] = (dp * x2m[...]).astype(out_dtype)
            dx2_ref[...] = (dp * x1m[...]).astype(out_dtype)
        else:
            dx1_ref[...] = dp.astype(out_dtype)

    in_specs, args = [], []
    m, p_, _ = _conv_specs(L, T, Cb, o1)
    in_specs += [m, p_]
    args += [x1a, x1a]
    if has_mul:
        m, p_, _ = _conv_specs(L, T, Cb, x2[1])
        in_specs += [m, p_]
        args += [x2[0], x2[0]]
    if has_gate:
        m, _, n_ = _conv_specs(L, T, Cb, gate[1])
        in_specs += [m, n_]
        args += [gate[0], gate[0]]
    m, _, n_ = _conv_specs(L, T, Cb, dy[1])
    in_specs += [m, n_, pl.BlockSpec((K, Cb), lambda j, i: (0, j))]
    args += [dy[0], dy[0], w]
    out = pl.BlockSpec((T, Cb), lambda j, i: (i, j))
    full = jax.ShapeDtypeStruct((L, ncol * Cb), out_dtype)
    n_out = 1 + int(has_mul) + int(has_gate)
    return pl.pallas_call(
        body, name=name, grid=(ncol, nrow), in_specs=in_specs,
        out_specs=[out] * n_out + [pl.BlockSpec((SUBLANES, Cb), lambda j, i: (0, j))],
        out_shape=[full] * n_out + [jax.ShapeDtypeStruct((SUBLANES, ncol * Cb), F32)],
        scratch_shapes=[pltpu.VMEM((T + HALO, Cb), F32), pltpu.VMEM((T + HALO, Cb), F32)],
        compiler_params=_params(("parallel", "arbitrary")),
    )(*args)


GLU_COLS = 256


def _silu(x):
    return x * (1.0 / (1.0 + jnp.exp(-x)))


def _glu(g, v):
    return _silu(g) * v


def _glu_grads(g, v, da):
    sig = 1.0 / (1.0 + jnp.exp(-g))
    gs = g * sig
    return da * v * (sig + gs * (1.0 - sig)), da * gs


def _causal_taps(w_ref, scr, first, rows, K):
    acc = w_ref[K - 1:K, :] * scr[first:first + rows, :]
    for k in range(K - 1):
        s = K - 1 - k
        acc = acc + w_ref[k:k + 1, :] * scr[first - s:first - s + rows, :]
    return acc


def _ffn_act_fwd(up, w, name):
    L = up.shape[0]
    T, Cb, K = min(CONV_ROW_TILE, L), GLU_COLS, FFN_CONV
    nb = D_FF // Cb

    def body(gm, gp, vm, vp, wg, wv, a_ref, sg, sv):
        i = pl.program_id(1)
        for main, prev, scr in ((gm, gp, sg), (vm, vp, sv)):
            scr[0:HALO, :] = jnp.where(i > 0, prev[...], 0.0)
            scr[HALO:HALO + T, :] = main[...]
        a_ref[...] = _glu(_causal_taps(wg, sg, HALO, T, K), _causal_taps(wv, sv, HALO, T, K)).astype(a_ref.dtype)

    gmain, gprev, _ = _conv_specs(L, T, Cb, 0)
    vmain, vprev, _ = _conv_specs(L, T, Cb, nb)
    return pl.pallas_call(
        body, name=name, grid=(nb, L // T),
        in_specs=[gmain, gprev, vmain, vprev, pl.BlockSpec((K, Cb), lambda j, i: (0, j)), pl.BlockSpec((K, Cb), lambda j, i: (0, nb + j))],
        out_specs=pl.BlockSpec((T, Cb), lambda j, i: (i, j)),
        out_shape=jax.ShapeDtypeStruct((L, D_FF), MXU_DTYPE),
        scratch_shapes=[pltpu.VMEM((T + HALO, Cb), F32), pltpu.VMEM((T + HALO, Cb), F32)],
        compiler_params=_params(("parallel", "arbitrary")),
    )(up, up, up, up, w, w)


def _ffn_act_bwd(up, w, dact, name):
    L = up.shape[0]
    T, Cb, K = min(CONV_ROW_TILE, L), GLU_COLS, FFN_CONV
    nb, nrow = D_FF // Cb, L // T

    def body(gm, gp, gn, vm, vp, vn, dam, dan, wg, wv, dg_ref, dv_ref, dwg_ref, dwv_ref, sg, sv, sdg, sdv):
        i = pl.program_id(1)
        for main, prev, nxt, scr in ((gm, gp, gn, sg), (vm, vp, vn, sv)):
            scr[0:HALO, :] = jnp.where(i > 0, prev[...], 0.0)
            scr[HALO:HALO + T, :] = main[...]
            scr[HALO + T:2 * HALO + T, :] = nxt[...]
        ug, uv = _causal_taps(wg, sg, HALO, T + HALO, K), _causal_taps(wv, sv, HALO, T + HALO, K)
        da = jnp.concatenate([dam[...], jnp.where(i < nrow - 1, dan[...], 0.0)], axis=0)
        sdg[...], sdv[...] = _glu_grads(ug, uv, da)

        @pl.when(i == 0)
        def _():
            dwg_ref[...] = jnp.zeros_like(dwg_ref)
            dwv_ref[...] = jnp.zeros_like(dwv_ref)

        for w_ref, scr, sd, d_ref, dw_ref in ((wg, sg, sdg, dg_ref, dwg_ref), (wv, sv, sdv, dv_ref, dwv_ref)):
            du = sd[0:T, :]
            dp = w_ref[K - 1:K, :] * du
            dw_ref[K - 1:K, :] += jnp.sum(du * scr[HALO:HALO + T, :], axis=0, keepdims=True)
            for k in range(K - 1):
                s = K - 1 - k
                dp = dp + w_ref[k:k + 1, :] * sd[s:s + T, :]
                dw_ref[k:k + 1, :] += jnp.sum(du * scr[HALO - s:HALO - s + T, :], axis=0, keepdims=True)
            d_ref[...] = dp.astype(d_ref.dtype)

    gmain, gprev, gnext = _conv_specs(L, T, Cb, 0)
    vmain, vprev, vnext = _conv_specs(L, T, Cb, nb)
    dmain, _, dnext = _conv_specs(L, T, Cb, 0)
    out = pl.BlockSpec((T, Cb), lambda j, i: (i, j))
    dwb = pl.BlockSpec((SUBLANES, Cb), lambda j, i: (0, j))
    half = jax.ShapeDtypeStruct((L, D_FF), MXU_DTYPE)
    dwh = jax.ShapeDtypeStruct((SUBLANES, D_FF), F32)
    dg, dv, dwg, dwv = pl.pallas_call(
        body, name=name, grid=(nb, nrow),
        in_specs=[gmain, gprev, gnext, vmain, vprev, vnext, dmain, dnext,
                  pl.BlockSpec((K, Cb), lambda j, i: (0, j)), pl.BlockSpec((K, Cb), lambda j, i: (0, nb + j))],
        out_specs=[out, out, dwb, dwb], out_shape=[half, half, dwh, dwh],
        scratch_shapes=[pltpu.VMEM((T + 2 * HALO, Cb), F32), pltpu.VMEM((T + 2 * HALO, Cb), F32),
                        pltpu.VMEM((T + HALO, Cb), F32), pltpu.VMEM((T + HALO, Cb), F32)],
        compiler_params=_params(("parallel", "arbitrary")),
    )(up, up, up, up, up, up, dact, dact, w, w)
    return jnp.concatenate([dg, dv], axis=1), jnp.concatenate([dwg, dwv], axis=1)


def _bdot_raw(a, b, mode):
    return lax.dot_general(a.astype(MXU_DTYPE), b.astype(MXU_DTYPE), _DIMS[mode], preferred_element_type=F32)


@functools.partial(jax.custom_vjp, nondiff_argnums=(2,))
def _bdot(a, b, mode):
    return _bdot_raw(a, b, mode)


def _bdot_fwd(a, b, mode):
    return _bdot_raw(a, b, mode), (a, b)


def _bdot_bwd(mode, res, ct):
    a, b = res
    if mode == "nn":
        return _bdot_raw(ct, b, "nt"), _bdot_raw(a, ct, "tn")
    if mode == "nt":
        return _bdot_raw(ct, b, "nn"), _bdot_raw(ct, a, "tn")
    return _bdot_raw(b, ct, "nt"), _bdot_raw(a, ct, "nn")


_bdot.defvjp(_bdot_fwd, _bdot_bwd)


def _hdot(a, b, mode="nn"):
    return lax.dot_general(a, b, _DIMS[mode], precision=lax.Precision.HIGH, preferred_element_type=F32)


@jax.custom_vjp
def _inv_unit_lower(a):
    R = a.shape[0]
    eye = (lax.broadcasted_iota(jnp.int32, (R, R), 0) == lax.broadcasted_iota(jnp.int32, (R, R), 1)).astype(F32)
    t = eye - a
    p = a
    n = 1
    while 2 * n < GDN_CHUNK:
        p = _hdot(p, p)
        t = t + _hdot(t, p)
        n *= 2
    return t


def _inv_fwd(a):
    t = _inv_unit_lower(a)
    return t, t


def _inv_bwd(t, ct):
    return (-_hdot(_hdot(t, ct, "tn"), t, "nt"),)


_inv_unit_lower.defvjp(_inv_fwd, _inv_bwd)


@jax.custom_vjp
def _inv_saved(a, t):
    return t


def _inv_saved_fwd(a, t):
    return t, t


def _inv_saved_bwd(t, ct):
    return _inv_bwd(t, ct)[0], jnp.zeros_like(t)


_inv_saved.defvjp(_inv_saved_fwd, _inv_saved_bwd)


def _softplus(x):
    return jnp.maximum(x, 0.0) + jnp.log(1.0 + jnp.exp(-jnp.abs(x)))


def _sigmoid(x):
    return 1.0 / (1.0 + jnp.exp(-x))


def _pick_lane(blk, lane):
    ids = lax.broadcasted_iota(jnp.int32, blk.shape, 1)
    return jnp.sum(jnp.where(ids == lane, blk, 0.0), axis=1, keepdims=True)


def _gdn_chunk(cq, ck, cv, gz, gab, alog, dtb, wn, S, t_saved=None):
    C, H, Dk = GDN_CHUNK, GDN_HEADS, GDN_HEAD_DIM
    R = H * C
    rows_of = lambda vals, n: jnp.concatenate([jnp.broadcast_to(x, (n, 1)) for x in vals], axis=0)
    ga = jnp.concatenate([_pick_lane(gab, h) for h in range(H)], axis=0)
    gb = jnp.concatenate([_pick_lane(gab, H + h) for h in range(H)], axis=0)
    al = rows_of([_pick_lane(alog, h) for h in range(H)], C)
    db = rows_of([_pick_lane(dtb, h) for h in range(H)], C)
    q, k, v = _silu(cq), _silu(ck), _silu(cv)
    q = q * lax.rsqrt(jnp.sum(q * q, axis=-1, keepdims=True) + NORM_EPS) * (Dk ** -0.5)
    k = k * lax.rsqrt(jnp.sum(k * k, axis=-1, keepdims=True) + NORM_EPS)
    beta = _sigmoid(gb)
    g = -jnp.exp(al) * _softplus(ga + db)
    row = lax.broadcasted_iota(jnp.int32, (R, R), 0)
    col = lax.broadcasted_iota(jnp.int32, (R, R), 1)
    same_head = (row // C) == (col // C)
    causal, strict = same_head & (row >= col), same_head & (row > col)
    gcb = _hdot(causal.astype(F32), jnp.broadcast_to(g, (R, Dk)))
    first = (lax.broadcasted_iota(jnp.int32, (R, Dk), 1) == 0).astype(F32)
    gr = _hdot(first, gcb, "nt")
    gc = _pick_lane(gcb, 0)
    decay = jnp.where(causal, jnp.exp(jnp.where(causal, gc - gr, 0.0)), 0.0)
    kb = k * beta
    lower = jnp.where(strict, _bdot(kb, k, "nt") * decay, 0.0)
    t = _inv_unit_lower(lower) if t_saved is None else _inv_saved(lower, t_saved)
    egc = jnp.exp(gc)
    u = _hdot(t, v * beta)
    w = _hdot(t, kb * egc)
    attn = jnp.where(causal, _bdot(q, k, "nt") * decay, 0.0)
    own = (lax.broadcasted_iota(jnp.int32, (R, H * Dk), 0) // C) == (lax.broadcasted_iota(jnp.int32, (R, H * Dk), 1) // Dk)
    spread = lambda x: jnp.where(own, jnp.concatenate([x] * H, axis=1), 0.0)
    v_new = u - _bdot(spread(w), S, "nn")
    o = _bdot(spread(q * egc), S, "nn") + _bdot(attn, v_new, "nn")
    last = lax.broadcasted_iota(jnp.int32, (R, 1), 0)
    g_last = [jnp.sum(jnp.where(last == h * C + C - 1, gc, 0.0), axis=0, keepdims=True) for h in range(H)]
    S_new = S * jnp.exp(rows_of(g_last, Dk)) + _bdot(spread(k * jnp.exp(rows_of(g_last, C) - gc)), v_new, "tn")
    y = o * lax.rsqrt(jnp.mean(o * o, axis=-1, keepdims=True) + NORM_EPS) * wn * _silu(gz)
    return y, S_new, t


def _stack_heads(ref, first, width=GDN_HEAD_DIM):
    return jnp.concatenate([ref[:, first + h * width:first + (h + 1) * width] for h in range(GDN_HEADS)], axis=0)


def _gdn_fwd(cqkv, proj, alog, dtb, wn, name):
    L = cqkv.shape[0]
    C, H, Dh = GDN_CHUNK, GDN_HEADS, GDN_HEAD_DIM
    N = L // C

    def body(c_ref, gz_ref, gab_ref, al_ref, db_ref, wn_ref, y_ref, sall_ref, tall_ref, S):
        n = pl.program_id(0)

        @pl.when(n == 0)
        def _():
            S[...] = jnp.zeros_like(S)

        s_in = S[...]
        sall_ref[0] = s_in
        y, s_new, t = _gdn_chunk(_stack_heads(c_ref, 0), _stack_heads(c_ref, GDN_WIDTH), _stack_heads(c_ref, 2 * GDN_WIDTH),
                                 _stack_heads(gz_ref, 0), gab_ref[...], al_ref[...], db_ref[...], wn_ref[...], s_in)
        for h in range(H):
            y_ref[:, h * Dh:(h + 1) * Dh] = y[h * C:(h + 1) * C].astype(y_ref.dtype)
        S[...] = s_new
        tall_ref[0] = t

    vec = pl.BlockSpec((1, LANES), lambda n: (0, 0))
    return pl.pallas_call(
        body, name=name, grid=(N,),
        in_specs=[pl.BlockSpec((C, 3 * GDN_WIDTH), lambda n: (n, 0)),
                  pl.BlockSpec((C, GDN_WIDTH), lambda n: (n, P_GZ // GDN_WIDTH)),
                  pl.BlockSpec((C, LANES), lambda n: (n, P_GAB // LANES)), vec, vec, vec],
        out_specs=[pl.BlockSpec((C, GDN_WIDTH), lambda n: (n, 0)), pl.BlockSpec((1, H * Dh, Dh), lambda n: (n, 0, 0)),
                   pl.BlockSpec((1, H * C, H * C), lambda n: (n, 0, 0))],
        out_shape=[jax.ShapeDtypeStruct((L, GDN_WIDTH), MXU_DTYPE), jax.ShapeDtypeStruct((N, H * Dh, Dh), F32),
                   jax.ShapeDtypeStruct((N, H * C, H * C), F32)],
        scratch_shapes=[pltpu.VMEM((H * Dh, Dh), F32)],
        compiler_params=_params(("arbitrary",)),
    )(cqkv, proj, proj, alog, dtb, wn)


def _gdn_bwd(cqkv, proj, alog, dtb, wn, s_all, t_all, dy, name):
    L = cqkv.shape[0]
    C, H, Dh = GDN_CHUNK, GDN_HEADS, GDN_HEAD_DIM
    N = L // C

    def body(c_ref, gz_ref, gab_ref, al_ref, db_ref, wn_ref, sall_ref, tall_ref, dy_ref,
             dc_ref, dgz_ref, dgab_ref, dal_ref, ddb_ref, dwn_ref, dS):
        n = pl.program_id(0)

        @pl.when(n == 0)
        def _():
            dS[...] = jnp.zeros_like(dS)
            dal_ref[...] = jnp.zeros_like(dal_ref)
            ddb_ref[...] = jnp.zeros_like(ddb_ref)
            dwn_ref[...] = jnp.zeros_like(dwn_ref)

        t_saved = tall_ref[0]
        chunk = lambda *a: _gdn_chunk(*a, t_saved=t_saved)[:2]
        _, vjp = jax.vjp(chunk, _stack_heads(c_ref, 0), _stack_heads(c_ref, GDN_WIDTH), _stack_heads(c_ref, 2 * GDN_WIDTH),
                         _stack_heads(gz_ref, 0), gab_ref[...], al_ref[...], db_ref[...], wn_ref[...], sall_ref[0])
        dq, dk, dv, dgz, dgab, dal, ddb, dwn, ds = vjp((_stack_heads(dy_ref, 0), dS[...]))
        for h in range(H):
            rows = slice(h * C, (h + 1) * C)
            dc_ref[:, h * Dh:(h + 1) * Dh] = dq[rows]
            dc_ref[:, (H + h) * Dh:(H + h + 1) * Dh] = dk[rows]
            dc_ref[:, (2 * H + h) * Dh:(2 * H + h + 1) * Dh] = dv[rows]
            dgz_ref[:, h * Dh:(h + 1) * Dh] = dgz[rows].astype(dgz_ref.dtype)
        dS[...] = ds
        dgab_ref[...] = dgab
        dal_ref[...] += dal
        ddb_ref[...] += ddb
        dwn_ref[...] += dwn

    vec = pl.BlockSpec((1, LANES), lambda n: (0, 0))
    rev = lambda n: N - 1 - n
    return pl.pallas_call(
        body, name=name, grid=(N,),
        in_specs=[pl.BlockSpec((C, 3 * GDN_WIDTH), lambda n: (rev(n), 0)),
                  pl.BlockSpec((C, GDN_WIDTH), lambda n: (rev(n), P_GZ // GDN_WIDTH)),
                  pl.BlockSpec((C, LANES), lambda n: (rev(n), P_GAB // LANES)), vec, vec, vec,
                  pl.BlockSpec((1, H * Dh, Dh), lambda n: (rev(n), 0, 0)),
                  pl.BlockSpec((1, H * C, H * C), lambda n: (rev(n), 0, 0)),
                  pl.BlockSpec((C, GDN_WIDTH), lambda n: (rev(n), Y_GDN // GDN_WIDTH))],
        out_specs=[pl.BlockSpec((C, 3 * GDN_WIDTH), lambda n: (rev(n), 0)),
                   pl.BlockSpec((C, GDN_WIDTH), lambda n: (rev(n), 0)),
                   pl.BlockSpec((C, LANES), lambda n: (rev(n), 0)), vec, vec, vec],
        out_shape=[jax.ShapeDtypeStruct((L, 3 * GDN_WIDTH), F32), jax.ShapeDtypeStruct((L, GDN_WIDTH), MXU_DTYPE),
                   jax.ShapeDtypeStruct((L, LANES), F32)] + [jax.ShapeDtypeStruct((1, LANES), F32)] * 3,
        scratch_shapes=[pltpu.VMEM((H * Dh, Dh), F32)],
        compiler_params=_params(("arbitrary",)),
    )(cqkv, proj, proj, alog, dtb, wn, s_all, t_all, dy)


def _split_dot(x, m):
    R = x.shape[0]
    hi = x.astype(MXU_DTYPE)
    lo = (x - hi.astype(F32)).astype(MXU_DTYPE)
    both = jnp.dot(jnp.concatenate([hi, lo], axis=0), m, preferred_element_type=F32)
    return both[:R] + both[R:]


def _sb_kv_blocks(kv_ref, js):
    B = SB_BLOCK
    rows = [pl.ds(pl.multiple_of(j * B, B), B) for j in js]
    kps = [[kv_ref[r, p * LANES:(p + 1) * LANES] for p in range(SB_HEADS // 2)] for r in rows]
    vps = [[kv_ref[r, SB_WIDTH + p * LANES:SB_WIDTH + (p + 1) * LANES] for p in range(SB_HEADS // 2)] for r in rows]
    return rows, kps, vps


def _sb_pair_dots(x, mats, mode):
    B = SB_BLOCK
    return jnp.concatenate([lax.dot_general(x[2 * p * B:(2 * p + 2) * B], m, _DIMS[mode], preferred_element_type=F32)
                            for mp in mats for p, m in enumerate(mp)], axis=0)


SB_SCALE = SB_HEAD_DIM ** -0.5


def _sb_logits(qx, kps):
    z = _sb_pair_dots(qx, kps, "nt")
    return z, jnp.minimum(z, 0.0) - jnp.log(1.0 + jnp.exp(-jnp.abs(z)))


def _sb_running(start, sums, inclusive):
    R = start.shape[0]
    n = sums.shape[0] // R
    vals, cur = [], start
    for b in range(n):
        nxt = cur + sums[b * R:(b + 1) * R]
        vals.append(nxt if inclusive else cur)
        cur = nxt
    return (vals[0] if n == 1 else jnp.concatenate(vals, axis=0)), cur


def _sb_head_masks():
    low = lax.broadcasted_iota(jnp.int32, (SB_BLOCK, LANES), 1) < SB_HEAD_DIM
    return [low if h % 2 == 0 else jnp.logical_not(low) for h in range(SB_HEADS)]


def _sb_stack_heads(ref, scale=1.0):
    mine = _sb_head_masks()
    return jnp.concatenate([jnp.where(mine[h], ref[:, (h // 2) * LANES:(h // 2 + 1) * LANES] * scale, 0.0).astype(MXU_DTYPE)
                            for h in range(SB_HEADS)], axis=0)


def _sb_block_masks():
    B = SB_BLOCK
    row = lax.broadcasted_iota(jnp.int32, (B, B), 0)
    col = lax.broadcasted_iota(jnp.int32, (B, B), 1)
    row4 = lax.broadcasted_iota(jnp.int32, (SB_HEADS * B, B), 0) & (B - 1)
    col4 = lax.broadcasted_iota(jnp.int32, (SB_HEADS * B, B), 1)
    return (row > col).astype(MXU_DTYPE), (row < col).astype(MXU_DTYPE), col4 < row4


def _sb_fwd(proj, kv, name):
    L = proj.shape[0]
    B, H = SB_BLOCK, SB_HEADS

    def body(q_ref, kv_ref, y_ref, c_ref):
        i = pl.program_id(0)
        low = _sb_head_masks()[0]
        after, _, strict = _sb_block_masks()
        qx = _sb_stack_heads(q_ref, SB_SCALE)

        def sweep(js, c, accs, masked):
            _, kps, vps = _sb_kv_blocks(kv_ref, js)
            z, lb = _sb_logits(qx, kps)
            lom = lb - z
            if masked:
                lom = jnp.where(strict, lom, 0.0)
            before_block, c = _sb_running(c, jnp.sum(lom, axis=1, keepdims=True), False)
            a = jnp.exp(lb + _split_dot(lom, after) + before_block)
            if masked:
                a = jnp.where(strict, a, 0.0)
            a = a.astype(MXU_DTYPE)
            new_accs = list(accs)
            for b in range(len(js)):
                o = _sb_pair_dots(a[b * H * B:(b + 1) * H * B], [vps[b]], "nn")
                for p in range(H // 2):
                    new_accs[p] = new_accs[p] + jnp.where(low, o[2 * p * B:(2 * p + 1) * B], o[(2 * p + 1) * B:(2 * p + 2) * B])
            return c, new_accs

        c, accs = sweep([i], jnp.zeros((H * B, 1), F32), [jnp.zeros((B, LANES), F32)] * (H // 2), True)

        W, M = SB_SWEEP, SB_SWEEP // 2

        def wide(it, carry):
            j = i - 1 - W * it
            c, accs = sweep([j - b for b in range(W)], carry[0], list(carry[1:]), False)
            return (c,) + tuple(accs)

        def mid(it, carry):
            j = i % W - 1
            c, accs = sweep([j - b for b in range(M)], carry[0], list(carry[1:]), False)
            return (c,) + tuple(accs)

        def one(it, carry):
            c, accs = sweep([i % M - 1 - it], carry[0], list(carry[1:]), False)
            return (c,) + tuple(accs)

        carry = lax.fori_loop(0, i // W, wide, (c,) + tuple(accs))
        carry = lax.fori_loop(0, (i % W) // M, mid, carry)
        carry = lax.fori_loop(0, i % M, one, carry)
        for p in range(H // 2):
            y_ref[:, p * LANES:(p + 1) * LANES] = carry[1 + p].astype(y_ref.dtype)
        lane = lax.broadcasted_iota(jnp.int32, (B, LANES), 1)
        ct = jnp.zeros((B, LANES), F32)
        for h in range(H):
            ct = jnp.where(lane == h, carry[0][h * B:(h + 1) * B], ct)
        c_ref[...] = ct

    return pl.pallas_call(
        body, name=name, grid=(L // B,),
        in_specs=[pl.BlockSpec((B, SB_WIDTH), lambda i: (i, P_SB // SB_WIDTH)), pl.BlockSpec((L, 2 * SB_WIDTH), lambda i: (0, 0))],
        out_specs=[pl.BlockSpec((B, SB_WIDTH), lambda i: (i, 0)), pl.BlockSpec((B, LANES), lambda i: (i, 0))],
        out_shape=[jax.ShapeDtypeStruct((L, SB_WIDTH), MXU_DTYPE), jax.ShapeDtypeStruct((L, LANES), F32)],
        compiler_params=_params(("arbitrary",)),
    )(proj, kv)


def _sb_bwd(proj, kv, dy, ctot, name):
    L = proj.shape[0]
    B, H = SB_BLOCK, SB_HEADS
    nblk = L // B

    def body(q_ref, kv_ref, do_ref, ct_ref, dq_ref, dk_hbm, dv_hbm, dk_acc, dv_acc):
        i = pl.program_id(0)

        @pl.when(i == 0)
        def _():
            dk_acc[...] = jnp.zeros_like(dk_acc)
            dv_acc[...] = jnp.zeros_like(dv_acc)

        low = _sb_head_masks()[0]
        after, before, strict = _sb_block_masks()
        qx, dox = _sb_stack_heads(q_ref, SB_SCALE), _sb_stack_heads(do_ref)
        ct = ct_ref[...]
        ctot = jnp.concatenate([_pick_lane(ct, h) for h in range(H)], axis=0)

        def sweep(js, p, e, dqs, masked):
            n = len(js)
            rows, kps, vps = _sb_kv_blocks(kv_ref, js)
            z, lb = _sb_logits(qx, kps)
            sig = jnp.exp(lb)
            lom = lb - z
            if masked:
                lom = jnp.where(strict, lom, 0.0)
            through_block, p = _sb_running(p, jnp.sum(lom, axis=1, keepdims=True), True)
            right_of_block = (ctot if n == 1 else jnp.concatenate([ctot] * n, axis=0)) - through_block
            a = jnp.exp(lb + _split_dot(lom, after) + right_of_block)
            if masked:
                a = jnp.where(strict, a, 0.0)
            ea = _sb_pair_dots(dox, vps, "nt") * a
            left_of_block, e = _sb_running(e, jnp.sum(ea, axis=1, keepdims=True), False)
            dlom = left_of_block + jnp.dot(ea.astype(MXU_DTYPE), before, preferred_element_type=F32)
            if masked:
                dlom = jnp.where(strict, dlom, 0.0)
            dz = (ea * (1.0 - sig) - dlom * sig).astype(MXU_DTYPE)
            ab = a.astype(MXU_DTYPE)
            new_dq = list(dqs)
            for b in range(n):
                for pr in range(H // 2):
                    heads = slice(2 * pr * B, (2 * pr + 2) * B)
                    both, cols = slice((b * H + 2 * pr) * B, (b * H + 2 * pr + 2) * B), slice(pr * LANES, (pr + 1) * LANES)
                    dqp = jnp.dot(dz[both], kps[b][pr], preferred_element_type=F32)
                    new_dq[pr] = new_dq[pr] + jnp.where(low, dqp[:B], dqp[B:])
                    dk_acc[rows[b], cols] += lax.dot_general(dz[both], qx[heads], _DIMS["tn"], preferred_element_type=F32)
                    dv_acc[rows[b], cols] += lax.dot_general(ab[both], dox[heads], _DIMS["tn"], preferred_element_type=F32)
            return p, e, new_dq

        W, M = SB_SWEEP, SB_SWEEP // 2

        def wide(it, carry):
            p, e, dqs = sweep([W * it + b for b in range(W)], carry[0], carry[1], list(carry[2:]), False)
            return (p, e) + tuple(dqs)

        def mid(it, carry):
            p, e, dqs = sweep([i - i % W + b for b in range(M)], carry[0], carry[1], list(carry[2:]), False)
            return (p, e) + tuple(dqs)

        def one(it, carry):
            p, e, dqs = sweep([i - i % M + it], carry[0], carry[1], list(carry[2:]), False)
            return (p, e) + tuple(dqs)

        zero = jnp.zeros((H * B, 1), F32)
        carry = lax.fori_loop(0, i // W, wide, (zero, zero) + (jnp.zeros((B, LANES), F32),) * (H // 2))
        carry = lax.fori_loop(0, (i % W) // M, mid, carry)
        carry = lax.fori_loop(0, i % M, one, carry)
        _, _, dqs = sweep([i], carry[0], carry[1], list(carry[2:]), True)
        for pr in range(H // 2):
            dq_ref[:, pr * LANES:(pr + 1) * LANES] = (dqs[pr] * SB_SCALE).astype(dq_ref.dtype)

        @pl.when(i == nblk - 1)
        def _():
            pltpu.sync_copy(dk_acc, dk_hbm)
            pltpu.sync_copy(dv_acc, dv_hbm)

    hbm = pl.BlockSpec(memory_space=pl.ANY)
    acc = jax.ShapeDtypeStruct((L, SB_WIDTH), F32)
    return pl.pallas_call(
        body, name=name, grid=(nblk,),
        in_specs=[pl.BlockSpec((B, SB_WIDTH), lambda i: (i, P_SB // SB_WIDTH)), pl.BlockSpec((L, 2 * SB_WIDTH), lambda i: (0, 0)),
                  pl.BlockSpec((B, SB_WIDTH), lambda i: (i, Y_SB // SB_WIDTH)), pl.BlockSpec((B, LANES), lambda i: (i, 0))],
        out_specs=[pl.BlockSpec((B, SB_WIDTH), lambda i: (i, 0)), hbm, hbm],
        out_shape=[jax.ShapeDtypeStruct((L, SB_WIDTH), MXU_DTYPE), acc, acc],
        scratch_shapes=[pltpu.VMEM((L, SB_WIDTH), F32), pltpu.VMEM((L, SB_WIDTH), F32)],
        compiler_params=_params(("arbitrary",)),
    )(proj, kv, dy, ctot)


def _prep_w_in(w):
    sc, qkv, gz, gab, sb = w[:, 0:768], w[:, 768:2304], w[:, 2304:2816], w[:, 2816:2824], w[:, 2824:3592]
    pad = jnp.zeros((w.shape[0], P_END - D_IN_PROJ), w.dtype)
    return jnp.concatenate([qkv, gz, sc, sb, gab, pad], axis=1).astype(MXU_DTYPE)


def _unprep_dw_in(dw):
    qkv, gz, sc, sb, gab = dw[:, P_QKV:P_GZ], dw[:, P_GZ:P_SC], dw[:, P_SC:P_SB], dw[:, P_SB:P_GAB], dw[:, P_GAB:P_GAB + 8]
    return jnp.concatenate([sc, qkv, gz, gab, sb], axis=1)


def _prep_w_out(w):
    return jnp.concatenate([w[256:768], w[0:256], w[768:]], axis=0).astype(MXU_DTYPE)


def _unprep_dw_out(dw):
    return jnp.concatenate([dw[512:768], dw[0:512], dw[768:]], axis=0)


def _pad_lanes(v):
    return jnp.zeros((1, LANES), F32).at[0, :v.shape[0]].set(v)


def _layer_fwd(x, p, l):
    L = x.shape[0]
    tm = min(2048, L)
    n = f"l{l}_"
    h = _rms_fwd(x, p["norm_mix"], n + "rms_mix")
    proj = _matmul(h, p["w_in"], "nn", F32, n + "mm_in", tm, 768, 1024)
    cb = SC_WIDTH
    y_sc = _conv_fwd((proj, P_SC // cb + 1), p["w_sconv"], SC_KERNEL, cb, 1, MXU_DTYPE, n + "sconv",
                     x2=(proj, P_SC // cb + 2), gate=(proj, P_SC // cb))
    cqkv = _conv_fwd((proj, 0), p["w_gconv"], GDN_CONV, 256, 6, F32, n + "gconv")
    y_gdn, s_all, t_all = _gdn_fwd(cqkv, proj, p["a_log"], p["dt_bias"], p["gdn_norm"], n + "gdn")
    kv = proj[:, P_SB + SB_WIDTH:P_SB + 3 * SB_WIDTH].astype(MXU_DTYPE)
    y_sb, ctot = _sb_fwd(proj, kv, n + "sb")
    ycat = jnp.concatenate([y_gdn, y_sc, y_sb], axis=1)
    x1 = _matmul(ycat, p["w_out"], "nn", F32, n + "mm_out", tm, 512, 1024, resid=x)
    h2 = _rms_fwd(x1, p["norm_ffn"], n + "rms_ffn")
    up = _matmul(h2, p["w_up"], "nn", F32, n + "mm_up", tm, 512, 1024)
    act = _ffn_act_fwd(up, p["w_fconv"], n + "ffn_act")
    x2 = _matmul(act, p["w_down"], "nn", F32, n + "mm_down", tm, 512, 1408, resid=x1)
    saved = dict(x=x, h=h, proj=proj, cqkv=cqkv, s_all=s_all, t_all=t_all, kv=kv, ctot=ctot, ycat=ycat, x1=x1, h2=h2, up=up, act=act)
    return x2, saved


def _layer_bwd(dx2, dx2b, p, s, l):
    L = dx2.shape[0]
    tm, tkl = min(1024, L), min(1024, L)
    n = f"l{l}_"
    g = {}
    g["w_ffn_down"] = _matmul(s["act"], dx2b, "tn", F32, n + "mm_ddown", 1408, 1024, tkl)
    dact = _matmul(dx2b, p["w_down"], "nt", F32, n + "mm_dact", tm, 1408, 1024)
    dup, dwf = _ffn_act_bwd(s["up"], p["w_fconv"], dact, n + "dffn_act")
    g["w_ffn_conv"] = dwf[:FFN_CONV]
    g["w_ffn_up"] = _matmul(s["h2"], dup, "tn", F32, n + "mm_dup", 1024, 2816, tkl)
    dh2 = _matmul(dup, p["w_up"], "nt", F32, n + "mm_dh2", tm, 512, 2 * D_FF)
    dx1, dx1b, dwn = _rms_bwd(s["x1"], p["norm_ffn"], dh2, dx2, n + "drms_ffn")
    g["w_norm_ffn"] = dwn[0]

    g["w_mix_out"] = _unprep_dw_out(_matmul(s["ycat"], dx1b, "tn", F32, n + "mm_dout", 1024, 1024, tkl))
    dycat = _matmul(dx1b, p["w_out"], "nt", F32, n + "mm_dycat", tm, 512, 1024)
    proj = s["proj"]
    cb = SC_WIDTH
    dsc_c, dsc_h, dsc_b, dws = _conv_bwd((proj, P_SC // cb + 1), p["w_sconv"], (dycat, Y_SC // cb), SC_KERNEL, cb, 1,
                                         MXU_DTYPE, n + "dsconv", x2=(proj, P_SC // cb + 2), gate=(proj, P_SC // cb))
    g["w_sconv"] = dws[:SC_KERNEL]
    dcqkv, dgz, dgab, dal, ddb, dgn = _gdn_bwd(s["cqkv"], proj, p["a_log"], p["dt_bias"], p["gdn_norm"], s["s_all"], s["t_all"], dycat,
                                               n + "dgdn")
    g["gdn_a_log"], g["gdn_dt_bias"], g["w_gdn_norm"] = dal[0, :GDN_HEADS], ddb[0, :GDN_HEADS], dgn[0]
    dqkv, dwg = _conv_bwd((proj, 0), p["w_gconv"], (dcqkv, 0), GDN_CONV, 256, 6, MXU_DTYPE, n + "dgconv")
    g["w_gdn_conv"] = dwg[:GDN_CONV]
    dq, dk, dv = _sb_bwd(proj, s["kv"], dycat, s["ctot"], n + "dsb")
    dproj = jnp.concatenate(
        [dqkv, dgz, dsc_b, dsc_c, dsc_h, dq, dk.astype(MXU_DTYPE), dv.astype(MXU_DTYPE), dgab.astype(MXU_DTYPE),
         jnp.zeros((L, P_END - P_GAB - LANES), MXU_DTYPE)], axis=1)
    g["w_mix_in"] = _unprep_dw_in(_matmul(s["h"], dproj, "tn", F32, n + "mm_din", 1024, 1920, tkl))
    dh = _matmul(dproj, p["w_in"], "nt", F32, n + "mm_dh", tm, 512, P_END)
    dx, dxb, dwm = _rms_bwd(s["x"], p["norm_mix"], dh, dx1, n + "drms_mix")
    g["w_norm_mix"] = dwm[0]
    return dx, dxb, g


WEIGHTS = ["w_norm_mix", "w_mix_in", "w_sconv", "w_gdn_conv", "gdn_a_log", "gdn_dt_bias", "w_gdn_norm", "w_mix_out",
           "w_norm_ffn", "w_ffn_up", "w_ffn_conv", "w_ffn_down", "w_norm_final"]


def _local_step(x, w, target):
    layers = []
    for l in range(DEPTH):
        layers.append(dict(
            norm_mix=w["w_norm_mix"][l][None], w_in=_prep_w_in(w["w_mix_in"][l]), w_sconv=w["w_sconv"][l],
            w_gconv=w["w_gdn_conv"][l], a_log=_pad_lanes(w["gdn_a_log"][l]), dt_bias=_pad_lanes(w["gdn_dt_bias"][l]),
            gdn_norm=w["w_gdn_norm"][l][None], w_out=_prep_w_out(w["w_mix_out"][l]), norm_ffn=w["w_norm_ffn"][l][None],
            w_up=w["w_ffn_up"][l].astype(MXU_DTYPE), w_fconv=w["w_ffn_conv"][l], w_down=w["w_ffn_down"][l].astype(MXU_DTYPE)))
    saved = []
    for l in range(DEPTH):
        x, s = _layer_fwd(x, layers[l], l)
        saved.append(s)
    loss, dx, dxb, dwf = _loss_head(x, w["w_norm_final"][None], target, "loss_head")
    grads = [None] * DEPTH
    for l in reversed(range(DEPTH)):
        dx, dxb, grads[l] = _layer_bwd(dx, dxb, layers[l], saved[l], l)
    out = {k: jnp.stack([grads[l][k] for l in range(DEPTH)]) for k in WEIGHTS if k != "w_norm_final"}
    out["w_norm_final"] = dwf[0]
    return loss, dx, out


N_CHIPS = 4
_HBM = pl.BlockSpec(memory_space=pl.ANY)


def _other_chips(x, y):
    return [(1 - x, y), (x, 1 - y), (1 - x, 1 - y)]


def _remote(src, dst, send_sems, recv_sems, k, to):
    return pltpu.make_async_remote_copy(src_ref=src, dst_ref=dst, send_sem=send_sems.at[k], recv_sem=recv_sems.at[k],
                                        device_id=to, device_id_type=pl.DeviceIdType.MESH)


def _all_gather(bufs, name):
    nb = len(bufs)

    def body(*refs):
        ins, outs = refs[:nb], refs[nb:2 * nb]
        send_sems, recv_sems, local_sems = refs[2 * nb:]
        x, y, c = lax.axis_index("x"), lax.axis_index("y"), lax.axis_index("c")
        me, sibling, chips = (x, y, c), (x, y, 1 - c), _other_chips(x, y)

        def copy(b, k, block, to, src=None):
            slot = outs[b].at[4 * block[0] + 2 * block[1] + block[2]]
            return _remote(slot if src is None else src, slot, send_sems.at[b], recv_sems.at[b], k, to)

        local = [pltpu.make_async_copy(ins[b], outs[b].at[4 * x + 2 * y + c], local_sems.at[b]) for b in range(nb)]
        first = [copy(b, 0, me, sibling, src=ins[b]) for b in range(nb)]
        first += [copy(b, 1 + j, me, (*chip, c), src=ins[b]) for j, chip in enumerate(chips) for b in range(nb)]
        for cp in local + first:
            cp.start()
        passed = []
        for j, chip in enumerate(chips):
            for b in range(nb):
                copy(b, 1 + j, (*chip, c), me).wait_recv()
                passed.append(copy(b, 4 + j, (*chip, c), sibling))
                passed[-1].start()
        for b in range(nb):
            copy(b, 0, sibling, me).wait_recv()
        for j, chip in enumerate(chips):
            for b in range(nb):
                copy(b, 4 + j, (*chip, 1 - c), me).wait_recv()
        for cp in first + passed:
            cp.wait_send()
        for cp in local:
            cp.wait()

    return pl.pallas_call(
        body, name=name, in_specs=[_HBM] * nb, out_specs=[_HBM] * nb,
        out_shape=[jax.ShapeDtypeStruct((N_DEV,) + b.shape, b.dtype) for b in bufs],
        scratch_shapes=[pltpu.SemaphoreType.DMA((nb, N_DEV - 1)), pltpu.SemaphoreType.DMA((nb, N_DEV - 1)),
                        pltpu.SemaphoreType.DMA((nb,))],
    )(*bufs)


def _to_sibling(bufs, name):
    nb = len(bufs)

    def body(*refs):
        ins, outs = refs[:nb], refs[nb:2 * nb]
        send_sems, recv_sems = refs[2 * nb:]
        x, y, c = lax.axis_index("x"), lax.axis_index("y"), lax.axis_index("c")
        copies = [_remote(ins[b].at[1 - c], outs[b], send_sems, recv_sems, b, (x, y, 1 - c)) for b in range(nb)]
        for cp in copies:
            cp.start()
        for cp in copies:
            cp.wait()

    return pl.pallas_call(
        body, name=name, in_specs=[_HBM] * nb, out_specs=[_HBM] * nb,
        out_shape=[jax.ShapeDtypeStruct(b.shape[1:], b.dtype) for b in bufs],
        scratch_shapes=[pltpu.SemaphoreType.DMA((nb,)), pltpu.SemaphoreType.DMA((nb,))],
    )(*bufs)


def _to_chips(bufs, name):
    nb = len(bufs)

    def body(*refs):
        ins, outs = refs[:nb], refs[nb:2 * nb]
        send_sems, recv_sems, local_sems = refs[2 * nb:]
        x, y, c = lax.axis_index("x"), lax.axis_index("y"), lax.axis_index("c")
        here = 2 * x + y
        local = [pltpu.make_async_copy(ins[b].at[here], outs[b].at[here], local_sems.at[b]) for b in range(nb)]
        remote = [_remote(ins[b].at[2 * px + py], outs[b].at[here], send_sems.at[b], recv_sems.at[b], j, (px, py, c))
                  for j, (px, py) in enumerate(_other_chips(x, y)) for b in range(nb)]
        for cp in local + remote:
            cp.start()
        for cp in remote:
            cp.wait()
        for cp in local:
            cp.wait()

    return pl.pallas_call(
        body, name=name, in_specs=[_HBM] * nb, out_specs=[_HBM] * nb,
        out_shape=[jax.ShapeDtypeStruct(b.shape, b.dtype) for b in bufs],
        scratch_shapes=[pltpu.SemaphoreType.DMA((nb, N_CHIPS - 1)), pltpu.SemaphoreType.DMA((nb, N_CHIPS - 1)),
                        pltpu.SemaphoreType.DMA((nb,))],
    )(*bufs)


def _pair_sum(a, b, row_tile, name):
    n, D0, R, C = a.shape

    def body(a_ref, b_ref, o_ref):
        o_ref[...] = (a_ref[...].astype(F32) + b_ref[...].astype(F32)).astype(o_ref.dtype)

    blk = pl.BlockSpec((1, 1, row_tile, C), lambda s, l, i: (s, l, i, 0))
    return pl.pallas_call(
        body, name=name, grid=(n, D0, R // row_tile), in_specs=[blk, blk], out_specs=blk,
        out_shape=jax.ShapeDtypeStruct(a.shape, a.dtype), compiler_params=_params(("parallel", "parallel", "parallel")),
    )(a, b)


def _sum_sources(recv, row_tile, name):
    n, R, _ = recv.shape

    def body(r_ref, o_ref):
        acc = r_ref[0].astype(F32)
        for s in range(1, n):
            acc = acc + r_ref[s].astype(F32)
        o_ref[...] = acc

    return pl.pallas_call(
        body, name=name, grid=(R // row_tile,),
        in_specs=[pl.BlockSpec((n, row_tile, LANES), lambda i: (0, i, 0))],
        out_specs=pl.BlockSpec((row_tile, LANES), lambda i: (i, 0)),
        out_shape=jax.ShapeDtypeStruct((R, LANES), F32),
        compiler_params=_params(("parallel",)),
    )(recv)


def _adamw_math(g, w, m, v):
    nm = ADAM_B1 * m + (1.0 - ADAM_B1) * g
    nv = ADAM_B2 * v + (1.0 - ADAM_B2) * (g * g)
    m_hat = nm / (1.0 - ADAM_B1 ** ADAM_STEP)
    v_hat = nv / (1.0 - ADAM_B2 ** ADAM_STEP)
    return -ADAM_LR * (m_hat / (jnp.sqrt(v_hat) + ADAM_EPS) + ADAM_WD * w), nm, nv


def _sum_adamw(recv, w, m, v, row_tile, name):
    D0, R, C = w.shape
    n = recv.shape[0]

    def body(r_ref, w_ref, m_ref, v_ref, g_ref, d_ref, nm_ref, nv_ref):
        g = r_ref[0, 0].astype(F32)
        for s in range(1, n):
            g = g + r_ref[s, 0].astype(F32)
        g_ref[0] = g
        d_ref[0], nm_ref[0], nv_ref[0] = _adamw_math(g, w_ref[0], m_ref[0], v_ref[0])

    blk = pl.BlockSpec((1, row_tile, C), lambda l, i: (l, i, 0))
    out = jax.ShapeDtypeStruct((D0, R, C), F32)
    return pl.pallas_call(
        body, name=name, grid=(D0, R // row_tile),
        in_specs=[pl.BlockSpec((n, 1, row_tile, C), lambda l, i: (0, l, i, 0)), blk, blk, blk],
        out_specs=[blk] * 4, out_shape=[out] * 4,
        compiler_params=_params(("parallel", "parallel")),
    )(recv, w, m, v)


def _adamw(g, w, m, v, row_tile, name):
    R = g.shape[0]

    def body(g_ref, w_ref, m_ref, v_ref, d_ref, nm_ref, nv_ref):
        d_ref[...], nm_ref[...], nv_ref[...] = _adamw_math(g_ref[...], w_ref[...], m_ref[...], v_ref[...])

    blk = pl.BlockSpec((row_tile, LANES), lambda i: (i, 0))
    out = jax.ShapeDtypeStruct((R, LANES), F32)
    return pl.pallas_call(
        body, name=name, grid=(R // row_tile,), in_specs=[blk] * 4, out_specs=[blk] * 3, out_shape=[out] * 3,
        compiler_params=_params(("parallel",)),
    )(g, w, m, v)


def _pack(arrs, rows, dtype):
    flat = jnp.concatenate([a.reshape(-1).astype(dtype) for a in arrs])
    return jnp.pad(flat, (0, rows * LANES - flat.shape[0])).reshape(rows, LANES)


def _unpack(buf, shapes):
    lead = buf.shape[:-2]
    flat = buf.reshape(lead + (-1,))
    out, off = [], 0
    for shp in shapes:
        n = 1
        for d in shp:
            n *= d
        out.append(flat[..., off:off + n].reshape(lead + tuple(shp)))
        off += n
    return out


BIG = ["w_mix_in", "w_mix_out", "w_ffn_up", "w_ffn_down"]
BIG_AXIS = {"w_mix_in": 2, "w_mix_out": 1, "w_ffn_up": 2, "w_ffn_down": 1}
CONV = ["w_sconv", "w_gdn_conv", "w_ffn_conv"]
REPL = ["w_norm_mix", "gdn_a_log", "gdn_dt_bias", "w_gdn_norm", "w_norm_ffn", "w_norm_final"]
BIG_ROW_TILE = {"w_mix_in": 512, "w_mix_out": 128, "w_ffn_up": 512, "w_ffn_down": 352}
SMALL_ROWS = 416
CONV_ROWS = 48


def kernel(x, w_norm_mix, w_mix_in, w_sconv, w_gdn_conv, gdn_a_log, gdn_dt_bias, w_gdn_norm, w_mix_out, w_norm_ffn, w_ffn_up, w_ffn_conv, w_ffn_down, w_norm_final, loss_target, m_w_norm_mix, m_w_mix_in, m_w_sconv, m_w_gdn_conv, m_gdn_a_log, m_gdn_dt_bias, m_w_gdn_norm, m_w_mix_out, m_w_norm_ffn, m_w_ffn_up, m_w_ffn_conv, m_w_ffn_down, m_w_norm_final, v_w_norm_mix, v_w_mix_in, v_w_sconv, v_w_gdn_conv, v_gdn_a_log, v_gdn_dt_bias, v_w_gdn_norm, v_w_mix_out, v_w_norm_ffn, v_w_ffn_up, v_w_ffn_conv, v_w_ffn_down, v_w_norm_final):
    w = dict(w_norm_mix=w_norm_mix, w_mix_in=w_mix_in, w_sconv=w_sconv, w_gdn_conv=w_gdn_conv, gdn_a_log=gdn_a_log,
             gdn_dt_bias=gdn_dt_bias, w_gdn_norm=w_gdn_norm, w_mix_out=w_mix_out, w_norm_ffn=w_norm_ffn, w_ffn_up=w_ffn_up,
             w_ffn_conv=w_ffn_conv, w_ffn_down=w_ffn_down, w_norm_final=w_norm_final)
    m = dict(w_norm_mix=m_w_norm_mix, w_mix_in=m_w_mix_in, w_sconv=m_w_sconv, w_gdn_conv=m_w_gdn_conv, gdn_a_log=m_gdn_a_log,
             gdn_dt_bias=m_gdn_dt_bias, w_gdn_norm=m_w_gdn_norm, w_mix_out=m_w_mix_out, w_norm_ffn=m_w_norm_ffn,
             w_ffn_up=m_w_ffn_up, w_ffn_conv=m_w_ffn_conv, w_ffn_down=m_w_ffn_down, w_norm_final=m_w_norm_final)
    v = dict(w_norm_mix=v_w_norm_mix, w_mix_in=v_w_mix_in, w_sconv=v_w_sconv, w_gdn_conv=v_w_gdn_conv, gdn_a_log=v_gdn_a_log,
             gdn_dt_bias=v_gdn_dt_bias, w_gdn_norm=v_w_gdn_norm, w_mix_out=v_w_mix_out, w_norm_ffn=v_w_norm_ffn,
             w_ffn_up=v_w_ffn_up, w_ffn_conv=v_w_ffn_conv, w_ffn_down=v_w_ffn_down, w_norm_final=v_w_norm_final)
    me = 4 * lax.axis_index("x") + 2 * lax.axis_index("y") + lax.axis_index("c")
    conv_shapes = [w[k].shape for k in CONV]

    gathered = _all_gather([w[k].astype(MXU_DTYPE) for k in BIG] + [_pack([w[k] for k in CONV], CONV_ROWS, F32)],
                           "gather_weights")
    full = dict(w)
    for k, got in zip(BIG, gathered):
        full[k] = jnp.concatenate([got[s] for s in range(N_DEV)], axis=BIG_AXIS[k])
    for k, got in zip(CONV, _unpack(gathered[-1], conv_shapes)):
        full[k] = jnp.concatenate([got[s] for s in range(N_DEV)], axis=2)

    loss, dx, grads = _local_step(x[0], full, loss_target[0])

    small = CONV + REPL
    core = lax.axis_index("c")
    by_core = []
    for k in BIG:
        piece = jnp.split(grads[k], N_DEV, axis=BIG_AXIS[k])
        by_core.append(jnp.stack([jnp.stack([piece[2 * ch + p] for ch in range(N_CHIPS)]) for p in range(2)]).astype(MXU_DTYPE))
    from_sibling = _to_sibling(by_core, "grads_to_sibling")
    chip_sums = [_pair_sum(lax.dynamic_index_in_dim(mine, core, 0, keepdims=False), theirs, BIG_ROW_TILE[k], "pair_sum_" + k)
                 for k, mine, theirs in zip(BIG, by_core, from_sibling)]
    received = _to_chips(chip_sums, "grads_to_chips")
    g, delta, new_m, new_v = {}, {}, {}, {}
    for k, got in zip(BIG, received):
        g[k], delta[k], new_m[k], new_v[k] = _sum_adamw(got, w[k], m[k], v[k], BIG_ROW_TILE[k], "adamw_" + k)
    small_parts = _all_gather([_pack([grads[k] for k in small], SMALL_ROWS, F32)], "gather_small_grads")[0]
    g_small = _unpack(_sum_sources(small_parts, SMALL_ROWS, "sum_small"), [grads[k].shape for k in small])
    for k, gs in zip(small, g_small):
        g[k] = lax.dynamic_slice_in_dim(gs, me * w[k].shape[2], w[k].shape[2], axis=2) if k in CONV else gs

    small_shapes = [w[k].shape for k in small]
    small_rows = -(-sum(w[k].size for k in small) // (SUBLANES * LANES)) * SUBLANES
    d_sm, m_sm, v_sm = _adamw(_pack([g[k] for k in small], small_rows, F32), _pack([w[k] for k in small], small_rows, F32),
                              _pack([m[k] for k in small], small_rows, F32), _pack([v[k] for k in small], small_rows, F32),
                              small_rows, "adamw_small")
    for dst, small_buf in ((delta, d_sm), (new_m, m_sm), (new_v, v_sm)):
        dst.update(zip(small, _unpack(small_buf, small_shapes)))

    loss_all = lax.psum(loss[0, 0], ("x", "y", "c"))
    return (loss_all, dx[None], *[g[k] for k in WEIGHTS], *[delta[k] for k in WEIGHTS], *[new_m[k] for k in WEIGHTS],
            *[new_v[k] for k in WEIGHTS])
```

```python
import functools

import jax
import jax.numpy as jnp
from jax import lax
from jax.experimental import pallas as pl
from jax.experimental.pallas import tpu as pltpu

F32 = jnp.float32
MXU_DTYPE = jnp.bfloat16
HIGHEST = lax.Precision.HIGHEST

D_MODEL = 1024
DEPTH = 2
SC_WIDTH = 256
SC_KERNEL = 3
GDN_WIDTH = 512
GDN_HEADS = 4
GDN_HEAD_DIM = 128
GDN_CONV = 4
GDN_CHUNK = 64
SB_WIDTH = 256
SB_HEADS = 4
SB_HEAD_DIM = 64
SB_BLOCK = 128
SB_SWEEP = 8
D_FF = 2816
FFN_CONV = 3
NORM_EPS = 1e-6
D_IN_PROJ = 3592
ADAM_LR, ADAM_B1, ADAM_B2, ADAM_EPS, ADAM_WD, ADAM_STEP = 0.001, 0.9, 0.999, 1e-08, 0.01, 10

N_DEV = 8
LANES = 128
SUBLANES = 8
VMEM_LIMIT = 48 * 1024 * 1024

P_QKV, P_GZ, P_SC, P_SB, P_GAB, P_END = 0, 1536, 2048, 2816, 3584, 3840
Y_GDN, Y_SC, Y_SB = 0, 512, 768


def _params(semantics):
    return pltpu.CompilerParams(dimension_semantics=semantics, vmem_limit_bytes=VMEM_LIMIT)


_DIMS = {"nn": (((1,), (0,)), ((), ())), "nt": (((1,), (1,)), ((), ())), "tn": (((0,), (0,)), ((), ()))}


def _matmul(a, b, mode, out_dtype, name, tm, tn, tk, resid=None):
    if mode == "tn":
        (K, M), (K2, N) = a.shape, b.shape
    elif mode == "nt":
        (M, K), (N, K2) = a.shape, b.shape
    else:
        (M, K), (K2, N) = a.shape, b.shape
    assert K == K2 and M % tm == 0 and N % tn == 0 and K % tk == 0, (name, a.shape, b.shape, tm, tn, tk)
    nk = K // tk
    has_resid = resid is not None
    assert nk == 1 or out_dtype == F32, name

    def body(*refs):
        if has_resid:
            a_ref, b_ref, r_ref, o_ref = refs
        else:
            a_ref, b_ref, o_ref = refs
        k = pl.program_id(2)
        part = lax.dot_general(a_ref[...], b_ref[...], _DIMS[mode], preferred_element_type=F32)
        if nk == 1:
            o_ref[...] = ((part + r_ref[...]) if has_resid else part).astype(out_dtype)
        else:
            @pl.when(k == 0)
            def _():
                o_ref[...] = (part + r_ref[...]) if has_resid else part

            @pl.when(k > 0)
            def _():
                o_ref[...] += part

    a_spec = pl.BlockSpec((tk, tm), lambda i, j, k: (k, i)) if mode == "tn" else pl.BlockSpec((tm, tk), lambda i, j, k: (i, k))
    b_spec = pl.BlockSpec((tn, tk), lambda i, j, k: (j, k)) if mode == "nt" else pl.BlockSpec((tk, tn), lambda i, j, k: (k, j))
    o_spec = pl.BlockSpec((tm, tn), lambda i, j, k: (i, j))
    in_specs = [a_spec, b_spec] + ([o_spec] if has_resid else [])
    args = (a, b) + ((resid,) if has_resid else ())
    return pl.pallas_call(
        body, name=name, grid=(M // tm, N // tn, nk), in_specs=in_specs, out_specs=o_spec,
        out_shape=jax.ShapeDtypeStruct((M, N), out_dtype),
        compiler_params=_params(("parallel", "parallel", "arbitrary")),
    )(*args)


def _rms(x, w):
    return x * lax.rsqrt(jnp.mean(x * x, axis=-1, keepdims=True) + NORM_EPS) * w


ROW_TILE = 512


def _rms_fwd(x, w, name):
    L, Dm = x.shape

    def body(x_ref, w_ref, h_ref):
        h_ref[...] = _rms(x_ref[...], w_ref[...]).astype(h_ref.dtype)

    return pl.pallas_call(
        body, name=name, grid=(L // ROW_TILE,),
        in_specs=[pl.BlockSpec((ROW_TILE, Dm), lambda i: (i, 0)), pl.BlockSpec((1, Dm), lambda i: (0, 0))],
        out_specs=pl.BlockSpec((ROW_TILE, Dm), lambda i: (i, 0)),
        out_shape=jax.ShapeDtypeStruct((L, Dm), MXU_DTYPE),
        compiler_params=_params(("parallel",)),
    )(x, w)


def _rms_bwd(x, w, dh, dres, name):
    L, Dm = x.shape

    def body(x_ref, w_ref, dh_ref, dres_ref, dx_ref, dxb_ref, dw_ref):
        _, vjp = jax.vjp(_rms, x_ref[...], w_ref[...])
        dx, dw = vjp(dh_ref[...])
        dx = dres_ref[...] + dx
        dx_ref[...] = dx
        dxb_ref[...] = dx.astype(dxb_ref.dtype)

        @pl.when(pl.program_id(0) == 0)
        def _():
            dw_ref[...] = jnp.zeros_like(dw_ref)

        dw_ref[...] += dw

    row = pl.BlockSpec((ROW_TILE, Dm), lambda i: (i, 0))
    vec = pl.BlockSpec((1, Dm), lambda i: (0, 0))
    return pl.pallas_call(
        body, name=name, grid=(L // ROW_TILE,), in_specs=[row, vec, row, row], out_specs=[row, row, vec],
        out_shape=[jax.ShapeDtypeStruct((L, Dm), F32), jax.ShapeDtypeStruct((L, Dm), MXU_DTYPE), jax.ShapeDtypeStruct((1, Dm), F32)],
        compiler_params=_params(("arbitrary",)),
    )(x, w, dh, dres)


def _loss_head(x, w, target, name):
    L, Dm = x.shape

    def block_loss(xb, wb, tb):
        err = _rms(xb, wb) - tb
        return 0.5 * jnp.sum(jnp.sum(err * err, axis=-1, keepdims=True) * (1.0 / Dm), axis=0, keepdims=True)

    def body(x_ref, w_ref, t_ref, loss_ref, dx_ref, dxb_ref, dw_ref):
        val, vjp = jax.vjp(lambda xb, wb: block_loss(xb, wb, t_ref[...]), x_ref[...], w_ref[...])
        dx, dw = vjp(jnp.ones_like(val))
        dx_ref[...] = dx
        dxb_ref[...] = dx.astype(dxb_ref.dtype)

        @pl.when(pl.program_id(0) == 0)
        def _():
            dw_ref[...] = jnp.zeros_like(dw_ref)
            loss_ref[...] = jnp.zeros_like(loss_ref)

        dw_ref[...] += dw
        loss_ref[...] += val

    row = pl.BlockSpec((ROW_TILE, Dm), lambda i: (i, 0))
    vec = pl.BlockSpec((1, Dm), lambda i: (0, 0))
    one = pl.BlockSpec((1, 1), lambda i: (0, 0))
    return pl.pallas_call(
        body, name=name, grid=(L // ROW_TILE,), in_specs=[row, vec, row], out_specs=[one, row, row, vec],
        out_shape=[jax.ShapeDtypeStruct((1, 1), F32), jax.ShapeDtypeStruct((L, Dm), F32), jax.ShapeDtypeStruct((L, Dm), MXU_DTYPE),
                   jax.ShapeDtypeStruct((1, Dm), F32)],
        compiler_params=_params(("arbitrary",)),
    )(x, w, target)


HALO = SUBLANES
CONV_ROW_TILE = 2048


def _conv_specs(L, T, Cb, off):
    main = pl.BlockSpec((T, Cb), lambda j, i: (i, off + j))
    prev = pl.BlockSpec((HALO, Cb), lambda j, i: (jnp.maximum(i * (T // HALO) - 1, 0), off + j))
    nxt = pl.BlockSpec((HALO, Cb), lambda j, i: (jnp.minimum((i + 1) * (T // HALO), L // HALO - 1), off + j))
    return main, prev, nxt


def _conv_fwd(x1, w, K, Cb, ncol, out_dtype, name, x2=None, gate=None):
    (x1a, o1) = x1
    L = x1a.shape[0]
    T = min(CONV_ROW_TILE, L)
    has_mul, has_gate = x2 is not None, gate is not None

    def body(*refs):
        it = iter(refs)
        x1m, x1p = next(it), next(it)
        if has_mul:
            x2m, x2p = next(it), next(it)
        if has_gate:
            gm = next(it)
        w_ref, y_ref, scr = next(it), next(it), next(it)
        i = pl.program_id(1)
        p, pp = x1m[...].astype(F32), x1p[...].astype(F32)
        if has_mul:
            p, pp = p * x2m[...], pp * x2p[...]
        scr[0:HALO, :] = jnp.where(i > 0, pp, 0.0)
        scr[HALO:HALO + T, :] = p
        acc = w_ref[K - 1:K, :] * p
        for k in range(K - 1):
            s = K - 1 - k
            acc = acc + w_ref[k:k + 1, :] * scr[HALO - s:HALO - s + T, :]
        if has_gate:
            acc = acc * gm[...]
        y_ref[...] = acc.astype(out_dtype)

    in_specs, args = [], []
    m, p_, _ = _conv_specs(L, T, Cb, o1)
    in_specs += [m, p_]
    args += [x1a, x1a]
    if has_mul:
        m, p_, _ = _conv_specs(L, T, Cb, x2[1])
        in_specs += [m, p_]
        args += [x2[0], x2[0]]
    if has_gate:
        m, _, _ = _conv_specs(L, T, Cb, gate[1])
        in_specs += [m]
        args += [gate[0]]
    in_specs.append(pl.BlockSpec((K, Cb), lambda j, i: (0, j)))
    args.append(w)
    return pl.pallas_call(
        body, name=name, grid=(ncol, L // T), in_specs=in_specs,
        out_specs=pl.BlockSpec((T, Cb), lambda j, i: (i, j)),
        out_shape=jax.ShapeDtypeStruct((L, ncol * Cb), out_dtype),
        scratch_shapes=[pltpu.VMEM((T + HALO, Cb), F32)],
        compiler_params=_params(("parallel", "arbitrary")),
    )(*args)


def _conv_bwd(x1, w, dy, K, Cb, ncol, out_dtype, name, x2=None, gate=None):
    (x1a, o1) = x1
    L = x1a.shape[0]
    T = min(CONV_ROW_TILE, L)
    nrow = L // T
    has_mul, has_gate = x2 is not None, gate is not None

    def body(*refs):
        it = iter(refs)
        x1m, x1p = next(it), next(it)
        if has_mul:
            x2m, x2p = next(it), next(it)
        if has_gate:
            gm, gn = next(it), next(it)
        dym, dyn, w_ref = next(it), next(it), next(it)
        dx1_ref = next(it)
        if has_mul:
            dx2_ref = next(it)
        if has_gate:
            dg_ref = next(it)
        dw_ref, scr_p, scr_d = next(it), next(it), next(it)
        i = pl.program_id(1)
        p, pp = x1m[...].astype(F32), x1p[...].astype(F32)
        if has_mul:
            p, pp = p * x2m[...], pp * x2p[...]
        scr_p[0:HALO, :] = jnp.where(i > 0, pp, 0.0)
        scr_p[HALO:HALO + T, :] = p
        dcv, dcn = dym[...].astype(F32), dyn[...].astype(F32)
        if has_gate:
            dcv, dcn = dcv * gm[...], dcn * gn[...]
        scr_d[0:T, :] = dcv
        scr_d[T:T + HALO, :] = jnp.where(i < nrow - 1, dcn, 0.0)

        @pl.when(i == 0)
        def _():
            dw_ref[...] = jnp.zeros_like(dw_ref)

        dp = w_ref[K - 1:K, :] * dcv
        cv = w_ref[K - 1:K, :] * p
        dw_ref[K - 1:K, :] += jnp.sum(dcv * p, axis=0, keepdims=True)
        for k in range(K - 1):
            s = K - 1 - k
            dp = dp + w_ref[k:k + 1, :] * scr_d[s:s + T, :]
            sh = scr_p[HALO - s:HALO - s + T, :]
            dw_ref[k:k + 1, :] += jnp.sum(dcv * sh, axis=0, keepdims=True)
            if has_gate:
                cv = cv + w_ref[k:k + 1, :] * sh
        if has_gate:
            dg_ref[...] = (dym[...].astype(F32) * cv).astype(out_dtype)
        if has_mul:
            dx1_ref[...] = (dp * x2m[...]).astype(out_dtype)
            dx2_ref[...] = (dp * x1m[...]).astype(out_dtype)
        else:
            dx1_ref[...] = dp.astype(out_dtype)

    in_specs, args = [], []
    m, p_, _ = _conv_specs(L, T, Cb, o1)
    in_specs += [m, p_]
    args += [x1a, x1a]
    if has_mul:
        m, p_, _ = _conv_specs(L, T, Cb, x2[1])
        in_specs += [m, p_]
        args += [x2[0], x2[0]]
    if has_gate:
        m, _, n_ = _conv_specs(L, T, Cb, gate[1])
        in_specs += [m, n_]
        args += [gate[0], gate[0]]
    m, _, n_ = _conv_specs(L, T, Cb, dy[1])
    in_specs += [m, n_, pl.BlockSpec((K, Cb), lambda j, i: (0, j))]
    args += [dy[0], dy[0], w]
    out = pl.BlockSpec((T, Cb), lambda j, i: (i, j))
    full = jax.ShapeDtypeStruct((L, ncol * Cb), out_dtype)
    n_out = 1 + int(has_mul) + int(has_gate)
    return pl.pallas_call(
        body, name=name, grid=(ncol, nrow), in_specs=in_specs,
        out_specs=[out] * n_out + [pl.BlockSpec((SUBLANES, Cb), lambda j, i: (0, j))],
        out_shape=[full] * n_out + [jax.ShapeDtypeStruct((SUBLANES, ncol * Cb), F32)],
        scratch_shapes=[pltpu.VMEM((T + HALO, Cb), F32), pltpu.VMEM((T + HALO, Cb), F32)],
        compiler_params=_params(("parallel", "arbitrary")),
    )(*args)


GLU_COLS = 256


def _silu(x):
    return x * (1.0 / (1.0 + jnp.exp(-x)))


def _glu(g, v):
    return _silu(g) * v


def _glu_grads(g, v, da):
    sig = 1.0 / (1.0 + jnp.exp(-g))
    gs = g * sig
    return da * v * (sig + gs * (1.0 - sig)), da * gs


def _causal_taps(w_ref, scr, first, rows, K):
    acc = w_ref[K - 1:K, :] * scr[first:first + rows, :]
    for k in range(K - 1):
        s = K - 1 - k
        acc = acc + w_ref[k:k + 1, :] * scr[first - s:first - s + rows, :]
    return acc


def _ffn_act_fwd(up, w, name):
    L = up.shape[0]
    T, Cb, K = min(CONV_ROW_TILE, L), GLU_COLS, FFN_CONV
    nb = D_FF // Cb

    def body(gm, gp, vm, vp, wg, wv, a_ref, sg, sv):
        i = pl.program_id(1)
        for main, prev, scr in ((gm, gp, sg), (vm, vp, sv)):
            scr[0:HALO, :] = jnp.where(i > 0, prev[...], 0.0)
            scr[HALO:HALO + T, :] = main[...]
        a_ref[...] = _glu(_causal_taps(wg, sg, HALO, T, K), _causal_taps(wv, sv, HALO, T, K)).astype(a_ref.dtype)

    gmain, gprev, _ = _conv_specs(L, T, Cb, 0)
    vmain, vprev, _ = _conv_specs(L, T, Cb, nb)
    return pl.pallas_call(
        body, name=name, grid=(nb, L // T),
        in_specs=[gmain, gprev, vmain, vprev, pl.BlockSpec((K, Cb), lambda j, i: (0, j)), pl.BlockSpec((K, Cb), lambda j, i: (0, nb + j))],
        out_specs=pl.BlockSpec((T, Cb), lambda j, i: (i, j)),
        out_shape=jax.ShapeDtypeStruct((L, D_FF), MXU_DTYPE),
        scratch_shapes=[pltpu.VMEM((T + HALO, Cb), F32), pltpu.VMEM((T + HALO, Cb), F32)],
        compiler_params=_params(("parallel", "arbitrary")),
    )(up, up, up, up, w, w)


def _ffn_act_bwd(up, w, dact, name):
    L = up.shape[0]
    T, Cb, K = min(CONV_ROW_TILE, L), GLU_COLS, FFN_CONV
    nb, nrow = D_FF // Cb, L // T

    def body(gm, gp, gn, vm, vp, vn, dam, dan, wg, wv, dg_ref, dv_ref, dwg_ref, dwv_ref, sg, sv, sdg, sdv):
        i = pl.program_id(1)
        for main, prev, nxt, scr in ((gm, gp, gn, sg), (vm, vp, vn, sv)):
            scr[0:HALO, :] = jnp.where(i > 0, prev[...], 0.0)
            scr[HALO:HALO + T, :] = main[...]
            scr[HALO + T:2 * HALO + T, :] = nxt[...]
        ug, uv = _causal_taps(wg, sg, HALO, T + HALO, K), _causal_taps(wv, sv, HALO, T + HALO, K)
        da = jnp.concatenate([dam[...], jnp.where(i < nrow - 1, dan[...], 0.0)], axis=0)
        sdg[...], sdv[...] = _glu_grads(ug, uv, da)

        @pl.when(i == 0)
        def _():
            dwg_ref[...] = jnp.zeros_like(dwg_ref)
            dwv_ref[...] = jnp.zeros_like(dwv_ref)

        for w_ref, scr, sd, d_ref, dw_ref in ((wg, sg, sdg, dg_ref, dwg_ref), (wv, sv, sdv, dv_ref, dwv_ref)):
            du = sd[0:T, :]
            dp = w_ref[K - 1:K, :] * du
            dw_ref[K - 1:K, :] += jnp.sum(du * scr[HALO:HALO + T, :], axis=0, keepdims=True)
            for k in range(K - 1):
                s = K - 1 - k
                dp = dp + w_ref[k:k + 1, :] * sd[s:s + T, :]
                dw_ref[k:k + 1, :] += jnp.sum(du * scr[HALO - s:HALO - s + T, :], axis=0, keepdims=True)
            d_ref[...] = dp.astype(d_ref.dtype)

    gmain, gprev, gnext = _conv_specs(L, T, Cb, 0)
    vmain, vprev, vnext = _conv_specs(L, T, Cb, nb)
    dmain, _, dnext = _conv_specs(L, T, Cb, 0)
    out = pl.BlockSpec((T, Cb), lambda j, i: (i, j))
    dwb = pl.BlockSpec((SUBLANES, Cb), lambda j, i: (0, j))
    half = jax.ShapeDtypeStruct((L, D_FF), MXU_DTYPE)
    dwh = jax.ShapeDtypeStruct((SUBLANES, D_FF), F32)
    dg, dv, dwg, dwv = pl.pallas_call(
        body, name=name, grid=(nb, nrow),
        in_specs=[gmain, gprev, gnext, vmain, vprev, vnext, dmain, dnext,
                  pl.BlockSpec((K, Cb), lambda j, i: (0, j)), pl.BlockSpec((K, Cb), lambda j, i: (0, nb + j))],
        out_specs=[out, out, dwb, dwb], out_shape=[half, half, dwh, dwh],
        scratch_shapes=[pltpu.VMEM((T + 2 * HALO, Cb), F32), pltpu.VMEM((T + 2 * HALO, Cb), F32),
                        pltpu.VMEM((T + HALO, Cb), F32), pltpu.VMEM((T + HALO, Cb), F32)],
        compiler_params=_params(("parallel", "arbitrary")),
    )(up, up, up, up, up, up, dact, dact, w, w)
    return jnp.concatenate([dg, dv], axis=1), jnp.concatenate([dwg, dwv], axis=1)


def _bdot_raw(a, b, mode):
    return lax.dot_general(a.astype(MXU_DTYPE), b.astype(MXU_DTYPE), _DIMS[mode], preferred_element_type=F32)


@functools.partial(jax.custom_vjp, nondiff_argnums=(2,))
def _bdot(a, b, mode):
    return _bdot_raw(a, b, mode)


def _bdot_fwd(a, b, mode):
    return _bdot_raw(a, b, mode), (a, b)


def _bdot_bwd(mode, res, ct):
    a, b = res
    if mode == "nn":
        return _bdot_raw(ct, b, "nt"), _bdot_raw(a, ct, "tn")
    if mode == "nt":
        return _bdot_raw(ct, b, "nn"), _bdot_raw(ct, a, "tn")
    return _bdot_raw(b, ct, "nt"), _bdot_raw(a, ct, "nn")


_bdot.defvjp(_bdot_fwd, _bdot_bwd)


def _hdot(a, b, mode="nn"):
    return lax.dot_general(a, b, _DIMS[mode], precision=lax.Precision.HIGH, preferred_element_type=F32)


@jax.custom_vjp
def _inv_unit_lower(a):
    R = a.shape[0]
    eye = (lax.broadcasted_iota(jnp.int32, (R, R), 0) == lax.broadcasted_iota(jnp.int32, (R, R), 1)).astype(F32)
    t = eye - a
    p = a
    n = 1
    while 2 * n < GDN_CHUNK:
        p = _hdot(p, p)
        t = t + _hdot(t, p)
        n *= 2
    return t


def _inv_fwd(a):
    t = _inv_unit_lower(a)
    return t, t


def _inv_bwd(t, ct):
    return (-_hdot(_hdot(t, ct, "tn"), t, "nt"),)


_inv_unit_lower.defvjp(_inv_fwd, _inv_bwd)


@jax.custom_vjp
def _inv_saved(a, t):
    return t


def _inv_saved_fwd(a, t):
    return t, t


def _inv_saved_bwd(t, ct):
    return _inv_bwd(t, ct)[0], jnp.zeros_like(t)


_inv_saved.defvjp(_inv_saved_fwd, _inv_saved_bwd)


def _softplus(x):
    return jnp.maximum(x, 0.0) + jnp.log(1.0 + jnp.exp(-jnp.abs(x)))


def _sigmoid(x):
    return 1.0 / (1.0 + jnp.exp(-x))


def _pick_lane(blk, lane):
    ids = lax.broadcasted_iota(jnp.int32, blk.shape, 1)
    return jnp.sum(jnp.where(ids == lane, blk, 0.0), axis=1, keepdims=True)


def _gdn_chunk(cq, ck, cv, gz, gab, alog, dtb, wn, S, t_saved=None):
    C, H, Dk = GDN_CHUNK, GDN_HEADS, GDN_HEAD_DIM
    R = H * C
    rows_of = lambda vals, n: jnp.concatenate([jnp.broadcast_to(x, (n, 1)) for x in vals], axis=0)
    ga = jnp.concatenate([_pick_lane(gab, h) for h in range(H)], axis=0)
    gb = jnp.concatenate([_pick_lane(gab, H + h) for h in range(H)], axis=0)
    al = rows_of([_pick_lane(alog, h) for h in range(H)], C)
    db = rows_of([_pick_lane(dtb, h) for h in range(H)], C)
    q, k, v = _silu(cq), _silu(ck), _silu(cv)
    q = q * lax.rsqrt(jnp.sum(q * q, axis=-1, keepdims=True) + NORM_EPS) * (Dk ** -0.5)
    k = k * lax.rsqrt(jnp.sum(k * k, axis=-1, keepdims=True) + NORM_EPS)
    beta = _sigmoid(gb)
    g = -jnp.exp(al) * _softplus(ga + db)
    row = lax.broadcasted_iota(jnp.int32, (R, R), 0)
    col = lax.broadcasted_iota(jnp.int32, (R, R), 1)
    same_head = (row // C) == (col // C)
    causal, strict = same_head & (row >= col), same_head & (row > col)
    gcb = _hdot(causal.astype(F32), jnp.broadcast_to(g, (R, Dk)))
    first = (lax.broadcasted_iota(jnp.int32, (R, Dk), 1) == 0).astype(F32)
    gr = _hdot(first, gcb, "nt")
    gc = _pick_lane(gcb, 0)
    decay = jnp.where(causal, jnp.exp(jnp.where(causal, gc - gr, 0.0)), 0.0)
    kb = k * beta
    lower = jnp.where(strict, _bdot(kb, k, "nt") * decay, 0.0)
    t = _inv_unit_lower(lower) if t_saved is None else _inv_saved(lower, t_saved)
    egc = jnp.exp(gc)
    u = _hdot(t, v * beta)
    w = _hdot(t, kb * egc)
    attn = jnp.where(causal, _bdot(q, k, "nt") * decay, 0.0)
    own = (lax.broadcasted_iota(jnp.int32, (R, H * Dk), 0) // C) == (lax.broadcasted_iota(jnp.int32, (R, H * Dk), 1) // Dk)
    spread = lambda x: jnp.where(own, jnp.concatenate([x] * H, axis=1), 0.0)
    v_new = u - _bdot(spread(w), S, "nn")
    o = _bdot(spread(q * egc), S, "nn") + _bdot(attn, v_new, "nn")
    last = lax.broadcasted_iota(jnp.int32, (R, 1), 0)
    g_last = [jnp.sum(jnp.where(last == h * C + C - 1, gc, 0.0), axis=0, keepdims=True) for h in range(H)]
    S_new = S * jnp.exp(rows_of(g_last, Dk)) + _bdot(spread(k * jnp.exp(rows_of(g_last, C) - gc)), v_new, "tn")
    y = o * lax.rsqrt(jnp.mean(o * o, axis=-1, keepdims=True) + NORM_EPS) * wn * _silu(gz)
    return y, S_new, t


def _stack_heads(ref, first, width=GDN_HEAD_DIM):
    return jnp.concatenate([ref[:, first + h * width:first + (h + 1) * width] for h in range(GDN_HEADS)], axis=0)


def _gdn_fwd(cqkv, proj, alog, dtb, wn, name):
    L = cqkv.shape[0]
    C, H, Dh = GDN_CHUNK, GDN_HEADS, GDN_HEAD_DIM
    N = L // C

    def body(c_ref, gz_ref, gab_ref, al_ref, db_ref, wn_ref, y_ref, sall_ref, tall_ref, S):
        n = pl.program_id(0)

        @pl.when(n == 0)
        def _():
            S[...] = jnp.zeros_like(S)

        s_in = S[...]
        sall_ref[0] = s_in
        y, s_new, t = _gdn_chunk(_stack_heads(c_ref, 0), _stack_heads(c_ref, GDN_WIDTH), _stack_heads(c_ref, 2 * GDN_WIDTH),
                                 _stack_heads(gz_ref, 0), gab_ref[...], al_ref[...], db_ref[...], wn_ref[...], s_in)
        for h in range(H):
            y_ref[:, h * Dh:(h + 1) * Dh] = y[h * C:(h + 1) * C].astype(y_ref.dtype)
        S[...] = s_new
        tall_ref[0] = t

    vec = pl.BlockSpec((1, LANES), lambda n: (0, 0))
    return pl.pallas_call(
        body, name=name, grid=(N,),
        in_specs=[pl.BlockSpec((C, 3 * GDN_WIDTH), lambda n: (n, 0)),
                  pl.BlockSpec((C, GDN_WIDTH), lambda n: (n, P_GZ // GDN_WIDTH)),
                  pl.BlockSpec((C, LANES), lambda n: (n, P_GAB // LANES)), vec, vec, vec],
        out_specs=[pl.BlockSpec((C, GDN_WIDTH), lambda n: (n, 0)), pl.BlockSpec((1, H * Dh, Dh), lambda n: (n, 0, 0)),
                   pl.BlockSpec((1, H * C, H * C), lambda n: (n, 0, 0))],
        out_shape=[jax.ShapeDtypeStruct((L, GDN_WIDTH), MXU_DTYPE), jax.ShapeDtypeStruct((N, H * Dh, Dh), F32),
                   jax.ShapeDtypeStruct((N, H * C, H * C), F32)],
        scratch_shapes=[pltpu.VMEM((H * Dh, Dh), F32)],
        compiler_params=_params(("arbitrary",)),
    )(cqkv, proj, proj, alog, dtb, wn)


def _gdn_bwd(cqkv, proj, alog, dtb, wn, s_all, t_all, dy, name):
    L = cqkv.shape[0]
    C, H, Dh = GDN_CHUNK, GDN_HEADS, GDN_HEAD_DIM
    N = L // C

    def body(c_ref, gz_ref, gab_ref, al_ref, db_ref, wn_ref, sall_ref, tall_ref, dy_ref,
             dc_ref, dgz_ref, dgab_ref, dal_ref, ddb_ref, dwn_ref, dS):
        n = pl.program_id(0)

        @pl.when(n == 0)
        def _():
            dS[...] = jnp.zeros_like(dS)
            dal_ref[...] = jnp.zeros_like(dal_ref)
            ddb_ref[...] = jnp.zeros_like(ddb_ref)
            dwn_ref[...] = jnp.zeros_like(dwn_ref)

        t_saved = tall_ref[0]
        chunk = lambda *a: _gdn_chunk(*a, t_saved=t_saved)[:2]
        _, vjp = jax.vjp(chunk, _stack_heads(c_ref, 0), _stack_heads(c_ref, GDN_WIDTH), _stack_heads(c_ref, 2 * GDN_WIDTH),
                         _stack_heads(gz_ref, 0), gab_ref[...], al_ref[...], db_ref[...], wn_ref[...], sall_ref[0])
        dq, dk, dv, dgz, dgab, dal, ddb, dwn, ds = vjp((_stack_heads(dy_ref, 0), dS[...]))
        for h in range(H):
            rows = slice(h * C, (h + 1) * C)
            dc_ref[:, h * Dh:(h + 1) * Dh] = dq[rows]
            dc_ref[:, (H + h) * Dh:(H + h + 1) * Dh] = dk[rows]
            dc_ref[:, (2 * H + h) * Dh:(2 * H + h + 1) * Dh] = dv[rows]
            dgz_ref[:, h * Dh:(h + 1) * Dh] = dgz[rows].astype(dgz_ref.dtype)
        dS[...] = ds
        dgab_ref[...] = dgab
        dal_ref[...] += dal
        ddb_ref[...] += ddb
        dwn_ref[...] += dwn

    vec = pl.BlockSpec((1, LANES), lambda n: (0, 0))
    rev = lambda n: N - 1 - n
    return pl.pallas_call(
        body, name=name, grid=(N,),
        in_specs=[pl.BlockSpec((C, 3 * GDN_WIDTH), lambda n: (rev(n), 0)),
                  pl.BlockSpec((C, GDN_WIDTH), lambda n: (rev(n), P_GZ // GDN_WIDTH)),
                  pl.BlockSpec((C, LANES), lambda n: (rev(n), P_GAB // LANES)), vec, vec, vec,
                  pl.BlockSpec((1, H * Dh, Dh), lambda n: (rev(n), 0, 0)),
                  pl.BlockSpec((1, H * C, H * C), lambda n: (rev(n), 0, 0)),
                  pl.BlockSpec((C, GDN_WIDTH), lambda n: (rev(n), Y_GDN // GDN_WIDTH))],
        out_specs=[pl.BlockSpec((C, 3 * GDN_WIDTH), lambda n: (rev(n), 0)),
                   pl.BlockSpec((C, GDN_WIDTH), lambda n: (rev(n), 0)),
                   pl.BlockSpec((C, LANES), lambda n: (rev(n), 0)), vec, vec, vec],
        out_shape=[jax.ShapeDtypeStruct((L, 3 * GDN_WIDTH), F32), jax.ShapeDtypeStruct((L, GDN_WIDTH), MXU_DTYPE),
                   jax.ShapeDtypeStruct((L, LANES), F32)] + [jax.ShapeDtypeStruct((1, LANES), F32)] * 3,
        scratch_shapes=[pltpu.VMEM((H * Dh, Dh), F32)],
        compiler_params=_params(("arbitrary",)),
    )(cqkv, proj, proj, alog, dtb, wn, s_all, t_all, dy)


def _split_dot(x, m):
    R = x.shape[0]
    hi = x.astype(MXU_DTYPE)
    lo = (x - hi.astype(F32)).astype(MXU_DTYPE)
    both = jnp.dot(jnp.concatenate([hi, lo], axis=0), m, preferred_element_type=F32)
    return both[:R] + both[R:]


def _sb_kv_blocks(kv_ref, js):
    B = SB_BLOCK
    rows = [pl.ds(pl.multiple_of(j * B, B), B) for j in js]
    kps = [[kv_ref[r, p * LANES:(p + 1) * LANES] for p in range(SB_HEADS // 2)] for r in rows]
    vps = [[kv_ref[r, SB_WIDTH + p * LANES:SB_WIDTH + (p + 1) * LANES] for p in range(SB_HEADS // 2)] for r in rows]
    return rows, kps, vps


def _sb_pair_dots(x, mats, mode):
    B = SB_BLOCK
    return jnp.concatenate([lax.dot_general(x[2 * p * B:(2 * p + 2) * B], m, _DIMS[mode], preferred_element_type=F32)
                            for mp in mats for p, m in enumerate(mp)], axis=0)


SB_SCALE = SB_HEAD_DIM ** -0.5


def _sb_logits(qx, kps):
    z = _sb_pair_dots(qx, kps, "nt")
    return z, jnp.minimum(z, 0.0) - jnp.log(1.0 + jnp.exp(-jnp.abs(z)))


def _sb_running(start, sums, inclusive):
    R = start.shape[0]
    n = sums.shape[0] // R
    vals, cur = [], start
    for b in range(n):
        nxt = cur + sums[b * R:(b + 1) * R]
        vals.append(nxt if inclusive else cur)
        cur = nxt
    return (vals[0] if n == 1 else jnp.concatenate(vals, axis=0)), cur


def _sb_head_masks():
    low = lax.broadcasted_iota(jnp.int32, (SB_BLOCK, LANES), 1) < SB_HEAD_DIM
    return [low if h % 2 == 0 else jnp.logical_not(low) for h in range(SB_HEADS)]


def _sb_stack_heads(ref, scale=1.0):
    mine = _sb_head_masks()
    return jnp.concatenate([jnp.where(mine[h], ref[:, (h // 2) * LANES:(h // 2 + 1) * LANES] * scale, 0.0).astype(MXU_DTYPE)
                            for h in range(SB_HEADS)], axis=0)


def _sb_block_masks():
    B = SB_BLOCK
    row = lax.broadcasted_iota(jnp.int32, (B, B), 0)
    col = lax.broadcasted_iota(jnp.int32, (B, B), 1)
    row4 = lax.broadcasted_iota(jnp.int32, (SB_HEADS * B, B), 0) & (B - 1)
    col4 = lax.broadcasted_iota(jnp.int32, (SB_HEADS * B, B), 1)
    return (row > col).astype(MXU_DTYPE), (row < col).astype(MXU_DTYPE), col4 < row4


def _sb_fwd(proj, kv, name):
    L = proj.shape[0]
    B, H = SB_BLOCK, SB_HEADS

    def body(q_ref, kv_ref, y_ref, c_ref):
        i = pl.program_id(0)
        low = _sb_head_masks()[0]
        after, _, strict = _sb_block_masks()
        qx = _sb_stack_heads(q_ref, SB_SCALE)

        def sweep(js, c, accs, masked):
            _, kps, vps = _sb_kv_blocks(kv_ref, js)
            z, lb = _sb_logits(qx, kps)
            lom = lb - z
            if masked:
                lom = jnp.where(strict, lom, 0.0)
            before_block, c = _sb_running(c, jnp.sum(lom, axis=1, keepdims=True), False)
            a = jnp.exp(lb + _split_dot(lom, after) + before_block)
            if masked:
                a = jnp.where(strict, a, 0.0)
            a = a.astype(MXU_DTYPE)
            new_accs = list(accs)
            for b in range(len(js)):
                o = _sb_pair_dots(a[b * H * B:(b + 1) * H * B], [vps[b]], "nn")
                for p in range(H // 2):
                    new_accs[p] = new_accs[p] + jnp.where(low, o[2 * p * B:(2 * p + 1) * B], o[(2 * p + 1) * B:(2 * p + 2) * B])
            return c, new_accs

        c, accs = sweep([i], jnp.zeros((H * B, 1), F32), [jnp.zeros((B, LANES), F32)] * (H // 2), True)

        W, M = SB_SWEEP, SB_SWEEP // 2

        def wide(it, carry):
            j = i - 1 - W * it
            c, accs = sweep([j - b for b in range(W)], carry[0], list(carry[1:]), False)
            return (c,) + tuple(accs)

        def mid(it, carry):
            j = i % W - 1
            c, accs = sweep([j - b for b in range(M)], carry[0], list(carry[1:]), False)
            return (c,) + tuple(accs)

        def one(it, carry):
            c, accs = sweep([i % M - 1 - it], carry[0], list(carry[1:]), False)
            return (c,) + tuple(accs)

        carry = lax.fori_loop(0, i // W, wide, (c,) + tuple(accs))
        carry = lax.fori_loop(0, (i % W) // M, mid, carry)
        carry = lax.fori_loop(0, i % M, one, carry)
        for p in range(H // 2):
            y_ref[:, p * LANES:(p + 1) * LANES] = carry[1 + p].astype(y_ref.dtype)
        lane = lax.broadcasted_iota(jnp.int32, (B, LANES), 1)
        ct = jnp.zeros((B, LANES), F32)
        for h in range(H):
            ct = jnp.where(lane == h, carry[0][h * B:(h + 1) * B], ct)
        c_ref[...] = ct

    return pl.pallas_call(
        body, name=name, grid=(L // B,),
        in_specs=[pl.BlockSpec((B, SB_WIDTH), lambda i: (i, P_SB // SB_WIDTH)), pl.BlockSpec((L, 2 * SB_WIDTH), lambda i: (0, 0))],
        out_specs=[pl.BlockSpec((B, SB_WIDTH), lambda i: (i, 0)), pl.BlockSpec((B, LANES), lambda i: (i, 0))],
        out_shape=[jax.ShapeDtypeStruct((L, SB_WIDTH), MXU_DTYPE), jax.ShapeDtypeStruct((L, LANES), F32)],
        compiler_params=_params(("arbitrary",)),
    )(proj, kv)


def _sb_bwd(proj, kv, dy, ctot, name):
    L = proj.shape[0]
    B, H = SB_BLOCK, SB_HEADS
    nblk = L // B

    def body(q_ref, kv_ref, do_ref, ct_ref, dq_ref, dk_hbm, dv_hbm, dk_acc, dv_acc):
        i = pl.program_id(0)

        @pl.when(i == 0)
        def _():
            dk_acc[...] = jnp.zeros_like(dk_acc)
            dv_acc[...] = jnp.zeros_like(dv_acc)

        low = _sb_head_masks()[0]
        after, before, strict = _sb_block_masks()
        qx, dox = _sb_stack_heads(q_ref, SB_SCALE), _sb_stack_heads(do_ref)
        ct = ct_ref[...]
        ctot = jnp.concatenate([_pick_lane(ct, h) for h in range(H)], axis=0)

        def sweep(js, p, e, dqs, masked):
            n = len(js)
            rows, kps, vps = _sb_kv_blocks(kv_ref, js)
            z, lb = _sb_logits(qx, kps)
            sig = jnp.exp(lb)
            lom = lb - z
            if masked:
                lom = jnp.where(strict, lom, 0.0)
            through_block, p = _sb_running(p, jnp.sum(lom, axis=1, keepdims=True), True)
            right_of_block = (ctot if n == 1 else jnp.concatenate([ctot] * n, axis=0)) - through_block
            a = jnp.exp(lb + _split_dot(lom, after) + right_of_block)
            if masked:
                a = jnp.where(strict, a, 0.0)
            ea = _sb_pair_dots(dox, vps, "nt") * a
            left_of_block, e = _sb_running(e, jnp.sum(ea, axis=1, keepdims=True), False)
            dlom = left_of_block + jnp.dot(ea.astype(MXU_DTYPE), before, preferred_element_type=F32)
            if masked:
                dlom = jnp.where(strict, dlom, 0.0)
            dz = (ea * (1.0 - sig) - dlom * sig).astype(MXU_DTYPE)
            ab = a.astype(MXU_DTYPE)
            new_dq = list(dqs)
            for b in range(n):
                for pr in range(H // 2):
                    heads = slice(2 * pr * B, (2 * pr + 2) * B)
                    both, cols = slice((b * H + 2 * pr) * B, (b * H + 2 * pr + 2) * B), slice(pr * LANES, (pr + 1) * LANES)
                    dqp = jnp.dot(dz[both], kps[b][pr], preferred_element_type=F32)
                    new_dq[pr] = new_dq[pr] + jnp.where(low, dqp[:B], dqp[B:])
                    dk_acc[rows[b], cols] += lax.dot_general(dz[both], qx[heads], _DIMS["tn"], preferred_element_type=F32)
                    dv_acc[rows[b], cols] += lax.dot_general(ab[both], dox[heads], _DIMS["tn"], preferred_element_type=F32)
            return p, e, new_dq

        W, M = SB_SWEEP, SB_SWEEP // 2

        def wide(it, carry):
            p, e, dqs = sweep([W * it + b for b in range(W)], carry[0], carry[1], list(carry[2:]), False)
            return (p, e) + tuple(dqs)

        def mid(it, carry):
            p, e, dqs = sweep([i - i % W + b for b in range(M)], carry[0], carry[1], list(carry[2:]), False)
            return (p, e) + tuple(dqs)

        def one(it, carry):
            p, e, dqs = sweep([i - i % M + it], carry[0], carry[1], list(carry[2:]), False)
            return (p, e) + tuple(dqs)

        zero = jnp.zeros((H * B, 1), F32)
        carry = lax.fori_loop(0, i // W, wide, (zero, zero) + (jnp.zeros((B, LANES), F32),) * (H // 2))
        carry = lax.fori_loop(0, (i % W) // M, mid, carry)
        carry = lax.fori_loop(0, i % M, one, carry)
        _, _, dqs = sweep([i], carry[0], carry[1], list(carry[2:]), True)
        for pr in range(H // 2):
            dq_ref[:, pr * LANES:(pr + 1) * LANES] = (dqs[pr] * SB_SCALE).astype(dq_ref.dtype)

        @pl.when(i == nblk - 1)
        def _():
            pltpu.sync_copy(dk_acc, dk_hbm)
            pltpu.sync_copy(dv_acc, dv_hbm)

    hbm = pl.BlockSpec(memory_space=pl.ANY)
    acc = jax.ShapeDtypeStruct((L, SB_WIDTH), F32)
    return pl.pallas_call(
        body, name=name, grid=(nblk,),
        in_specs=[pl.BlockSpec((B, SB_WIDTH), lambda i: (i, P_SB // SB_WIDTH)), pl.BlockSpec((L, 2 * SB_WIDTH), lambda i: (0, 0)),
                  pl.BlockSpec((B, SB_WIDTH), lambda i: (i, Y_SB // SB_WIDTH)), pl.BlockSpec((B, LANES), lambda i: (i, 0))],
        out_specs=[pl.BlockSpec((B, SB_WIDTH), lambda i: (i, 0)), hbm, hbm],
        out_shape=[jax.ShapeDtypeStruct((L, SB_WIDTH), MXU_DTYPE), acc, acc],
        scratch_shapes=[pltpu.VMEM((L, SB_WIDTH), F32), pltpu.VMEM((L, SB_WIDTH), F32)],
        compiler_params=_params(("arbitrary",)),
    )(proj, kv, dy, ctot)


def _prep_w_in(w):
    sc, qkv, gz, gab, sb = w[:, 0:768], w[:, 768:2304], w[:, 2304:2816], w[:, 2816:2824], w[:, 2824:3592]
    pad = jnp.zeros((w.shape[0], P_END - D_IN_PROJ), w.dtype)
    return jnp.concatenate([qkv, gz, sc, sb, gab, pad], axis=1).astype(MXU_DTYPE)


def _unprep_dw_in(dw):
    qkv, gz, sc, sb, gab = dw[:, P_QKV:P_GZ], dw[:, P_GZ:P_SC], dw[:, P_SC:P_SB], dw[:, P_SB:P_GAB], dw[:, P_GAB:P_GAB + 8]
    return jnp.concatenate([sc, qkv, gz, gab, sb], axis=1)


def _prep_w_out(w):
    return jnp.concatenate([w[256:768], w[0:256], w[768:]], axis=0).astype(MXU_DTYPE)


def _unprep_dw_out(dw):
    return jnp.concatenate([dw[512:768], dw[0:512], dw[768:]], axis=0)


def _pad_lanes(v):
    return jnp.zeros((1, LANES), F32).at[0, :v.shape[0]].set(v)


def _layer_fwd(x, p, l):
    L = x.shape[0]
    tm = min(2048, L)
    n = f"l{l}_"
    h = _rms_fwd(x, p["norm_mix"], n + "rms_mix")
    proj = _matmul(h, p["w_in"], "nn", F32, n + "mm_in", tm, 768, 1024)
    cb = SC_WIDTH
    y_sc = _conv_fwd((proj, P_SC // cb + 1), p["w_sconv"], SC_KERNEL, cb, 1, MXU_DTYPE, n + "sconv",
                     x2=(proj, P_SC // cb + 2), gate=(proj, P_SC // cb))
    cqkv = _conv_fwd((proj, 0), p["w_gconv"], GDN_CONV, 256, 6, F32, n + "gconv")
    y_gdn, s_all, t_all = _gdn_fwd(cqkv, proj, p["a_log"], p["dt_bias"], p["gdn_norm"], n + "gdn")
    kv = proj[:, P_SB + SB_WIDTH:P_SB + 3 * SB_WIDTH].astype(MXU_DTYPE)
    y_sb, ctot = _sb_fwd(proj, kv, n + "sb")
    ycat = jnp.concatenate([y_gdn, y_sc, y_sb], axis=1)
    x1 = _matmul(ycat, p["w_out"], "nn", F32, n + "mm_out", tm, 512, 1024, resid=x)
    h2 = _rms_fwd(x1, p["norm_ffn"], n + "rms_ffn")
    up = _matmul(h2, p["w_up"], "nn", F32, n + "mm_up", tm, 512, 1024)
    act = _ffn_act_fwd(up, p["w_fconv"], n + "ffn_act")
    x2 = _matmul(act, p["w_down"], "nn", F32, n + "mm_down", tm, 512, 1408, resid=x1)
    saved = dict(x=x, h=h, proj=proj, cqkv=cqkv, s_all=s_all, t_all=t_all, kv=kv, ctot=ctot, ycat=ycat, x1=x1, h2=h2, up=up, act=act)
    return x2, saved


def _layer_bwd(dx2, dx2b, p, s, l):
    L = dx2.shape[0]
    tm, tkl = min(1024, L), min(1024, L)
    n = f"l{l}_"
    g = {}
    g["w_ffn_down"] = _matmul(s["act"], dx2b, "tn", F32, n + "mm_ddown", 1408, 1024, tkl)
    dact = _matmul(dx2b, p["w_down"], "nt", F32, n + "mm_dact", tm, 1408, 1024)
    dup, dwf = _ffn_act_bwd(s["up"], p["w_fconv"], dact, n + "dffn_act")
    g["w_ffn_conv"] = dwf[:FFN_CONV]
    g["w_ffn_up"] = _matmul(s["h2"], dup, "tn", F32, n + "mm_dup", 1024, 2816, tkl)
    dh2 = _matmul(dup, p["w_up"], "nt", F32, n + "mm_dh2", tm, 512, 2 * D_FF)
    dx1, dx1b, dwn = _rms_bwd(s["x1"], p["norm_ffn"], dh2, dx2, n + "drms_ffn")
    g["w_norm_ffn"] = dwn[0]

    g["w_mix_out"] = _unprep_dw_out(_matmul(s["ycat"], dx1b, "tn", F32, n + "mm_dout", 1024, 1024, tkl))
    dycat = _matmul(dx1b, p["w_out"], "nt", F32, n + "mm_dycat", tm, 512, 1024)
    proj = s["proj"]
    cb = SC_WIDTH
    dsc_c, dsc_h, dsc_b, dws = _conv_bwd((proj, P_SC // cb + 1), p["w_sconv"], (dycat, Y_SC // cb), SC_KERNEL, cb, 1,
                                         MXU_DTYPE, n + "dsconv", x2=(proj, P_SC // cb + 2), gate=(proj, P_SC // cb))
    g["w_sconv"] = dws[:SC_KERNEL]
    dcqkv, dgz, dgab, dal, ddb, dgn = _gdn_bwd(s["cqkv"], proj, p["a_log"], p["dt_bias"], p["gdn_norm"], s["s_all"], s["t_all"], dycat,
                                               n + "dgdn")
    g["gdn_a_log"], g["gdn_dt_bias"], g["w_gdn_norm"] = dal[0, :GDN_HEADS], ddb[0, :GDN_HEADS], dgn[0]
    dqkv, dwg = _conv_bwd((proj, 0), p["w_gconv"], (dcqkv, 0), GDN_CONV, 256, 6, MXU_DTYPE, n + "dgconv")
    g["w_gdn_conv"] = dwg[:GDN_CONV]
    dq, dk, dv = _sb_bwd(proj, s["kv"], dycat, s["ctot"], n + "dsb")
    dproj = jnp.concatenate(
        [dqkv, dgz, dsc_b, dsc_c, dsc_h, dq, dk.astype(MXU_DTYPE), dv.astype(MXU_DTYPE), dgab.astype(MXU_DTYPE),
         jnp.zeros((L, P_END - P_GAB - LANES), MXU_DTYPE)], axis=1)
    g["w_mix_in"] = _unprep_dw_in(_matmul(s["h"], dproj, "tn", F32, n + "mm_din", 1024, 1920, tkl))
    dh = _matmul(dproj, p["w_in"], "nt", F32, n + "mm_dh", tm, 512, P_END)
    dx, dxb, dwm = _rms_bwd(s["x"], p["norm_mix"], dh, dx1, n + "drms_mix")
    g["w_norm_mix"] = dwm[0]
    return dx, dxb, g


WEIGHTS = ["w_norm_mix", "w_mix_in", "w_sconv", "w_gdn_conv", "gdn_a_log", "gdn_dt_bias", "w_gdn_norm", "w_mix_out",
           "w_norm_ffn", "w_ffn_up", "w_ffn_conv", "w_ffn_down", "w_norm_final"]


def _local_step(x, w, target):
    layers = []
    for l in range(DEPTH):
        layers.append(dict(
            norm_mix=w["w_norm_mix"][l][None], w_in=_prep_w_in(w["w_mix_in"][l]), w_sconv=w["w_sconv"][l],
            w_gconv=w["w_gdn_conv"][l], a_log=_pad_lanes(w["gdn_a_log"][l]), dt_bias=_pad_lanes(w["gdn_dt_bias"][l]),
            gdn_norm=w["w_gdn_norm"][l][None], w_out=_prep_w_out(w["w_mix_out"][l]), norm_ffn=w["w_norm_ffn"][l][None],
            w_up=w["w_ffn_up"][l].astype(MXU_DTYPE), w_fconv=w["w_ffn_conv"][l], w_down=w["w_ffn_down"][l].astype(MXU_DTYPE)))
    saved = []
    for l in range(DEPTH):
        x, s = _layer_fwd(x, layers[l], l)
        saved.append(s)
    loss, dx, dxb, dwf = _loss_head(x, w["w_norm_final"][None], target, "loss_head")
    grads = [None] * DEPTH
    for l in reversed(range(DEPTH)):
        dx, dxb, grads[l] = _layer_bwd(dx, dxb, layers[l], saved[l], l)
    out = {k: jnp.stack([grads[l][k] for l in range(DEPTH)]) for k in WEIGHTS if k != "w_norm_final"}
    out["w_norm_final"] = dwf[0]
    return loss, dx, out


N_CHIPS = 4
_HBM = pl.BlockSpec(memory_space=pl.ANY)


def _other_chips(x, y):
    return [(1 - x, y), (x, 1 - y), (1 - x, 1 - y)]


def _remote(src, dst, send_sems, recv_sems, k, to):
    return pltpu.make_async_remote_copy(src_ref=src, dst_ref=dst, send_sem=send_sems.at[k], recv_sem=recv_sems.at[k],
                                        device_id=to, device_id_type=pl.DeviceIdType.MESH)


def _all_gather(bufs, name):
    nb = len(bufs)

    def body(*refs):
        ins, outs = refs[:nb], refs[nb:2 * nb]
        send_sems, recv_sems, local_sems = refs[2 * nb:]
        x, y, c = lax.axis_index("x"), lax.axis_index("y"), lax.axis_index("c")
        me, sibling, chips = (x, y, c), (x, y, 1 - c), _other_chips(x, y)

        def copy(b, k, block, to, src=None):
            slot = outs[b].at[4 * block[0] + 2 * block[1] + block[2]]
            return _remote(slot if src is None else src, slot, send_sems.at[b], recv_sems.at[b], k, to)

        local = [pltpu.make_async_copy(ins[b], outs[b].at[4 * x + 2 * y + c], local_sems.at[b]) for b in range(nb)]
        first = [copy(b, 0, me, sibling, src=ins[b]) for b in range(nb)]
        first += [copy(b, 1 + j, me, (*chip, c), src=ins[b]) for j, chip in enumerate(chips) for b in range(nb)]
        for cp in local + first:
            cp.start()
        passed = []
        for j, chip in enumerate(chips):
            for b in range(nb):
                copy(b, 1 + j, (*chip, c), me).wait_recv()
                passed.append(copy(b, 4 + j, (*chip, c), sibling))
                passed[-1].start()
        for b in range(nb):
            copy(b, 0, sibling, me).wait_recv()
        for j, chip in enumerate(chips):
            for b in range(nb):
                copy(b, 4 + j, (*chip, 1 - c), me).wait_recv()
        for cp in first + passed:
            cp.wait_send()
        for cp in local:
            cp.wait()

    return pl.pallas_call(
        body, name=name, in_specs=[_HBM] * nb, out_specs=[_HBM] * nb,
        out_shape=[jax.ShapeDtypeStruct((N_DEV,) + b.shape, b.dtype) for b in bufs],
        scratch_shapes=[pltpu.SemaphoreType.DMA((nb, N_DEV - 1)), pltpu.SemaphoreType.DMA((nb, N_DEV - 1)),
                        pltpu.SemaphoreType.DMA((nb,))],
    )(*bufs)


def _to_sibling(bufs, name):
    nb = len(bufs)

    def body(*refs):
        ins, outs = refs[:nb], refs[nb:2 * nb]
        send_sems, recv_sems = refs[2 * nb:]
        x, y, c = lax.axis_index("x"), lax.axis_index("y"), lax.axis_index("c")
        copies = [_remote(ins[b].at[1 - c], outs[b], send_sems, recv_sems, b, (x, y, 1 - c)) for b in range(nb)]
        for cp in copies:
            cp.start()
        for cp in copies:
            cp.wait()

    return pl.pallas_call(
        body, name=name, in_specs=[_HBM] * nb, out_specs=[_HBM] * nb,
        out_shape=[jax.ShapeDtypeStruct(b.shape[1:], b.dtype) for b in bufs],
        scratch_shapes=[pltpu.SemaphoreType.DMA((nb,)), pltpu.SemaphoreType.DMA((nb,))],
    )(*bufs)


def _to_chips(bufs, name):
    nb = len(bufs)

    def body(*refs):
        ins, outs = refs[:nb], refs[nb:2 * nb]
        send_sems, recv_sems, local_sems = refs[2 * nb:]
        x, y, c = lax.axis_index("x"), lax.axis_index("y"), lax.axis_index("c")
        here = 2 * x + y
        local = [pltpu.make_async_copy(ins[b].at[here], outs[b].at[here], local_sems.at[b]) for b in range(nb)]
        remote = [_remote(ins[b].at[2 * px + py], outs[b].at[here], send_sems.at[b], recv_sems.at[b], j, (px, py, c))
                  for j, (px, py) in enumerate(_other_chips(x, y)) for b in range(nb)]
        for cp in local + remote:
            cp.start()
        for cp in remote:
            cp.wait()
        for cp in local:
            cp.wait()

    return pl.pallas_call(
        body, name=name, in_specs=[_HBM] * nb, out_specs=[_HBM] * nb,
        out_shape=[jax.ShapeDtypeStruct(b.shape, b.dtype) for b in bufs],
        scratch_shapes=[pltpu.SemaphoreType.DMA((nb, N_CHIPS - 1)), pltpu.SemaphoreType.DMA((nb, N_CHIPS - 1)),
                        pltpu.SemaphoreType.DMA((nb,))],
    )(*bufs)


def _pair_sum(a, b, row_tile, name):
    n, D0, R, C = a.shape

    def body(a_ref, b_ref, o_ref):
        o_ref[...] = (a_ref[...].astype(F32) + b_ref[...].astype(F32)).astype(o_ref.dtype)

    blk = pl.BlockSpec((1, 1, row_tile, C), lambda s, l, i: (s, l, i, 0))
    return pl.pallas_call(
        body, name=name, grid=(n, D0, R // row_tile), in_specs=[blk, blk], out_specs=blk,
        out_shape=jax.ShapeDtypeStruct(a.shape, a.dtype), compiler_params=_params(("parallel", "parallel", "parallel")),
    )(a, b)


def _sum_sources(recv, row_tile, name):
    n, R, _ = recv.shape

    def body(r_ref, o_ref):
        acc = r_ref[0].astype(F32)
        for s in range(1, n):
            acc = acc + r_ref[s].astype(F32)
        o_ref[...] = acc

    return pl.pallas_call(
        body, name=name, grid=(R // row_tile,),
        in_specs=[pl.BlockSpec((n, row_tile, LANES), lambda i: (0, i, 0))],
        out_specs=pl.BlockSpec((row_tile, LANES), lambda i: (i, 0)),
        out_shape=jax.ShapeDtypeStruct((R, LANES), F32),
        compiler_params=_params(("parallel",)),
    )(recv)


def _adamw_math(g, w, m, v):
    nm = ADAM_B1 * m + (1.0 - ADAM_B1) * g
    nv = ADAM_B2 * v + (1.0 - ADAM_B2) * (g * g)
    m_hat = nm / (1.0 - ADAM_B1 ** ADAM_STEP)
    v_hat = nv / (1.0 - ADAM_B2 ** ADAM_STEP)
    return -ADAM_LR * (m_hat / (jnp.sqrt(v_hat) + ADAM_EPS) + ADAM_WD * w), nm, nv


def _sum_adamw(recv, w, m, v, row_tile, name):
    D0, R, C = w.shape
    n = recv.shape[0]

    def body(r_ref, w_ref, m_ref, v_ref, g_ref, d_ref, nm_ref, nv_ref):
        g = r_ref[0, 0].astype(F32)
        for s in range(1, n):
            g = g + r_ref[s, 0].astype(F32)
        g_ref[0] = g
        d_ref[0], nm_ref[0], nv_ref[0] = _adamw_math(g, w_ref[0], m_ref[0], v_ref[0])

    blk = pl.BlockSpec((1, row_tile, C), lambda l, i: (l, i, 0))
    out = jax.ShapeDtypeStruct((D0, R, C), F32)
    return pl.pallas_call(
        body, name=name, grid=(D0, R // row_tile),
        in_specs=[pl.BlockSpec((n, 1, row_tile, C), lambda l, i: (0, l, i, 0)), blk, blk, blk],
        out_specs=[blk] * 4, out_shape=[out] * 4,
        compiler_params=_params(("parallel", "parallel")),
    )(recv, w, m, v)


def _adamw(g, w, m, v, row_tile, name):
    R = g.shape[0]

    def body(g_ref, w_ref, m_ref, v_ref, d_ref, nm_ref, nv_ref):
        d_ref[...], nm_ref[...], nv_ref[...] = _adamw_math(g_ref[...], w_ref[...], m_ref[...], v_ref[...])

    blk = pl.BlockSpec((row_tile, LANES), lambda i: (i, 0))
    out = jax.ShapeDtypeStruct((R, LANES), F32)
    return pl.pallas_call(
        body, name=name, grid=(R // row_tile,), in_specs=[blk] * 4, out_specs=[blk] * 3, out_shape=[out] * 3,
        compiler_params=_params(("parallel",)),
    )(g, w, m, v)


def _pack(arrs, rows, dtype):
    flat = jnp.concatenate([a.reshape(-1).astype(dtype) for a in arrs])
    return jnp.pad(flat, (0, rows * LANES - flat.shape[0])).reshape(rows, LANES)


def _unpack(buf, shapes):
    lead = buf.shape[:-2]
    flat = buf.reshape(lead + (-1,))
    out, off = [], 0
    for shp in shapes:
        n = 1
        for d in shp:
            n *= d
        out.append(flat[..., off:off + n].reshape(lead + tuple(shp)))
        off += n
    return out


BIG = ["w_mix_in", "w_mix_out", "w_ffn_up", "w_ffn_down"]
BIG_AXIS = {"w_mix_in": 2, "w_mix_out": 1, "w_ffn_up": 2, "w_ffn_down": 1}
CONV = ["w_sconv", "w_gdn_conv", "w_ffn_conv"]
REPL = ["w_norm_mix", "gdn_a_log", "gdn_dt_bias", "w_gdn_norm", "w_norm_ffn", "w_norm_final"]
BIG_ROW_TILE = {"w_mix_in": 512, "w_mix_out": 128, "w_ffn_up": 512, "w_ffn_down": 352}
SMALL_ROWS = 416
CONV_ROWS = 48


def kernel(x, w_norm_mix, w_mix_in, w_sconv, w_gdn_conv, gdn_a_log, gdn_dt_bias, w_gdn_norm, w_mix_out, w_norm_ffn, w_ffn_up, w_ffn_conv, w_ffn_down, w_norm_final, loss_target, m_w_norm_mix, m_w_mix_in, m_w_sconv, m_w_gdn_conv, m_gdn_a_log, m_gdn_dt_bias, m_w_gdn_norm, m_w_mix_out, m_w_norm_ffn, m_w_ffn_up, m_w_ffn_conv, m_w_ffn_down, m_w_norm_final, v_w_norm_mix, v_w_mix_in, v_w_sconv, v_w_gdn_conv, v_gdn_a_log, v_gdn_dt_bias, v_w_gdn_norm, v_w_mix_out, v_w_norm_ffn, v_w_ffn_up, v_w_ffn_conv, v_w_ffn_down, v_w_norm_final):
    w = dict(w_norm_mix=w_norm_mix, w_mix_in=w_mix_in, w_sconv=w_sconv, w_gdn_conv=w_gdn_conv, gdn_a_log=gdn_a_log,
             gdn_dt_bias=gdn_dt_bias, w_gdn_norm=w_gdn_norm, w_mix_out=w_mix_out, w_norm_ffn=w_norm_ffn, w_ffn_up=w_ffn_up,
             w_ffn_conv=w_ffn_conv, w_ffn_down=w_ffn_down, w_norm_final=w_norm_final)
    m = dict(w_norm_mix=m_w_norm_mix, w_mix_in=m_w_mix_in, w_sconv=m_w_sconv, w_gdn_conv=m_w_gdn_conv, gdn_a_log=m_gdn_a_log,
             gdn_dt_bias=m_gdn_dt_bias, w_gdn_norm=m_w_gdn_norm, w_mix_out=m_w_mix_out, w_norm_ffn=m_w_norm_ffn,
             w_ffn_up=m_w_ffn_up, w_ffn_conv=m_w_ffn_conv, w_ffn_down=m_w_ffn_down, w_norm_final=m_w_norm_final)
    v = dict(w_norm_mix=v_w_norm_mix, w_mix_in=v_w_mix_in, w_sconv=v_w_sconv, w_gdn_conv=v_w_gdn_conv, gdn_a_log=v_gdn_a_log,
             gdn_dt_bias=v_gdn_dt_bias, w_gdn_norm=v_w_gdn_norm, w_mix_out=v_w_mix_out, w_norm_ffn=v_w_norm_ffn,
             w_ffn_up=v_w_ffn_up, w_ffn_conv=v_w_ffn_conv, w_ffn_down=v_w_ffn_down, w_norm_final=v_w_norm_final)
    me = 4 * lax.axis_index("x") + 2 * lax.axis_index("y") + lax.axis_index("c")
    conv_shapes = [w[k].shape for k in CONV]

    gathered = _all_gather([w[k].astype(MXU_DTYPE) for k in BIG] + [_pack([w[k] for k in CONV], CONV_ROWS, F32)],
                           "gather_weights")
    full = dict(w)
    for k, got in zip(BIG, gathered):
        full[k] = jnp.concatenate([got[s] for s in range(N_DEV)], axis=BIG_AXIS[k])
    for k, got in zip(CONV, _unpack(gathered[-1], conv_shapes)):
        full[k] = jnp.concatenate([got[s] for s in range(N_DEV)], axis=2)

    loss, dx, grads = _local_step(x[0], full, loss_target[0])

    small = CONV + REPL
    core = lax.axis_index("c")
    by_core = []
    for k in BIG:
        piece = jnp.split(grads[k], N_DEV, axis=BIG_AXIS[k])
        by_core.append(jnp.stack([jnp.stack([piece[2 * ch + p] for ch in range(N_CHIPS)]) for p in range(2)]).astype(MXU_DTYPE))
    from_sibling = _to_sibling(by_core, "grads_to_sibling")
    chip_sums = [_pair_sum(lax.dynamic_index_in_dim(mine, core, 0, keepdims=False), theirs, BIG_ROW_TILE[k], "pair_sum_" + k)
                 for k, mine, theirs in zip(BIG, by_core, from_sibling)]
    received = _to_chips(chip_sums, "grads_to_chips")
    g, delta, new_m, new_v = {}, {}, {}, {}
    for k, got in zip(BIG, received):
        g[k], delta[k], new_m[k], new_v[k] = _sum_adamw(got, w[k], m[k], v[k], BIG_ROW_TILE[k], "adamw_" + k)
    small_parts = _all_gather([_pack([grads[k] for k in small], SMALL_ROWS, F32)], "gather_small_grads")[0]
    g_small = _unpack(_sum_sources(small_parts, SMALL_ROWS, "sum_small"), [grads[k].shape for k in small])
    for k, gs in zip(small, g_small):
        g[k] = lax.dynamic_slice_in_dim(gs, me * w[k].shape[2], w[k].shape[2], axis=2) if k in CONV else gs

    small_shapes = [w[k].shape for k in small]
    small_rows = -(-sum(w[k].size for k in small) // (SUBLANES * LANES)) * SUBLANES
    d_sm, m_sm, v_sm = _adamw(_pack([g[k] for k in small], small_rows, F32), _pack([w[k] for k in small], small_rows, F32),
                              _pack([m[k] for k in small], small_rows, F32), _pack([v[k] for k in small], small_rows, F32),
                              small_rows, "adamw_small")
    for dst, small_buf in ((delta, d_sm), (new_m, m_sm), (new_v, v_sm)):
        dst.update(zip(small, _unpack(small_buf, small_shapes)))

    loss_all = lax.psum(loss[0, 0], ("x", "y", "c"))
    return (loss_all, dx[None], *[g[k] for k in WEIGHTS], *[delta[k] for k in WEIGHTS], *[new_m[k] for k in WEIGHTS],
            *[new_v[k] for k in WEIGHTS])
```

```python
import functools

import jax
import jax.numpy as jnp
from jax import lax
from jax.experimental import pallas as pl
from jax.experimental.pallas import tpu as pltpu

F32 = jnp.float32
MXU_DTYPE = jnp.bfloat16
HIGHEST = lax.Precision.HIGHEST

D_MODEL = 1024
DEPTH = 2
SC_WIDTH = 256
SC_KERNEL = 3
GDN_WIDTH = 512
GDN_HEADS = 4
GDN_HEAD_DIM = 128
GDN_CONV = 4
GDN_CHUNK = 64
SB_WIDTH = 256
SB_HEADS = 4
SB_HEAD_DIM = 64
SB_BLOCK = 128
SB_SWEEP = 8
D_FF = 2816
FFN_CONV = 3
NORM_EPS = 1e-6
D_IN_PROJ = 3592
ADAM_LR, ADAM_B1, ADAM_B2, ADAM_EPS, ADAM_WD, ADAM_STEP = 0.001, 0.9, 0.999, 1e-08, 0.01, 10

N_DEV = 8
LANES = 128
SUBLANES = 8
VMEM_LIMIT = 48 * 1024 * 1024

P_QKV, P_GZ, P_SC, P_SB, P_GAB, P_END = 0, 1536, 2048, 2816, 3584, 3840
Y_GDN, Y_SC, Y_SB = 0, 512, 768


def _params(semantics):
    return pltpu.CompilerParams(dimension_semantics=semantics, vmem_limit_bytes=VMEM_LIMIT)


_DIMS = {"nn": (((1,), (0,)), ((), ())), "nt": (((1,), (1,)), ((), ())), "tn": (((0,), (0,)), ((), ()))}


def _matmul(a, b, mode, out_dtype, name, tm, tn, tk, resid=None):
    if mode == "tn":
        (K, M), (K2, N) = a.shape, b.shape
    elif mode == "nt":
        (M, K), (N, K2) = a.shape, b.shape
    else:
        (M, K), (K2, N) = a.shape, b.shape
    assert K == K2 and M % tm == 0 and N % tn == 0 and K % tk == 0, (name, a.shape, b.shape, tm, tn, tk)
    nk = K // tk
    has_resid = resid is not None
    assert nk == 1 or out_dtype == F32, name

    def body(*refs):
        if has_resid:
            a_ref, b_ref, r_ref, o_ref = refs
        else:
            a_ref, b_ref, o_ref = refs
        k = pl.program_id(2)
        part = lax.dot_general(a_ref[...], b_ref[...], _DIMS[mode], preferred_element_type=F32)
        if nk == 1:
            o_ref[...] = ((part + r_ref[...]) if has_resid else part).astype(out_dtype)
        else:
            @pl.when(k == 0)
            def _():
                o_ref[...] = (part + r_ref[...]) if has_resid else part

            @pl.when(k > 0)
            def _():
                o_ref[...] += part

    a_spec = pl.BlockSpec((tk, tm), lambda i, j, k: (k, i)) if mode == "tn" else pl.BlockSpec((tm, tk), lambda i, j, k: (i, k))
    b_spec = pl.BlockSpec((tn, tk), lambda i, j, k: (j, k)) if mode == "nt" else pl.BlockSpec((tk, tn), lambda i, j, k: (k, j))
    o_spec = pl.BlockSpec((tm, tn), lambda i, j, k: (i, j))
    in_specs = [a_spec, b_spec] + ([o_spec] if has_resid else [])
    args = (a, b) + ((resid,) if has_resid else ())
    return pl.pallas_call(
        body, name=name, grid=(M // tm, N // tn, nk), in_specs=in_specs, out_specs=o_spec,
        out_shape=jax.ShapeDtypeStruct((M, N), out_dtype),
        compiler_params=_params(("parallel", "parallel", "arbitrary")),
    )(*args)


def _rms(x, w):
    return x * lax.rsqrt(jnp.mean(x * x, axis=-1, keepdims=True) + NORM_EPS) * w


ROW_TILE = 512


def _rms_fwd(x, w, name):
    L, Dm = x.shape

    def body(x_ref, w_ref, h_ref):
        h_ref[...] = _rms(x_ref[...], w_ref[...]).astype(h_ref.dtype)

    return pl.pallas_call(
        body, name=name, grid=(L // ROW_TILE,),
        in_specs=[pl.BlockSpec((ROW_TILE, Dm), lambda i: (i, 0)), pl.BlockSpec((1, Dm), lambda i: (0, 0))],
        out_specs=pl.BlockSpec((ROW_TILE, Dm), lambda i: (i, 0)),
        out_shape=jax.ShapeDtypeStruct((L, Dm), MXU_DTYPE),
        compiler_params=_params(("parallel",)),
    )(x, w)


def _rms_bwd(x, w, dh, dres, name):
    L, Dm = x.shape

    def body(x_ref, w_ref, dh_ref, dres_ref, dx_ref, dxb_ref, dw_ref):
        _, vjp = jax.vjp(_rms, x_ref[...], w_ref[...])
        dx, dw = vjp(dh_ref[...])
        dx = dres_ref[...] + dx
        dx_ref[...] = dx
        dxb_ref[...] = dx.astype(dxb_ref.dtype)

        @pl.when(pl.program_id(0) == 0)
        def _():
            dw_ref[...] = jnp.zeros_like(dw_ref)

        dw_ref[...] += dw

    row = pl.BlockSpec((ROW_TILE, Dm), lambda i: (i, 0))
    vec = pl.BlockSpec((1, Dm), lambda i: (0, 0))
    return pl.pallas_call(
        body, name=name, grid=(L // ROW_TILE,), in_specs=[row, vec, row, row], out_specs=[row, row, vec],
        out_shape=[jax.ShapeDtypeStruct((L, Dm), F32), jax.ShapeDtypeStruct((L, Dm), MXU_DTYPE), jax.ShapeDtypeStruct((1, Dm), F32)],
        compiler_params=_params(("arbitrary",)),
    )(x, w, dh, dres)


def _loss_head(x, w, target, name):
    L, Dm = x.shape

    def block_loss(xb, wb, tb):
        err = _rms(xb, wb) - tb
        return 0.5 * jnp.sum(jnp.sum(err * err, axis=-1, keepdims=True) * (1.0 / Dm), axis=0, keepdims=True)

    def body(x_ref, w_ref, t_ref, loss_ref, dx_ref, dxb_ref, dw_ref):
        val, vjp = jax.vjp(lambda xb, wb: block_loss(xb, wb, t_ref[...]), x_ref[...], w_ref[...])
        dx, dw = vjp(jnp.ones_like(val))
        dx_ref[...] = dx
        dxb_ref[...] = dx.astype(dxb_ref.dtype)

        @pl.when(pl.program_id(0) == 0)
        def _():
            dw_ref[...] = jnp.zeros_like(dw_ref)
            loss_ref[...] = jnp.zeros_like(loss_ref)

        dw_ref[...] += dw
        loss_ref[...] += val

    row = pl.BlockSpec((ROW_TILE, Dm), lambda i: (i, 0))
    vec = pl.BlockSpec((1, Dm), lambda i: (0, 0))
    one = pl.BlockSpec((1, 1), lambda i: (0, 0))
    return pl.pallas_call(
        body, name=name, grid=(L // ROW_TILE,), in_specs=[row, vec, row], out_specs=[one, row, row, vec],
        out_shape=[jax.ShapeDtypeStruct((1, 1), F32), jax.ShapeDtypeStruct((L, Dm), F32), jax.ShapeDtypeStruct((L, Dm), MXU_DTYPE),
                   jax.ShapeDtypeStruct((1, Dm), F32)],
        compiler_params=_params(("arbitrary",)),
    )(x, w, target)


HALO = SUBLANES
CONV_ROW_TILE = 2048


def _conv_specs(L, T, Cb, off):
    main = pl.BlockSpec((T, Cb), lambda j, i: (i, off + j))
    prev = pl.BlockSpec((HALO, Cb), lambda j, i: (jnp.maximum(i * (T // HALO) - 1, 0), off + j))
    nxt = pl.BlockSpec((HALO, Cb), lambda j, i: (jnp.minimum((i + 1) * (T // HALO), L // HALO - 1), off + j))
    return main, prev, nxt


def _conv_fwd(x1, w, K, Cb, ncol, out_dtype, name, x2=None, gate=None):
    (x1a, o1) = x1
    L = x1a.shape[0]
    T = min(CONV_ROW_TILE, L)
    has_mul, has_gate = x2 is not None, gate is not None

    def body(*refs):
        it = iter(refs)
        x1m, x1p = next(it), next(it)
        if has_mul:
            x2m, x2p = next(it), next(it)
        if has_gate:
            gm = next(it)
        w_ref, y_ref, scr = next(it), next(it), next(it)
        i = pl.program_id(1)
        p, pp = x1m[...].astype(F32), x1p[...].astype(F32)
        if has_mul:
            p, pp = p * x2m[...], pp * x2p[...]
        scr[0:HALO, :] = jnp.where(i > 0, pp, 0.0)
        scr[HALO:HALO + T, :] = p
        acc = w_ref[K - 1:K, :] * p
        for k in range(K - 1):
            s = K - 1 - k
            acc = acc + w_ref[k:k + 1, :] * scr[HALO - s:HALO - s + T, :]
        if has_gate:
            acc = acc * gm[...]
        y_ref[...] = acc.astype(out_dtype)

    in_specs, args = [], []
    m, p_, _ = _conv_specs(L, T, Cb, o1)
    in_specs += [m, p_]
    args += [x1a, x1a]
    if has_mul:
        m, p_, _ = _conv_specs(L, T, Cb, x2[1])
        in_specs += [m, p_]
        args += [x2[0], x2[0]]
    if has_gate:
        m, _, _ = _conv_specs(L, T, Cb, gate[1])
        in_specs += [m]
        args += [gate[0]]
    in_specs.append(pl.BlockSpec((K, Cb), lambda j, i: (0, j)))
    args.append(w)
    return pl.pallas_call(
        body, name=name, grid=(ncol, L // T), in_specs=in_specs,
        out_specs=pl.BlockSpec((T, Cb), lambda j, i: (i, j)),
        out_shape=jax.ShapeDtypeStruct((L, ncol * Cb), out_dtype),
        scratch_shapes=[pltpu.VMEM((T + HALO, Cb), F32)],
        compiler_params=_params(("parallel", "arbitrary")),
    )(*args)


def _conv_bwd(x1, w, dy, K, Cb, ncol, out_dtype, name, x2=None, gate=None):
    (x1a, o1) = x1
    L = x1a.shape[0]
    T = min(CONV_ROW_TILE, L)
    nrow = L // T
    has_mul, has_gate = x2 is not None, gate is not None

    def body(*refs):
        it = iter(refs)
        x1m, x1p = next(it), next(it)
        if has_mul:
            x2m, x2p = next(it), next(it)
        if has_gate:
            gm, gn = next(it), next(it)
        dym, dyn, w_ref = next(it), next(it), next(it)
        dx1_ref = next(it)
        if has_mul:
            dx2_ref = next(it)
        if has_gate:
            dg_ref = next(it)
        dw_ref, scr_p, scr_d = next(it), next(it), next(it)
        i = pl.program_id(1)
        p, pp = x1m[...].astype(F32), x1p[...].astype(F32)
        if has_mul:
            p, pp = p * x2m[...], pp * x2p[...]
        scr_p[0:HALO, :] = jnp.where(i > 0, pp, 0.0)
        scr_p[HALO:HALO + T, :] = p
        dcv, dcn = dym[...].astype(F32), dyn[...].astype(F32)
        if has_gate:
            dcv, dcn = dcv * gm[...], dcn * gn[...]
        scr_d[0:T, :] = dcv
        scr_d[T:T + HALO, :] = jnp.where(i < nrow - 1, dcn, 0.0)

        @pl.when(i == 0)
        def _():
            dw_ref[...] = jnp.zeros_like(dw_ref)

        dp = w_ref[K - 1:K, :] * dcv
        cv = w_ref[K - 1:K, :] * p
        dw_ref[K - 1:K, :] += jnp.sum(dcv * p, axis=0, keepdims=True)
        for k in range(K - 1):
            s = K - 1 - k
            dp = dp + w_ref[k:k + 1, :] * scr_d[s:s + T, :]
            sh = scr_p[HALO - s:HALO - s + T, :]
            dw_ref[k:k + 1, :] += jnp.sum(dcv * sh, axis=0, keepdims=True)
            if has_gate:
                cv = cv + w_ref[k:k + 1, :] * sh
        if has_gate:
            dg_ref[...] = (dym[...].astype(F32) * cv).astype(out_dtype)
        if has_mul:
            dx1_ref[...] = (dp * x2m[...]).astype(out_dtype)
            dx2_ref[...] = (dp * x1m[...]).astype(out_dtype)
        else:
            dx1_ref[...] = dp.astype(out_dtype)

    in_specs, args = [], []
    m, p_, _ = _conv_specs(L, T, Cb, o1)
    in_specs += [m, p_]
    args += [x1a, x1a]
    if has_mul:
        m, p_, _ = _conv_specs(L, T, Cb, x2[1])
        in_specs += [m, p_]
        args += [x2[0], x2[0]]
    if has_gate:
        m, _, n_ = _conv_specs(L, T, Cb, gate[1])
        in_specs += [m, n_]
        args += [gate[0], gate[0]]
    m, _, n_ = _conv_specs(L, T, Cb, dy[1])
    in_specs += [m, n_, pl.BlockSpec((K, Cb), lambda j, i: (0, j))]
    args += [dy[0], dy[0], w]
    out = pl.BlockSpec((T, Cb), lambda j, i: (i, j))
    full = jax.ShapeDtypeStruct((L, ncol * Cb), out_dtype)
    n_out = 1 + int(has_mul) + int(has_gate)
    return pl.pallas_call(
        body, name=name, grid=(ncol, nrow), in_specs=in_specs,
        out_specs=[out] * n_out + [pl.BlockSpec((SUBLANES, Cb), lambda j, i: (0, j))],
        out_shape=[full] * n_out + [jax.ShapeDtypeStruct((SUBLANES, ncol * Cb), F32)],
        scratch_shapes=[pltpu.VMEM((T + HALO, Cb), F32), pltpu.VMEM((T + HALO, Cb), F32)],
        compiler_params=_params(("parallel", "arbitrary")),
    )(*args)


GLU_COLS = 256


def _silu(x):
    return x * (1.0 / (1.0 + jnp.exp(-x)))


def _glu(g, v):
    return _silu(g) * v


def _glu_grads(g, v, da):
    sig = 1.0 / (1.0 + jnp.exp(-g))
    gs = g * sig
    return da * v * (sig + gs * (1.0 - sig)), da * gs


def _causal_taps(w_ref, scr, first, rows, K):
    acc = w_ref[K - 1:K, :] * scr[first:first + rows, :]
    for k in range(K - 1):
        s = K - 1 - k
        acc = acc + w_ref[k:k + 1, :] * scr[first - s:first - s + rows, :]
    return acc


def _ffn_act_fwd(up, w, name):
    L = up.shape[0]
    T, Cb, K = min(CONV_ROW_TILE, L), GLU_COLS, FFN_CONV
    nb = D_FF // Cb

    def body(gm, gp, vm, vp, wg, wv, a_ref, sg, sv):
        i = pl.program_id(1)
        for main, prev, scr in ((gm, gp, sg), (vm, vp, sv)):
            scr[0:HALO, :] = jnp.where(i > 0, prev[...], 0.0)
            scr[HALO:HALO + T, :] = main[...]
        a_ref[...] = _glu(_causal_taps(wg, sg, HALO, T, K), _causal_taps(wv, sv, HALO, T, K)).astype(a_ref.dtype)

    gmain, gprev, _ = _conv_specs(L, T, Cb, 0)
    vmain, vprev, _ = _conv_specs(L, T, Cb, nb)
    return pl.pallas_call(
        body, name=name, grid=(nb, L // T),
        in_specs=[gmain, gprev, vmain, vprev, pl.BlockSpec((K, Cb), lambda j, i: (0, j)), pl.BlockSpec((K, Cb), lambda j, i: (0, nb + j))],
        out_specs=pl.BlockSpec((T, Cb), lambda j, i: (i, j)),
        out_shape=jax.ShapeDtypeStruct((L, D_FF), MXU_DTYPE),
        scratch_shapes=[pltpu.VMEM((T + HALO, Cb), F32), pltpu.VMEM((T + HALO, Cb), F32)],
        compiler_params=_params(("parallel", "arbitrary")),
    )(up, up, up, up, w, w)


def _ffn_act_bwd(up, w, dact, name):
    L = up.shape[0]
    T, Cb, K = min(CONV_ROW_TILE, L), GLU_COLS, FFN_CONV
    nb, nrow = D_FF // Cb, L // T

    def body(gm, gp, gn, vm, vp, vn, dam, dan, wg, wv, dg_ref, dv_ref, dwg_ref, dwv_ref, sg, sv, sdg, sdv):
        i = pl.program_id(1)
        for main, prev, nxt, scr in ((gm, gp, gn, sg), (vm, vp, vn, sv)):
            scr[0:HALO, :] = jnp.where(i > 0, prev[...], 0.0)
            scr[HALO:HALO + T, :] = main[...]
            scr[HALO + T:2 * HALO + T, :] = nxt[...]
        ug, uv = _causal_taps(wg, sg, HALO, T + HALO, K), _causal_taps(wv, sv, HALO, T + HALO, K)
        da = jnp.concatenate([dam[...], jnp.where(i < nrow - 1, dan[...], 0.0)], axis=0)
        sdg[...], sdv[...] = _glu_grads(ug, uv, da)

        @pl.when(i == 0)
        def _():
            dwg_ref[...] = jnp.zeros_like(dwg_ref)
            dwv_ref[...] = jnp.zeros_like(dwv_ref)

        for w_ref, scr, sd, d_ref, dw_ref in ((wg, sg, sdg, dg_ref, dwg_ref), (wv, sv, sdv, dv_ref, dwv_ref)):
            du = sd[0:T, :]
            dp = w_ref[K - 1:K, :] * du
            dw_ref[K - 1:K, :] += jnp.sum(du * scr[HALO:HALO + T, :], axis=0, keepdims=True)
            for k in range(K - 1):
                s = K - 1 - k
                dp = dp + w_ref[k:k + 1, :] * sd[s:s + T, :]
                dw_ref[k:k + 1, :] += jnp.sum(du * scr[HALO - s:HALO - s + T, :], axis=0, keepdims=True)
            d_ref[...] = dp.astype(d_ref.dtype)

    gmain, gprev, gnext = _conv_specs(L, T, Cb, 0)
    vmain, vprev, vnext = _conv_specs(L, T, Cb, nb)
    dmain, _, dnext = _conv_specs(L, T, Cb, 0)
    out = pl.BlockSpec((T, Cb), lambda j, i: (i, j))
    dwb = pl.BlockSpec((SUBLANES, Cb), lambda j, i: (0, j))
    half = jax.ShapeDtypeStruct((L, D_FF), MXU_DTYPE)
    dwh = jax.ShapeDtypeStruct((SUBLANES, D_FF), F32)
    dg, dv, dwg, dwv = pl.pallas_call(
        body, name=name, grid=(nb, nrow),
        in_specs=[gmain, gprev, gnext, vmain, vprev, vnext, dmain, dnext,
                  pl.BlockSpec((K, Cb), lambda j, i: (0, j)), pl.BlockSpec((K, Cb), lambda j, i: (0, nb + j))],
        out_specs=[out, out, dwb, dwb], out_shape=[half, half, dwh, dwh],
        scratch_shapes=[pltpu.VMEM((T + 2 * HALO, Cb), F32), pltpu.VMEM((T + 2 * HALO, Cb), F32),
                        pltpu.VMEM((T + HALO, Cb), F32), pltpu.VMEM((T + HALO, Cb), F32)],
        compiler_params=_params(("parallel", "arbitrary")),
    )(up, up, up, up, up, up, dact, dact, w, w)
    return jnp.concatenate([dg, dv], axis=1), jnp.concatenate([dwg, dwv], axis=1)


def _bdot_raw(a, b, mode):
    return lax.dot_general(a.astype(MXU_DTYPE), b.astype(MXU_DTYPE), _DIMS[mode], preferred_element_type=F32)


@functools.partial(jax.custom_vjp, nondiff_argnums=(2,))
def _bdot(a, b, mode):
    return _bdot_raw(a, b, mode)


def _bdot_fwd(a, b, mode):
    return _bdot_raw(a, b, mode), (a, b)


def _bdot_bwd(mode, res, ct):
    a, b = res
    if mode == "nn":
        return _bdot_raw(ct, b, "nt"), _bdot_raw(a, ct, "tn")
    if mode == "nt":
        return _bdot_raw(ct, b, "nn"), _bdot_raw(ct, a, "tn")
    return _bdot_raw(b, ct, "nt"), _bdot_raw(a, ct, "nn")


_bdot.defvjp(_bdot_fwd, _bdot_bwd)


def _hdot(a, b, mode="nn"):
    return lax.dot_general(a, b, _DIMS[mode], precision=lax.Precision.HIGH, preferred_element_type=F32)


@jax.custom_vjp
def _inv_unit_lower(a):
    R = a.shape[0]
    eye = (lax.broadcasted_iota(jnp.int32, (R, R), 0) == lax.broadcasted_iota(jnp.int32, (R, R), 1)).astype(F32)
    t = eye - a
    p = a
    n = 1
    while 2 * n < GDN_CHUNK:
        p = _hdot(p, p)
        t = t + _hdot(t, p)
        n *= 2
    return t


def _inv_fwd(a):
    t = _inv_unit_lower(a)
    return t, t


def _inv_bwd(t, ct):
    return (-_hdot(_hdot(t, ct, "tn"), t, "nt"),)


_inv_unit_lower.defvjp(_inv_fwd, _inv_bwd)


@jax.custom_vjp
def _inv_saved(a, t):
    return t


def _inv_saved_fwd(a, t):
    return t, t


def _inv_saved_bwd(t, ct):
    return _inv_bwd(t, ct)[0], jnp.zeros_like(t)


_inv_saved.defvjp(_inv_saved_fwd, _inv_saved_bwd)


def _softplus(x):
    return jnp.maximum(x, 0.0) + jnp.log(1.0 + jnp.exp(-jnp.abs(x)))


def _sigmoid(x):
    return 1.0 / (1.0 + jnp.exp(-x))


def _pick_lane(blk, lane):
    ids = lax.broadcasted_iota(jnp.int32, blk.shape, 1)
    return jnp.sum(jnp.where(ids == lane, blk, 0.0), axis=1, keepdims=True)


def _gdn_chunk(cq, ck, cv, gz, gab, alog, dtb, wn, S, t_saved=None):
    C, H, Dk = GDN_CHUNK, GDN_HEADS, GDN_HEAD_DIM
    R = H * C
    rows_of = lambda vals, n: jnp.concatenate([jnp.broadcast_to(x, (n, 1)) for x in vals], axis=0)
    ga = jnp.concatenate([_pick_lane(gab, h) for h in range(H)], axis=0)
    gb = jnp.concatenate([_pick_lane(gab, H + h) for h in range(H)], axis=0)
    al = rows_of([_pick_lane(alog, h) for h in range(H)], C)
    db = rows_of([_pick_lane(dtb, h) for h in range(H)], C)
    q, k, v = _silu(cq), _silu(ck), _silu(cv)
    q = q * lax.rsqrt(jnp.sum(q * q, axis=-1, keepdims=True) + NORM_EPS) * (Dk ** -0.5)
    k = k * lax.rsqrt(jnp.sum(k * k, axis=-1, keepdims=True) + NORM_EPS)
    beta = _sigmoid(gb)
    g = -jnp.exp(al) * _softplus(ga + db)
    row = lax.broadcasted_iota(jnp.int32, (R, R), 0)
    col = lax.broadcasted_iota(jnp.int32, (R, R), 1)
    same_head = (row // C) == (col // C)
    causal, strict = same_head & (row >= col), same_head & (row > col)
    gcb = _hdot(causal.astype(F32), jnp.broadcast_to(g, (R, Dk)))
    first = (lax.broadcasted_iota(jnp.int32, (R, Dk), 1) == 0).astype(F32)
    gr = _hdot(first, gcb, "nt")
    gc = _pick_lane(gcb, 0)
    decay = jnp.where(causal, jnp.exp(jnp.where(causal, gc - gr, 0.0)), 0.0)
    kb = k * beta
    lower = jnp.where(strict, _bdot(kb, k, "nt") * decay, 0.0)
    t = _inv_unit_lower(lower) if t_saved is None else _inv_saved(lower, t_saved)
    egc = jnp.exp(gc)
    u = _bdot(t, v * beta, "nn")
    w = _bdot(t, kb * egc, "nn")
    attn = jnp.where(causal, _bdot(q, k, "nt") * decay, 0.0)
    own = (lax.broadcasted_iota(jnp.int32, (R, H * Dk), 0) // C) == (lax.broadcasted_iota(jnp.int32, (R, H * Dk), 1) // Dk)
    spread = lambda x: jnp.where(own, jnp.concatenate([x] * H, axis=1), 0.0)
    v_new = u - _bdot(spread(w), S, "nn")
    o = _bdot(spread(q * egc), S, "nn") + _bdot(attn, v_new, "nn")
    last = lax.broadcasted_iota(jnp.int32, (R, 1), 0)
    g_last = [jnp.sum(jnp.where(last == h * C + C - 1, gc, 0.0), axis=0, keepdims=True) for h in range(H)]
    S_new = S * jnp.exp(rows_of(g_last, Dk)) + _bdot(spread(k * jnp.exp(rows_of(g_last, C) - gc)), v_new, "tn")
    y = o * lax.rsqrt(jnp.mean(o * o, axis=-1, keepdims=True) + NORM_EPS) * wn * _silu(gz)
    return y, S_new, t


def _stack_heads(ref, first, width=GDN_HEAD_DIM):
    return jnp.concatenate([ref[:, first + h * width:first + (h + 1) * width] for h in range(GDN_HEADS)], axis=0)


def _gdn_fwd(cqkv, proj, alog, dtb, wn, name):
    L = cqkv.shape[0]
    C, H, Dh = GDN_CHUNK, GDN_HEADS, GDN_HEAD_DIM
    N = L // C

    def body(c_ref, gz_ref, gab_ref, al_ref, db_ref, wn_ref, y_ref, sall_ref, tall_ref, S):
        n = pl.program_id(0)

        @pl.when(n == 0)
        def _():
            S[...] = jnp.zeros_like(S)

        s_in = S[...]
        sall_ref[0] = s_in
        y, s_new, t = _gdn_chunk(_stack_heads(c_ref, 0), _stack_heads(c_ref, GDN_WIDTH), _stack_heads(c_ref, 2 * GDN_WIDTH),
                                 _stack_heads(gz_ref, 0), gab_ref[...], al_ref[...], db_ref[...], wn_ref[...], s_in)
        for h in range(H):
            y_ref[:, h * Dh:(h + 1) * Dh] = y[h * C:(h + 1) * C].astype(y_ref.dtype)
        S[...] = s_new
        tall_ref[0] = t

    vec = pl.BlockSpec((1, LANES), lambda n: (0, 0))
    return pl.pallas_call(
        body, name=name, grid=(N,),
        in_specs=[pl.BlockSpec((C, 3 * GDN_WIDTH), lambda n: (n, 0)),
                  pl.BlockSpec((C, GDN_WIDTH), lambda n: (n, P_GZ // GDN_WIDTH)),
                  pl.BlockSpec((C, LANES), lambda n: (n, P_GAB // LANES)), vec, vec, vec],
        out_specs=[pl.BlockSpec((C, GDN_WIDTH), lambda n: (n, 0)), pl.BlockSpec((1, H * Dh, Dh), lambda n: (n, 0, 0)),
                   pl.BlockSpec((1, H * C, H * C), lambda n: (n, 0, 0))],
        out_shape=[jax.ShapeDtypeStruct((L, GDN_WIDTH), MXU_DTYPE), jax.ShapeDtypeStruct((N, H * Dh, Dh), F32),
                   jax.ShapeDtypeStruct((N, H * C, H * C), F32)],
        scratch_shapes=[pltpu.VMEM((H * Dh, Dh), F32)],
        compiler_params=_params(("arbitrary",)),
    )(cqkv, proj, proj, alog, dtb, wn)


def _gdn_bwd(cqkv, proj, alog, dtb, wn, s_all, t_all, dy, name):
    L = cqkv.shape[0]
    C, H, Dh = GDN_CHUNK, GDN_HEADS, GDN_HEAD_DIM
    N = L // C

    def body(c_ref, gz_ref, gab_ref, al_ref, db_ref, wn_ref, sall_ref, tall_ref, dy_ref,
             dc_ref, dgz_ref, dgab_ref, dal_ref, ddb_ref, dwn_ref, dS):
        n = pl.program_id(0)

        @pl.when(n == 0)
        def _():
            dS[...] = jnp.zeros_like(dS)
            dal_ref[...] = jnp.zeros_like(dal_ref)
            ddb_ref[...] = jnp.zeros_like(ddb_ref)
            dwn_ref[...] = jnp.zeros_like(dwn_ref)

        t_saved = tall_ref[0]
        chunk = lambda *a: _gdn_chunk(*a, t_saved=t_saved)[:2]
        _, vjp = jax.vjp(chunk, _stack_heads(c_ref, 0), _stack_heads(c_ref, GDN_WIDTH), _stack_heads(c_ref, 2 * GDN_WIDTH),
                         _stack_heads(gz_ref, 0), gab_ref[...], al_ref[...], db_ref[...], wn_ref[...], sall_ref[0])
        dq, dk, dv, dgz, dgab, dal, ddb, dwn, ds = vjp((_stack_heads(dy_ref, 0), dS[...]))
        for h in range(H):
            rows = slice(h * C, (h + 1) * C)
            dc_ref[:, h * Dh:(h + 1) * Dh] = dq[rows]
            dc_ref[:, (H + h) * Dh:(H + h + 1) * Dh] = dk[rows]
            dc_ref[:, (2 * H + h) * Dh:(2 * H + h + 1) * Dh] = dv[rows]
            dgz_ref[:, h * Dh:(h + 1) * Dh] = dgz[rows].astype(dgz_ref.dtype)
        dS[...] = ds
        dgab_ref[...] = dgab
        dal_ref[...] += dal
        ddb_ref[...] += ddb
        dwn_ref[...] += dwn

    vec = pl.BlockSpec((1, LANES), lambda n: (0, 0))
    rev = lambda n: N - 1 - n
    return pl.pallas_call(
        body, name=name, grid=(N,),
        in_specs=[pl.BlockSpec((C, 3 * GDN_WIDTH), lambda n: (rev(n), 0)),
                  pl.BlockSpec((C, GDN_WIDTH), lambda n: (rev(n), P_GZ // GDN_WIDTH)),
                  pl.BlockSpec((C, LANES), lambda n: (rev(n), P_GAB // LANES)), vec, vec, vec,
                  pl.BlockSpec((1, H * Dh, Dh), lambda n: (rev(n), 0, 0)),
                  pl.BlockSpec((1, H * C, H * C), lambda n: (rev(n), 0, 0)),
                  pl.BlockSpec((C, GDN_WIDTH), lambda n: (rev(n), Y_GDN // GDN_WIDTH))],
        out_specs=[pl.BlockSpec((C, 3 * GDN_WIDTH), lambda n: (rev(n), 0)),
                   pl.BlockSpec((C, GDN_WIDTH), lambda n: (rev(n), 0)),
                   pl.BlockSpec((C, LANES), lambda n: (rev(n), 0)), vec, vec, vec],
        out_shape=[jax.ShapeDtypeStruct((L, 3 * GDN_WIDTH), F32), jax.ShapeDtypeStruct((L, GDN_WIDTH), MXU_DTYPE),
                   jax.ShapeDtypeStruct((L, LANES), F32)] + [jax.ShapeDtypeStruct((1, LANES), F32)] * 3,
        scratch_shapes=[pltpu.VMEM((H * Dh, Dh), F32)],
        compiler_params=_params(("arbitrary",)),
    )(cqkv, proj, proj, alog, dtb, wn, s_all, t_all, dy)


def _split_dot(x, m):
    R = x.shape[0]
    hi = x.astype(MXU_DTYPE)
    lo = (x - hi.astype(F32)).astype(MXU_DTYPE)
    both = jnp.dot(jnp.concatenate([hi, lo], axis=0), m, preferred_element_type=F32)
    return both[:R] + both[R:]


def _sb_kv_blocks(kv_ref, js):
    B = SB_BLOCK
    rows = [pl.ds(pl.multiple_of(j * B, B), B) for j in js]
    kps = [[kv_ref[r, p * LANES:(p + 1) * LANES] for p in range(SB_HEADS // 2)] for r in rows]
    vps = [[kv_ref[r, SB_WIDTH + p * LANES:SB_WIDTH + (p + 1) * LANES] for p in range(SB_HEADS // 2)] for r in rows]
    return rows, kps, vps


def _sb_pair_dots(x, mats, mode):
    B = SB_BLOCK
    return jnp.concatenate([lax.dot_general(x[2 * p * B:(2 * p + 2) * B], m, _DIMS[mode], preferred_element_type=F32)
                            for mp in mats for p, m in enumerate(mp)], axis=0)


SB_SCALE = SB_HEAD_DIM ** -0.5


def _sb_logits(qx, kps):
    z = _sb_pair_dots(qx, kps, "nt")
    return z, jnp.minimum(z, 0.0) - jnp.log(1.0 + jnp.exp(-jnp.abs(z)))


def _sb_running(start, sums, inclusive):
    R = start.shape[0]
    n = sums.shape[0] // R
    vals, cur = [], start
    for b in range(n):
        nxt = cur + sums[b * R:(b + 1) * R]
        vals.append(nxt if inclusive else cur)
        cur = nxt
    return (vals[0] if n == 1 else jnp.concatenate(vals, axis=0)), cur


def _sb_head_masks():
    low = lax.broadcasted_iota(jnp.int32, (SB_BLOCK, LANES), 1) < SB_HEAD_DIM
    return [low if h % 2 == 0 else jnp.logical_not(low) for h in range(SB_HEADS)]


def _sb_stack_heads(ref, scale=1.0):
    mine = _sb_head_masks()
    return jnp.concatenate([jnp.where(mine[h], ref[:, (h // 2) * LANES:(h // 2 + 1) * LANES] * scale, 0.0).astype(MXU_DTYPE)
                            for h in range(SB_HEADS)], axis=0)


def _sb_block_masks():
    B = SB_BLOCK
    row = lax.broadcasted_iota(jnp.int32, (B, B), 0)
    col = lax.broadcasted_iota(jnp.int32, (B, B), 1)
    row4 = lax.broadcasted_iota(jnp.int32, (SB_HEADS * B, B), 0) & (B - 1)
    col4 = lax.broadcasted_iota(jnp.int32, (SB_HEADS * B, B), 1)
    return (row > col).astype(MXU_DTYPE), (row < col).astype(MXU_DTYPE), col4 < row4


def _sb_fwd(proj, kv, name):
    L = proj.shape[0]
    B, H = SB_BLOCK, SB_HEADS

    def body(q_ref, kv_ref, y_ref, c_ref):
        i = pl.program_id(0)
        low = _sb_head_masks()[0]
        after, _, strict = _sb_block_masks()
        qx = _sb_stack_heads(q_ref, SB_SCALE)

        def sweep(js, c, accs, masked):
            _, kps, vps = _sb_kv_blocks(kv_ref, js)
            z, lb = _sb_logits(qx, kps)
            lom = lb - z
            if masked:
                lom = jnp.where(strict, lom, 0.0)
            before_block, c = _sb_running(c, jnp.sum(lom, axis=1, keepdims=True), False)
            a = jnp.exp(lb + _split_dot(lom, after) + before_block)
            if masked:
                a = jnp.where(strict, a, 0.0)
            a = a.astype(MXU_DTYPE)
            new_accs = list(accs)
            for b in range(len(js)):
                o = _sb_pair_dots(a[b * H * B:(b + 1) * H * B], [vps[b]], "nn")
                for p in range(H // 2):
                    new_accs[p] = new_accs[p] + jnp.where(low, o[2 * p * B:(2 * p + 1) * B], o[(2 * p + 1) * B:(2 * p + 2) * B])
            return c, new_accs

        c, accs = sweep([i], jnp.zeros((H * B, 1), F32), [jnp.zeros((B, LANES), F32)] * (H // 2), True)

        W, M = SB_SWEEP, SB_SWEEP // 2

        def wide(it, carry):
            j = i - 1 - W * it
            c, accs = sweep([j - b for b in range(W)], carry[0], list(carry[1:]), False)
            return (c,) + tuple(accs)

        def mid(it, carry):
            j = i % W - 1
            c, accs = sweep([j - b for b in range(M)], carry[0], list(carry[1:]), False)
            return (c,) + tuple(accs)

        def one(it, carry):
            c, accs = sweep([i % M - 1 - it], carry[0], list(carry[1:]), False)
            return (c,) + tuple(accs)

        carry = lax.fori_loop(0, i // W, wide, (c,) + tuple(accs))
        carry = lax.fori_loop(0, (i % W) // M, mid, carry)
        carry = lax.fori_loop(0, i % M, one, carry)
        for p in range(H // 2):
            y_ref[:, p * LANES:(p + 1) * LANES] = carry[1 + p].astype(y_ref.dtype)
        lane = lax.broadcasted_iota(jnp.int32, (B, LANES), 1)
        ct = jnp.zeros((B, LANES), F32)
        for h in range(H):
            ct = jnp.where(lane == h, carry[0][h * B:(h + 1) * B], ct)
        c_ref[...] = ct

    return pl.pallas_call(
        body, name=name, grid=(L // B,),
        in_specs=[pl.BlockSpec((B, SB_WIDTH), lambda i: (i, P_SB // SB_WIDTH)), pl.BlockSpec((L, 2 * SB_WIDTH), lambda i: (0, 0))],
        out_specs=[pl.BlockSpec((B, SB_WIDTH), lambda i: (i, 0)), pl.BlockSpec((B, LANES), lambda i: (i, 0))],
        out_shape=[jax.ShapeDtypeStruct((L, SB_WIDTH), MXU_DTYPE), jax.ShapeDtypeStruct((L, LANES), F32)],
        compiler_params=_params(("arbitrary",)),
    )(proj, kv)


def _sb_bwd(proj, kv, dy, ctot, name):
    L = proj.shape[0]
    B, H = SB_BLOCK, SB_HEADS
    nblk = L // B

    def body(q_ref, kv_ref, do_ref, ct_ref, dq_ref, dk_hbm, dv_hbm, dk_acc, dv_acc):
        i = pl.program_id(0)

        @pl.when(i == 0)
        def _():
            dk_acc[...] = jnp.zeros_like(dk_acc)
            dv_acc[...] = jnp.zeros_like(dv_acc)

        low = _sb_head_masks()[0]
        after, before, strict = _sb_block_masks()
        qx, dox = _sb_stack_heads(q_ref, SB_SCALE), _sb_stack_heads(do_ref)
        ct = ct_ref[...]
        ctot = jnp.concatenate([_pick_lane(ct, h) for h in range(H)], axis=0)

        def sweep(js, p, e, dqs, masked):
            n = len(js)
            rows, kps, vps = _sb_kv_blocks(kv_ref, js)
            z, lb = _sb_logits(qx, kps)
            sig = jnp.exp(lb)
            lom = lb - z
            if masked:
                lom = jnp.where(strict, lom, 0.0)
            through_block, p = _sb_running(p, jnp.sum(lom, axis=1, keepdims=True), True)
            right_of_block = (ctot if n == 1 else jnp.concatenate([ctot] * n, axis=0)) - through_block
            a = jnp.exp(lb + _split_dot(lom, after) + right_of_block)
            if masked:
                a = jnp.where(strict, a, 0.0)
            ea = _sb_pair_dots(dox, vps, "nt") * a
            left_of_block, e = _sb_running(e, jnp.sum(ea, axis=1, keepdims=True), False)
            dlom = left_of_block + jnp.dot(ea.astype(MXU_DTYPE), before, preferred_element_type=F32)
            if masked:
                dlom = jnp.where(strict, dlom, 0.0)
            dz = (ea * (1.0 - sig) - dlom * sig).astype(MXU_DTYPE)
            ab = a.astype(MXU_DTYPE)
            new_dq = list(dqs)
            for b in range(n):
                for pr in range(H // 2):
                    heads = slice(2 * pr * B, (2 * pr + 2) * B)
                    both, cols = slice((b * H + 2 * pr) * B, (b * H + 2 * pr + 2) * B), slice(pr * LANES, (pr + 1) * LANES)
                    dqp = jnp.dot(dz[both], kps[b][pr], preferred_element_type=F32)
                    new_dq[pr] = new_dq[pr] + jnp.where(low, dqp[:B], dqp[B:])
                    dk_acc[rows[b], cols] += lax.dot_general(dz[both], qx[heads], _DIMS["tn"], preferred_element_type=F32)
                    dv_acc[rows[b], cols] += lax.dot_general(ab[both], dox[heads], _DIMS["tn"], preferred_element_type=F32)
            return p, e, new_dq

        W, M = SB_SWEEP, SB_SWEEP // 2

        def wide(it, carry):
            p, e, dqs = sweep([W * it + b for b in range(W)], carry[0], carry[1], list(carry[2:]), False)
            return (p, e) + tuple(dqs)

        def mid(it, carry):
            p, e, dqs = sweep([i - i % W + b for b in range(M)], carry[0], carry[1], list(carry[2:]), False)
            return (p, e) + tuple(dqs)

        def one(it, carry):
            p, e, dqs = sweep([i - i % M + it], carry[0], carry[1], list(carry[2:]), False)
            return (p, e) + tuple(dqs)

        zero = jnp.zeros((H * B, 1), F32)
        carry = lax.fori_loop(0, i // W, wide, (zero, zero) + (jnp.zeros((B, LANES), F32),) * (H // 2))
        carry = lax.fori_loop(0, (i % W) // M, mid, carry)
        carry = lax.fori_loop(0, i % M, one, carry)
        _, _, dqs = sweep([i], carry[0], carry[1], list(carry[2:]), True)
        for pr in range(H // 2):
            dq_ref[:, pr * LANES:(pr + 1) * LANES] = (dqs[pr] * SB_SCALE).astype(dq_ref.dtype)

        @pl.when(i == nblk - 1)
        def _():
            pltpu.sync_copy(dk_acc, dk_hbm)
            pltpu.sync_copy(dv_acc, dv_hbm)

    hbm = pl.BlockSpec(memory_space=pl.ANY)
    acc = jax.ShapeDtypeStruct((L, SB_WIDTH), F32)
    return pl.pallas_call(
        body, name=name, grid=(nblk,),
        in_specs=[pl.BlockSpec((B, SB_WIDTH), lambda i: (i, P_SB // SB_WIDTH)), pl.BlockSpec((L, 2 * SB_WIDTH), lambda i: (0, 0)),
                  pl.BlockSpec((B, SB_WIDTH), lambda i: (i, Y_SB // SB_WIDTH)), pl.BlockSpec((B, LANES), lambda i: (i, 0))],
        out_specs=[pl.BlockSpec((B, SB_WIDTH), lambda i: (i, 0)), hbm, hbm],
        out_shape=[jax.ShapeDtypeStruct((L, SB_WIDTH), MXU_DTYPE), acc, acc],
        scratch_shapes=[pltpu.VMEM((L, SB_WIDTH), F32), pltpu.VMEM((L, SB_WIDTH), F32)],
        compiler_params=_params(("arbitrary",)),
    )(proj, kv, dy, ctot)


def _prep_w_in(w):
    sc, qkv, gz, gab, sb = w[:, 0:768], w[:, 768:2304], w[:, 2304:2816], w[:, 2816:2824], w[:, 2824:3592]
    pad = jnp.zeros((w.shape[0], P_END - D_IN_PROJ), w.dtype)
    return jnp.concatenate([qkv, gz, sc, sb, gab, pad], axis=1).astype(MXU_DTYPE)


def _unprep_dw_in(dw):
    qkv, gz, sc, sb, gab = dw[:, P_QKV:P_GZ], dw[:, P_GZ:P_SC], dw[:, P_SC:P_SB], dw[:, P_SB:P_GAB], dw[:, P_GAB:P_GAB + 8]
    return jnp.concatenate([sc, qkv, gz, gab, sb], axis=1)


def _prep_w_out(w):
    return jnp.concatenate([w[256:768], w[0:256], w[768:]], axis=0).astype(MXU_DTYPE)


def _unprep_dw_out(dw):
    return jnp.concatenate([dw[512:768], dw[0:512], dw[768:]], axis=0)


def _pad_lanes(v):
    return jnp.zeros((1, LANES), F32).at[0, :v.shape[0]].set(v)


def _layer_fwd(x, p, l):
    L = x.shape[0]
    tm = min(2048, L)
    n = f"l{l}_"
    h = _rms_fwd(x, p["norm_mix"], n + "rms_mix")
    proj = _matmul(h, p["w_in"], "nn", F32, n + "mm_in", tm, 768, 1024)
    cb = SC_WIDTH
    y_sc = _conv_fwd((proj, P_SC // cb + 1), p["w_sconv"], SC_KERNEL, cb, 1, MXU_DTYPE, n + "sconv",
                     x2=(proj, P_SC // cb + 2), gate=(proj, P_SC // cb))
    cqkv = _conv_fwd((proj, 0), p["w_gconv"], GDN_CONV, 256, 6, F32, n + "gconv")
    y_gdn, s_all, t_all = _gdn_fwd(cqkv, proj, p["a_log"], p["dt_bias"], p["gdn_norm"], n + "gdn")
    kv = proj[:, P_SB + SB_WIDTH:P_SB + 3 * SB_WIDTH].astype(MXU_DTYPE)
    y_sb, ctot = _sb_fwd(proj, kv, n + "sb")
    ycat = jnp.concatenate([y_gdn, y_sc, y_sb], axis=1)
    x1 = _matmul(ycat, p["w_out"], "nn", F32, n + "mm_out", tm, 512, 1024, resid=x)
    h2 = _rms_fwd(x1, p["norm_ffn"], n + "rms_ffn")
    up = _matmul(h2, p["w_up"], "nn", F32, n + "mm_up", tm, 512, 1024)
    act = _ffn_act_fwd(up, p["w_fconv"], n + "ffn_act")
    x2 = _matmul(act, p["w_down"], "nn", F32, n + "mm_down", tm, 512, 1408, resid=x1)
    saved = dict(x=x, h=h, proj=proj, cqkv=cqkv, s_all=s_all, t_all=t_all, kv=kv, ctot=ctot, ycat=ycat, x1=x1, h2=h2, up=up, act=act)
    return x2, saved


def _layer_bwd(dx2, dx2b, p, s, l):
    L = dx2.shape[0]
    tm, tkl = min(1024, L), min(1024, L)
    n = f"l{l}_"
    g = {}
    g["w_ffn_down"] = _matmul(s["act"], dx2b, "tn", F32, n + "mm_ddown", 1408, 1024, tkl)
    dact = _matmul(dx2b, p["w_down"], "nt", F32, n + "mm_dact", tm, 1408, 1024)
    dup, dwf = _ffn_act_bwd(s["up"], p["w_fconv"], dact, n + "dffn_act")
    g["w_ffn_conv"] = dwf[:FFN_CONV]
    g["w_ffn_up"] = _matmul(s["h2"], dup, "tn", F32, n + "mm_dup", 1024, 2816, tkl)
    dh2 = _matmul(dup, p["w_up"], "nt", F32, n + "mm_dh2", tm, 512, 2 * D_FF)
    dx1, dx1b, dwn = _rms_bwd(s["x1"], p["norm_ffn"], dh2, dx2, n + "drms_ffn")
    g["w_norm_ffn"] = dwn[0]

    g["w_mix_out"] = _unprep_dw_out(_matmul(s["ycat"], dx1b, "tn", F32, n + "mm_dout", 1024, 1024, tkl))
    dycat = _matmul(dx1b, p["w_out"], "nt", F32, n + "mm_dycat", tm, 512, 1024)
    proj = s["proj"]
    cb = SC_WIDTH
    dsc_c, dsc_h, dsc_b, dws = _conv_bwd((proj, P_SC // cb + 1), p["w_sconv"], (dycat, Y_SC // cb), SC_KERNEL, cb, 1,
                                         MXU_DTYPE, n + "dsconv", x2=(proj, P_SC // cb + 2), gate=(proj, P_SC // cb))
    g["w_sconv"] = dws[:SC_KERNEL]
    dcqkv, dgz, dgab, dal, ddb, dgn = _gdn_bwd(s["cqkv"], proj, p["a_log"], p["dt_bias"], p["gdn_norm"], s["s_all"], s["t_all"], dycat,
                                               n + "dgdn")
    g["gdn_a_log"], g["gdn_dt_bias"], g["w_gdn_norm"] = dal[0, :GDN_HEADS], ddb[0, :GDN_HEADS], dgn[0]
    dqkv, dwg = _conv_bwd((proj, 0), p["w_gconv"], (dcqkv, 0), GDN_CONV, 256, 6, MXU_DTYPE, n + "dgconv")
    g["w_gdn_conv"] = dwg[:GDN_CONV]
    dq, dk, dv = _sb_bwd(proj, s["kv"], dycat, s["ctot"], n + "dsb")
    dproj = jnp.concatenate(
        [dqkv, dgz, dsc_b, dsc_c, dsc_h, dq, dk.astype(MXU_DTYPE), dv.astype(MXU_DTYPE), dgab.astype(MXU_DTYPE),
         jnp.zeros((L, P_END - P_GAB - LANES), MXU_DTYPE)], axis=1)
    g["w_mix_in"] = _unprep_dw_in(_matmul(s["h"], dproj, "tn", F32, n + "mm_din", 1024, 1920, tkl))
    dh = _matmul(dproj, p["w_in"], "nt", F32, n + "mm_dh", tm, 512, P_END)
    dx, dxb, dwm = _rms_bwd(s["x"], p["norm_mix"], dh, dx1, n + "drms_mix")
    g["w_norm_mix"] = dwm[0]
    return dx, dxb, g


WEIGHTS = ["w_norm_mix", "w_mix_in", "w_sconv", "w_gdn_conv", "gdn_a_log", "gdn_dt_bias", "w_gdn_norm", "w_mix_out",
           "w_norm_ffn", "w_ffn_up", "w_ffn_conv", "w_ffn_down", "w_norm_final"]


def _local_step(x, w, target):
    layers = []
    for l in range(DEPTH):
        layers.append(dict(
            norm_mix=w["w_norm_mix"][l][None], w_in=_prep_w_in(w["w_mix_in"][l]), w_sconv=w["w_sconv"][l],
            w_gconv=w["w_gdn_conv"][l], a_log=_pad_lanes(w["gdn_a_log"][l]), dt_bias=_pad_lanes(w["gdn_dt_bias"][l]),
            gdn_norm=w["w_gdn_norm"][l][None], w_out=_prep_w_out(w["w_mix_out"][l]), norm_ffn=w["w_norm_ffn"][l][None],
            w_up=w["w_ffn_up"][l].astype(MXU_DTYPE), w_fconv=w["w_ffn_conv"][l], w_down=w["w_ffn_down"][l].astype(MXU_DTYPE)))
    saved = []
    for l in range(DEPTH):
        x, s = _layer_fwd(x, layers[l], l)
        saved.append(s)
    loss, dx, dxb, dwf = _loss_head(x, w["w_norm_final"][None], target, "loss_head")
    grads = [None] * DEPTH
    for l in reversed(range(DEPTH)):
        dx, dxb, grads[l] = _layer_bwd(dx, dxb, layers[l], saved[l], l)
    out = {k: jnp.stack([grads[l][k] for l in range(DEPTH)]) for k in WEIGHTS if k != "w_norm_final"}
    out["w_norm_final"] = dwf[0]
    return loss, dx, out


N_CHIPS = 4
_HBM = pl.BlockSpec(memory_space=pl.ANY)


def _other_chips(x, y):
    return [(1 - x, y), (x, 1 - y), (1 - x, 1 - y)]


def _remote(src, dst, send_sems, recv_sems, k, to):
    return pltpu.make_async_remote_copy(src_ref=src, dst_ref=dst, send_sem=send_sems.at[k], recv_sem=recv_sems.at[k],
                                        device_id=to, device_id_type=pl.DeviceIdType.MESH)


def _all_gather(bufs, name):
    nb = len(bufs)

    def body(*refs):
        ins, outs = refs[:nb], refs[nb:2 * nb]
        send_sems, recv_sems, local_sems = refs[2 * nb:]
        x, y, c = lax.axis_index("x"), lax.axis_index("y"), lax.axis_index("c")
        me, sibling, chips = (x, y, c), (x, y, 1 - c), _other_chips(x, y)

        def copy(b, k, block, to, src=None):
            slot = outs[b].at[4 * block[0] + 2 * block[1] + block[2]]
            return _remote(slot if src is None else src, slot, send_sems.at[b], recv_sems.at[b], k, to)

        local = [pltpu.make_async_copy(ins[b], outs[b].at[4 * x + 2 * y + c], local_sems.at[b]) for b in range(nb)]
        first = [copy(b, 0, me, sibling, src=ins[b]) for b in range(nb)]
        first += [copy(b, 1 + j, me, (*chip, c), src=ins[b]) for j, chip in enumerate(chips) for b in range(nb)]
        for cp in local + first:
            cp.start()
        passed = []
        for j, chip in enumerate(chips):
            for b in range(nb):
                copy(b, 1 + j, (*chip, c), me).wait_recv()
                passed.append(copy(b, 4 + j, (*chip, c), sibling))
                passed[-1].start()
        for b in range(nb):
            copy(b, 0, sibling, me).wait_recv()
        for j, chip in enumerate(chips):
            for b in range(nb):
                copy(b, 4 + j, (*chip, 1 - c), me).wait_recv()
        for cp in first + passed:
            cp.wait_send()
        for cp in local:
            cp.wait()

    return pl.pallas_call(
        body, name=name, in_specs=[_HBM] * nb, out_specs=[_HBM] * nb,
        out_shape=[jax.ShapeDtypeStruct((N_DEV,) + b.shape, b.dtype) for b in bufs],
        scratch_shapes=[pltpu.SemaphoreType.DMA((nb, N_DEV - 1)), pltpu.SemaphoreType.DMA((nb, N_DEV - 1)),
                        pltpu.SemaphoreType.DMA((nb,))],
    )(*bufs)


def _to_sibling(bufs, name):
    nb = len(bufs)

    def body(*refs):
        ins, outs = refs[:nb], refs[nb:2 * nb]
        send_sems, recv_sems = refs[2 * nb:]
        x, y, c = lax.axis_index("x"), lax.axis_index("y"), lax.axis_index("c")
        copies = [_remote(ins[b].at[1 - c], outs[b], send_sems, recv_sems, b, (x, y, 1 - c)) for b in range(nb)]
        for cp in copies:
            cp.start()
        for cp in copies:
            cp.wait()

    return pl.pallas_call(
        body, name=name, in_specs=[_HBM] * nb, out_specs=[_HBM] * nb,
        out_shape=[jax.ShapeDtypeStruct(b.shape[1:], b.dtype) for b in bufs],
        scratch_shapes=[pltpu.SemaphoreType.DMA((nb,)), pltpu.SemaphoreType.DMA((nb,))],
    )(*bufs)


def _to_chips(bufs, name):
    nb = len(bufs)

    def body(*refs):
        ins, outs = refs[:nb], refs[nb:2 * nb]
        send_sems, recv_sems, local_sems = refs[2 * nb:]
        x, y, c = lax.axis_index("x"), lax.axis_index("y"), lax.axis_index("c")
        here = 2 * x + y
        local = [pltpu.make_async_copy(ins[b].at[here], outs[b].at[here], local_sems.at[b]) for b in range(nb)]
        remote = [_remote(ins[b].at[2 * px + py], outs[b].at[here], send_sems.at[b], recv_sems.at[b], j, (px, py, c))
                  for j, (px, py) in enumerate(_other_chips(x, y)) for b in range(nb)]
        for cp in local + remote:
            cp.start()
        for cp in remote:
            cp.wait()
        for cp in local:
            cp.wait()

    return pl.pallas_call(
        body, name=name, in_specs=[_HBM] * nb, out_specs=[_HBM] * nb,
        out_shape=[jax.ShapeDtypeStruct(b.shape, b.dtype) for b in bufs],
        scratch_shapes=[pltpu.SemaphoreType.DMA((nb, N_CHIPS - 1)), pltpu.SemaphoreType.DMA((nb, N_CHIPS - 1)),
                        pltpu.SemaphoreType.DMA((nb,))],
    )(*bufs)


def _pair_sum(a, b, row_tile, name):
    n, D0, R, C = a.shape

    def body(a_ref, b_ref, o_ref):
        o_ref[...] = (a_ref[...].astype(F32) + b_ref[...].astype(F32)).astype(o_ref.dtype)

    blk = pl.BlockSpec((1, 1, row_tile, C), lambda s, l, i: (s, l, i, 0))
    return pl.pallas_call(
        body, name=name, grid=(n, D0, R // row_tile), in_specs=[blk, blk], out_specs=blk,
        out_shape=jax.ShapeDtypeStruct(a.shape, a.dtype), compiler_params=_params(("parallel", "parallel", "parallel")),
    )(a, b)


def _sum_sources(recv, row_tile, name):
    n, R, _ = recv.shape

    def body(r_ref, o_ref):
        acc = r_ref[0].astype(F32)
        for s in range(1, n):
            acc = acc + r_ref[s].astype(F32)
        o_ref[...] = acc

    return pl.pallas_call(
        body, name=name, grid=(R // row_tile,),
        in_specs=[pl.BlockSpec((n, row_tile, LANES), lambda i: (0, i, 0))],
        out_specs=pl.BlockSpec((row_tile, LANES), lambda i: (i, 0)),
        out_shape=jax.ShapeDtypeStruct((R, LANES), F32),
        compiler_params=_params(("parallel",)),
    )(recv)


def _adamw_math(g, w, m, v):
    nm = ADAM_B1 * m + (1.0 - ADAM_B1) * g
    nv = ADAM_B2 * v + (1.0 - ADAM_B2) * (g * g)
    m_hat = nm / (1.0 - ADAM_B1 ** ADAM_STEP)
    v_hat = nv / (1.0 - ADAM_B2 ** ADAM_STEP)
    return -ADAM_LR * (m_hat / (jnp.sqrt(v_hat) + ADAM_EPS) + ADAM_WD * w), nm, nv


def _sum_adamw(recv, w, m, v, row_tile, name):
    D0, R, C = w.shape
    n = recv.shape[0]

    def body(r_ref, w_ref, m_ref, v_ref, g_ref, d_ref, nm_ref, nv_ref):
        g = r_ref[0, 0].astype(F32)
        for s in range(1, n):
            g = g + r_ref[s, 0].astype(F32)
        g_ref[0] = g
        d_ref[0], nm_ref[0], nv_ref[0] = _adamw_math(g, w_ref[0], m_ref[0], v_ref[0])

    blk = pl.BlockSpec((1, row_tile, C), lambda l, i: (l, i, 0))
    out = jax.ShapeDtypeStruct((D0, R, C), F32)
    return pl.pallas_call(
        body, name=name, grid=(D0, R // row_tile),
        in_specs=[pl.BlockSpec((n, 1, row_tile, C), lambda l, i: (0, l, i, 0)), blk, blk, blk],
        out_specs=[blk] * 4, out_shape=[out] * 4,
        compiler_params=_params(("parallel", "parallel")),
    )(recv, w, m, v)


def _adamw(g, w, m, v, row_tile, name):
    R = g.shape[0]

    def body(g_ref, w_ref, m_ref, v_ref, d_ref, nm_ref, nv_ref):
        d_ref[...], nm_ref[...], nv_ref[...] = _adamw_math(g_ref[...], w_ref[...], m_ref[...], v_ref[...])

    blk = pl.BlockSpec((row_tile, LANES), lambda i: (i, 0))
    out = jax.ShapeDtypeStruct((R, LANES), F32)
    return pl.pallas_call(
        body, name=name, grid=(R // row_tile,), in_specs=[blk] * 4, out_specs=[blk] * 3, out_shape=[out] * 3,
        compiler_params=_params(("parallel",)),
    )(g, w, m, v)


def _pack(arrs, rows, dtype):
    flat = jnp.concatenate([a.reshape(-1).astype(dtype) for a in arrs])
    return jnp.pad(flat, (0, rows * LANES - flat.shape[0])).reshape(rows, LANES)


def _unpack(buf, shapes):
    lead = buf.shape[:-2]
    flat = buf.reshape(lead + (-1,))
    out, off = [], 0
    for shp in shapes:
        n = 1
        for d in shp:
            n *= d
        out.append(flat[..., off:off + n].reshape(lead + tuple(shp)))
        off += n
    return out


BIG = ["w_mix_in", "w_mix_out", "w_ffn_up", "w_ffn_down"]
BIG_AXIS = {"w_mix_in": 2, "w_mix_out": 1, "w_ffn_up": 2, "w_ffn_down": 1}
CONV = ["w_sconv", "w_gdn_conv", "w_ffn_conv"]
REPL = ["w_norm_mix", "gdn_a_log", "gdn_dt_bias", "w_gdn_norm", "w_norm_ffn", "w_norm_final"]
BIG_ROW_TILE = {"w_mix_in": 512, "w_mix_out": 128, "w_ffn_up": 512, "w_ffn_down": 352}
SMALL_ROWS = 416
CONV_ROWS = 48


def kernel(x, w_norm_mix, w_mix_in, w_sconv, w_gdn_conv, gdn_a_log, gdn_dt_bias, w_gdn_norm, w_mix_out, w_norm_ffn, w_ffn_up, w_ffn_conv, w_ffn_down, w_norm_final, loss_target, m_w_norm_mix, m_w_mix_in, m_w_sconv, m_w_gdn_conv, m_gdn_a_log, m_gdn_dt_bias, m_w_gdn_norm, m_w_mix_out, m_w_norm_ffn, m_w_ffn_up, m_w_ffn_conv, m_w_ffn_down, m_w_norm_final, v_w_norm_mix, v_w_mix_in, v_w_sconv, v_w_gdn_conv, v_gdn_a_log, v_gdn_dt_bias, v_w_gdn_norm, v_w_mix_out, v_w_norm_ffn, v_w_ffn_up, v_w_ffn_conv, v_w_ffn_down, v_w_norm_final):
    w = dict(w_norm_mix=w_norm_mix, w_mix_in=w_mix_in, w_sconv=w_sconv, w_gdn_conv=w_gdn_conv, gdn_a_log=gdn_a_log,
             gdn_dt_bias=gdn_dt_bias, w_gdn_norm=w_gdn_norm, w_mix_out=w_mix_out, w_norm_ffn=w_norm_ffn, w_ffn_up=w_ffn_up,
             w_ffn_conv=w_ffn_conv, w_ffn_down=w_ffn_down, w_norm_final=w_norm_final)
    m = dict(w_norm_mix=m_w_norm_mix, w_mix_in=m_w_mix_in, w_sconv=m_w_sconv, w_gdn_conv=m_w_gdn_conv, gdn_a_log=m_gdn_a_log,
             gdn_dt_bias=m_gdn_dt_bias, w_gdn_norm=m_w_gdn_norm, w_mix_out=m_w_mix_out, w_norm_ffn=m_w_norm_ffn,
             w_ffn_up=m_w_ffn_up, w_ffn_conv=m_w_ffn_conv, w_ffn_down=m_w_ffn_down, w_norm_final=m_w_norm_final)
    v = dict(w_norm_mix=v_w_norm_mix, w_mix_in=v_w_mix_in, w_sconv=v_w_sconv, w_gdn_conv=v_w_gdn_conv, gdn_a_log=v_gdn_a_log,
             gdn_dt_bias=v_gdn_dt_bias, w_gdn_norm=v_w_gdn_norm, w_mix_out=v_w_mix_out, w_norm_ffn=v_w_norm_ffn,
             w_ffn_up=v_w_ffn_up, w_ffn_conv=v_w_ffn_conv, w_ffn_down=v_w_ffn_down, w_norm_final=v_w_norm_final)
    me = 4 * lax.axis_index("x") + 2 * lax.axis_index("y") + lax.axis_index("c")
    conv_shapes = [w[k].shape for k in CONV]

    gathered = _all_gather([w[k].astype(MXU_DTYPE) for k in BIG] + [_pack([w[k] for k in CONV], CONV_ROWS, F32)],
                           "gather_weights")
    full = dict(w)
    for k, got in zip(BIG, gathered):
        full[k] = jnp.concatenate([got[s] for s in range(N_DEV)], axis=BIG_AXIS[k])
    for k, got in zip(CONV, _unpack(gathered[-1], conv_shapes)):
        full[k] = jnp.concatenate([got[s] for s in range(N_DEV)], axis=2)

    loss, dx, grads = _local_step(x[0], full, loss_target[0])

    small = CONV + REPL
    core = lax.axis_index("c")
    by_core = []
    for k in BIG:
        piece = jnp.split(grads[k], N_DEV, axis=BIG_AXIS[k])
        by_core.append(jnp.stack([jnp.stack([piece[2 * ch + p] for ch in range(N_CHIPS)]) for p in range(2)]).astype(MXU_DTYPE))
    from_sibling = _to_sibling(by_core, "grads_to_sibling")
    chip_sums = [_pair_sum(lax.dynamic_index_in_dim(mine, core, 0, keepdims=False), theirs, BIG_ROW_TILE[k], "pair_sum_" + k)
                 for k, mine, theirs in zip(BIG, by_core, from_sibling)]
    received = _to_chips(chip_sums, "grads_to_chips")
    g, delta, new_m, new_v = {}, {}, {}, {}
    for k, got in zip(BIG, received):
        g[k], delta[k], new_m[k], new_v[k] = _sum_adamw(got, w[k], m[k], v[k], BIG_ROW_TILE[k], "adamw_" + k)
    small_parts = _all_gather([_pack([grads[k] for k in small], SMALL_ROWS, F32)], "gather_small_grads")[0]
    g_small = _unpack(_sum_sources(small_parts, SMALL_ROWS, "sum_small"), [grads[k].shape for k in small])
    for k, gs in zip(small, g_small):
        g[k] = lax.dynamic_slice_in_dim(gs, me * w[k].shape[2], w[k].shape[2], axis=2) if k in CONV else gs

    small_shapes = [w[k].shape for k in small]
    small_rows = -(-sum(w[k].size for k in small) // (SUBLANES * LANES)) * SUBLANES
    d_sm, m_sm, v_sm = _adamw(_pack([g[k] for k in small], small_rows, F32), _pack([w[k] for k in small], small_rows, F32),
                              _pack([m[k] for k in small], small_rows, F32), _pack([v[k] for k in small], small_rows, F32),
                              small_rows, "adamw_small")
    for dst, small_buf in ((delta, d_sm), (new_m, m_sm), (new_v, v_sm)):
        dst.update(zip(small, _unpack(small_buf, small_shapes)))

    loss_all = lax.psum(loss[0, 0], ("x", "y", "c"))
    return (loss_all, dx[None], *[g[k] for k in WEIGHTS], *[delta[k] for k in WEIGHTS], *[new_m[k] for k in WEIGHTS],
            *[new_v[k] for k in WEIGHTS])
```
